```python
import math
import jax, jax.numpy as jnp
from jax import lax
import numpy as np

D_MODEL = 1024
BATCH = 8
SEQ = 2048
DEPTH = 4

N_MIXERS = 4
RMS_EPS = 1e-6
HEAD_DIM = 64
N_BUCKETS = 32
MAX_DISTANCE = 1024
N_BIAS_HEADS = D_MODEL // HEAD_DIM
POOL_WINDOWS = (2, 4, 8, 16)
N_POOL_GROUPS = len(POOL_WINDOWS)
POOL_GROUP = D_MODEL // N_POOL_GROUPS
DIL_PAIRS = ((128, 1), (512, 4), (2048, 16))
DIL_HEADS = D_MODEL // HEAD_DIM
DIFF_HEADS = D_MODEL // (2 * HEAD_DIM)
Q_BLOCK = 128
GQA_Q_HEADS = D_MODEL // HEAD_DIM
GQA_KV_HEADS = 4
GQA_WINDOW = 128
D_FF = 7 * D_MODEL // 2
N_EXPERTS = 8
TOP_K = 2

kernel_name = "hybrid_interleaved_encoder"


def _rmsnorm(x, g):
    xf = x.astype(jnp.float32)
    y = xf * lax.rsqrt(jnp.mean(xf * xf, axis=-1, keepdims=True) + RMS_EPS)
    return (y * g.astype(jnp.float32)).astype(x.dtype)


def _rel_bucket(rel):
    half = N_BUCKETS // 2
    max_exact = half // 2
    n = jnp.abs(rel)
    big = max_exact + (jnp.log(jnp.maximum(n, 1).astype(jnp.float32) / max_exact)
                       / math.log(MAX_DISTANCE / max_exact) * (half - max_exact)).astype(jnp.int32)
    big = jnp.minimum(big, half - 1)
    return jnp.where(rel > 0, half, 0) + jnp.where(n < max_exact, n, big)


def _block_len(length):
    for c in (64, 32, 16, 8):
        if length % c == 0:
            return c
    return length


def _pool_mixer(h, w_in, w_group, scale, w_out):
    b, s, _ = h.shape
    u = (h @ w_in).reshape(b, s, N_POOL_GROUPS, POOL_GROUP)
    cs = jnp.cumsum(u.astype(jnp.float32), axis=1)
    cs = jnp.concatenate([jnp.zeros_like(cs[:, :1]), cs], axis=1)
    pos = jnp.arange(s)
    pooled = []
    for g, w in enumerate(POOL_WINDOWS):
        lo = jnp.clip(pos - w // 2, 0, s)
        hi = jnp.clip(pos + (w - w // 2), 0, s)
        cnt = (hi - lo).astype(jnp.float32)
        pooled.append((cs[:, hi, g] - cs[:, lo, g]) / cnt[None, :, None])
    pooled = jnp.stack(pooled, axis=2).astype(u.dtype)
    mixed = pooled - u
    y = jnp.einsum('bsgc,gcd->bsgd', mixed, w_group).reshape(b, s, -1) * scale
    return y @ w_out


def _strided_band_attention(q, k, v, dil, half, rel_bias):
    b, s, h, c = q.shape
    L = s // dil
    qb = _block_len(L)
    nb = L // qb
    kb = qb + 2 * half

    def split(t):
        return t.reshape(b, L, dil, h, c).transpose(0, 2, 3, 1, 4)

    qs, ks, vs = split(q), split(k), split(v)
    pad = ((0, 0), (0, 0), (0, 0), (half, half), (0, 0))
    idx = jnp.arange(nb)[:, None] * qb + jnp.arange(kb)[None, :]
    kw = jnp.pad(ks, pad)[:, :, :, idx]
    vw = jnp.pad(vs, pad)[:, :, :, idx]
    qr = qs.reshape(b, dil, h, nb, qb, c)
    logits = jnp.einsum('brhnqc,brhnkc->brhnqk', qr, kw).astype(jnp.float32) * (c ** -0.5)
    rel = jnp.arange(kb)[None, :] - half - jnp.arange(qb)[:, None]
    kpos = idx - half
    valid = (jnp.abs(rel) <= half)[None] & ((kpos >= 0) & (kpos < L))[:, None, :]
    bias = rel_bias[_rel_bucket(rel * dil)].astype(jnp.float32).transpose(2, 0, 1)
    logits = jnp.where(valid, logits + bias[:, None], -jnp.inf)
    lse = jax.nn.logsumexp(logits, axis=-1)
    p = jnp.exp(logits - lse[..., None]).astype(v.dtype)
    o = jnp.einsum('brhnqk,brhnkc->brhnqc', p, vw)
    o = o.reshape(b, dil, h, L, c).transpose(0, 3, 1, 2, 4).reshape(b, s, h, c)
    lse = lse.reshape(b, dil, h, L).transpose(0, 3, 1, 2).reshape(b, s, h)
    return o, lse


def _dilated_mixer(hn, w_in, w_out, rel_bias):
    b, s, _ = hn.shape
    n_groups = len(DIL_PAIRS)
    qkv = (hn @ w_in).reshape(b, s, n_groups, 3, DIL_HEADS, HEAD_DIM)
    outs, lses = [], []
    for g, (win, dil) in enumerate(DIL_PAIRS):
        o, lse = _strided_band_attention(qkv[:, :, g, 0], qkv[:, :, g, 1], qkv[:, :, g, 2],
                                         dil, win // (2 * dil), rel_bias)
        outs.append(o)
        lses.append(lse)
    wts = jax.nn.softmax(jnp.stack(lses, axis=0), axis=0)
    o = jnp.sum(jnp.stack(outs, axis=0) * wts[..., None].astype(hn.dtype), axis=0)
    return o.reshape(b, s, -1) @ w_out


def _diff_mixer(hn, w_in, lam_vecs, subln, w_out, rel_bias, layer):
    b, s, d = hn.shape
    qkv = hn @ w_in
    q = qkv[..., :d].reshape(b, s, DIFF_HEADS, 2, HEAD_DIM)
    k = qkv[..., d:2 * d].reshape(b, s, DIFF_HEADS, 2, HEAD_DIM)
    v = qkv[..., 2 * d:].reshape(b, s, DIFF_HEADS, 2 * HEAD_DIM)
    lam_init = 0.8 - 0.6 * math.exp(-0.3 * layer)
    lv = lam_vecs.astype(jnp.float32)
    lam = jnp.exp(jnp.sum(lv[0] * lv[1])) - jnp.exp(jnp.sum(lv[2] * lv[3])) + lam_init
    scale = HEAD_DIM ** -0.5
    nq = s // Q_BLOCK
    qblocks = jnp.moveaxis(q.reshape(b, nq, Q_BLOCK, DIFF_HEADS, 2, HEAD_DIM), 1, 0)
    kpos = jnp.arange(s)

    def one_block(args):
        qblk, n = args
        rel = kpos[None, :] - (n * Q_BLOCK + jnp.arange(Q_BLOCK))[:, None]
        bias = rel_bias[_rel_bucket(rel)].astype(jnp.float32)
        bias = bias.reshape(Q_BLOCK, s, DIFF_HEADS, 2).transpose(2, 3, 0, 1)
        logits = jnp.einsum('bqhjc,bkhjc->bhjqk', qblk, k).astype(jnp.float32) * scale + bias
        p = jax.nn.softmax(logits, axis=-1)
        a = (p[:, :, 0] - lam * p[:, :, 1]).astype(v.dtype)
        return jnp.einsum('bhqk,bkhc->bqhc', a, v)

    o = lax.map(one_block, (qblocks, jnp.arange(nq)))
    o = jnp.moveaxis(o, 0, 1).reshape(b, s, DIFF_HEADS, 2 * HEAD_DIM)
    o = _rmsnorm(o, subln) * (1.0 - lam_init)
    return o.reshape(b, s, d) @ w_out


def _window_mixer(hn, w_in, sink, w_out, rel_bias):
    b, s, _ = hn.shape
    grp = GQA_Q_HEADS // GQA_KV_HEADS
    W = GQA_WINDOW
    nqc = GQA_Q_HEADS * HEAD_DIM
    nkc = GQA_KV_HEADS * HEAD_DIM
    qkv = hn @ w_in
    q = qkv[..., :nqc].reshape(b, s, GQA_KV_HEADS, grp, HEAD_DIM)
    k = qkv[..., nqc:nqc + nkc].reshape(b, s, GQA_KV_HEADS, HEAD_DIM)
    v = qkv[..., nqc + nkc:].reshape(b, s, GQA_KV_HEADS, HEAD_DIM)
    nb = s // W
    kb = 3 * W
    idx = jnp.arange(nb)[:, None] * W + jnp.arange(kb)[None, :]
    pad = ((0, 0), (W, W), (0, 0), (0, 0))
    kw = jnp.pad(k, pad)[:, idx]
    vw = jnp.pad(v, pad)[:, idx]
    qr = q.reshape(b, nb, W, GQA_KV_HEADS, grp, HEAD_DIM)
    logits = jnp.einsum('bnqhgc,bnjhc->bnhgqj', qr, kw).astype(jnp.float32) * (HEAD_DIM ** -0.5)
    rel = jnp.arange(kb)[None, :] - W - jnp.arange(W)[:, None]
    kpos = idx - W
    valid = (jnp.abs(rel) <= W)[None] & ((kpos >= 0) & (kpos < s))[:, None, :]
    bias = rel_bias[_rel_bucket(rel)].astype(jnp.float32).transpose(2, 0, 1)
    bias = bias.reshape(GQA_KV_HEADS, grp, W, kb)
    logits = jnp.where(valid[:, None, None], logits + bias, -jnp.inf)
    sk = sink.astype(jnp.float32).reshape(GQA_KV_HEADS, grp)[:, :, None, None]
    m = jnp.maximum(jnp.max(logits, axis=-1, keepdims=True), sk)
    e = jnp.exp(logits - m)
    p = e / (jnp.sum(e, axis=-1, keepdims=True) + jnp.exp(sk - m))
    o = jnp.einsum('bnhgqj,bnjhc->bnqhgc', p.astype(v.dtype), vw)
    return o.reshape(b, s, -1) @ w_out


def _swiglu(h, w_gate_up, w_down):
    gu = h @ w_gate_up
    g, u = gu[..., :D_FF], gu[..., D_FF:]
    return (jax.nn.silu(g) * u) @ w_down


def _moe(h, router, w_gate_up, w_down):
    logits = (h @ router).astype(jnp.float32)
    top_v, top_i = lax.top_k(logits, TOP_K)
    gates = jax.nn.softmax(top_v, axis=-1)
    gate_full = jnp.sum(jax.nn.one_hot(top_i, N_EXPERTS, dtype=jnp.float32) * gates[..., None], axis=-2)
    gate_full = gate_full.astype(h.dtype)
    y = jnp.zeros_like(h)
    for e in range(N_EXPERTS):
        y = y + gate_full[..., e:e + 1] * _swiglu(h, w_gate_up[e], w_down[e])
    return y


def setup_inputs(seed: int = 0) -> dict:
    key = jax.random.key(seed)
    ks = jax.random.split(key, 24)
    f32 = jnp.float32

    def nrm(k, shape, scale):
        return jax.random.normal(k, shape, f32) * scale

    n_a = len(range(0, DEPTH, N_MIXERS))
    n_b = len(range(1, DEPTH, N_MIXERS))
    n_c = len(range(2, DEPTH, N_MIXERS))
    n_d = len(range(3, DEPTH, N_MIXERS))
    n_f = len(range(0, DEPTH, 2))
    n_m = len(range(1, DEPTH, 2))
    D = D_MODEL
    n_dil = len(DIL_PAIRS)
    gqa_cols = (GQA_Q_HEADS + 2 * GQA_KV_HEADS) * HEAD_DIM
    return {
        "x": nrm(ks[0], (BATCH, SEQ, D), 1.0),
        "rel_bias": nrm(ks[1], (N_BUCKETS, N_BIAS_HEADS), 0.5),
        "norm_mix": 1.0 + nrm(ks[2], (DEPTH, D), 0.1),
        "norm_ffn": 1.0 + nrm(ks[3], (DEPTH, D), 0.1),
        "norm_final": 1.0 + nrm(ks[4], (D,), 0.1),
        "a_w_in": nrm(ks[5], (n_a, D, D), D ** -0.5),
        "a_w_group": nrm(ks[6], (n_a, N_POOL_GROUPS, POOL_GROUP, POOL_GROUP), POOL_GROUP ** -0.5),
        "a_scale": 1.0 + nrm(ks[7], (n_a, D), 0.1),
        "a_w_out": nrm(ks[8], (n_a, D, D), D ** -0.5),
        "b_w_in": nrm(ks[9], (n_b, D, n_dil * 3 * DIL_HEADS * HEAD_DIM), D ** -0.5),
        "b_w_out": nrm(ks[10], (n_b, DIL_HEADS * HEAD_DIM, D), (DIL_HEADS * HEAD_DIM) ** -0.5),
        "c_w_in": nrm(ks[11], (n_c, D, 3 * D), D ** -0.5),
        "c_lambda": nrm(ks[12], (n_c, 4, HEAD_DIM), 0.1),
        "c_subln": 1.0 + nrm(ks[13], (n_c, 2 * HEAD_DIM), 0.1),
        "c_w_out": nrm(ks[14], (n_c, D, D), D ** -0.5),
        "d_w_in": nrm(ks[15], (n_d, D, gqa_cols), D ** -0.5),
        "d_sink": nrm(ks[16], (n_d, GQA_Q_HEADS), 1.0),
        "d_w_out": nrm(ks[17], (n_d, GQA_Q_HEADS * HEAD_DIM, D), (GQA_Q_HEADS * HEAD_DIM) ** -0.5),
        "f_w_gate_up": nrm(ks[18], (n_f, D, 2 * D_FF), D ** -0.5),
        "f_w_down": nrm(ks[19], (n_f, D_FF, D), D_FF ** -0.5),
        "m_router": nrm(ks[20], (n_m, D, N_EXPERTS), D ** -0.5),
        "m_w_gate_up": nrm(ks[21], (n_m, N_EXPERTS, D, 2 * D_FF), D ** -0.5),
        "m_w_down": nrm(ks[22], (n_m, N_EXPERTS, D_FF, D), D_FF ** -0.5),
    }


def reference(x, rel_bias, norm_mix, norm_ffn, norm_final,
              a_w_in, a_w_group, a_scale, a_w_out,
              b_w_in, b_w_out,
              c_w_in, c_lambda, c_subln, c_w_out,
              d_w_in, d_sink, d_w_out,
              f_w_gate_up, f_w_down,
              m_router, m_w_gate_up, m_w_down):
    for i in range(DEPTH):
        mixer = i % N_MIXERS
        slot = i // N_MIXERS
        hn = _rmsnorm(x, norm_mix[i])
        if mixer == 0:
            y = _pool_mixer(hn, a_w_in[slot], a_w_group[slot], a_scale[slot], a_w_out[slot])
        elif mixer == 1:
            y = _dilated_mixer(hn, b_w_in[slot], b_w_out[slot], rel_bias)
        elif mixer == 2:
            y = _diff_mixer(hn, c_w_in[slot], c_lambda[slot], c_subln[slot], c_w_out[slot], rel_bias, i)
        else:
            y = _window_mixer(hn, d_w_in[slot], d_sink[slot], d_w_out[slot], rel_bias)
        x = x + y
        hn = _rmsnorm(x, norm_ffn[i])
        if i % 2 == 0:
            y = _swiglu(hn, f_w_gate_up[i // 2], f_w_down[i // 2])
        else:
            y = _moe(hn, m_router[i // 2], m_w_gate_up[i // 2], m_w_down[i // 2])
        x = x + y
    return _rmsnorm(x, norm_final)
```

```python
import functools
import math

import numpy as np
import jax
import jax.numpy as jnp
from jax import lax
from jax.experimental import pallas as pl
from jax.experimental.pallas import tpu as pltpu

F32 = jnp.float32
BF16 = jnp.bfloat16
I32 = jnp.int32
U32 = jnp.uint32

RMS_EPS = 1e-6
HEAD_DIM = 64
N_BUCKETS = 32
MAX_DISTANCE = 1024
POOL_WINDOWS = (2, 4, 8, 16)
DIL_PAIRS = ((128, 1), (512, 4), (2048, 16))
Q_SCALE = HEAD_DIM ** -0.5
GQA_KV_HEADS = 4
GQA_WINDOW = 128
N_EXPERTS = 8
DIFF_LAYER = 2

LANES = 128
SUBLANES = 8
VMEM_LIMIT = 56 * 1024 * 1024
ROW_TILE = 1024
FF_TILE = 512
MOE_TILE = 1024
DMA_TILE = 512
ATT_TQ = 128
DIFF_TQ = 512
MASK_VALUE = -1e30


def _cparams(n_axes, vmem=VMEM_LIMIT):
    return pltpu.CompilerParams(dimension_semantics=("arbitrary",) * n_axes,
                                vmem_limit_bytes=vmem)


def _next_pow2(n):
    return 1 << (n - 1).bit_length()


def _rms_rows(x, g):
    ms = jnp.mean(x * x, axis=-1, keepdims=True)
    return x * lax.rsqrt(ms + RMS_EPS) * g


def _norm_matmul_kernel(x_ref, g_ref, w_ref, o_ref, hn_ref):
    @pl.when(pl.program_id(1) == 0)
    def _():
        hn_ref[...] = _rms_rows(x_ref[...], g_ref[...]).astype(BF16)

    o_ref[...] = jnp.dot(hn_ref[...], w_ref[...].astype(BF16),
                         preferred_element_type=F32).astype(o_ref.dtype)


def _norm_matmul(x, g, w, out_dtype):
    t, d = x.shape
    n = w.shape[1]
    tm = ROW_TILE
    tn = next(c for c in (1024, 768, 512, 256, 128) if n % c == 0)
    return pl.pallas_call(
        _norm_matmul_kernel,
        out_shape=jax.ShapeDtypeStruct((t, n), out_dtype),
        grid=(t // tm, n // tn),
        in_specs=[pl.BlockSpec((tm, d), lambda i, j: (i, 0)),
                  pl.BlockSpec((1, d), lambda i, j: (0, 0)),
                  pl.BlockSpec((d, tn), lambda i, j: (0, j))],
        out_specs=pl.BlockSpec((tm, tn), lambda i, j: (i, j)),
        scratch_shapes=[pltpu.VMEM((tm, d), BF16)],
        compiler_params=_cparams(2),
        name="norm_matmul",
    )(x, g.reshape(1, d), w)


def _matmul_res_kernel(a_ref, w_ref, x_ref, o_ref, wb_ref):
    @pl.when(pl.program_id(0) == 0)
    def _():
        wb_ref[...] = w_ref[...].astype(BF16)

    o_ref[...] = x_ref[...] + jnp.dot(a_ref[...], wb_ref[...], preferred_element_type=F32)


def _matmul_res(a, w, x):
    t, k = a.shape
    d = w.shape[1]
    tm = ROW_TILE
    return pl.pallas_call(
        _matmul_res_kernel,
        out_shape=jax.ShapeDtypeStruct((t, d), F32),
        grid=(t // tm,),
        in_specs=[pl.BlockSpec((tm, k), lambda i: (i, 0)),
                  pl.BlockSpec((k, d), lambda i: (0, 0)),
                  pl.BlockSpec((tm, d), lambda i: (i, 0))],
        out_specs=pl.BlockSpec((tm, d), lambda i: (i, 0)),
        scratch_shapes=[pltpu.VMEM((k, d), BF16)],
        compiler_params=_cparams(1),
        name="matmul_res",
    )(a, w, x)


def _silu_mul(g, u):
    return g * (1.0 / (1.0 + jnp.exp(-g))) * u


def _ffn_kernel(x_ref, g_ref, wg_ref, wu_ref, wd_ref, o_ref, hn_ref, acc_ref):
    f = pl.program_id(1)

    @pl.when(f == 0)
    def _():
        hn_ref[...] = _rms_rows(x_ref[...], g_ref[...]).astype(BF16)
        acc_ref[...] = jnp.zeros_like(acc_ref)

    h = hn_ref[...]
    gate = jnp.dot(h, wg_ref[...].astype(BF16), preferred_element_type=F32)
    up = jnp.dot(h, wu_ref[...].astype(BF16), preferred_element_type=F32)
    a = _silu_mul(gate, up).astype(BF16)
    acc_ref[...] += jnp.dot(a, wd_ref[...].astype(BF16), preferred_element_type=F32)

    @pl.when(f == pl.num_programs(1) - 1)
    def _():
        o_ref[...] = x_ref[...] + acc_ref[...]


def _ffn(x, g, w_gate_up, w_down):
    t, d = x.shape
    ff = w_down.shape[0]
    tm, tf = ROW_TILE, FF_TILE
    nf = ff // tf
    return pl.pallas_call(
        _ffn_kernel,
        out_shape=jax.ShapeDtypeStruct((t, d), F32),
        grid=(t // tm, nf),
        in_specs=[pl.BlockSpec((tm, d), lambda i, f: (i, 0)),
                  pl.BlockSpec((1, d), lambda i, f: (0, 0)),
                  pl.BlockSpec((d, tf), lambda i, f: (0, f)),
                  pl.BlockSpec((d, tf), lambda i, f: (0, nf + f)),
                  pl.BlockSpec((tf, d), lambda i, f: (f, 0))],
        out_specs=pl.BlockSpec((tm, d), lambda i, f: (i, 0)),
        scratch_shapes=[pltpu.VMEM((tm, d), BF16), pltpu.VMEM((tm, d), F32)],
        compiler_params=_cparams(2),
        name="ffn",
    )(x, g.reshape(1, d), w_gate_up, w_gate_up, w_down)


POOL_PAD = 16


def _pool_kernel(u_ref, wg_ref, sc_ref, o_ref, pad_ref):
    s, c = u_ref.shape
    grp = pl.program_id(1)
    pad_ref[pl.ds(0, POOL_PAD), :] = jnp.zeros((POOL_PAD, c), F32)
    pad_ref[pl.ds(POOL_PAD + s, POOL_PAD), :] = jnp.zeros((POOL_PAD, c), F32)
    pad_ref[pl.ds(POOL_PAD, s), :] = u_ref[...]
    wb = wg_ref[...].astype(BF16)
    chunk = 256

    for gi, w in enumerate(POOL_WINDOWS):
        @pl.when(grp == gi)
        def _(w=w):
            for c0 in range(0, s, chunk):
                acc = None
                for dlt in range(-(w // 2), w - w // 2):
                    piece = pad_ref[pl.ds(POOL_PAD + c0 + dlt, chunk), :]
                    acc = piece if acc is None else acc + piece
                pos = c0 + lax.broadcasted_iota(I32, (chunk, 1), 0)
                lo = jnp.maximum(pos - w // 2, 0)
                hi = jnp.minimum(pos + (w - w // 2), s)
                cnt = (hi - lo).astype(F32)
                mixed = acc / cnt - u_ref[pl.ds(c0, chunk), :]
                y = jnp.dot(mixed.astype(BF16), wb, preferred_element_type=F32)
                o_ref[pl.ds(c0, chunk), :] = (y * sc_ref[...]).astype(o_ref.dtype)


def _pool_mix(u, w_group, scale, batch, seq):
    d = u.shape[1]
    ng = len(POOL_WINDOWS)
    c = d // ng
    u3 = u.reshape(batch, seq, d)
    out = pl.pallas_call(
        _pool_kernel,
        out_shape=jax.ShapeDtypeStruct((batch, seq, d), BF16),
        grid=(batch, ng),
        in_specs=[pl.BlockSpec((None, seq, c), lambda b, g: (b, 0, g)),
                  pl.BlockSpec((None, c, c), lambda b, g: (g, 0, 0)),
                  pl.BlockSpec((1, c), lambda b, g: (0, g))],
        out_specs=pl.BlockSpec((None, seq, c), lambda b, g: (b, 0, g)),
        scratch_shapes=[pltpu.VMEM((seq + 2 * POOL_PAD, c), F32)],
        compiler_params=_cparams(2),
        name="pool_mix",
    )(u3, w_group, scale.reshape(1, d))
    return out.reshape(batch * seq, d)


def _rel_bucket_np(rel):
    half = N_BUCKETS // 2
    max_exact = half // 2
    n = np.abs(rel)
    ratio = np.log(np.maximum(n, 1).astype(np.float32) / np.float32(max_exact))
    big = max_exact + (ratio / np.float32(math.log(MAX_DISTANCE / max_exact))
                       * np.float32(half - max_exact)).astype(np.int32)
    big = np.minimum(big, half - 1)
    return np.where(rel > 0, half, 0) + np.where(n < max_exact, n, big)


def _seg_table(rel_bias, rel, valid):
    bucket = jnp.asarray(_rel_bucket_np(rel).astype(np.int32))
    vals = jnp.take(rel_bias.astype(F32), bucket, axis=0)
    vals = jnp.where(jnp.asarray(valid)[..., None], vals, MASK_VALUE)
    return jnp.transpose(vals, (2, 0, 1))


def _toeplitz(seg_row, rows, cols):
    w = seg_row.shape[1]
    full = jnp.broadcast_to(seg_row, (rows, w))
    rolled = pltpu.roll(full, w - (rows - 1), 1, stride=1, stride_axis=0)
    return rolled[:, :cols]


def _band_variants(tq, hw, kw, n_tiles):
    if n_tiles == 1:
        return (0,)
    return (0, -hw, -2 * hw)


def _band_segs(rel_bias, tq, kw, hw, half, dil, variants, width):
    c = np.arange(width)
    rel = np.stack([r0 + c - (tq - 1) for r0 in variants])
    valid = (np.abs(rel) <= half) & (c[None, :] < tq + kw - 1)
    return _seg_table(rel_bias, rel * dil, valid)


def _dil_kernel(*refs, n_heads, tq, kw, hw, n_var, first, last):
    if first:
        seg_ref, q_ref, k_ref, v_ref = refs[:4]
        rest = refs[4:]
    else:
        seg_ref, q_ref, k_ref, v_ref, acc_in, m_in, l_in = refs[:7]
        rest = refs[7:]
    if last:
        o_ref, bias_ref = rest
    else:
        acc_out, m_out, l_out, bias_ref = rest

    b, r, qi = pl.program_id(0), pl.program_id(1), pl.program_id(2)
    nq = pl.num_programs(2)
    seq_len = k_ref.shape[0]

    @pl.when((b == 0) & (r == 0) & (qi == 0))
    def _():
        for v in range(n_var):
            for h in range(n_heads):
                bias_ref[v, h] = _toeplitz(seg_ref[h, pl.ds(v, 1), :], tq, kw)

    if n_var == 1:
        var = 0
        ks = 0
    else:
        var = jnp.where(qi == 0, 0, jnp.where(qi == nq - 1, 2, 1))
        ks = pl.multiple_of(jnp.clip(qi * tq - hw, 0, seq_len - kw), hw)

    lane = lax.broadcasted_iota(I32, (tq, LANES), 1)
    outs = []
    m_blk = jnp.zeros((tq, LANES), F32)
    l_blk = jnp.zeros((tq, LANES), F32)
    for h in range(n_heads):
        cs = slice(h * HEAD_DIM, (h + 1) * HEAD_DIM)
        qh = q_ref[:, cs] * Q_SCALE
        kh = k_ref[pl.ds(ks, kw), cs]
        vh = v_ref[pl.ds(ks, kw), cs]
        s = lax.dot_general(qh, kh, (((1,), (1,)), ((), ())), preferred_element_type=F32)
        s = s + bias_ref[var, h]
        m_c = jnp.max(s, axis=-1, keepdims=True)
        p = jnp.exp(s - m_c)
        l_c = jnp.sum(p, axis=-1, keepdims=True)
        o_c = jnp.dot(p.astype(BF16), vh, preferred_element_type=F32)
        if first:
            m_n, l_n, acc_n = m_c, l_c, o_c
        else:
            m_o = m_in[:, h:h + 1]
            l_o = l_in[:, h:h + 1]
            m_n = jnp.maximum(m_o, m_c)
            a_o = jnp.exp(m_o - m_n)
            a_c = jnp.exp(m_c - m_n)
            l_n = a_o * l_o + a_c * l_c
            acc_n = a_o * acc_in[:, cs] + a_c * o_c
        if last:
            outs.append(acc_n / l_n)
        else:
            outs.append(acc_n)
            m_blk = jnp.where(lane == h, m_n, m_blk)
            l_blk = jnp.where(lane == h, l_n, l_blk)
    full = jnp.concatenate(outs, axis=1)
    if last:
        o_ref[...] = full.astype(o_ref.dtype)
    else:
        acc_out[...] = full
        m_out[...] = m_blk
        l_out[...] = l_blk


def _dil_group(qkv, rel_bias, state, g, batch, seq, n_groups):
    win, dil = DIL_PAIRS[g]
    half = win // (2 * dil)
    t, ncol = qkv.shape
    d = ncol // (3 * n_groups)
    n_heads = d // HEAD_DIM
    ln = seq // dil
    tq = min(ATT_TQ, ln)
    nq = ln // tq
    hw = half
    kw = min(tq + 2 * hw, ln)
    variants = _band_variants(tq, hw, kw, nq)
    n_var = len(variants)
    width = _next_pow2(tq + kw - 1)
    seg = _band_segs(rel_bias, tq, kw, hw, half, dil, variants, width)
    first, last = g == 0, g == n_groups - 1
    cpb = ncol // d

    qkv_v = qkv.reshape(batch, ln, dil * ncol)
    in_specs = [pl.BlockSpec((n_heads, n_var, width), lambda b, r, qi: (0, 0, 0)),
                pl.BlockSpec((None, tq, d), lambda b, r, qi: (b, qi, r * cpb + g * 3)),
                pl.BlockSpec((None, ln, d), lambda b, r, qi: (b, 0, r * cpb + g * 3 + 1)),
                pl.BlockSpec((None, ln, d), lambda b, r, qi: (b, 0, r * cpb + g * 3 + 2))]
    args = [seg, qkv_v, qkv_v, qkv_v]
    acc_spec = pl.BlockSpec((None, tq, d), lambda b, r, qi: (b, qi, r))
    ml_spec = pl.BlockSpec((None, tq, LANES), lambda b, r, qi: (b, qi, r))
    if not first:
        acc, m, l = state
        in_specs += [acc_spec, ml_spec, ml_spec]
        args += [acc.reshape(batch, ln, dil * d), m.reshape(batch, ln, dil * LANES),
                 l.reshape(batch, ln, dil * LANES)]
    if last:
        out_shape = jax.ShapeDtypeStruct((batch, ln, dil * d), BF16)
        out_specs = acc_spec
    else:
        out_shape = (jax.ShapeDtypeStruct((batch, ln, dil * d), F32),
                     jax.ShapeDtypeStruct((batch, ln, dil * LANES), F32),
                     jax.ShapeDtypeStruct((batch, ln, dil * LANES), F32))
        out_specs = (acc_spec, ml_spec, ml_spec)
    res = pl.pallas_call(
        functools.partial(_dil_kernel, n_heads=n_heads, tq=tq, kw=kw, hw=hw, n_var=n_var,
                          first=first, last=last),
        out_shape=out_shape,
        grid=(batch, dil, nq),
        in_specs=in_specs,
        out_specs=out_specs,
        scratch_shapes=[pltpu.VMEM((n_var, n_heads, tq, kw), F32)],
        compiler_params=_cparams(3),
        name=f"dil_attn_g{g}",
    )(*args)
    if last:
        return res.reshape(t, d)
    acc, m, l = res
    return acc.reshape(t, d), m.reshape(t, LANES), l.reshape(t, LANES)


def _dilated_attention(qkv, rel_bias, batch, seq):
    n_groups = len(DIL_PAIRS)
    state = None
    for g in range(n_groups):
        state = _dil_group(qkv, rel_bias, state, g, batch, seq, n_groups)
    return state


def _gqa_kernel(sink_ref, seg_ref, q_ref, k_ref, v_ref, o_ref, bias_ref, *, tq, kw, hw, grp):
    b, qi = pl.program_id(0), pl.program_id(1)
    nq = pl.num_programs(1)
    seq_len = k_ref.shape[0]
    n_kv = k_ref.shape[1] // HEAD_DIM

    @pl.when((b == 0) & (qi == 0))
    def _():
        for v in range(3):
            for kh in range(n_kv):
                for gq in range(grp):
                    bias_ref[v, kh, pl.ds(gq * tq, tq), :] = _toeplitz(
                        seg_ref[kh * grp + gq, pl.ds(v, 1), :], tq, kw)

    var = jnp.where(qi == 0, 0, jnp.where(qi == nq - 1, 2, 1))
    ks = pl.multiple_of(jnp.clip(qi * tq - hw, 0, seq_len - kw), hw)
    row = lax.broadcasted_iota(I32, (grp * tq, 1), 0)
    outs = [None] * (n_kv * grp)
    for kh in range(n_kv):
        cs = slice(kh * HEAD_DIM, (kh + 1) * HEAD_DIM)
        qs = jnp.concatenate(
            [q_ref[:, (kh * grp + gq) * HEAD_DIM:(kh * grp + gq + 1) * HEAD_DIM] for gq in range(grp)],
            axis=0) * Q_SCALE
        kk = k_ref[pl.ds(ks, kw), cs]
        vv = v_ref[pl.ds(ks, kw), cs]
        s = lax.dot_general(qs, kk, (((1,), (1,)), ((), ())), preferred_element_type=F32)
        s = s + bias_ref[var, kh]
        sk = jnp.zeros((grp * tq, 1), F32)
        for gq in range(grp):
            sk = jnp.where((row >= gq * tq) & (row < (gq + 1) * tq), sink_ref[kh * grp + gq], sk)
        m = jnp.maximum(jnp.max(s, axis=-1, keepdims=True), sk)
        e = jnp.exp(s - m)
        den = jnp.sum(e, axis=-1, keepdims=True) + jnp.exp(sk - m)
        o = jnp.dot(e.astype(BF16), vv, preferred_element_type=F32) / den
        for gq in range(grp):
            outs[kh * grp + gq] = o[gq * tq:(gq + 1) * tq, :]
    o_ref[...] = jnp.concatenate(outs, axis=1).astype(o_ref.dtype)


def _gqa_attention(qkv, sink, rel_bias, batch, seq):
    t, ncol = qkv.shape
    n_q = sink.shape[0]
    d = n_q * HEAD_DIM
    kvw = GQA_KV_HEADS * HEAD_DIM
    grp = n_q // GQA_KV_HEADS
    tq = hw = GQA_WINDOW
    kw = 3 * GQA_WINDOW
    nq = seq // tq
    variants = _band_variants(tq, hw, kw, nq)
    width = _next_pow2(tq + kw - 1)
    seg = _band_segs(rel_bias, tq, kw, hw, GQA_WINDOW, 1, variants, width)
    qkv_v = qkv.reshape(batch, seq, ncol)
    out = pl.pallas_call(
        functools.partial(_gqa_kernel, tq=tq, kw=kw, hw=hw, grp=grp),
        out_shape=jax.ShapeDtypeStruct((batch, seq, d), BF16),
        grid=(batch, nq),
        in_specs=[pl.BlockSpec(memory_space=pltpu.SMEM),
                  pl.BlockSpec((n_q, 3, width), lambda b, qi: (0, 0, 0)),
                  pl.BlockSpec((None, tq, d), lambda b, qi: (b, qi, 0)),
                  pl.BlockSpec((None, seq, kvw), lambda b, qi: (b, 0, d // kvw)),
                  pl.BlockSpec((None, seq, kvw), lambda b, qi: (b, 0, d // kvw + 1))],
        out_specs=pl.BlockSpec((None, tq, d), lambda b, qi: (b, qi, 0)),
        scratch_shapes=[pltpu.VMEM((3, GQA_KV_HEADS, grp * tq, kw), F32)],
        compiler_params=_cparams(2),
        name="gqa_attn",
    )(sink.astype(F32), seg, qkv_v, qkv_v, qkv_v)
    return out.reshape(t, d)


def _diff_kernel(seg_ref, lam_ref, sub_ref, q_ref, k_ref, v_ref, o_ref, bias_ref, *, tq, lam_init):
    h, qi, b = pl.program_id(0), pl.program_id(1), pl.program_id(2)
    seq_len = k_ref.shape[0]
    nk = seq_len // tq

    @pl.when(b == 0)
    def _():
        for j in range(2):
            for ki in range(nk):
                row = seg_ref[h * 2 + j, pl.ds(ki - qi + nk - 1, 1), :]
                bias_ref[j, :, pl.ds(ki * tq, tq)] = _toeplitz(row, tq, tq)

    lv = lam_ref[...]
    s01 = jnp.sum(lv[0:1, :] * lv[1:2, :], axis=-1, keepdims=True)
    s23 = jnp.sum(lv[2:3, :] * lv[3:4, :], axis=-1, keepdims=True)
    lam = jnp.exp(s01) - jnp.exp(s23) + lam_init

    probs = []
    for j in range(2):
        cs = slice(j * HEAD_DIM, (j + 1) * HEAD_DIM)
        qj = q_ref[:, cs] * Q_SCALE
        kj = k_ref[:, cs]
        s = lax.dot_general(qj, kj, (((1,), (1,)), ((), ())), preferred_element_type=F32)
        s = s + bias_ref[j]
        m = jnp.max(s, axis=-1, keepdims=True)
        p = jnp.exp(s - m)
        l = jnp.sum(p, axis=-1, keepdims=True)
        probs.append((p, l))
    (p0, l0), (p1, l1) = probs
    a = p0 * (1.0 / l0) - p1 * (lam / l1)
    o = jnp.dot(a.astype(BF16), v_ref[...], preferred_element_type=F32)
    o = _rms_rows(o, sub_ref[...]) * (1.0 - lam_init)
    o_ref[...] = o.astype(o_ref.dtype)


def _diff_attention(qkv, lam_vecs, subln, rel_bias, batch, seq):
    t, ncol = qkv.shape
    d = ncol // 3
    hd2 = 2 * HEAD_DIM
    n_heads = d // hd2
    tq = DIFF_TQ
    nk = seq // tq
    width = _next_pow2(2 * tq - 1)
    c = np.arange(width)
    rel = np.stack([(dl - (nk - 1)) * tq + c - (tq - 1) for dl in range(2 * nk - 1)])
    seg = _seg_table(rel_bias, rel, np.ones_like(rel, dtype=bool))
    lam_init = 0.8 - 0.6 * math.exp(-0.3 * DIFF_LAYER)
    qkv_v = qkv.reshape(batch, seq, ncol)
    out = pl.pallas_call(
        functools.partial(_diff_kernel, tq=tq, lam_init=lam_init),
        out_shape=jax.ShapeDtypeStruct((batch, seq, d), BF16),
        grid=(n_heads, seq // tq, batch),
        in_specs=[pl.BlockSpec(seg.shape, lambda h, qi, b: (0, 0, 0)),
                  pl.BlockSpec(lam_vecs.shape, lambda h, qi, b: (0, 0)),
                  pl.BlockSpec((1, hd2), lambda h, qi, b: (0, 0)),
                  pl.BlockSpec((None, tq, hd2), lambda h, qi, b: (b, qi, h)),
                  pl.BlockSpec((None, seq, hd2), lambda h, qi, b: (b, 0, n_heads + h)),
                  pl.BlockSpec((None, seq, hd2), lambda h, qi, b: (b, 0, 2 * n_heads + h))],
        out_specs=pl.BlockSpec((None, tq, hd2), lambda h, qi, b: (b, qi, h)),
        scratch_shapes=[pltpu.VMEM((2, tq, seq), F32)],
        compiler_params=_cparams(3),
        name="diff_attn",
    )(seg, lam_vecs.astype(F32), subln.reshape(1, hd2).astype(F32), qkv_v, qkv_v, qkv_v)
    return out.reshape(t, d)


def _pack_bf16_pair(lo, hi):
    lo_b = lax.bitcast_convert_type(lo.astype(BF16).astype(F32), U32)
    hi_b = lax.bitcast_convert_type(hi.astype(BF16).astype(F32), U32)
    return (lo_b >> 16) | (hi_b & jnp.uint32(0xFFFF0000))


def _unpack_bf16_pair(w):
    lo = lax.bitcast_convert_type(w << 16, F32)
    hi = lax.bitcast_convert_type(w & jnp.uint32(0xFFFF0000), F32)
    return lo, hi


def _router_kernel(x_ref, g_ref, r_ref, hp_ref, mi_ref, mf_ref, cnt_ref, tri_ref, carry_ref):
    i = pl.program_id(0)
    tm, d = x_ref.shape

    @pl.when(i == 0)
    def _():
        rr = lax.broadcasted_iota(I32, (tm, tm), 0)
        cc = lax.broadcasted_iota(I32, (tm, tm), 1)
        tri_ref[...] = (cc < rr).astype(BF16)
        carry_ref[...] = jnp.zeros_like(carry_ref)

    hn = _rms_rows(x_ref[...], g_ref[...])
    hp_ref[...] = _pack_bf16_pair(hn[:, :d // 2], hn[:, d // 2:])
    logits = jnp.dot(hn, r_ref[...], preferred_element_type=F32, precision=lax.Precision.HIGHEST)
    lane = lax.broadcasted_iota(I32, (tm, LANES), 1)
    logits = jnp.where(lane < N_EXPERTS, logits, -jnp.inf)
    v1 = jnp.max(logits, axis=-1, keepdims=True)
    i1 = jnp.min(jnp.where(logits == v1, lane, LANES), axis=-1, keepdims=True)
    oh1 = lane == i1
    rest = jnp.where(oh1, -jnp.inf, logits)
    v2 = jnp.max(rest, axis=-1, keepdims=True)
    i2 = jnp.min(jnp.where(rest == v2, lane, LANES), axis=-1, keepdims=True)
    oh2 = lane == i2
    e2 = jnp.exp(v2 - v1)
    g1 = 1.0 / (1.0 + e2)
    g2 = e2 / (1.0 + e2)

    sel = (oh1 | oh2)
    before = jnp.dot(tri_ref[...], sel.astype(BF16), preferred_element_type=F32) + carry_ref[...]
    rank1 = jnp.sum(jnp.where(oh1, before, 0.0), axis=-1, keepdims=True).astype(I32)
    rank2 = jnp.sum(jnp.where(oh2, before, 0.0), axis=-1, keepdims=True).astype(I32)
    carry_ref[...] += jnp.sum(sel.astype(F32), axis=0, keepdims=True)

    mi_ref[...] = jnp.where(lane == 0, i1, jnp.where(lane == 1, i2,
                            jnp.where(lane == 2, rank1, jnp.where(lane == 3, rank2, 0))))
    mf_ref[...] = jnp.where(lane == 0, g1, jnp.where(lane == 1, g2, 0.0))
    cnt_ref[...] = carry_ref[...].astype(I32)


def _router(x, g, router):
    t, d = x.shape
    tm = ROW_TILE
    r_pad = jnp.zeros((d, LANES), F32).at[:, :N_EXPERTS].set(router.astype(F32))
    return pl.pallas_call(
        _router_kernel,
        out_shape=(jax.ShapeDtypeStruct((t, d // 2), U32),
                   jax.ShapeDtypeStruct((t, LANES), I32),
                   jax.ShapeDtypeStruct((t, LANES), F32),
                   jax.ShapeDtypeStruct((1, LANES), I32)),
        grid=(t // tm,),
        in_specs=[pl.BlockSpec((tm, d), lambda i: (i, 0)),
                  pl.BlockSpec((1, d), lambda i: (0, 0)),
                  pl.BlockSpec((d, LANES), lambda i: (0, 0))],
        out_specs=(pl.BlockSpec((tm, d // 2), lambda i: (i, 0)),
                   pl.BlockSpec((tm, LANES), lambda i: (i, 0)),
                   pl.BlockSpec((tm, LANES), lambda i: (i, 0)),
                   pl.BlockSpec((1, LANES), lambda i: (0, 0))),
        scratch_shapes=[pltpu.VMEM((tm, tm), BF16), pltpu.VMEM((1, LANES), F32)],
        compiler_params=_cparams(1),
        name="moe_router",
    )(x, g.reshape(1, d), r_pad)


def _row_copy(src, s_row, dst, d_row, sem):
    return pltpu.make_async_copy(src.at[pl.ds(s_row, 1)], dst.at[pl.ds(d_row, 1)], sem)


def _dispatch_kernel(tail_ref, pos_ref, hp_ref, xs_ref, zero_ref, sem):
    i = pl.program_id(0)
    tm = hp_ref.shape[0]
    tg = zero_ref.shape[0]

    @pl.when(i == 0)
    def _():
        zero_ref[...] = jnp.zeros_like(zero_ref)
        for e in range(N_EXPERTS):
            tail = pl.multiple_of(tail_ref[e], SUBLANES)
            pltpu.make_async_copy(zero_ref, xs_ref.at[pl.ds(tail, tg)], sem.at[0]).start()
        for e in range(N_EXPERTS):
            tail = pl.multiple_of(tail_ref[e], SUBLANES)
            pltpu.make_async_copy(zero_ref, xs_ref.at[pl.ds(tail, tg)], sem.at[0]).wait()

    def issue(r, c):
        for k in range(2):
            _row_copy(hp_ref, r, xs_ref, pos_ref[0, k, r], sem.at[0]).start()
        return c

    lax.fori_loop(0, tm, issue, 0)

    def drain(r, c):
        for k in range(2):
            _row_copy(hp_ref, 0, xs_ref, 0, sem.at[0]).wait()
        return c

    lax.fori_loop(0, tm, drain, 0)


def _dispatch(hp, pos, tails, n_rows):
    t, dw = hp.shape
    tm = DMA_TILE
    return pl.pallas_call(
        _dispatch_kernel,
        out_shape=jax.ShapeDtypeStruct((n_rows, dw), U32),
        grid_spec=pltpu.PrefetchScalarGridSpec(
            num_scalar_prefetch=1,
            grid=(t // tm,),
            in_specs=[pl.BlockSpec((1, 2, tm), lambda i, tl: (i, 0, 0), memory_space=pltpu.SMEM),
                      pl.BlockSpec((tm, dw), lambda i, tl: (i, 0))],
            out_specs=pl.BlockSpec(memory_space=pl.ANY),
            scratch_shapes=[pltpu.VMEM((MOE_TILE, dw), U32), pltpu.SemaphoreType.DMA((1,))]),
        compiler_params=pltpu.CompilerParams(dimension_semantics=("arbitrary",),
                                             vmem_limit_bytes=VMEM_LIMIT, has_side_effects=True),
        name="moe_dispatch",
    )(tails, pos, hp)


def _expert_kernel(te_ref, tb_ref, nv_ref, xs_ref, wg_ref, wu_ref, wd_ref, ys_ref, hn_ref, acc_ref):
    i, f = pl.program_id(0), pl.program_id(1)
    half = xs_ref.shape[1]

    @pl.when(i < nv_ref[0])
    def _():
        @pl.when(f == 0)
        def _():
            lo, hi = _unpack_bf16_pair(xs_ref[...])
            hn_ref[:, :half] = lo.astype(BF16)
            hn_ref[:, half:] = hi.astype(BF16)
            acc_ref[...] = jnp.zeros_like(acc_ref)

        h = hn_ref[...]
        gate = jnp.dot(h, wg_ref[...].astype(BF16), preferred_element_type=F32)
        up = jnp.dot(h, wu_ref[...].astype(BF16), preferred_element_type=F32)
        a = _silu_mul(gate, up).astype(BF16)
        acc_ref[...] += jnp.dot(a, wd_ref[...].astype(BF16), preferred_element_type=F32)

        @pl.when(f == pl.num_programs(1) - 1)
        def _():
            acc = acc_ref[...]
            ys_ref[...] = _pack_bf16_pair(acc[:, :half], acc[:, half:])


def _experts(xs, w_gate_up, w_down, tile_expert, tile_block, n_valid, n_tiles):
    n_rows, dw = xs.shape
    d = 2 * dw
    ff = w_down.shape[1]
    tg, tf = MOE_TILE, FF_TILE
    nf = ff // tf

    def fidx(i, f, nv):
        return jnp.where(i < nv[0], f, nf - 1)

    return pl.pallas_call(
        _expert_kernel,
        out_shape=jax.ShapeDtypeStruct((n_rows, dw), U32),
        grid_spec=pltpu.PrefetchScalarGridSpec(
            num_scalar_prefetch=3,
            grid=(n_tiles, nf),
            in_specs=[pl.BlockSpec((tg, dw), lambda i, f, te, tb, nv: (tb[i], 0)),
                      pl.BlockSpec((None, d, tf), lambda i, f, te, tb, nv: (te[i], 0, fidx(i, f, nv))),
                      pl.BlockSpec((None, d, tf), lambda i, f, te, tb, nv: (te[i], 0, nf + fidx(i, f, nv))),
                      pl.BlockSpec((None, tf, d), lambda i, f, te, tb, nv: (te[i], fidx(i, f, nv), 0))],
            out_specs=pl.BlockSpec((tg, dw), lambda i, f, te, tb, nv: (tb[i], 0)),
            scratch_shapes=[pltpu.VMEM((tg, d), BF16), pltpu.VMEM((tg, d), F32)]),
        compiler_params=_cparams(2),
        name="moe_experts",
    )(tile_expert, tile_block, n_valid, xs, w_gate_up, w_gate_up, w_down)


def _combine_kernel(pos_ref, x_ref, mf_ref, gf_ref, ys_ref, o_ref, buf_ref, sem, *, final_norm):
    tm = x_ref.shape[0]
    half = buf_ref.shape[2]

    def issue(r, c):
        for k in range(2):
            pltpu.make_async_copy(ys_ref.at[pl.ds(pos_ref[0, k, r], 1)],
                                  buf_ref.at[k, pl.ds(r, 1)], sem.at[0]).start()
        return c

    lax.fori_loop(0, tm, issue, 0)

    def drain(r, c):
        for k in range(2):
            pltpu.make_async_copy(ys_ref.at[pl.ds(0, 1)], buf_ref.at[k, pl.ds(0, 1)], sem.at[0]).wait()
        return c

    lax.fori_loop(0, tm, drain, 0)

    gates = mf_ref[...]
    out_lo = x_ref[:, :half]
    out_hi = x_ref[:, half:]
    for k in range(2):
        lo, hi = _unpack_bf16_pair(buf_ref[k])
        gk = gates[:, k:k + 1]
        out_lo = out_lo + gk * lo
        out_hi = out_hi + gk * hi
    out = jnp.concatenate([out_lo, out_hi], axis=1)
    if final_norm:
        out = _rms_rows(out, gf_ref[...])
    o_ref[...] = out


def _combine(x, mf, ys, pos, final_g):
    t, d = x.shape
    tm = DMA_TILE
    final_norm = final_g is not None
    gf = (final_g if final_norm else jnp.ones((d,), F32)).reshape(1, d).astype(F32)
    return pl.pallas_call(
        functools.partial(_combine_kernel, final_norm=final_norm),
        out_shape=jax.ShapeDtypeStruct((t, d), F32),
        grid=(t // tm,),
        in_specs=[pl.BlockSpec((1, 2, tm), lambda i: (i, 0, 0), memory_space=pltpu.SMEM),
                  pl.BlockSpec((tm, d), lambda i: (i, 0)),
                  pl.BlockSpec((tm, LANES), lambda i: (i, 0)),
                  pl.BlockSpec((1, d), lambda i: (0, 0)),
                  pl.BlockSpec(memory_space=pl.ANY)],
        out_specs=pl.BlockSpec((tm, d), lambda i: (i, 0)),
        scratch_shapes=[pltpu.VMEM((2, tm, d // 2), U32), pltpu.SemaphoreType.DMA((1,))],
        compiler_params=_cparams(1),
        name="moe_combine",
    )(pos, x, mf, gf, ys)


def _moe(x, g, router, w_gate_up, w_down, final_g=None):
    t, d = x.shape
    tg = MOE_TILE
    hp, mi, mf, cnt = _router(x, g, router)

    counts = cnt[0, :N_EXPERTS]
    tiles_e = (counts + tg - 1) // tg
    tile_start = jnp.cumsum(tiles_e) - tiles_e
    row_start = tile_start * tg
    n_valid = jnp.sum(tiles_e).astype(I32)
    n_tiles = 2 * t // tg + N_EXPERTS
    n_rows = (n_tiles + 1) * tg
    tidx = jnp.minimum(jnp.arange(n_tiles, dtype=I32), n_valid - 1)
    tile_expert = (jnp.sum(tidx[:, None] >= tile_start[None, :], axis=1) - 1).astype(I32)
    tails = (row_start + counts // SUBLANES * SUBLANES).astype(I32)
    experts = mi[:, 0:2]
    pos = (row_start[experts] + mi[:, 2:4]).astype(I32)
    pos = pos.T.reshape(2, t // DMA_TILE, DMA_TILE).transpose(1, 0, 2)

    xs = _dispatch(hp, pos, tails, n_rows)
    ys = _experts(xs, w_gate_up, w_down, tile_expert, tidx, n_valid.reshape(1), n_tiles)
    return _combine(x, mf, ys, pos, final_g)


def kernel(x, rel_bias, norm_mix, norm_ffn, norm_final, a_w_in, a_w_group, a_scale, a_w_out,
           b_w_in, b_w_out, c_w_in, c_lambda, c_subln, c_w_out, d_w_in, d_sink, d_w_out,
           f_w_gate_up, f_w_down, m_router, m_w_gate_up, m_w_down):
    batch, seq, d = x.shape
    h = x.reshape(batch * seq, d)

    u = _norm_matmul(h, norm_mix[0], a_w_in[0], F32)
    y = _pool_mix(u, a_w_group[0], a_scale[0], batch, seq)
    h = _matmul_res(y, a_w_out[0], h)
    h = _ffn(h, norm_ffn[0], f_w_gate_up[0], f_w_down[0])

    qkv = _norm_matmul(h, norm_mix[1], b_w_in[0], BF16)
    o = _dilated_attention(qkv, rel_bias, batch, seq)
    h = _matmul_res(o, b_w_out[0], h)
    h = _moe(h, norm_ffn[1], m_router[0], m_w_gate_up[0], m_w_down[0])

    qkv = _norm_matmul(h, norm_mix[2], c_w_in[0], BF16)
    o = _diff_attention(qkv, c_lambda[0], c_subln[0], rel_bias, batch, seq)
    h = _matmul_res(o, c_w_out[0], h)
    h = _ffn(h, norm_ffn[2], f_w_gate_up[1], f_w_down[1])

    qkv = _norm_matmul(h, norm_mix[3], d_w_in[0], BF16)
    o = _gqa_attention(qkv, d_sink[0], rel_bias, batch, seq)
    h = _matmul_res(o, d_w_out[0], h)
    h = _moe(h, norm_ffn[3], m_router[1], m_w_gate_up[1], m_w_down[1], final_g=norm_final)
    return h.reshape(batch, seq, d)
```

```python
import collections
import functools
import math

import numpy as np
import jax
import jax.numpy as jnp
from jax import lax
from jax.experimental import pallas as pl
from jax.experimental.pallas import tpu as pltpu

F32 = jnp.float32
BF16 = jnp.bfloat16
I32 = jnp.int32
U32 = jnp.uint32

RMS_EPS = 1e-6
HEAD_DIM = 64
N_BUCKETS = 32
MAX_DISTANCE = 1024
POOL_WINDOWS = (2, 4, 8, 16)
DIL_PAIRS = ((128, 1), (512, 4), (2048, 16))
Q_SCALE = HEAD_DIM ** -0.5
GQA_KV_HEADS = 4
GQA_WINDOW = 128
N_EXPERTS = 8
DIFF_LAYER = 2

LANES = 128
SUBLANES = 8
VMEM_LIMIT = 56 * 1024 * 1024
ROW_TILE = 1024
FF_TILE = 512
MOE_TILE = 1024
DMA_TILE = 512
ATT_TQ = 128
DIL_UNITS = 2
DIL_HEAD_SPLIT = 2
DIFF_TQ = 512
MASK_VALUE = -1e30


def _cparams(n_axes, vmem=VMEM_LIMIT):
    return pltpu.CompilerParams(dimension_semantics=("arbitrary",) * n_axes,
                                vmem_limit_bytes=vmem)


def _next_pow2(n):
    return 1 << (n - 1).bit_length()


def _rms_rows(x, g):
    ms = jnp.mean(x * x, axis=-1, keepdims=True)
    return x * lax.rsqrt(ms + RMS_EPS) * g


def _norm_matmul_kernel(x_ref, g_ref, w_ref, o_ref, hn_ref):
    @pl.when(pl.program_id(1) == 0)
    def _():
        hn_ref[...] = _rms_rows(x_ref[...], g_ref[...]).astype(BF16)

    o_ref[...] = jnp.dot(hn_ref[...], w_ref[...].astype(BF16),
                         preferred_element_type=F32).astype(o_ref.dtype)


def _norm_matmul(x, g, w, slot, out_dtype):
    t, d = x.shape
    n = w.shape[-1]
    tm = ROW_TILE
    tn = next(c for c in (1024, 768, 512, 256, 128) if n % c == 0)
    return pl.pallas_call(
        _norm_matmul_kernel,
        out_shape=jax.ShapeDtypeStruct((t, n), out_dtype),
        grid=(t // tm, n // tn),
        in_specs=[pl.BlockSpec((tm, d), lambda i, j: (i, 0)),
                  pl.BlockSpec((1, d), lambda i, j: (0, 0)),
                  pl.BlockSpec((None, d, tn), lambda i, j: (slot, 0, j))],
        out_specs=pl.BlockSpec((tm, tn), lambda i, j: (i, j)),
        scratch_shapes=[pltpu.VMEM((tm, d), BF16)],
        compiler_params=_cparams(2),
        name="norm_matmul",
    )(x, g.reshape(1, d), w)


def _matmul_res_kernel(a_ref, w_ref, x_ref, o_ref, wb_ref):
    @pl.when(pl.program_id(0) == 0)
    def _():
        wb_ref[...] = w_ref[...].astype(BF16)

    o_ref[...] = x_ref[...] + jnp.dot(a_ref[...], wb_ref[...], preferred_element_type=F32)


def _matmul_res(a, w, slot, x):
    t, k = a.shape
    d = w.shape[-1]
    tm = ROW_TILE
    return pl.pallas_call(
        _matmul_res_kernel,
        out_shape=jax.ShapeDtypeStruct((t, d), F32),
        grid=(t // tm,),
        in_specs=[pl.BlockSpec((tm, k), lambda i: (i, 0)),
                  pl.BlockSpec((None, k, d), lambda i: (slot, 0, 0)),
                  pl.BlockSpec((tm, d), lambda i: (i, 0))],
        out_specs=pl.BlockSpec((tm, d), lambda i: (i, 0)),
        scratch_shapes=[pltpu.VMEM((k, d), BF16)],
        compiler_params=_cparams(1),
        name="matmul_res",
    )(a, w, x)


def _silu_mul(g, u):
    return g * (1.0 / (1.0 + jnp.exp(-g))) * u


def _ffn_kernel(x_ref, g_ref, wg_ref, wu_ref, wd_ref, o_ref, hn_ref, acc_ref):
    f = pl.program_id(1)

    @pl.when(f == 0)
    def _():
        hn_ref[...] = _rms_rows(x_ref[...], g_ref[...]).astype(BF16)
        acc_ref[...] = jnp.zeros_like(acc_ref)

    h = hn_ref[...]
    gate = jnp.dot(h, wg_ref[...].astype(BF16), preferred_element_type=F32)
    up = jnp.dot(h, wu_ref[...].astype(BF16), preferred_element_type=F32)
    a = _silu_mul(gate, up).astype(BF16)
    acc_ref[...] += jnp.dot(a, wd_ref[...].astype(BF16), preferred_element_type=F32)

    @pl.when(f == pl.num_programs(1) - 1)
    def _():
        o_ref[...] = x_ref[...] + acc_ref[...]


def _ffn(x, g, w_gate_up, w_down, slot):
    t, d = x.shape
    ff = w_down.shape[-2]
    tm, tf = ROW_TILE, FF_TILE
    nf = ff // tf
    return pl.pallas_call(
        _ffn_kernel,
        out_shape=jax.ShapeDtypeStruct((t, d), F32),
        grid=(t // tm, nf),
        in_specs=[pl.BlockSpec((tm, d), lambda i, f: (i, 0)),
                  pl.BlockSpec((1, d), lambda i, f: (0, 0)),
                  pl.BlockSpec((None, d, tf), lambda i, f: (slot, 0, f)),
                  pl.BlockSpec((None, d, tf), lambda i, f: (slot, 0, nf + f)),
                  pl.BlockSpec((None, tf, d), lambda i, f: (slot, f, 0))],
        out_specs=pl.BlockSpec((tm, d), lambda i, f: (i, 0)),
        scratch_shapes=[pltpu.VMEM((tm, d), BF16), pltpu.VMEM((tm, d), F32)],
        compiler_params=_cparams(2),
        name="ffn",
    )(x, g.reshape(1, d), w_gate_up, w_gate_up, w_down)


POOL_PAD = 16


def _pool_kernel(u_ref, wg_ref, sc_ref, o_ref, pad_ref):
    s, c = u_ref.shape
    grp = pl.program_id(1)
    pad_ref[pl.ds(0, POOL_PAD), :] = jnp.zeros((POOL_PAD, c), F32)
    pad_ref[pl.ds(POOL_PAD + s, POOL_PAD), :] = jnp.zeros((POOL_PAD, c), F32)
    pad_ref[pl.ds(POOL_PAD, s), :] = u_ref[...]
    wb = wg_ref[...].astype(BF16)
    chunk = 256

    for gi, w in enumerate(POOL_WINDOWS):
        @pl.when(grp == gi)
        def _(w=w):
            for c0 in range(0, s, chunk):
                acc = None
                for dlt in range(-(w // 2), w - w // 2):
                    piece = pad_ref[pl.ds(POOL_PAD + c0 + dlt, chunk), :]
                    acc = piece if acc is None else acc + piece
                pos = c0 + lax.broadcasted_iota(I32, (chunk, 1), 0)
                lo = jnp.maximum(pos - w // 2, 0)
                hi = jnp.minimum(pos + (w - w // 2), s)
                cnt = (hi - lo).astype(F32)
                mixed = acc / cnt - u_ref[pl.ds(c0, chunk), :]
                y = jnp.dot(mixed.astype(BF16), wb, preferred_element_type=F32)
                o_ref[pl.ds(c0, chunk), :] = (y * sc_ref[...]).astype(o_ref.dtype)


def _pool_mix(u, w_group, scale, slot, batch, seq):
    d = u.shape[1]
    ng = len(POOL_WINDOWS)
    c = d // ng
    u3 = u.reshape(batch, seq, d)
    out = pl.pallas_call(
        _pool_kernel,
        out_shape=jax.ShapeDtypeStruct((batch, seq, d), BF16),
        grid=(batch, ng),
        in_specs=[pl.BlockSpec((None, seq, c), lambda b, g: (b, 0, g)),
                  pl.BlockSpec((None, None, c, c), lambda b, g: (slot, g, 0, 0)),
                  pl.BlockSpec((1, c), lambda b, g: (slot, g))],
        out_specs=pl.BlockSpec((None, seq, c), lambda b, g: (b, 0, g)),
        scratch_shapes=[pltpu.VMEM((seq + 2 * POOL_PAD, c), F32)],
        compiler_params=_cparams(2),
        name="pool_mix",
    )(u3, w_group, scale)
    return out.reshape(batch * seq, d)


def _rel_bucket_np(rel):
    half = N_BUCKETS // 2
    max_exact = half // 2
    n = np.abs(rel)
    ratio = np.log(np.maximum(n, 1).astype(np.float32) / np.float32(max_exact))
    big = max_exact + (ratio / np.float32(math.log(MAX_DISTANCE / max_exact))
                       * np.float32(half - max_exact)).astype(np.int32)
    big = np.minimum(big, half - 1)
    return np.where(rel > 0, half, 0) + np.where(n < max_exact, n, big)


def _seg_table(rel_bias, rel, valid):
    bucket = jnp.asarray(_rel_bucket_np(rel).astype(np.int32))
    vals = jnp.take(rel_bias.astype(F32), bucket, axis=0)
    vals = jnp.where(jnp.asarray(valid)[..., None], vals, MASK_VALUE)
    return jnp.transpose(vals, (2, 0, 1))


def _toeplitz(seg_row, rows, cols):
    w = seg_row.shape[1]
    full = jnp.broadcast_to(seg_row, (rows, w))
    rolled = pltpu.roll(full, w - (rows - 1), 1, stride=1, stride_axis=0)
    return rolled[:, :cols]


def _band_variants(hw, n_tiles):
    if n_tiles == 1:
        return (0,)
    return (0, -hw, -2 * hw)


def _band_segs(rel_bias, tq, kw, half, dil, variants, width):
    c = np.arange(width)
    rel = np.stack([r0 + c - (tq - 1) for r0 in variants])
    valid = (np.abs(rel) <= half) & (c[None, :] < tq + kw - 1)
    return _seg_table(rel_bias, rel * dil, valid)


def _band_window(qi, nq, tq, hw, kw, ln):
    if nq == 1:
        return 0, 0
    var = jnp.where(qi == 0, 0, jnp.where(qi == nq - 1, 2, 1))
    return var, jnp.clip(qi * tq - hw, 0, ln - kw)


DilCfg = collections.namedtuple("DilCfg", "dil ln nq hw kw n_var")


def _dil_proj_kernel(x_ref, g_ref, w_ref, o_ref, slab_ref, hn_ref, *, dils):
    grp, c = pl.program_id(1), pl.program_id(2)
    seq, d = x_ref.shape
    n_slabs = d // LANES
    chunk = 256

    @pl.when((grp == 0) & (c == 0))
    def _():
        for c0 in range(0, seq, chunk):
            hn = _rms_rows(x_ref[pl.ds(c0, chunk), :], g_ref[...])
            for s in range(n_slabs):
                slab_ref[s, pl.ds(c0, chunk), :] = hn[:, s * LANES:(s + 1) * LANES]

    for gi, dil in enumerate(dils):
        @pl.when((grp == gi) & (c == 0))
        def _(dil=dil):
            ln = seq // dil
            for r in range(dil):
                for c0 in range(0, ln, chunk):
                    n = min(chunk, ln)
                    rows = [slab_ref[s, pl.ds(r + c0 * dil, n, stride=dil), :] for s in range(n_slabs)]
                    hn_ref[pl.ds(r * ln + c0, n), :] = jnp.concatenate(rows, axis=1).astype(BF16)

    o_ref[...] = jnp.dot(hn_ref[...], w_ref[...].astype(BF16),
                         preferred_element_type=F32).astype(o_ref.dtype)


def _dil_proj(x3, g, w_in, slot, dils):
    batch, seq, d = x3.shape
    n = w_in.shape[-1]
    tn = FF_TILE
    per_group = n // len(dils) // tn
    return pl.pallas_call(
        functools.partial(_dil_proj_kernel, dils=dils),
        out_shape=jax.ShapeDtypeStruct((batch, seq, n), BF16),
        grid=(batch, len(dils), per_group),
        in_specs=[pl.BlockSpec((None, seq, d), lambda b, g, c: (b, 0, 0)),
                  pl.BlockSpec((1, d), lambda b, g, c: (0, 0)),
                  pl.BlockSpec((None, d, tn), lambda b, g, c: (slot, 0, g * per_group + c))],
        out_specs=pl.BlockSpec((None, seq, tn), lambda b, g, c: (b, 0, g * per_group + c)),
        scratch_shapes=[pltpu.VMEM((d // LANES, seq, LANES), F32), pltpu.VMEM((seq, d), BF16)],
        compiler_params=_cparams(3),
        name="dil_proj",
    )(x3, g.reshape(1, d), w_in)


def _dil_unit(cfg, unit, q, k_ref, v_ref, bias_ref, acc_ref, m_ref, l_ref, *, tq, n_heads, first, last):
    dil, ln, nq = cfg.dil, cfg.ln, cfg.nq
    if nq == 1:
        r, qi = unit, 0
    elif dil == 1:
        r, qi = 0, unit
    else:
        r, qi = unit // nq, unit % nq
    var, ks = _band_window(qi, nq, tq, cfg.hw, cfg.kw, ln)
    krow = pl.multiple_of(r * ln + ks, cfg.hw)
    if dil == 1:
        rows = pl.ds(pl.multiple_of(qi * tq, tq), tq)
    else:
        rows = pl.ds(qi * tq * dil + r, tq, stride=dil)
    if not first:
        m_old = m_ref[rows, :]
        l_old = l_ref[rows, :]
    lane = lax.broadcasted_iota(I32, (tq, LANES), 1)
    m_blk = jnp.zeros((tq, LANES), F32)
    l_blk = jnp.zeros((tq, LANES), F32)
    pair = []
    acc_old = None
    for h in range(n_heads):
        cs = slice(h * HEAD_DIM, (h + 1) * HEAD_DIM)
        qh = q[:, cs] * Q_SCALE
        kh = k_ref[pl.ds(krow, cfg.kw), cs]
        vh = v_ref[pl.ds(krow, cfg.kw), cs]
        s = lax.dot_general(qh, kh, (((1,), (1,)), ((), ())), preferred_element_type=F32)
        s = s + bias_ref[var, h]
        m_c = jnp.max(s, axis=-1, keepdims=True)
        p = jnp.exp(s - m_c)
        l_c = jnp.sum(p, axis=-1, keepdims=True)
        o_c = jnp.dot(p.astype(BF16), vh, preferred_element_type=F32)
        if first:
            m_n, l_n, acc_n = m_c, l_c, o_c
        else:
            if h % 2 == 0:
                acc_old = acc_ref[h // 2, rows, :]
            lo = (h % 2) * HEAD_DIM
            m_o = m_old[:, h:h + 1]
            l_o = l_old[:, h:h + 1]
            m_n = jnp.maximum(m_o, m_c)
            a_o = jnp.exp(m_o - m_n)
            a_c = jnp.exp(m_c - m_n)
            l_n = a_o * l_o + a_c * l_c
            acc_n = a_o * acc_old[:, lo:lo + HEAD_DIM] + a_c * o_c
        pair.append(acc_n / l_n if last else acc_n)
        if h % 2 == 1:
            acc_ref[h // 2, rows, :] = jnp.concatenate(pair, axis=1)
            pair = []
        if not last:
            m_blk = jnp.where(lane == h, m_n, m_blk)
            l_blk = jnp.where(lane == h, l_n, l_blk)
    if not last:
        m_ref[rows, :] = m_blk
        l_ref[rows, :] = l_blk


def _dil_attn_kernel(*refs, cfgs, tq, n_heads):
    ng = len(cfgs)
    seg_refs = refs[:ng]
    q_ref, k_ref, v_ref, o_ref = refs[ng:ng + 4]
    bias_refs = refs[ng + 4:2 * ng + 4]
    acc_ref, m_ref, l_ref = refs[2 * ng + 4:]
    hf, b, grp, step = (pl.program_id(i) for i in range(4))

    @pl.when((b == 0) & (grp == 0) & (step == 0))
    def _():
        for gi, cfg in enumerate(cfgs):
            for v in range(cfg.n_var):
                for h in range(n_heads):
                    row = seg_refs[gi][hf * n_heads + h, pl.ds(v, 1), :]
                    bias_refs[gi][v, h] = _toeplitz(row, tq, cfg.kw)

    for gi, cfg in enumerate(cfgs):
        @pl.when(grp == gi)
        def _(gi=gi, cfg=cfg):
            for uu in range(DIL_UNITS):
                _dil_unit(cfg, step * DIL_UNITS + uu, q_ref[pl.ds(uu * tq, tq), :], k_ref, v_ref,
                          bias_refs[gi], acc_ref, m_ref, l_ref, tq=tq, n_heads=n_heads,
                          first=gi == 0, last=gi == ng - 1)

    @pl.when((grp == ng - 1) & (step == pl.num_programs(3) - 1))
    def _():
        slabs = [acc_ref[s] for s in range(acc_ref.shape[0])]
        o_ref[...] = jnp.concatenate(slabs, axis=1).astype(o_ref.dtype)


def _dilated_attention(qkv, rel_bias, batch, seq):
    ng = len(DIL_PAIRS)
    ncol = qkv.shape[-1]
    d = ncol // (3 * ng)
    hd = d // DIL_HEAD_SPLIT
    n_heads = hd // HEAD_DIM
    tq = ATT_TQ
    cfgs, segs = [], []
    for win, dil in DIL_PAIRS:
        half = win // (2 * dil)
        ln = seq // dil
        nq = ln // tq
        kw = min(tq + 2 * half, ln)
        variants = _band_variants(half, nq)
        cfgs.append(DilCfg(dil, ln, nq, half, kw, len(variants)))
        segs.append(_band_segs(rel_bias, tq, kw, half, dil, variants, _next_pow2(tq + kw - 1)))
    steps = seq // tq // DIL_UNITS
    cb = DIL_HEAD_SPLIT

    in_specs = [pl.BlockSpec(sg.shape, lambda hf, b, g, s: (0, 0, 0)) for sg in segs]
    in_specs += [pl.BlockSpec((None, DIL_UNITS * tq, hd), lambda hf, b, g, s: (b, s, (g * 3) * cb + hf)),
                 pl.BlockSpec((None, seq, hd), lambda hf, b, g, s: (b, 0, (g * 3 + 1) * cb + hf)),
                 pl.BlockSpec((None, seq, hd), lambda hf, b, g, s: (b, 0, (g * 3 + 2) * cb + hf))]
    scratch = [pltpu.VMEM((c.n_var, n_heads, tq, c.kw), F32) for c in cfgs]
    scratch += [pltpu.VMEM((hd // LANES, seq, LANES), F32), pltpu.VMEM((seq, LANES), F32),
                pltpu.VMEM((seq, LANES), F32)]
    out = pl.pallas_call(
        functools.partial(_dil_attn_kernel, cfgs=tuple(cfgs), tq=tq, n_heads=n_heads),
        out_shape=jax.ShapeDtypeStruct((batch, seq, d), BF16),
        grid=(DIL_HEAD_SPLIT, batch, ng, steps),
        in_specs=in_specs,
        out_specs=pl.BlockSpec((None, seq, hd), lambda hf, b, g, s: (b, 0, hf)),
        scratch_shapes=scratch,
        compiler_params=_cparams(4),
        name="dil_attn",
    )(*segs, qkv, qkv, qkv)
    return out.reshape(batch * seq, d)


def _gqa_kernel(sink_ref, seg_ref, q_ref, k_ref, v_ref, o_ref, bias_ref, *, tq, kw, hw, grp):
    b, qi = pl.program_id(0), pl.program_id(1)
    nq = pl.num_programs(1)
    seq_len = k_ref.shape[0]
    n_kv = k_ref.shape[1] // HEAD_DIM

    @pl.when((b == 0) & (qi == 0))
    def _():
        for v in range(3):
            for kh in range(n_kv):
                for gq in range(grp):
                    bias_ref[v, kh, pl.ds(gq * tq, tq), :] = _toeplitz(
                        seg_ref[kh * grp + gq, pl.ds(v, 1), :], tq, kw)

    var = jnp.where(qi == 0, 0, jnp.where(qi == nq - 1, 2, 1))
    ks = pl.multiple_of(jnp.clip(qi * tq - hw, 0, seq_len - kw), hw)
    row = lax.broadcasted_iota(I32, (grp * tq, 1), 0)
    outs = [None] * (n_kv * grp)
    for kh in range(n_kv):
        cs = slice(kh * HEAD_DIM, (kh + 1) * HEAD_DIM)
        qs = jnp.concatenate(
            [q_ref[:, (kh * grp + gq) * HEAD_DIM:(kh * grp + gq + 1) * HEAD_DIM] for gq in range(grp)],
            axis=0) * Q_SCALE
        kk = k_ref[pl.ds(ks, kw), cs]
        vv = v_ref[pl.ds(ks, kw), cs]
        s = lax.dot_general(qs, kk, (((1,), (1,)), ((), ())), preferred_element_type=F32)
        s = s + bias_ref[var, kh]
        sk = jnp.zeros((grp * tq, 1), F32)
        for gq in range(grp):
            sk = jnp.where((row >= gq * tq) & (row < (gq + 1) * tq), sink_ref[kh * grp + gq], sk)
        m = jnp.maximum(jnp.max(s, axis=-1, keepdims=True), sk)
        e = jnp.exp(s - m)
        den = jnp.sum(e, axis=-1, keepdims=True) + jnp.exp(sk - m)
        o = jnp.dot(e.astype(BF16), vv, preferred_element_type=F32) / den
        for gq in range(grp):
            outs[kh * grp + gq] = o[gq * tq:(gq + 1) * tq, :]
    o_ref[...] = jnp.concatenate(outs, axis=1).astype(o_ref.dtype)


def _gqa_attention(qkv, sink, rel_bias, batch, seq):
    t, ncol = qkv.shape
    n_q = sink.shape[0]
    d = n_q * HEAD_DIM
    kvw = GQA_KV_HEADS * HEAD_DIM
    grp = n_q // GQA_KV_HEADS
    tq = hw = GQA_WINDOW
    kw = 3 * GQA_WINDOW
    nq = seq // tq
    variants = _band_variants(hw, nq)
    width = _next_pow2(tq + kw - 1)
    seg = _band_segs(rel_bias, tq, kw, GQA_WINDOW, 1, variants, width)
    qkv_v = qkv.reshape(batch, seq, ncol)
    out = pl.pallas_call(
        functools.partial(_gqa_kernel, tq=tq, kw=kw, hw=hw, grp=grp),
        out_shape=jax.ShapeDtypeStruct((batch, seq, d), BF16),
        grid=(batch, nq),
        in_specs=[pl.BlockSpec(memory_space=pltpu.SMEM),
                  pl.BlockSpec((n_q, 3, width), lambda b, qi: (0, 0, 0)),
                  pl.BlockSpec((None, tq, d), lambda b, qi: (b, qi, 0)),
                  pl.BlockSpec((None, seq, kvw), lambda b, qi: (b, 0, d // kvw)),
                  pl.BlockSpec((None, seq, kvw), lambda b, qi: (b, 0, d // kvw + 1))],
        out_specs=pl.BlockSpec((None, tq, d), lambda b, qi: (b, qi, 0)),
        scratch_shapes=[pltpu.VMEM((3, GQA_KV_HEADS, grp * tq, kw), F32)],
        compiler_params=_cparams(2),
        name="gqa_attn",
    )(sink.astype(F32), seg, qkv_v, qkv_v, qkv_v)
    return out.reshape(t, d)


def _diff_kernel(seg_ref, lam_ref, sub_ref, q_ref, k_ref, v_ref, o_ref, bias_ref, *, tq, lam_init):
    h, qi, b = pl.program_id(0), pl.program_id(1), pl.program_id(2)
    seq_len = k_ref.shape[0]
    nk = seq_len // tq

    @pl.when(b == 0)
    def _():
        for j in range(2):
            for ki in range(nk):
                row = seg_ref[h * 2 + j, pl.ds(ki - qi + nk - 1, 1), :]
                bias_ref[j, :, pl.ds(ki * tq, tq)] = _toeplitz(row, tq, tq)

    lv = lam_ref[...]
    s01 = jnp.sum(lv[0:1, :] * lv[1:2, :], axis=-1, keepdims=True)
    s23 = jnp.sum(lv[2:3, :] * lv[3:4, :], axis=-1, keepdims=True)
    lam = jnp.exp(s01) - jnp.exp(s23) + lam_init

    probs = []
    for j in range(2):
        cs = slice(j * HEAD_DIM, (j + 1) * HEAD_DIM)
        qj = q_ref[:, cs] * Q_SCALE
        kj = k_ref[:, cs]
        s = lax.dot_general(qj, kj, (((1,), (1,)), ((), ())), preferred_element_type=F32)
        s = s + bias_ref[j]
        m = jnp.max(s, axis=-1, keepdims=True)
        p = jnp.exp(s - m)
        l = jnp.sum(p, axis=-1, keepdims=True)
        probs.append((p, l))
    (p0, l0), (p1, l1) = probs
    a = p0 * (1.0 / l0) - p1 * (lam / l1)
    o = jnp.dot(a.astype(BF16), v_ref[...], preferred_element_type=F32)
    o = _rms_rows(o, sub_ref[...]) * (1.0 - lam_init)
    o_ref[...] = o.astype(o_ref.dtype)


def _diff_attention(qkv, lam_vecs, subln, slot, rel_bias, batch, seq):
    t, ncol = qkv.shape
    d = ncol // 3
    hd2 = 2 * HEAD_DIM
    n_heads = d // hd2
    tq = DIFF_TQ
    nk = seq // tq
    width = _next_pow2(2 * tq - 1)
    c = np.arange(width)
    rel = np.stack([(dl - (nk - 1)) * tq + c - (tq - 1) for dl in range(2 * nk - 1)])
    seg = _seg_table(rel_bias, rel, np.ones_like(rel, dtype=bool))
    lam_init = 0.8 - 0.6 * math.exp(-0.3 * DIFF_LAYER)
    qkv_v = qkv.reshape(batch, seq, ncol)
    out = pl.pallas_call(
        functools.partial(_diff_kernel, tq=tq, lam_init=lam_init),
        out_shape=jax.ShapeDtypeStruct((batch, seq, d), BF16),
        grid=(n_heads, seq // tq, batch),
        in_specs=[pl.BlockSpec(seg.shape, lambda h, qi, b: (0, 0, 0)),
                  pl.BlockSpec((None,) + lam_vecs.shape[1:], lambda h, qi, b: (slot, 0, 0)),
                  pl.BlockSpec((1, hd2), lambda h, qi, b: (slot, 0)),
                  pl.BlockSpec((None, tq, hd2), lambda h, qi, b: (b, qi, h)),
                  pl.BlockSpec((None, seq, hd2), lambda h, qi, b: (b, 0, n_heads + h)),
                  pl.BlockSpec((None, seq, hd2), lambda h, qi, b: (b, 0, 2 * n_heads + h))],
        out_specs=pl.BlockSpec((None, tq, hd2), lambda h, qi, b: (b, qi, h)),
        scratch_shapes=[pltpu.VMEM((2, tq, seq), F32)],
        compiler_params=_cparams(3),
        name="diff_attn",
    )(seg, lam_vecs, subln, qkv_v, qkv_v, qkv_v)
    return out.reshape(t, d)


def _pack_bf16_pair(lo, hi):
    lo_b = lax.bitcast_convert_type(lo.astype(BF16).astype(F32), U32)
    hi_b = lax.bitcast_convert_type(hi.astype(BF16).astype(F32), U32)
    return (lo_b >> 16) | (hi_b & jnp.uint32(0xFFFF0000))


def _unpack_bf16_pair(w):
    lo = lax.bitcast_convert_type(w << 16, F32)
    hi = lax.bitcast_convert_type(w & jnp.uint32(0xFFFF0000), F32)
    return lo, hi


def _router_kernel(x_ref, g_ref, r_ref, hp_ref, mi_ref, mf_ref, cnt_ref, tri_ref, carry_ref):
    i = pl.program_id(0)
    tm, d = x_ref.shape

    @pl.when(i == 0)
    def _():
        rr = lax.broadcasted_iota(I32, (tm, tm), 0)
        cc = lax.broadcasted_iota(I32, (tm, tm), 1)
        tri_ref[...] = (cc < rr).astype(BF16)
        carry_ref[...] = jnp.zeros_like(carry_ref)

    hn = _rms_rows(x_ref[...], g_ref[...])
    hp_ref[...] = _pack_bf16_pair(hn[:, :d // 2], hn[:, d // 2:])
    logits = jnp.dot(hn, r_ref[...], preferred_element_type=F32, precision=lax.Precision.HIGHEST)
    lane = lax.broadcasted_iota(I32, (tm, LANES), 1)
    logits = jnp.where(lane < N_EXPERTS, logits, -jnp.inf)
    v1 = jnp.max(logits, axis=-1, keepdims=True)
    i1 = jnp.min(jnp.where(logits == v1, lane, LANES), axis=-1, keepdims=True)
    oh1 = lane == i1
    rest = jnp.where(oh1, -jnp.inf, logits)
    v2 = jnp.max(rest, axis=-1, keepdims=True)
    i2 = jnp.min(jnp.where(rest == v2, lane, LANES), axis=-1, keepdims=True)
    oh2 = lane == i2
    e2 = jnp.exp(v2 - v1)
    g1 = 1.0 / (1.0 + e2)
    g2 = e2 / (1.0 + e2)

    sel = (oh1 | oh2)
    before = jnp.dot(tri_ref[...], sel.astype(BF16), preferred_element_type=F32) + carry_ref[...]
    rank1 = jnp.sum(jnp.where(oh1, before, 0.0), axis=-1, keepdims=True).astype(I32)
    rank2 = jnp.sum(jnp.where(oh2, before, 0.0), axis=-1, keepdims=True).astype(I32)
    carry_ref[...] += jnp.sum(sel.astype(F32), axis=0, keepdims=True)

    mi_ref[...] = jnp.where(lane == 0, i1, jnp.where(lane == 1, i2,
                            jnp.where(lane == 2, rank1, jnp.where(lane == 3, rank2, 0))))
    mf_ref[...] = jnp.where(lane == 0, g1, jnp.where(lane == 1, g2, 0.0))
    cnt_ref[...] = carry_ref[...].astype(I32)


def _router(x, g, router):
    t, d = x.shape
    tm = ROW_TILE
    r_pad = jnp.zeros((d, LANES), F32).at[:, :N_EXPERTS].set(router.astype(F32))
    return pl.pallas_call(
        _router_kernel,
        out_shape=(jax.ShapeDtypeStruct((t, d // 2), U32),
                   jax.ShapeDtypeStruct((t, LANES), I32),
                   jax.ShapeDtypeStruct((t, LANES), F32),
                   jax.ShapeDtypeStruct((1, LANES), I32)),
        grid=(t // tm,),
        in_specs=[pl.BlockSpec((tm, d), lambda i: (i, 0)),
                  pl.BlockSpec((1, d), lambda i: (0, 0)),
                  pl.BlockSpec((d, LANES), lambda i: (0, 0))],
        out_specs=(pl.BlockSpec((tm, d // 2), lambda i: (i, 0)),
                   pl.BlockSpec((tm, LANES), lambda i: (i, 0)),
                   pl.BlockSpec((tm, LANES), lambda i: (i, 0)),
                   pl.BlockSpec((1, LANES), lambda i: (0, 0))),
        scratch_shapes=[pltpu.VMEM((tm, tm), BF16), pltpu.VMEM((1, LANES), F32)],
        compiler_params=_cparams(1),
        name="moe_router",
    )(x, g.reshape(1, d), r_pad)


def _row_copy(src, s_row, dst, d_row, sem):
    return pltpu.make_async_copy(src.at[pl.ds(s_row, 1)], dst.at[pl.ds(d_row, 1)], sem)


def _dispatch_kernel(tail_ref, pos_ref, hp_ref, xs_ref, zero_ref, sem):
    i = pl.program_id(0)
    tm = hp_ref.shape[0]
    tg = zero_ref.shape[0]

    @pl.when(i == 0)
    def _():
        zero_ref[...] = jnp.zeros_like(zero_ref)
        for e in range(N_EXPERTS):
            tail = pl.multiple_of(tail_ref[e], SUBLANES)
            pltpu.make_async_copy(zero_ref, xs_ref.at[pl.ds(tail, tg)], sem.at[0]).start()
        for e in range(N_EXPERTS):
            tail = pl.multiple_of(tail_ref[e], SUBLANES)
            pltpu.make_async_copy(zero_ref, xs_ref.at[pl.ds(tail, tg)], sem.at[0]).wait()

    def issue(r, c):
        for k in range(2):
            _row_copy(hp_ref, r, xs_ref, pos_ref[0, k, r], sem.at[0]).start()
        return c

    lax.fori_loop(0, tm, issue, 0)

    def drain(r, c):
        for k in range(2):
            _row_copy(hp_ref, 0, xs_ref, 0, sem.at[0]).wait()
        return c

    lax.fori_loop(0, tm, drain, 0)


def _dispatch(hp, pos, tails, n_rows):
    t, dw = hp.shape
    tm = DMA_TILE
    return pl.pallas_call(
        _dispatch_kernel,
        out_shape=jax.ShapeDtypeStruct((n_rows, dw), U32),
        grid_spec=pltpu.PrefetchScalarGridSpec(
            num_scalar_prefetch=1,
            grid=(t // tm,),
            in_specs=[pl.BlockSpec((1, 2, tm), lambda i, tl: (i, 0, 0), memory_space=pltpu.SMEM),
                      pl.BlockSpec((tm, dw), lambda i, tl: (i, 0))],
            out_specs=pl.BlockSpec(memory_space=pl.ANY),
            scratch_shapes=[pltpu.VMEM((MOE_TILE, dw), U32), pltpu.SemaphoreType.DMA((1,))]),
        compiler_params=pltpu.CompilerParams(dimension_semantics=("arbitrary",),
                                             vmem_limit_bytes=VMEM_LIMIT, has_side_effects=True),
        name="moe_dispatch",
    )(tails, pos, hp)


def _expert_kernel(te_ref, tb_ref, nv_ref, xs_ref, wg_ref, wu_ref, wd_ref, ys_ref, hn_ref, acc_ref):
    i, f = pl.program_id(0), pl.program_id(1)
    half = xs_ref.shape[1]

    @pl.when(i < nv_ref[0])
    def _():
        @pl.when(f == 0)
        def _():
            lo, hi = _unpack_bf16_pair(xs_ref[...])
            hn_ref[:, :half] = lo.astype(BF16)
            hn_ref[:, half:] = hi.astype(BF16)
            acc_ref[...] = jnp.zeros_like(acc_ref)

        h = hn_ref[...]
        gate = jnp.dot(h, wg_ref[...].astype(BF16), preferred_element_type=F32)
        up = jnp.dot(h, wu_ref[...].astype(BF16), preferred_element_type=F32)
        a = _silu_mul(gate, up).astype(BF16)
        acc_ref[...] += jnp.dot(a, wd_ref[...].astype(BF16), preferred_element_type=F32)

        @pl.when(f == pl.num_programs(1) - 1)
        def _():
            acc = acc_ref[...]
            ys_ref[...] = _pack_bf16_pair(acc[:, :half], acc[:, half:])


def _experts(xs, w_gate_up, w_down, slot, tile_expert, tile_block, n_valid, n_tiles):
    n_rows, dw = xs.shape
    d = 2 * dw
    ff = w_down.shape[-2]
    tg, tf = MOE_TILE, FF_TILE
    nf = ff // tf

    def fidx(i, f, nv):
        return jnp.where(i < nv[0], f, nf - 1)

    return pl.pallas_call(
        _expert_kernel,
        out_shape=jax.ShapeDtypeStruct((n_rows, dw), U32),
        grid_spec=pltpu.PrefetchScalarGridSpec(
            num_scalar_prefetch=3,
            grid=(n_tiles, nf),
            in_specs=[pl.BlockSpec((tg, dw), lambda i, f, te, tb, nv: (tb[i], 0)),
                      pl.BlockSpec((None, None, d, tf),
                                   lambda i, f, te, tb, nv: (slot, te[i], 0, fidx(i, f, nv))),
                      pl.BlockSpec((None, None, d, tf),
                                   lambda i, f, te, tb, nv: (slot, te[i], 0, nf + fidx(i, f, nv))),
                      pl.BlockSpec((None, None, tf, d),
                                   lambda i, f, te, tb, nv: (slot, te[i], fidx(i, f, nv), 0))],
            out_specs=pl.BlockSpec((tg, dw), lambda i, f, te, tb, nv: (tb[i], 0)),
            scratch_shapes=[pltpu.VMEM((tg, d), BF16), pltpu.VMEM((tg, d), F32)]),
        compiler_params=_cparams(2),
        name="moe_experts",
    )(tile_expert, tile_block, n_valid, xs, w_gate_up, w_gate_up, w_down)


def _combine_kernel(pos_ref, x_ref, mf_ref, gf_ref, ys_ref, o_ref, buf_ref, sem, *, final_norm):
    tm = x_ref.shape[0]
    half = buf_ref.shape[2]

    def issue(r, c):
        for k in range(2):
            pltpu.make_async_copy(ys_ref.at[pl.ds(pos_ref[0, k, r], 1)],
                                  buf_ref.at[k, pl.ds(r, 1)], sem.at[0]).start()
        return c

    lax.fori_loop(0, tm, issue, 0)

    def drain(r, c):
        for k in range(2):
            pltpu.make_async_copy(ys_ref.at[pl.ds(0, 1)], buf_ref.at[k, pl.ds(0, 1)], sem.at[0]).wait()
        return c

    lax.fori_loop(0, tm, drain, 0)

    gates = mf_ref[...]
    out_lo = x_ref[:, :half]
    out_hi = x_ref[:, half:]
    for k in range(2):
        lo, hi = _unpack_bf16_pair(buf_ref[k])
        gk = gates[:, k:k + 1]
        out_lo = out_lo + gk * lo
        out_hi = out_hi + gk * hi
    out = jnp.concatenate([out_lo, out_hi], axis=1)
    if final_norm:
        out = _rms_rows(out, gf_ref[...])
    o_ref[...] = out


def _combine(x, mf, ys, pos, final_g):
    t, d = x.shape
    tm = DMA_TILE
    final_norm = final_g is not None
    gf = (final_g if final_norm else jnp.ones((d,), F32)).reshape(1, d).astype(F32)
    return pl.pallas_call(
        functools.partial(_combine_kernel, final_norm=final_norm),
        out_shape=jax.ShapeDtypeStruct((t, d), F32),
        grid=(t // tm,),
        in_specs=[pl.BlockSpec((1, 2, tm), lambda i: (i, 0, 0), memory_space=pltpu.SMEM),
                  pl.BlockSpec((tm, d), lambda i: (i, 0)),
                  pl.BlockSpec((tm, LANES), lambda i: (i, 0)),
                  pl.BlockSpec((1, d), lambda i: (0, 0)),
                  pl.BlockSpec(memory_space=pl.ANY)],
        out_specs=pl.BlockSpec((tm, d), lambda i: (i, 0)),
        scratch_shapes=[pltpu.VMEM((2, tm, d // 2), U32), pltpu.SemaphoreType.DMA((1,))],
        compiler_params=_cparams(1),
        name="moe_combine",
    )(pos, x, mf, gf, ys)


def _moe(x, g, router, w_gate_up, w_down, slot, final_g=None):
    t, d = x.shape
    tg = MOE_TILE
    hp, mi, mf, cnt = _router(x, g, router)

    counts = cnt[0, :N_EXPERTS]
    tiles_e = (counts + tg - 1) // tg
    tile_start = jnp.cumsum(tiles_e) - tiles_e
    row_start = tile_start * tg
    n_valid = jnp.sum(tiles_e).astype(I32)
    n_tiles = 2 * t // tg + N_EXPERTS
    n_rows = (n_tiles + 1) * tg
    tidx = jnp.minimum(jnp.arange(n_tiles, dtype=I32), n_valid - 1)
    tile_expert = (jnp.sum(tidx[:, None] >= tile_start[None, :], axis=1) - 1).astype(I32)
    tails = (row_start + counts // SUBLANES * SUBLANES).astype(I32)
    experts = mi[:, 0:2]
    pos = (row_start[experts] + mi[:, 2:4]).astype(I32)
    pos = pos.T.reshape(2, t // DMA_TILE, DMA_TILE).transpose(1, 0, 2)

    xs = _dispatch(hp, pos, tails, n_rows)
    ys = _experts(xs, w_gate_up, w_down, slot, tile_expert, tidx, n_valid.reshape(1), n_tiles)
    return _combine(x, mf, ys, pos, final_g)


def kernel(x, rel_bias, norm_mix, norm_ffn, norm_final, a_w_in, a_w_group, a_scale, a_w_out,
           b_w_in, b_w_out, c_w_in, c_lambda, c_subln, c_w_out, d_w_in, d_sink, d_w_out,
           f_w_gate_up, f_w_down, m_router, m_w_gate_up, m_w_down):
    batch, seq, d = x.shape
    h = x.reshape(batch * seq, d)

    u = _norm_matmul(h, norm_mix[0], a_w_in, 0, F32)
    y = _pool_mix(u, a_w_group, a_scale, 0, batch, seq)
    h = _matmul_res(y, a_w_out, 0, h)
    h = _ffn(h, norm_ffn[0], f_w_gate_up, f_w_down, 0)

    qkv = _dil_proj(h.reshape(batch, seq, d), norm_mix[1], b_w_in, 0, tuple(p[1] for p in DIL_PAIRS))
    o = _dilated_attention(qkv, rel_bias, batch, seq)
    h = _matmul_res(o, b_w_out, 0, h)
    h = _moe(h, norm_ffn[1], m_router[0], m_w_gate_up, m_w_down, 0)

    qkv = _norm_matmul(h, norm_mix[2], c_w_in, 0, BF16)
    o = _diff_attention(qkv, c_lambda, c_subln, 0, rel_bias, batch, seq)
    h = _matmul_res(o, c_w_out, 0, h)
    h = _ffn(h, norm_ffn[2], f_w_gate_up, f_w_down, 1)

    qkv = _norm_matmul(h, norm_mix[3], d_w_in, 0, BF16)
    o = _gqa_attention(qkv, d_sink[0], rel_bias, batch, seq)
    h = _matmul_res(o, d_w_out, 0, h)
    h = _moe(h, norm_ffn[3], m_router[1], m_w_gate_up, m_w_down, 1, final_g=norm_final)
    return h.reshape(batch, seq, d)
```

```python
import collections
import functools
import math

import numpy as np
import jax
import jax.numpy as jnp
from jax import lax
from jax.experimental import pallas as pl
from jax.experimental.pallas import tpu as pltpu

F32 = jnp.float32
BF16 = jnp.bfloat16
I32 = jnp.int32
U32 = jnp.uint32

RMS_EPS = 1e-6
HEAD_DIM = 64
N_BUCKETS = 32
MAX_DISTANCE = 1024
POOL_WINDOWS = (2, 4, 8, 16)
DIL_PAIRS = ((128, 1), (512, 4), (2048, 16))
Q_SCALE = HEAD_DIM ** -0.5
GQA_KV_HEADS = 4
GQA_WINDOW = 128
N_EXPERTS = 8
DIFF_LAYER = 2

LANES = 128
SUBLANES = 8
VMEM_LIMIT = 56 * 1024 * 1024
ROW_TILE = 1024
FF_TILE = 512
MOE_TILE = 1024
DMA_TILE = 512
DMA_UNROLL = 8
ATT_TQ = 128
DIL_UNITS = 2
DIL_HEAD_SPLIT = 2
DIFF_TQ = 512
MASK_VALUE = -1e30


def _cparams(n_axes, vmem=VMEM_LIMIT):
    return pltpu.CompilerParams(dimension_semantics=("arbitrary",) * n_axes,
                                vmem_limit_bytes=vmem)


def _next_pow2(n):
    return 1 << (n - 1).bit_length()


def _rms_rows(x, g):
    ms = jnp.mean(x * x, axis=-1, keepdims=True)
    return x * lax.rsqrt(ms + RMS_EPS) * g


def _norm_matmul_kernel(x_ref, g_ref, w_ref, o_ref, hn_ref):
    @pl.when(pl.program_id(1) == 0)
    def _():
        hn_ref[...] = _rms_rows(x_ref[...], g_ref[...]).astype(BF16)

    o_ref[...] = jnp.dot(hn_ref[...], w_ref[...].astype(BF16),
                         preferred_element_type=F32).astype(o_ref.dtype)


def _norm_matmul(x, g, w, slot, out_dtype):
    t, d = x.shape
    n = w.shape[-1]
    tm = ROW_TILE
    tn = next(c for c in (1024, 768, 512, 256, 128) if n % c == 0)
    return pl.pallas_call(
        _norm_matmul_kernel,
        out_shape=jax.ShapeDtypeStruct((t, n), out_dtype),
        grid=(t // tm, n // tn),
        in_specs=[pl.BlockSpec((tm, d), lambda i, j: (i, 0)),
                  pl.BlockSpec((1, d), lambda i, j: (0, 0)),
                  pl.BlockSpec((None, d, tn), lambda i, j: (slot, 0, j))],
        out_specs=pl.BlockSpec((tm, tn), lambda i, j: (i, j)),
        scratch_shapes=[pltpu.VMEM((tm, d), BF16)],
        compiler_params=_cparams(2),
        name="norm_matmul",
    )(x, g.reshape(1, d), w)


def _matmul_res_kernel(a_ref, w_ref, x_ref, o_ref, wb_ref):
    @pl.when(pl.program_id(0) == 0)
    def _():
        wb_ref[...] = w_ref[...].astype(BF16)

    o_ref[...] = x_ref[...] + jnp.dot(a_ref[...], wb_ref[...], preferred_element_type=F32)


def _matmul_res(a, w, slot, x):
    t, k = a.shape
    d = w.shape[-1]
    tm = ROW_TILE
    return pl.pallas_call(
        _matmul_res_kernel,
        out_shape=jax.ShapeDtypeStruct((t, d), F32),
        grid=(t // tm,),
        in_specs=[pl.BlockSpec((tm, k), lambda i: (i, 0)),
                  pl.BlockSpec((None, k, d), lambda i: (slot, 0, 0)),
                  pl.BlockSpec((tm, d), lambda i: (i, 0))],
        out_specs=pl.BlockSpec((tm, d), lambda i: (i, 0)),
        scratch_shapes=[pltpu.VMEM((k, d), BF16)],
        compiler_params=_cparams(1),
        name="matmul_res",
    )(a, w, x)


def _silu_mul(g, u):
    return g * (1.0 / (1.0 + jnp.exp(-g))) * u


def _ffn_kernel(x_ref, g_ref, wg_ref, wu_ref, wd_ref, o_ref, hn_ref, acc_ref):
    f = pl.program_id(1)

    @pl.when(f == 0)
    def _():
        hn_ref[...] = _rms_rows(x_ref[...], g_ref[...]).astype(BF16)
        acc_ref[...] = jnp.zeros_like(acc_ref)

    h = hn_ref[...]
    gate = jnp.dot(h, wg_ref[...].astype(BF16), preferred_element_type=F32)
    up = jnp.dot(h, wu_ref[...].astype(BF16), preferred_element_type=F32)
    a = _silu_mul(gate, up).astype(BF16)
    acc_ref[...] += jnp.dot(a, wd_ref[...].astype(BF16), preferred_element_type=F32)

    @pl.when(f == pl.num_programs(1) - 1)
    def _():
        o_ref[...] = x_ref[...] + acc_ref[...]


def _ffn(x, g, w_gate_up, w_down, slot):
    t, d = x.shape
    ff = w_down.shape[-2]
    tm, tf = ROW_TILE, FF_TILE
    nf = ff // tf
    return pl.pallas_call(
        _ffn_kernel,
        out_shape=jax.ShapeDtypeStruct((t, d), F32),
        grid=(t // tm, nf),
        in_specs=[pl.BlockSpec((tm, d), lambda i, f: (i, 0)),
                  pl.BlockSpec((1, d), lambda i, f: (0, 0)),
                  pl.BlockSpec((None, d, tf), lambda i, f: (slot, 0, f)),
                  pl.BlockSpec((None, d, tf), lambda i, f: (slot, 0, nf + f)),
                  pl.BlockSpec((None, tf, d), lambda i, f: (slot, f, 0))],
        out_specs=pl.BlockSpec((tm, d), lambda i, f: (i, 0)),
        scratch_shapes=[pltpu.VMEM((tm, d), BF16), pltpu.VMEM((tm, d), F32)],
        compiler_params=_cparams(2),
        name="ffn",
    )(x, g.reshape(1, d), w_gate_up, w_gate_up, w_down)


POOL_PAD = 16


def _pool_kernel(u_ref, wg_ref, sc_ref, o_ref, pad_ref):
    s, c = u_ref.shape
    grp = pl.program_id(1)
    pad_ref[pl.ds(0, POOL_PAD), :] = jnp.zeros((POOL_PAD, c), F32)
    pad_ref[pl.ds(POOL_PAD + s, POOL_PAD), :] = jnp.zeros((POOL_PAD, c), F32)
    pad_ref[pl.ds(POOL_PAD, s), :] = u_ref[...]
    wb = wg_ref[...].astype(BF16)
    chunk = 256

    for gi, w in enumerate(POOL_WINDOWS):
        @pl.when(grp == gi)
        def _(w=w):
            for c0 in range(0, s, chunk):
                acc = None
                for dlt in range(-(w // 2), w - w // 2):
                    piece = pad_ref[pl.ds(POOL_PAD + c0 + dlt, chunk), :]
                    acc = piece if acc is None else acc + piece
                pos = c0 + lax.broadcasted_iota(I32, (chunk, 1), 0)
                lo = jnp.maximum(pos - w // 2, 0)
                hi = jnp.minimum(pos + (w - w // 2), s)
                cnt = (hi - lo).astype(F32)
                mixed = acc / cnt - u_ref[pl.ds(c0, chunk), :]
                y = jnp.dot(mixed.astype(BF16), wb, preferred_element_type=F32)
                o_ref[pl.ds(c0, chunk), :] = (y * sc_ref[...]).astype(o_ref.dtype)


def _pool_mix(u, w_group, scale, slot, batch, seq):
    d = u.shape[1]
    ng = len(POOL_WINDOWS)
    c = d // ng
    u3 = u.reshape(batch, seq, d)
    out = pl.pallas_call(
        _pool_kernel,
        out_shape=jax.ShapeDtypeStruct((batch, seq, d), BF16),
        grid=(batch, ng),
        in_specs=[pl.BlockSpec((None, seq, c), lambda b, g: (b, 0, g)),
                  pl.BlockSpec((None, None, c, c), lambda b, g: (slot, g, 0, 0)),
                  pl.BlockSpec((1, c), lambda b, g: (slot, g))],
        out_specs=pl.BlockSpec((None, seq, c), lambda b, g: (b, 0, g)),
        scratch_shapes=[pltpu.VMEM((seq + 2 * POOL_PAD, c), F32)],
        compiler_params=_cparams(2),
        name="pool_mix",
    )(u3, w_group, scale)
    return out.reshape(batch * seq, d)


def _rel_bucket_np(rel):
    half = N_BUCKETS // 2
    max_exact = half // 2
    n = np.abs(rel)
    ratio = np.log(np.maximum(n, 1).astype(np.float32) / np.float32(max_exact))
    big = max_exact + (ratio / np.float32(math.log(MAX_DISTANCE / max_exact))
                       * np.float32(half - max_exact)).astype(np.int32)
    big = np.minimum(big, half - 1)
    return np.where(rel > 0, half, 0) + np.where(n < max_exact, n, big)


def _seg_table(rel_bias, rel, valid):
    bucket = jnp.asarray(_rel_bucket_np(rel).astype(np.int32))
    vals = jnp.take(rel_bias.astype(F32), bucket, axis=0)
    vals = jnp.where(jnp.asarray(valid)[..., None], vals, MASK_VALUE)
    return jnp.transpose(vals, (2, 0, 1))


def _toeplitz(seg_row, rows, cols):
    w = seg_row.shape[1]
    full = jnp.broadcast_to(seg_row, (rows, w))
    rolled = pltpu.roll(full, w - (rows - 1), 1, stride=1, stride_axis=0)
    return rolled[:, :cols]


def _band_variants(hw, n_tiles):
    if n_tiles == 1:
        return (0,)
    return (0, -hw, -2 * hw)


def _band_segs(rel_bias, tq, kw, half, dil, variants, width):
    c = np.arange(width)
    rel = np.stack([r0 + c - (tq - 1) for r0 in variants])
    valid = (np.abs(rel) <= half) & (c[None, :] < tq + kw - 1)
    return _seg_table(rel_bias, rel * dil, valid)


def _band_window(qi, nq, tq, hw, kw, ln):
    if nq == 1:
        return 0, 0
    var = jnp.where(qi == 0, 0, jnp.where(qi == nq - 1, 2, 1))
    return var, jnp.clip(qi * tq - hw, 0, ln - kw)


DilCfg = collections.namedtuple("DilCfg", "dil ln nq hw kw n_var")


def _dil_proj_kernel(x_ref, g_ref, w_ref, o_ref, slab_ref, hn_ref, *, dils):
    grp, c = pl.program_id(1), pl.program_id(2)
    seq, d = x_ref.shape
    n_slabs = d // LANES
    chunk = 256

    @pl.when((grp == 0) & (c == 0))
    def _():
        for c0 in range(0, seq, chunk):
            hn = _rms_rows(x_ref[pl.ds(c0, chunk), :], g_ref[...])
            for s in range(n_slabs):
                slab_ref[s, pl.ds(c0, chunk), :] = hn[:, s * LANES:(s + 1) * LANES]

    for gi, dil in enumerate(dils):
        @pl.when((grp == gi) & (c == 0))
        def _(dil=dil):
            ln = seq // dil
            for r in range(dil):
                for c0 in range(0, ln, chunk):
                    n = min(chunk, ln)
                    rows = [slab_ref[s, pl.ds(r + c0 * dil, n, stride=dil), :] for s in range(n_slabs)]
                    hn_ref[pl.ds(r * ln + c0, n), :] = jnp.concatenate(rows, axis=1).astype(BF16)

    res = jnp.dot(hn_ref[...], w_ref[...].astype(BF16), preferred_element_type=F32)
    res = res * jnp.where(c < pl.num_programs(2) // 3, Q_SCALE, 1.0)
    for s in range(o_ref.shape[0]):
        o_ref[s] = res[:, s * LANES:(s + 1) * LANES].astype(o_ref.dtype)


def _dil_proj(x3, g, w_in, slot, dils):
    batch, seq, d = x3.shape
    n = w_in.shape[-1]
    tn = FF_TILE
    per_group = n // len(dils) // tn
    spt = tn // LANES
    return pl.pallas_call(
        functools.partial(_dil_proj_kernel, dils=dils),
        out_shape=jax.ShapeDtypeStruct((batch, n // LANES, seq, LANES), BF16),
        grid=(batch, len(dils), per_group),
        in_specs=[pl.BlockSpec((None, seq, d), lambda b, g, c: (b, 0, 0)),
                  pl.BlockSpec((1, d), lambda b, g, c: (0, 0)),
                  pl.BlockSpec((None, d, tn), lambda b, g, c: (slot, 0, g * per_group + c))],
        out_specs=pl.BlockSpec((None, spt, seq, LANES), lambda b, g, c: (b, g * per_group + c, 0, 0)),
        scratch_shapes=[pltpu.VMEM((d // LANES, seq, LANES), F32), pltpu.VMEM((seq, d), BF16)],
        compiler_params=_cparams(3),
        name="dil_proj",
    )(x3, g.reshape(1, d), w_in)


def _dil_unit(cfg, unit, q4, k_ref, v_ref, bias_ref, acc_ref, m_ref, l_ref, *, tq, first, last):
    dil, ln, nq, kw = cfg.dil, cfg.ln, cfg.nq, cfg.kw
    n_pairs = q4.shape[0]
    if nq == 1:
        r, qi = unit, 0
    elif dil == 1:
        r, qi = 0, unit
    else:
        r, qi = unit // nq, unit % nq
    var, ks = _band_window(qi, nq, tq, cfg.hw, kw, ln)
    krow = pl.multiple_of(r * ln + ks, cfg.hw)
    if dil == 1:
        rows = pl.ds(pl.multiple_of(qi * tq, tq), tq)
    else:
        rows = pl.ds(qi * tq * dil + r, tq, stride=dil)

    even = lax.broadcasted_iota(I32, (n_pairs, tq, LANES), 2) < HEAD_DIM
    zero = jnp.zeros_like(q4)
    q8 = jnp.concatenate([jnp.where(even, q4, zero), jnp.where(even, zero, q4)], axis=0)
    k4 = k_ref[:, pl.ds(krow, kw), :]
    v4 = v_ref[:, pl.ds(krow, kw), :]
    k8 = jnp.concatenate([k4, k4], axis=0)
    va = jnp.concatenate([v4, jnp.ones_like(v4)], axis=2)
    v8 = jnp.concatenate([va, va], axis=0)
    s = lax.dot_general(q8, k8, (((2,), (2,)), ((0,), (0,))), preferred_element_type=F32)
    s = s + bias_ref[var]
    m8 = jnp.max(s, axis=-1, keepdims=True)
    p = jnp.exp(s - m8).astype(BF16)
    ov = lax.dot_general(p, v8, (((2,), (1,)), ((0,), (0,))), preferred_element_type=F32)
    o_c = jnp.where(even, ov[:n_pairs, :, :LANES], ov[n_pairs:, :, :LANES])
    l_c = jnp.where(even, ov[:n_pairs, :, LANES:], ov[n_pairs:, :, LANES:])
    m_c = jnp.where(even, m8[:n_pairs], m8[n_pairs:])
    if first:
        m_n, l_n, acc_n = m_c, l_c, o_c
    else:
        m_o = jnp.stack([m_ref[pp, rows, :] for pp in range(n_pairs)])
        l_o = jnp.stack([l_ref[pp, rows, :] for pp in range(n_pairs)])
        acc_o = jnp.stack([acc_ref[pp, rows, :] for pp in range(n_pairs)])
        m_n = jnp.maximum(m_o, m_c)
        a_o = jnp.exp(m_o - m_n)
        a_c = jnp.exp(m_c - m_n)
        l_n = a_o * l_o + a_c * l_c
        acc_n = a_o * acc_o + a_c * o_c
    if last:
        acc_n = acc_n / l_n
    for pp in range(n_pairs):
        acc_ref[pp, rows, :] = acc_n[pp]
        if not last:
            m_ref[pp, rows, :] = m_n[pp]
            l_ref[pp, rows, :] = l_n[pp]


def _dil_attn_kernel(*refs, cfgs, tq):
    ng = len(cfgs)
    seg_refs = refs[:ng]
    q_ref, k_ref, v_ref, o_ref = refs[ng:ng + 4]
    bias_refs = refs[ng + 4:2 * ng + 4]
    acc_ref, m_ref, l_ref = refs[2 * ng + 4:]
    hf, b, grp, step = (pl.program_id(i) for i in range(4))
    n_pairs = q_ref.shape[0]

    @pl.when((b == 0) & (grp == 0) & (step == 0))
    def _():
        for gi, cfg in enumerate(cfgs):
            for v in range(cfg.n_var):
                for odd in range(2):
                    for pp in range(n_pairs):
                        head = hf * 2 * n_pairs + 2 * pp + odd
                        row = seg_refs[gi][head, pl.ds(v, 1), :]
                        bias_refs[gi][v, odd * n_pairs + pp] = _toeplitz(row, tq, cfg.kw)

    for gi, cfg in enumerate(cfgs):
        @pl.when(grp == gi)
        def _(gi=gi, cfg=cfg):
            for uu in range(DIL_UNITS):
                _dil_unit(cfg, step * DIL_UNITS + uu, q_ref[:, pl.ds(uu * tq, tq), :], k_ref, v_ref,
                          bias_refs[gi], acc_ref, m_ref, l_ref, tq=tq, first=gi == 0, last=gi == ng - 1)

    @pl.when((grp == ng - 1) & (step == pl.num_programs(3) - 1))
    def _():
        slabs = [acc_ref[s] for s in range(n_pairs)]
        o_ref[...] = jnp.concatenate(slabs, axis=1).astype(o_ref.dtype)


def _dilated_attention(qkv, rel_bias, batch, seq):
    ng = len(DIL_PAIRS)
    d = qkv.shape[1] * LANES // (3 * ng)
    hd = d // DIL_HEAD_SPLIT
    n_pairs = hd // LANES
    tq = ATT_TQ
    cfgs, segs = [], []
    for win, dil in DIL_PAIRS:
        half = win // (2 * dil)
        ln = seq // dil
        nq = ln // tq
        kw = min(tq + 2 * half, ln)
        variants = _band_variants(half, nq)
        cfgs.append(DilCfg(dil, ln, nq, half, kw, len(variants)))
        segs.append(_band_segs(rel_bias, tq, kw, half, dil, variants, _next_pow2(tq + kw - 1)))
    steps = seq // tq // DIL_UNITS
    cb = DIL_HEAD_SPLIT

    in_specs = [pl.BlockSpec(sg.shape, lambda hf, b, g, s: (0, 0, 0)) for sg in segs]
    in_specs += [pl.BlockSpec((None, n_pairs, DIL_UNITS * tq, LANES),
                              lambda hf, b, g, s: (b, (g * 3) * cb + hf, s, 0)),
                 pl.BlockSpec((None, n_pairs, seq, LANES),
                              lambda hf, b, g, s: (b, (g * 3 + 1) * cb + hf, 0, 0)),
                 pl.BlockSpec((None, n_pairs, seq, LANES),
                              lambda hf, b, g, s: (b, (g * 3 + 2) * cb + hf, 0, 0))]
    scratch = [pltpu.VMEM((c.n_var, 2 * n_pairs, tq, c.kw), F32) for c in cfgs]
    scratch += [pltpu.VMEM((n_pairs, seq, LANES), F32) for _ in range(3)]
    out = pl.pallas_call(
        functools.partial(_dil_attn_kernel, cfgs=tuple(cfgs), tq=tq),
        out_shape=jax.ShapeDtypeStruct((batch, seq, d), BF16),
        grid=(DIL_HEAD_SPLIT, batch, ng, steps),
        in_specs=in_specs,
        out_specs=pl.BlockSpec((None, seq, hd), lambda hf, b, g, s: (b, 0, hf)),
        scratch_shapes=scratch,
        compiler_params=_cparams(4),
        name="dil_attn",
    )(*segs, qkv, qkv, qkv)
    return out.reshape(batch * seq, d)


def _gqa_kernel(sink_ref, seg_ref, q_ref, k_ref, v_ref, o_ref, bias_ref, *, tq, kw, hw, grp):
    b, qi = pl.program_id(0), pl.program_id(1)
    nq = pl.num_programs(1)
    seq_len = k_ref.shape[0]
    n_kv = k_ref.shape[1] // HEAD_DIM

    @pl.when((b == 0) & (qi == 0))
    def _():
        for v in range(3):
            for kh in range(n_kv):
                for gq in range(grp):
                    bias_ref[v, kh, pl.ds(gq * tq, tq), :] = _toeplitz(
                        seg_ref[kh * grp + gq, pl.ds(v, 1), :], tq, kw)

    var = jnp.where(qi == 0, 0, jnp.where(qi == nq - 1, 2, 1))
    ks = pl.multiple_of(jnp.clip(qi * tq - hw, 0, seq_len - kw), hw)
    row = lax.broadcasted_iota(I32, (grp * tq, 1), 0)
    outs = [None] * (n_kv * grp)
    for kh in range(n_kv):
        cs = slice(kh * HEAD_DIM, (kh + 1) * HEAD_DIM)
        qs = jnp.concatenate(
            [q_ref[:, (kh * grp + gq) * HEAD_DIM:(kh * grp + gq + 1) * HEAD_DIM] for gq in range(grp)],
            axis=0) * Q_SCALE
        kk = k_ref[pl.ds(ks, kw), cs]
        vv = v_ref[pl.ds(ks, kw), cs]
        s = lax.dot_general(qs, kk, (((1,), (1,)), ((), ())), preferred_element_type=F32)
        s = s + bias_ref[var, kh]
        sk = jnp.zeros((grp * tq, 1), F32)
        for gq in range(grp):
            sk = jnp.where((row >= gq * tq) & (row < (gq + 1) * tq), sink_ref[kh * grp + gq], sk)
        m = jnp.maximum(jnp.max(s, axis=-1, keepdims=True), sk)
        e = jnp.exp(s - m).astype(BF16)
        va = jnp.concatenate([vv, jnp.ones_like(vv)], axis=1)
        ov = jnp.dot(e, va, preferred_element_type=F32)
        o = ov[:, :HEAD_DIM] / (ov[:, HEAD_DIM:] + jnp.exp(sk - m))
        for gq in range(grp):
            outs[kh * grp + gq] = o[gq * tq:(gq + 1) * tq, :]
    o_ref[...] = jnp.concatenate(outs, axis=1).astype(o_ref.dtype)


def _gqa_attention(qkv, sink, rel_bias, batch, seq):
    t, ncol = qkv.shape
    n_q = sink.shape[0]
    d = n_q * HEAD_DIM
    kvw = GQA_KV_HEADS * HEAD_DIM
    grp = n_q // GQA_KV_HEADS
    tq = hw = GQA_WINDOW
    kw = 3 * GQA_WINDOW
    nq = seq // tq
    variants = _band_variants(hw, nq)
    width = _next_pow2(tq + kw - 1)
    seg = _band_segs(rel_bias, tq, kw, GQA_WINDOW, 1, variants, width)
    qkv_v = qkv.reshape(batch, seq, ncol)
    out = pl.pallas_call(
        functools.partial(_gqa_kernel, tq=tq, kw=kw, hw=hw, grp=grp),
        out_shape=jax.ShapeDtypeStruct((batch, seq, d), BF16),
        grid=(batch, nq),
        in_specs=[pl.BlockSpec(memory_space=pltpu.SMEM),
                  pl.BlockSpec((n_q, 3, width), lambda b, qi: (0, 0, 0)),
                  pl.BlockSpec((None, tq, d), lambda b, qi: (b, qi, 0)),
                  pl.BlockSpec((None, seq, kvw), lambda b, qi: (b, 0, d // kvw)),
                  pl.BlockSpec((None, seq, kvw), lambda b, qi: (b, 0, d // kvw + 1))],
        out_specs=pl.BlockSpec((None, tq, d), lambda b, qi: (b, qi, 0)),
        scratch_shapes=[pltpu.VMEM((3, GQA_KV_HEADS, grp * tq, kw), F32)],
        compiler_params=_cparams(2),
        name="gqa_attn",
    )(sink.astype(F32), seg, qkv_v, qkv_v, qkv_v)
    return out.reshape(t, d)


def _diff_kernel(seg_ref, lam_ref, sub_ref, q_ref, k_ref, v_ref, o_ref, bias_ref, *, tq, lam_init):
    h, qi, b = pl.program_id(0), pl.program_id(1), pl.program_id(2)
    seq_len = k_ref.shape[0]
    nk = seq_len // tq

    @pl.when(b == 0)
    def _():
        for j in range(2):
            for ki in range(nk):
                row = seg_ref[h * 2 + j, pl.ds(ki - qi + nk - 1, 1), :]
                bias_ref[j, :, pl.ds(ki * tq, tq)] = _toeplitz(row, tq, tq)

    lv = lam_ref[...]
    s01 = jnp.sum(lv[0:1, :] * lv[1:2, :], axis=-1, keepdims=True)
    s23 = jnp.sum(lv[2:3, :] * lv[3:4, :], axis=-1, keepdims=True)
    lam = jnp.exp(s01) - jnp.exp(s23) + lam_init

    q = q_ref[...] * Q_SCALE
    k = k_ref[...]
    v = v_ref[...]
    va = jnp.concatenate([v, jnp.ones_like(v)], axis=1)
    map0 = lax.broadcasted_iota(I32, q.shape, 1) < HEAD_DIM
    zero = jnp.zeros_like(q)
    outs = []
    for j in range(2):
        qj = jnp.where(map0, q, zero) if j == 0 else jnp.where(map0, zero, q)
        s = lax.dot_general(qj, k, (((1,), (1,)), ((), ())), preferred_element_type=F32)
        s = s + bias_ref[j]
        m = jnp.max(s, axis=-1, keepdims=True)
        p = jnp.exp(s - m).astype(BF16)
        ov = jnp.dot(p, va, preferred_element_type=F32)
        outs.append(ov[:, :2 * HEAD_DIM] / ov[:, 2 * HEAD_DIM:])
    o = outs[0] - lam * outs[1]
    o = _rms_rows(o, sub_ref[...]) * (1.0 - lam_init)
    o_ref[...] = o.astype(o_ref.dtype)


def _diff_attention(qkv, lam_vecs, subln, slot, rel_bias, batch, seq):
    t, ncol = qkv.shape
    d = ncol // 3
    hd2 = 2 * HEAD_DIM
    n_heads = d // hd2
    tq = DIFF_TQ
    nk = seq // tq
    width = _next_pow2(2 * tq - 1)
    c = np.arange(width)
    rel = np.stack([(dl - (nk - 1)) * tq + c - (tq - 1) for dl in range(2 * nk - 1)])
    seg = _seg_table(rel_bias, rel, np.ones_like(rel, dtype=bool))
    lam_init = 0.8 - 0.6 * math.exp(-0.3 * DIFF_LAYER)
    qkv_v = qkv.reshape(batch, seq, ncol)
    out = pl.pallas_call(
        functools.partial(_diff_kernel, tq=tq, lam_init=lam_init),
        out_shape=jax.ShapeDtypeStruct((batch, seq, d), BF16),
        grid=(n_heads, seq // tq, batch),
        in_specs=[pl.BlockSpec(seg.shape, lambda h, qi, b: (0, 0, 0)),
                  pl.BlockSpec((None,) + lam_vecs.shape[1:], lambda h, qi, b: (slot, 0, 0)),
                  pl.BlockSpec((1, hd2), lambda h, qi, b: (slot, 0)),
                  pl.BlockSpec((None, tq, hd2), lambda h, qi, b: (b, qi, h)),
                  pl.BlockSpec((None, seq, hd2), lambda h, qi, b: (b, 0, n_heads + h)),
                  pl.BlockSpec((None, seq, hd2), lambda h, qi, b: (b, 0, 2 * n_heads + h))],
        out_specs=pl.BlockSpec((None, tq, hd2), lambda h, qi, b: (b, qi, h)),
        scratch_shapes=[pltpu.VMEM((2, tq, seq), F32)],
        compiler_params=_cparams(3),
        name="diff_attn",
    )(seg, lam_vecs, subln, qkv_v, qkv_v, qkv_v)
    return out.reshape(t, d)


def _pack_bf16_pair(lo, hi):
    lo_b = lax.bitcast_convert_type(lo.astype(BF16).astype(F32), U32)
    hi_b = lax.bitcast_convert_type(hi.astype(BF16).astype(F32), U32)
    return (lo_b >> 16) | (hi_b & jnp.uint32(0xFFFF0000))


def _unpack_bf16_pair(w):
    lo = lax.bitcast_convert_type(w << 16, F32)
    hi = lax.bitcast_convert_type(w & jnp.uint32(0xFFFF0000), F32)
    return lo, hi


def _router_kernel(x_ref, g_ref, r_ref, hp_ref, mi_ref, mf_ref, cnt_ref, tri_ref, carry_ref):
    i = pl.program_id(0)
    tm, d = x_ref.shape

    @pl.when(i == 0)
    def _():
        rr = lax.broadcasted_iota(I32, (tm, tm), 0)
        cc = lax.broadcasted_iota(I32, (tm, tm), 1)
        tri_ref[...] = (cc < rr).astype(BF16)
        carry_ref[...] = jnp.zeros_like(carry_ref)

    hn = _rms_rows(x_ref[...], g_ref[...])
    hp_ref[...] = _pack_bf16_pair(hn[:, :d // 2], hn[:, d // 2:])
    logits = jnp.dot(hn, r_ref[...], preferred_element_type=F32, precision=lax.Precision.HIGHEST)
    lane = lax.broadcasted_iota(I32, (tm, LANES), 1)
    logits = jnp.where(lane < N_EXPERTS, logits, -jnp.inf)
    v1 = jnp.max(logits, axis=-1, keepdims=True)
    i1 = jnp.min(jnp.where(logits == v1, lane, LANES), axis=-1, keepdims=True)
    oh1 = lane == i1
    rest = jnp.where(oh1, -jnp.inf, logits)
    v2 = jnp.max(rest, axis=-1, keepdims=True)
    i2 = jnp.min(jnp.where(rest == v2, lane, LANES), axis=-1, keepdims=True)
    oh2 = lane == i2
    e2 = jnp.exp(v2 - v1)
    g1 = 1.0 / (1.0 + e2)
    g2 = e2 / (1.0 + e2)

    sel = (oh1 | oh2)
    before = jnp.dot(tri_ref[...], sel.astype(BF16), preferred_element_type=F32) + carry_ref[...]
    rank1 = jnp.sum(jnp.where(oh1, before, 0.0), axis=-1, keepdims=True).astype(I32)
    rank2 = jnp.sum(jnp.where(oh2, before, 0.0), axis=-1, keepdims=True).astype(I32)
    carry_ref[...] += jnp.sum(sel.astype(F32), axis=0, keepdims=True)

    mi_ref[...] = jnp.where(lane == 0, i1, jnp.where(lane == 1, i2,
                            jnp.where(lane == 2, rank1, jnp.where(lane == 3, rank2, 0))))
    mf_ref[...] = jnp.where(lane == 0, g1, jnp.where(lane == 1, g2, 0.0))
    cnt_ref[...] = carry_ref[...].astype(I32)


def _router(x, g, router):
    t, d = x.shape
    tm = ROW_TILE
    r_pad = jnp.zeros((d, LANES), F32).at[:, :N_EXPERTS].set(router.astype(F32))
    return pl.pallas_call(
        _router_kernel,
        out_shape=(jax.ShapeDtypeStruct((t, d // 2), U32),
                   jax.ShapeDtypeStruct((t, LANES), I32),
                   jax.ShapeDtypeStruct((t, LANES), F32),
                   jax.ShapeDtypeStruct((1, LANES), I32)),
        grid=(t // tm,),
        in_specs=[pl.BlockSpec((tm, d), lambda i: (i, 0)),
                  pl.BlockSpec((1, d), lambda i: (0, 0)),
                  pl.BlockSpec((d, LANES), lambda i: (0, 0))],
        out_specs=(pl.BlockSpec((tm, d // 2), lambda i: (i, 0)),
                   pl.BlockSpec((tm, LANES), lambda i: (i, 0)),
                   pl.BlockSpec((tm, LANES), lambda i: (i, 0)),
                   pl.BlockSpec((1, LANES), lambda i: (0, 0))),
        scratch_shapes=[pltpu.VMEM((tm, tm), BF16), pltpu.VMEM((1, LANES), F32)],
        compiler_params=_cparams(1),
        name="moe_router",
    )(x, g.reshape(1, d), r_pad)


def _row_copy(src, s_row, dst, d_row, sem):
    return pltpu.make_async_copy(src.at[pl.ds(s_row, 1)], dst.at[pl.ds(d_row, 1)], sem)


def _dispatch_kernel(tail_ref, pos_ref, hp_ref, xs_ref, zero_ref, sem):
    i = pl.program_id(0)
    tm = hp_ref.shape[0]
    tg = zero_ref.shape[0]

    @pl.when(i == 0)
    def _():
        zero_ref[...] = jnp.zeros_like(zero_ref)
        for e in range(N_EXPERTS):
            tail = pl.multiple_of(tail_ref[e], SUBLANES)
            pltpu.make_async_copy(zero_ref, xs_ref.at[pl.ds(tail, tg)], sem.at[0]).start()
        for e in range(N_EXPERTS):
            tail = pl.multiple_of(tail_ref[e], SUBLANES)
            pltpu.make_async_copy(zero_ref, xs_ref.at[pl.ds(tail, tg)], sem.at[0]).wait()

    def issue(r, c):
        for k in range(2):
            _row_copy(hp_ref, r, xs_ref, pos_ref[0, k, r], sem.at[0]).start(priority=k)
        return c

    lax.fori_loop(0, tm, issue, 0, unroll=DMA_UNROLL)

    def drain(r, c):
        for k in range(2):
            _row_copy(hp_ref, 0, xs_ref, 0, sem.at[0]).wait()
        return c

    lax.fori_loop(0, tm, drain, 0)


def _dispatch(hp, pos, tails, n_rows):
    t, dw = hp.shape
    tm = DMA_TILE
    return pl.pallas_call(
        _dispatch_kernel,
        out_shape=jax.ShapeDtypeStruct((n_rows, dw), U32),
        grid_spec=pltpu.PrefetchScalarGridSpec(
            num_scalar_prefetch=1,
            grid=(t // tm,),
            in_specs=[pl.BlockSpec((1, 2, tm), lambda i, tl: (i, 0, 0), memory_space=pltpu.SMEM),
                      pl.BlockSpec((tm, dw), lambda i, tl: (i, 0))],
            out_specs=pl.BlockSpec(memory_space=pl.ANY),
            scratch_shapes=[pltpu.VMEM((MOE_TILE, dw), U32), pltpu.SemaphoreType.DMA((1,))]),
        compiler_params=pltpu.CompilerParams(dimension_semantics=("arbitrary",),
                                             vmem_limit_bytes=VMEM_LIMIT, has_side_effects=True),
        name="moe_dispatch",
    )(tails, pos, hp)


def _expert_kernel(te_ref, tb_ref, nv_ref, xs_ref, wg_ref, wu_ref, wd_ref, ys_ref, hn_ref, acc_ref):
    i, f = pl.program_id(0), pl.program_id(1)
    half = xs_ref.shape[1]

    @pl.when(i < nv_ref[0])
    def _():
        @pl.when(f == 0)
        def _():
            lo, hi = _unpack_bf16_pair(xs_ref[...])
            hn_ref[:, :half] = lo.astype(BF16)
            hn_ref[:, half:] = hi.astype(BF16)
            acc_ref[...] = jnp.zeros_like(acc_ref)

        h = hn_ref[...]
        gate = jnp.dot(h, wg_ref[...].astype(BF16), preferred_element_type=F32)
        up = jnp.dot(h, wu_ref[...].astype(BF16), preferred_element_type=F32)
        a = _silu_mul(gate, up).astype(BF16)
        acc_ref[...] += jnp.dot(a, wd_ref[...].astype(BF16), preferred_element_type=F32)

        @pl.when(f == pl.num_programs(1) - 1)
        def _():
            acc = acc_ref[...]
            ys_ref[...] = _pack_bf16_pair(acc[:, :half], acc[:, half:])


def _experts(xs, w_gate_up, w_down, slot, tile_expert, tile_block, n_valid, n_tiles):
    n_rows, dw = xs.shape
    d = 2 * dw
    ff = w_down.shape[-2]
    tg, tf = MOE_TILE, FF_TILE
    nf = ff // tf

    def fidx(i, f, nv):
        return jnp.where(i < nv[0], f, nf - 1)

    return pl.pallas_call(
        _expert_kernel,
        out_shape=jax.ShapeDtypeStruct((n_rows, dw), U32),
        grid_spec=pltpu.PrefetchScalarGridSpec(
            num_scalar_prefetch=3,
            grid=(n_tiles, nf),
            in_specs=[pl.BlockSpec((tg, dw), lambda i, f, te, tb, nv: (tb[i], 0)),
                      pl.BlockSpec((None, None, d, tf),
                                   lambda i, f, te, tb, nv: (slot, te[i], 0, fidx(i, f, nv))),
                      pl.BlockSpec((None, None, d, tf),
                                   lambda i, f, te, tb, nv: (slot, te[i], 0, nf + fidx(i, f, nv))),
                      pl.BlockSpec((None, None, tf, d),
                                   lambda i, f, te, tb, nv: (slot, te[i], fidx(i, f, nv), 0))],
            out_specs=pl.BlockSpec((tg, dw), lambda i, f, te, tb, nv: (tb[i], 0)),
            scratch_shapes=[pltpu.VMEM((tg, d), BF16), pltpu.VMEM((tg, d), F32)]),
        compiler_params=_cparams(2),
        name="moe_experts",
    )(tile_expert, tile_block, n_valid, xs, w_gate_up, w_gate_up, w_down)


def _combine_kernel(pos_ref, x_ref, mf_ref, gf_ref, ys_ref, o_ref, buf_ref, sem, *, final_norm):
    tm = x_ref.shape[0]
    half = buf_ref.shape[2]

    def issue(r, c):
        for k in range(2):
            pltpu.make_async_copy(ys_ref.at[pl.ds(pos_ref[0, k, r], 1)],
                                  buf_ref.at[k, pl.ds(r, 1)], sem.at[0]).start(priority=k)
        return c

    lax.fori_loop(0, tm, issue, 0, unroll=DMA_UNROLL)

    def drain(r, c):
        for k in range(2):
            pltpu.make_async_copy(ys_ref.at[pl.ds(0, 1)], buf_ref.at[k, pl.ds(0, 1)], sem.at[0]).wait()
        return c

    lax.fori_loop(0, tm, drain, 0)

    gates = mf_ref[...]
    out_lo = x_ref[:, :half]
    out_hi = x_ref[:, half:]
    for k in range(2):
        lo, hi = _unpack_bf16_pair(buf_ref[k])
        gk = gates[:, k:k + 1]
        out_lo = out_lo + gk * lo
        out_hi = out_hi + gk * hi
    out = jnp.concatenate([out_lo, out_hi], axis=1)
    if final_norm:
        out = _rms_rows(out, gf_ref[...])
    o_ref[...] = out


def _combine(x, mf, ys, pos, final_g):
    t, d = x.shape
    tm = DMA_TILE
    final_norm = final_g is not None
    gf = (final_g if final_norm else jnp.ones((d,), F32)).reshape(1, d).astype(F32)
    return pl.pallas_call(
        functools.partial(_combine_kernel, final_norm=final_norm),
        out_shape=jax.ShapeDtypeStruct((t, d), F32),
        grid=(t // tm,),
        in_specs=[pl.BlockSpec((1, 2, tm), lambda i: (i, 0, 0), memory_space=pltpu.SMEM),
                  pl.BlockSpec((tm, d), lambda i: (i, 0)),
                  pl.BlockSpec((tm, LANES), lambda i: (i, 0)),
                  pl.BlockSpec((1, d), lambda i: (0, 0)),
                  pl.BlockSpec(memory_space=pl.ANY)],
        out_specs=pl.BlockSpec((tm, d), lambda i: (i, 0)),
        scratch_shapes=[pltpu.VMEM((2, tm, d // 2), U32), pltpu.SemaphoreType.DMA((1,))],
        compiler_params=_cparams(1),
        name="moe_combine",
    )(pos, x, mf, gf, ys)


def _moe(x, g, router, w_gate_up, w_down, slot, final_g=None):
    t, d = x.shape
    tg = MOE_TILE
    hp, mi, mf, cnt = _router(x, g, router)

    counts = cnt[0, :N_EXPERTS]
    tiles_e = (counts + tg - 1) // tg
    tile_start = jnp.cumsum(tiles_e) - tiles_e
    row_start = tile_start * tg
    n_valid = jnp.sum(tiles_e).astype(I32)
    n_tiles = 2 * t // tg + N_EXPERTS
    n_rows = (n_tiles + 1) * tg
    tidx = jnp.minimum(jnp.arange(n_tiles, dtype=I32), n_valid - 1)
    tile_expert = (jnp.sum(tidx[:, None] >= tile_start[None, :], axis=1) - 1).astype(I32)
    tails = (row_start + counts // SUBLANES * SUBLANES).astype(I32)
    experts = mi[:, 0:2]
    pos = (row_start[experts] + mi[:, 2:4]).astype(I32)
    pos = pos.T.reshape(2, t // DMA_TILE, DMA_TILE).transpose(1, 0, 2)

    xs = _dispatch(hp, pos, tails, n_rows)
    ys = _experts(xs, w_gate_up, w_down, slot, tile_expert, tidx, n_valid.reshape(1), n_tiles)
    return _combine(x, mf, ys, pos, final_g)


def kernel(x, rel_bias, norm_mix, norm_ffn, norm_final, a_w_in, a_w_group, a_scale, a_w_out,
           b_w_in, b_w_out, c_w_in, c_lambda, c_subln, c_w_out, d_w_in, d_sink, d_w_out,
           f_w_gate_up, f_w_down, m_router, m_w_gate_up, m_w_down):
    batch, seq, d = x.shape
    h = x.reshape(batch * seq, d)

    u = _norm_matmul(h, norm_mix[0], a_w_in, 0, F32)
    y = _pool_mix(u, a_w_group, a_scale, 0, batch, seq)
    h = _matmul_res(y, a_w_out, 0, h)
    h = _ffn(h, norm_ffn[0], f_w_gate_up, f_w_down, 0)

    qkv = _dil_proj(h.reshape(batch, seq, d), norm_mix[1], b_w_in, 0, tuple(p[1] for p in DIL_PAIRS))
    o = _dilated_attention(qkv, rel_bias, batch, seq)
    h = _matmul_res(o, b_w_out, 0, h)
    h = _moe(h, norm_ffn[1], m_router[0], m_w_gate_up, m_w_down, 0)

    qkv = _norm_matmul(h, norm_mix[2], c_w_in, 0, BF16)
    o = _diff_attention(qkv, c_lambda, c_subln, 0, rel_bias, batch, seq)
    h = _matmul_res(o, c_w_out, 0, h)
    h = _ffn(h, norm_ffn[2], f_w_gate_up, f_w_down, 1)

    qkv = _norm_matmul(h, norm_mix[3], d_w_in, 0, BF16)
    o = _gqa_attention(qkv, d_sink[0], rel_bias, batch, seq)
    h = _matmul_res(o, d_w_out, 0, h)
    h = _moe(h, norm_ffn[3], m_router[1], m_w_gate_up, m_w_down, 1, final_g=norm_final)
    return h.reshape(batch, seq, d)
```

```python
import collections
import functools
import math

import numpy as np
import jax
import jax.numpy as jnp
from jax import lax
from jax.experimental import pallas as pl
from jax.experimental.pallas import tpu as pltpu

F32 = jnp.float32
BF16 = jnp.bfloat16
I32 = jnp.int32

RMS_EPS = 1e-6
HEAD_DIM = 64
N_BUCKETS = 32
MAX_DISTANCE = 1024
POOL_WINDOWS = (2, 4, 8, 16)
DIL_PAIRS = ((128, 1), (512, 4), (2048, 16))
Q_SCALE = HEAD_DIM ** -0.5
GQA_KV_HEADS = 4
GQA_WINDOW = 128
N_EXPERTS = 8
DIFF_LAYER = 2

LANES = 128
SUBLANES = 8
VMEM_LIMIT = 56 * 1024 * 1024
ROW_TILE = 1024
FF_TILE = 512
MOE_TILE = 1024
SEG_TILE = 512
ATT_TQ = 128
DIL_UNITS = 2
DIL_HEAD_SPLIT = 2
DIFF_TQ = 512
MASK_VALUE = -1e30


def _cparams(n_axes, vmem=VMEM_LIMIT):
    return pltpu.CompilerParams(dimension_semantics=("arbitrary",) * n_axes,
                                vmem_limit_bytes=vmem)


def _next_pow2(n):
    return 1 << (n - 1).bit_length()


def _rms_rows(x, g):
    ms = jnp.mean(x * x, axis=-1, keepdims=True)
    return x * lax.rsqrt(ms + RMS_EPS) * g


def _norm_matmul_kernel(x_ref, g_ref, w_ref, o_ref, hn_ref):
    @pl.when(pl.program_id(1) == 0)
    def _():
        hn_ref[...] = _rms_rows(x_ref[...], g_ref[...]).astype(BF16)

    o_ref[...] = jnp.dot(hn_ref[...], w_ref[...].astype(BF16),
                         preferred_element_type=F32).astype(o_ref.dtype)


def _norm_matmul(x, g, w, slot, out_dtype):
    t, d = x.shape
    n = w.shape[-1]
    tm = ROW_TILE
    tn = next(c for c in (1024, 768, 512, 256, 128) if n % c == 0)
    return pl.pallas_call(
        _norm_matmul_kernel,
        out_shape=jax.ShapeDtypeStruct((t, n), out_dtype),
        grid=(t // tm, n // tn),
        in_specs=[pl.BlockSpec((tm, d), lambda i, j: (i, 0)),
                  pl.BlockSpec((1, d), lambda i, j: (0, 0)),
                  pl.BlockSpec((None, d, tn), lambda i, j: (slot, 0, j))],
        out_specs=pl.BlockSpec((tm, tn), lambda i, j: (i, j)),
        scratch_shapes=[pltpu.VMEM((tm, d), BF16)],
        compiler_params=_cparams(2),
        name="norm_matmul",
    )(x, g.reshape(1, d), w)


def _matmul_res_kernel(a_ref, w_ref, x_ref, o_ref, wb_ref):
    @pl.when(pl.program_id(0) == 0)
    def _():
        wb_ref[...] = w_ref[...].astype(BF16)

    o_ref[...] = x_ref[...] + jnp.dot(a_ref[...], wb_ref[...], preferred_element_type=F32)


def _matmul_res(a, w, slot, x):
    t, k = a.shape
    d = w.shape[-1]
    tm = ROW_TILE
    return pl.pallas_call(
        _matmul_res_kernel,
        out_shape=jax.ShapeDtypeStruct((t, d), F32),
        grid=(t // tm,),
        in_specs=[pl.BlockSpec((tm, k), lambda i: (i, 0)),
                  pl.BlockSpec((None, k, d), lambda i: (slot, 0, 0)),
                  pl.BlockSpec((tm, d), lambda i: (i, 0))],
        out_specs=pl.BlockSpec((tm, d), lambda i: (i, 0)),
        scratch_shapes=[pltpu.VMEM((k, d), BF16)],
        compiler_params=_cparams(1),
        name="matmul_res",
    )(a, w, x)


def _silu_mul(g, u):
    return g * (1.0 / (1.0 + jnp.exp(-g))) * u


def _ffn_kernel(x_ref, g_ref, wg_ref, wu_ref, wd_ref, o_ref, hn_ref, acc_ref):
    f = pl.program_id(1)

    @pl.when(f == 0)
    def _():
        hn_ref[...] = _rms_rows(x_ref[...], g_ref[...]).astype(BF16)
        acc_ref[...] = jnp.zeros_like(acc_ref)

    h = hn_ref[...]
    gate = jnp.dot(h, wg_ref[...].astype(BF16), preferred_element_type=F32)
    up = jnp.dot(h, wu_ref[...].astype(BF16), preferred_element_type=F32)
    a = _silu_mul(gate, up).astype(BF16)
    acc_ref[...] += jnp.dot(a, wd_ref[...].astype(BF16), preferred_element_type=F32)

    @pl.when(f == pl.num_programs(1) - 1)
    def _():
        o_ref[...] = x_ref[...] + acc_ref[...]


def _ffn(x, g, w_gate_up, w_down, slot):
    t, d = x.shape
    ff = w_down.shape[-2]
    tm, tf = ROW_TILE, FF_TILE
    nf = ff // tf
    return pl.pallas_call(
        _ffn_kernel,
        out_shape=jax.ShapeDtypeStruct((t, d), F32),
        grid=(t // tm, nf),
        in_specs=[pl.BlockSpec((tm, d), lambda i, f: (i, 0)),
                  pl.BlockSpec((1, d), lambda i, f: (0, 0)),
                  pl.BlockSpec((None, d, tf), lambda i, f: (slot, 0, f)),
                  pl.BlockSpec((None, d, tf), lambda i, f: (slot, 0, nf + f)),
                  pl.BlockSpec((None, tf, d), lambda i, f: (slot, f, 0))],
        out_specs=pl.BlockSpec((tm, d), lambda i, f: (i, 0)),
        scratch_shapes=[pltpu.VMEM((tm, d), BF16), pltpu.VMEM((tm, d), F32)],
        compiler_params=_cparams(2),
        name="ffn",
    )(x, g.reshape(1, d), w_gate_up, w_gate_up, w_down)


POOL_PAD = 16


def _pool_kernel(u_ref, wg_ref, sc_ref, o_ref, pad_ref):
    s, c = u_ref.shape
    grp = pl.program_id(1)
    pad_ref[pl.ds(0, POOL_PAD), :] = jnp.zeros((POOL_PAD, c), F32)
    pad_ref[pl.ds(POOL_PAD + s, POOL_PAD), :] = jnp.zeros((POOL_PAD, c), F32)
    pad_ref[pl.ds(POOL_PAD, s), :] = u_ref[...]
    wb = wg_ref[...].astype(BF16)
    chunk = 256

    for gi, w in enumerate(POOL_WINDOWS):
        @pl.when(grp == gi)
        def _(w=w):
            for c0 in range(0, s, chunk):
                acc = None
                for dlt in range(-(w // 2), w - w // 2):
                    piece = pad_ref[pl.ds(POOL_PAD + c0 + dlt, chunk), :]
                    acc = piece if acc is None else acc + piece
                pos = c0 + lax.broadcasted_iota(I32, (chunk, 1), 0)
                lo = jnp.maximum(pos - w // 2, 0)
                hi = jnp.minimum(pos + (w - w // 2), s)
                cnt = (hi - lo).astype(F32)
                mixed = acc / cnt - u_ref[pl.ds(c0, chunk), :]
                y = jnp.dot(mixed.astype(BF16), wb, preferred_element_type=F32)
                o_ref[pl.ds(c0, chunk), :] = (y * sc_ref[...]).astype(o_ref.dtype)


def _pool_mix(u, w_group, scale, slot, batch, seq):
    d = u.shape[1]
    ng = len(POOL_WINDOWS)
    c = d // ng
    u3 = u.reshape(batch, seq, d)
    out = pl.pallas_call(
        _pool_kernel,
        out_shape=jax.ShapeDtypeStruct((batch, seq, d), BF16),
        grid=(batch, ng),
        in_specs=[pl.BlockSpec((None, seq, c), lambda b, g: (b, 0, g)),
                  pl.BlockSpec((None, None, c, c), lambda b, g: (slot, g, 0, 0)),
                  pl.BlockSpec((1, c), lambda b, g: (slot, g))],
        out_specs=pl.BlockSpec((None, seq, c), lambda b, g: (b, 0, g)),
        scratch_shapes=[pltpu.VMEM((seq + 2 * POOL_PAD, c), F32)],
        compiler_params=_cparams(2),
        name="pool_mix",
    )(u3, w_group, scale)
    return out.reshape(batch * seq, d)


def _rel_bucket_np(rel):
    half = N_BUCKETS // 2
    max_exact = half // 2
    n = np.abs(rel)
    ratio = np.log(np.maximum(n, 1).astype(np.float32) / np.float32(max_exact))
    big = max_exact + (ratio / np.float32(math.log(MAX_DISTANCE / max_exact))
                       * np.float32(half - max_exact)).astype(np.int32)
    big = np.minimum(big, half - 1)
    return np.where(rel > 0, half, 0) + np.where(n < max_exact, n, big)


def _seg_table(rel_bias, rel, valid):
    bucket = jnp.asarray(_rel_bucket_np(rel).astype(np.int32))
    vals = jnp.take(rel_bias.astype(F32), bucket, axis=0)
    vals = jnp.where(jnp.asarray(valid)[..., None], vals, MASK_VALUE)
    return jnp.transpose(vals, (2, 0, 1))


def _toeplitz(seg_row, rows, cols):
    w = seg_row.shape[1]
    full = jnp.broadcast_to(seg_row, (rows, w))
    rolled = pltpu.roll(full, w - (rows - 1), 1, stride=1, stride_axis=0)
    return rolled[:, :cols]


def _band_variants(hw, n_tiles):
    if n_tiles == 1:
        return (0,)
    return (0, -hw, -2 * hw)


def _band_segs(rel_bias, tq, kw, half, dil, variants, width):
    c = np.arange(width)
    rel = np.stack([r0 + c - (tq - 1) for r0 in variants])
    valid = (np.abs(rel) <= half) & (c[None, :] < tq + kw - 1)
    return _seg_table(rel_bias, rel * dil, valid)


def _band_window(qi, nq, tq, hw, kw, ln):
    if nq == 1:
        return 0, 0
    var = jnp.where(qi == 0, 0, jnp.where(qi == nq - 1, 2, 1))
    return var, jnp.clip(qi * tq - hw, 0, ln - kw)


DilCfg = collections.namedtuple("DilCfg", "dil ln nq hw kw n_var")


def _dil_proj_kernel(x_ref, g_ref, w_ref, o_ref, slab_ref, hn_ref, *, dils):
    grp, c = pl.program_id(1), pl.program_id(2)
    seq, d = x_ref.shape
    n_slabs = d // LANES
    chunk = 256

    @pl.when((grp == 0) & (c == 0))
    def _():
        for c0 in range(0, seq, chunk):
            hn = _rms_rows(x_ref[pl.ds(c0, chunk), :], g_ref[...])
            for s in range(n_slabs):
                slab_ref[s, pl.ds(c0, chunk), :] = hn[:, s * LANES:(s + 1) * LANES]

    for gi, dil in enumerate(dils):
        @pl.when((grp == gi) & (c == 0))
        def _(dil=dil):
            ln = seq // dil
            for r in range(dil):
                for c0 in range(0, ln, chunk):
                    n = min(chunk, ln)
                    rows = [slab_ref[s, pl.ds(r + c0 * dil, n, stride=dil), :] for s in range(n_slabs)]
                    hn_ref[pl.ds(r * ln + c0, n), :] = jnp.concatenate(rows, axis=1).astype(BF16)

    res = jnp.dot(hn_ref[...], w_ref[...].astype(BF16), preferred_element_type=F32)
    res = res * jnp.where(c < pl.num_programs(2) // 3, Q_SCALE, 1.0)
    for s in range(o_ref.shape[0]):
        o_ref[s] = res[:, s * LANES:(s + 1) * LANES].astype(o_ref.dtype)


def _dil_proj(x3, g, w_in, slot, dils):
    batch, seq, d = x3.shape
    n = w_in.shape[-1]
    tn = FF_TILE
    per_group = n // len(dils) // tn
    spt = tn // LANES
    return pl.pallas_call(
        functools.partial(_dil_proj_kernel, dils=dils),
        out_shape=jax.ShapeDtypeStruct((batch, n // LANES, seq, LANES), BF16),
        grid=(batch, len(dils), per_group),
        in_specs=[pl.BlockSpec((None, seq, d), lambda b, g, c: (b, 0, 0)),
                  pl.BlockSpec((1, d), lambda b, g, c: (0, 0)),
                  pl.BlockSpec((None, d, tn), lambda b, g, c: (slot, 0, g * per_group + c))],
        out_specs=pl.BlockSpec((None, spt, seq, LANES), lambda b, g, c: (b, g * per_group + c, 0, 0)),
        scratch_shapes=[pltpu.VMEM((d // LANES, seq, LANES), F32), pltpu.VMEM((seq, d), BF16)],
        compiler_params=_cparams(3),
        name="dil_proj",
    )(x3, g.reshape(1, d), w_in)


def _dil_unit(cfg, unit, q4, k_ref, v_ref, bias_ref, acc_ref, m_ref, l_ref, *, tq, first, last):
    dil, ln, nq, kw = cfg.dil, cfg.ln, cfg.nq, cfg.kw
    n_pairs = q4.shape[0]
    if nq == 1:
        r, qi = unit, 0
    elif dil == 1:
        r, qi = 0, unit
    else:
        r, qi = unit // nq, unit % nq
    var, ks = _band_window(qi, nq, tq, cfg.hw, kw, ln)
    krow = pl.multiple_of(r * ln + ks, cfg.hw)
    if dil == 1:
        rows = pl.ds(pl.multiple_of(qi * tq, tq), tq)
    else:
        rows = pl.ds(qi * tq * dil + r, tq, stride=dil)

    even = lax.broadcasted_iota(I32, (n_pairs, tq, LANES), 2) < HEAD_DIM
    zero = jnp.zeros_like(q4)
    q8 = jnp.concatenate([jnp.where(even, q4, zero), jnp.where(even, zero, q4)], axis=0)
    k4 = k_ref[:, pl.ds(krow, kw), :]
    v4 = v_ref[:, pl.ds(krow, kw), :]
    k8 = jnp.concatenate([k4, k4], axis=0)
    va = jnp.concatenate([v4, jnp.ones_like(v4)], axis=2)
    v8 = jnp.concatenate([va, va], axis=0)
    s = lax.dot_general(q8, k8, (((2,), (2,)), ((0,), (0,))), preferred_element_type=F32)
    s = s + bias_ref[var]
    m8 = jnp.max(s, axis=-1, keepdims=True)
    p = jnp.exp(s - m8).astype(BF16)
    ov = lax.dot_general(p, v8, (((2,), (1,)), ((0,), (0,))), preferred_element_type=F32)
    o_c = jnp.where(even, ov[:n_pairs, :, :LANES], ov[n_pairs:, :, :LANES])
    l_c = jnp.where(even, ov[:n_pairs, :, LANES:], ov[n_pairs:, :, LANES:])
    m_c = jnp.where(even, m8[:n_pairs], m8[n_pairs:])
    if first:
        m_n, l_n, acc_n = m_c, l_c, o_c
    else:
        m_o = jnp.stack([m_ref[pp, rows, :] for pp in range(n_pairs)])
        l_o = jnp.stack([l_ref[pp, rows, :] for pp in range(n_pairs)])
        acc_o = jnp.stack([acc_ref[pp, rows, :] for pp in range(n_pairs)])
        m_n = jnp.maximum(m_o, m_c)
        a_o = jnp.exp(m_o - m_n)
        a_c = jnp.exp(m_c - m_n)
        l_n = a_o * l_o + a_c * l_c
        acc_n = a_o * acc_o + a_c * o_c
    if last:
        acc_n = acc_n / l_n
    for pp in range(n_pairs):
        acc_ref[pp, rows, :] = acc_n[pp]
        if not last:
            m_ref[pp, rows, :] = m_n[pp]
            l_ref[pp, rows, :] = l_n[pp]


def _dil_attn_kernel(*refs, cfgs, tq):
    ng = len(cfgs)
    seg_refs = refs[:ng]
    q_ref, k_ref, v_ref, o_ref = refs[ng:ng + 4]
    bias_refs = refs[ng + 4:2 * ng + 4]
    acc_ref, m_ref, l_ref = refs[2 * ng + 4:]
    hf, b, grp, step = (pl.program_id(i) for i in range(4))
    n_pairs = q_ref.shape[0]

    @pl.when((b == 0) & (grp == 0) & (step == 0))
    def _():
        for gi, cfg in enumerate(cfgs):
            for v in range(cfg.n_var):
                for odd in range(2):
                    for pp in range(n_pairs):
                        head = hf * 2 * n_pairs + 2 * pp + odd
                        row = seg_refs[gi][head, pl.ds(v, 1), :]
                        bias_refs[gi][v, odd * n_pairs + pp] = _toeplitz(row, tq, cfg.kw)

    for gi, cfg in enumerate(cfgs):
        @pl.when(grp == gi)
        def _(gi=gi, cfg=cfg):
            for uu in range(DIL_UNITS):
                _dil_unit(cfg, step * DIL_UNITS + uu, q_ref[:, pl.ds(uu * tq, tq), :], k_ref, v_ref,
                          bias_refs[gi], acc_ref, m_ref, l_ref, tq=tq, first=gi == 0, last=gi == ng - 1)

    @pl.when((grp == ng - 1) & (step == pl.num_programs(3) - 1))
    def _():
        slabs = [acc_ref[s] for s in range(n_pairs)]
        o_ref[...] = jnp.concatenate(slabs, axis=1).astype(o_ref.dtype)


def _dilated_attention(qkv, rel_bias, batch, seq):
    ng = len(DIL_PAIRS)
    d = qkv.shape[1] * LANES // (3 * ng)
    hd = d // DIL_HEAD_SPLIT
    n_pairs = hd // LANES
    tq = ATT_TQ
    cfgs, segs = [], []
    for win, dil in DIL_PAIRS:
        half = win // (2 * dil)
        ln = seq // dil
        nq = ln // tq
        kw = min(tq + 2 * half, ln)
        variants = _band_variants(half, nq)
        cfgs.append(DilCfg(dil, ln, nq, half, kw, len(variants)))
        segs.append(_band_segs(rel_bias, tq, kw, half, dil, variants, _next_pow2(tq + kw - 1)))
    steps = seq // tq // DIL_UNITS
    cb = DIL_HEAD_SPLIT

    in_specs = [pl.BlockSpec(sg.shape, lambda hf, b, g, s: (0, 0, 0)) for sg in segs]
    in_specs += [pl.BlockSpec((None, n_pairs, DIL_UNITS * tq, LANES),
                              lambda hf, b, g, s: (b, (g * 3) * cb + hf, s, 0)),
                 pl.BlockSpec((None, n_pairs, seq, LANES),
                              lambda hf, b, g, s: (b, (g * 3 + 1) * cb + hf, 0, 0)),
                 pl.BlockSpec((None, n_pairs, seq, LANES),
                              lambda hf, b, g, s: (b, (g * 3 + 2) * cb + hf, 0, 0))]
    scratch = [pltpu.VMEM((c.n_var, 2 * n_pairs, tq, c.kw), F32) for c in cfgs]
    scratch += [pltpu.VMEM((n_pairs, seq, LANES), F32) for _ in range(3)]
    out = pl.pallas_call(
        functools.partial(_dil_attn_kernel, cfgs=tuple(cfgs), tq=tq),
        out_shape=jax.ShapeDtypeStruct((batch, seq, d), BF16),
        grid=(DIL_HEAD_SPLIT, batch, ng, steps),
        in_specs=in_specs,
        out_specs=pl.BlockSpec((None, seq, hd), lambda hf, b, g, s: (b, 0, hf)),
        scratch_shapes=scratch,
        compiler_params=_cparams(4),
        name="dil_attn",
    )(*segs, qkv, qkv, qkv)
    return out.reshape(batch * seq, d)


def _gqa_kernel(sink_ref, seg_ref, q_ref, k_ref, v_ref, o_ref, bias_ref, *, tq, kw, hw, grp):
    b, qi = pl.program_id(0), pl.program_id(1)
    nq = pl.num_programs(1)
    seq_len = k_ref.shape[0]
    n_kv = k_ref.shape[1] // HEAD_DIM

    @pl.when((b == 0) & (qi == 0))
    def _():
        for v in range(3):
            for kh in range(n_kv):
                for gq in range(grp):
                    bias_ref[v, kh, pl.ds(gq * tq, tq), :] = _toeplitz(
                        seg_ref[kh * grp + gq, pl.ds(v, 1), :], tq, kw)

    var = jnp.where(qi == 0, 0, jnp.where(qi == nq - 1, 2, 1))
    ks = pl.multiple_of(jnp.clip(qi * tq - hw, 0, seq_len - kw), hw)
    row = lax.broadcasted_iota(I32, (grp * tq, 1), 0)
    outs = [None] * (n_kv * grp)
    for kh in range(n_kv):
        cs = slice(kh * HEAD_DIM, (kh + 1) * HEAD_DIM)
        qs = jnp.concatenate(
            [q_ref[:, (kh * grp + gq) * HEAD_DIM:(kh * grp + gq + 1) * HEAD_DIM] for gq in range(grp)],
            axis=0) * Q_SCALE
        kk = k_ref[pl.ds(ks, kw), cs]
        vv = v_ref[pl.ds(ks, kw), cs]
        s = lax.dot_general(qs, kk, (((1,), (1,)), ((), ())), preferred_element_type=F32)
        s = s + bias_ref[var, kh]
        sk = jnp.zeros((grp * tq, 1), F32)
        for gq in range(grp):
            sk = jnp.where((row >= gq * tq) & (row < (gq + 1) * tq), sink_ref[kh * grp + gq], sk)
        m = jnp.maximum(jnp.max(s, axis=-1, keepdims=True), sk)
        e = jnp.exp(s - m).astype(BF16)
        va = jnp.concatenate([vv, jnp.ones_like(vv)], axis=1)
        ov = jnp.dot(e, va, preferred_element_type=F32)
        o = ov[:, :HEAD_DIM] / (ov[:, HEAD_DIM:] + jnp.exp(sk - m))
        for gq in range(grp):
            outs[kh * grp + gq] = o[gq * tq:(gq + 1) * tq, :]
    o_ref[...] = jnp.concatenate(outs, axis=1).astype(o_ref.dtype)


def _gqa_attention(qkv, sink, rel_bias, batch, seq):
    t, ncol = qkv.shape
    n_q = sink.shape[0]
    d = n_q * HEAD_DIM
    kvw = GQA_KV_HEADS * HEAD_DIM
    grp = n_q // GQA_KV_HEADS
    tq = hw = GQA_WINDOW
    kw = 3 * GQA_WINDOW
    nq = seq // tq
    variants = _band_variants(hw, nq)
    width = _next_pow2(tq + kw - 1)
    seg = _band_segs(rel_bias, tq, kw, GQA_WINDOW, 1, variants, width)
    qkv_v = qkv.reshape(batch, seq, ncol)
    out = pl.pallas_call(
        functools.partial(_gqa_kernel, tq=tq, kw=kw, hw=hw, grp=grp),
        out_shape=jax.ShapeDtypeStruct((batch, seq, d), BF16),
        grid=(batch, nq),
        in_specs=[pl.BlockSpec(memory_space=pltpu.SMEM),
                  pl.BlockSpec((n_q, 3, width), lambda b, qi: (0, 0, 0)),
                  pl.BlockSpec((None, tq, d), lambda b, qi: (b, qi, 0)),
                  pl.BlockSpec((None, seq, kvw), lambda b, qi: (b, 0, d // kvw)),
                  pl.BlockSpec((None, seq, kvw), lambda b, qi: (b, 0, d // kvw + 1))],
        out_specs=pl.BlockSpec((None, tq, d), lambda b, qi: (b, qi, 0)),
        scratch_shapes=[pltpu.VMEM((3, GQA_KV_HEADS, grp * tq, kw), F32)],
        compiler_params=_cparams(2),
        name="gqa_attn",
    )(sink.astype(F32), seg, qkv_v, qkv_v, qkv_v)
    return out.reshape(t, d)


def _diff_kernel(seg_ref, lam_ref, sub_ref, q_ref, k_ref, v_ref, o_ref, bias_ref, *, tq, lam_init):
    h, qi, b = pl.program_id(0), pl.program_id(1), pl.program_id(2)
    seq_len = k_ref.shape[0]
    nk = seq_len // tq

    @pl.when(b == 0)
    def _():
        for j in range(2):
            for ki in range(nk):
                row = seg_ref[h * 2 + j, pl.ds(ki - qi + nk - 1, 1), :]
                bias_ref[j, :, pl.ds(ki * tq, tq)] = _toeplitz(row, tq, tq)

    lv = lam_ref[...]
    s01 = jnp.sum(lv[0:1, :] * lv[1:2, :], axis=-1, keepdims=True)
    s23 = jnp.sum(lv[2:3, :] * lv[3:4, :], axis=-1, keepdims=True)
    lam = jnp.exp(s01) - jnp.exp(s23) + lam_init

    q = q_ref[...] * Q_SCALE
    k = k_ref[...]
    v = v_ref[...]
    va = jnp.concatenate([v, jnp.ones_like(v)], axis=1)
    map0 = lax.broadcasted_iota(I32, q.shape, 1) < HEAD_DIM
    zero = jnp.zeros_like(q)
    outs = []
    for j in range(2):
        qj = jnp.where(map0, q, zero) if j == 0 else jnp.where(map0, zero, q)
        s = lax.dot_general(qj, k, (((1,), (1,)), ((), ())), preferred_element_type=F32)
        s = s + bias_ref[j]
        m = jnp.max(s, axis=-1, keepdims=True)
        p = jnp.exp(s - m).astype(BF16)
        ov = jnp.dot(p, va, preferred_element_type=F32)
        outs.append(ov[:, :2 * HEAD_DIM] / ov[:, 2 * HEAD_DIM:])
    o = outs[0] - lam * outs[1]
    o = _rms_rows(o, sub_ref[...]) * (1.0 - lam_init)
    o_ref[...] = o.astype(o_ref.dtype)


def _diff_attention(qkv, lam_vecs, subln, slot, rel_bias, batch, seq):
    t, ncol = qkv.shape
    d = ncol // 3
    hd2 = 2 * HEAD_DIM
    n_heads = d // hd2
    tq = DIFF_TQ
    nk = seq // tq
    width = _next_pow2(2 * tq - 1)
    c = np.arange(width)
    rel = np.stack([(dl - (nk - 1)) * tq + c - (tq - 1) for dl in range(2 * nk - 1)])
    seg = _seg_table(rel_bias, rel, np.ones_like(rel, dtype=bool))
    lam_init = 0.8 - 0.6 * math.exp(-0.3 * DIFF_LAYER)
    qkv_v = qkv.reshape(batch, seq, ncol)
    out = pl.pallas_call(
        functools.partial(_diff_kernel, tq=tq, lam_init=lam_init),
        out_shape=jax.ShapeDtypeStruct((batch, seq, d), BF16),
        grid=(n_heads, seq // tq, batch),
        in_specs=[pl.BlockSpec(seg.shape, lambda h, qi, b: (0, 0, 0)),
                  pl.BlockSpec((None,) + lam_vecs.shape[1:], lambda h, qi, b: (slot, 0, 0)),
                  pl.BlockSpec((1, hd2), lambda h, qi, b: (slot, 0)),
                  pl.BlockSpec((None, tq, hd2), lambda h, qi, b: (b, qi, h)),
                  pl.BlockSpec((None, seq, hd2), lambda h, qi, b: (b, 0, n_heads + h)),
                  pl.BlockSpec((None, seq, hd2), lambda h, qi, b: (b, 0, 2 * n_heads + h))],
        out_specs=pl.BlockSpec((None, tq, hd2), lambda h, qi, b: (b, qi, h)),
        scratch_shapes=[pltpu.VMEM((2, tq, seq), F32)],
        compiler_params=_cparams(3),
        name="diff_attn",
    )(seg, lam_vecs, subln, qkv_v, qkv_v, qkv_v)
    return out.reshape(t, d)


SEG_ALIGN = 16
SEG_SIZES = (512, 256, 128, 64, 32, 16)
SEG_STAGE = 2 * SEG_TILE + N_EXPERTS * SEG_ALIGN


def _route_kernel(x_ref, g_ref, r_ref, hn_ref, mi_ref, mf_ref, cnt_ref, tri_ref):
    tm, d = x_ref.shape

    @pl.when(pl.program_id(0) == 0)
    def _():
        rr = lax.broadcasted_iota(I32, (tm, tm), 0)
        cc = lax.broadcasted_iota(I32, (tm, tm), 1)
        tri_ref[...] = (cc < rr).astype(BF16)

    hn = _rms_rows(x_ref[...], g_ref[...])
    hn_ref[...] = hn.astype(BF16)
    logits = jnp.dot(hn, r_ref[...], preferred_element_type=F32, precision=lax.Precision.HIGHEST)
    lane = lax.broadcasted_iota(I32, (tm, LANES), 1)
    logits = jnp.where(lane < N_EXPERTS, logits, -jnp.inf)
    v1 = jnp.max(logits, axis=-1, keepdims=True)
    i1 = jnp.min(jnp.where(logits == v1, lane, LANES), axis=-1, keepdims=True)
    oh1 = lane == i1
    rest = jnp.where(oh1, -jnp.inf, logits)
    v2 = jnp.max(rest, axis=-1, keepdims=True)
    i2 = jnp.min(jnp.where(rest == v2, lane, LANES), axis=-1, keepdims=True)
    oh2 = lane == i2
    e2 = jnp.exp(v2 - v1)
    g1 = 1.0 / (1.0 + e2)
    g2 = e2 / (1.0 + e2)

    sel = (oh1 | oh2)
    before = jnp.dot(tri_ref[...], sel.astype(BF16), preferred_element_type=F32)
    rank1 = jnp.sum(jnp.where(oh1, before, 0.0), axis=-1, keepdims=True).astype(I32)
    rank2 = jnp.sum(jnp.where(oh2, before, 0.0), axis=-1, keepdims=True).astype(I32)
    counts = jnp.sum(sel.astype(F32), axis=0, keepdims=True).astype(I32)

    mi_ref[...] = jnp.where(lane == 0, i1, jnp.where(lane == 1, i2,
                            jnp.where(lane == 2, rank1, jnp.where(lane == 3, rank2, 0))))
    mf_ref[...] = jnp.where(lane == 0, g1, jnp.where(lane == 1, g2, 0.0))
    cnt_ref[...] = jnp.broadcast_to(counts, cnt_ref.shape)


def _route(x, g, router):
    t, d = x.shape
    tm = SEG_TILE
    r_pad = jnp.zeros((d, LANES), F32).at[:, :N_EXPERTS].set(router.astype(F32))
    return pl.pallas_call(
        _route_kernel,
        out_shape=(jax.ShapeDtypeStruct((t, d), BF16),
                   jax.ShapeDtypeStruct((t, LANES), I32),
                   jax.ShapeDtypeStruct((t, LANES), F32),
                   jax.ShapeDtypeStruct((t // tm, SUBLANES, LANES), I32)),
        grid=(t // tm,),
        in_specs=[pl.BlockSpec((tm, d), lambda i: (i, 0)),
                  pl.BlockSpec((1, d), lambda i: (0, 0)),
                  pl.BlockSpec((d, LANES), lambda i: (0, 0))],
        out_specs=(pl.BlockSpec((tm, d), lambda i: (i, 0)),
                   pl.BlockSpec((tm, LANES), lambda i: (i, 0)),
                   pl.BlockSpec((tm, LANES), lambda i: (i, 0)),
                   pl.BlockSpec((None, SUBLANES, LANES), lambda i: (i, 0, 0))),
        scratch_shapes=[pltpu.VMEM((tm, tm), BF16)],
        compiler_params=_cparams(1),
        name="moe_route",
    )(x, g.reshape(1, d), r_pad)


def _slot_rows(mi, loc_ref, base):
    e1, e2, d1, d2 = mi[0], mi[1], mi[2], mi[3]
    for e in range(N_EXPERTS):
        off = loc_ref[base + e]
        d1 = d1 + jnp.where(e1 == e, off, 0)
        d2 = d2 + jnp.where(e2 == e, off, 0)
    return d1, d2


def _segment_copies(base, seg_ref, n16_ref, loc_ref, hbm_ref, vmem_ref, sem, *, to_hbm, wait):
    for e in range(N_EXPERTS):
        n16 = n16_ref[base + e]
        hbm0 = seg_ref[base + e]
        vmem0 = loc_ref[base + e]
        off = 0
        for size in SEG_SIZES:
            @pl.when((n16 & size) != 0)
            def _(off=off, size=size):
                h = hbm_ref.at[pl.ds(pl.multiple_of(hbm0 + off, SEG_ALIGN), size)]
                v = vmem_ref.at[pl.ds(pl.multiple_of(vmem0 + off, SEG_ALIGN), size)]
                cp = pltpu.make_async_copy(v, h, sem.at[0]) if to_hbm else pltpu.make_async_copy(h, v, sem.at[0])
                if wait:
                    cp.wait()
                else:
                    cp.start()
            off = off + (n16 & size)


def _scatter_kernel(seg_ref, n16_ref, loc_ref, tail_ref, hn_ref, mi_ref, xs_ref, stage_ref, zero_ref, sem):
    i = pl.program_id(0)
    tm = hn_ref.shape[0]
    rows = stage_ref.shape[0]
    tg = zero_ref.shape[0]

    @pl.when(i == 0)
    def _():
        zero_ref[...] = jnp.zeros_like(zero_ref)
        for e in range(N_EXPERTS):
            tail = pl.multiple_of(tail_ref[e], SEG_ALIGN)
            pltpu.make_async_copy(zero_ref, xs_ref.at[pl.ds(tail, tg)], sem.at[0]).start()
        for e in range(N_EXPERTS):
            tail = pl.multiple_of(tail_ref[e], SEG_ALIGN)
            pltpu.make_async_copy(zero_ref, xs_ref.at[pl.ds(tail, tg)], sem.at[0]).wait()

    base = i * N_EXPERTS
    mi_t = mi_ref[...].T
    d1, d2 = _slot_rows([mi_t[k:k + 1, :] for k in range(4)], loc_ref, base)
    row = lax.broadcasted_iota(I32, (rows, tm), 0)
    onehot = ((row == d1) | (row == d2)).astype(BF16)
    stage_ref[...] = jnp.dot(onehot, hn_ref[...], preferred_element_type=F32).astype(BF16)
    kw = dict(to_hbm=True)
    _segment_copies(base, seg_ref, n16_ref, loc_ref, xs_ref, stage_ref, sem, wait=False, **kw)
    _segment_copies(base, seg_ref, n16_ref, loc_ref, xs_ref, stage_ref, sem, wait=True, **kw)


def _scatter(hn, mi, seg, n16, loc, tails, n_rows):
    t, d = hn.shape
    tm = SEG_TILE
    return pl.pallas_call(
        _scatter_kernel,
        out_shape=jax.ShapeDtypeStruct((n_rows, d), BF16),
        grid_spec=pltpu.PrefetchScalarGridSpec(
            num_scalar_prefetch=4,
            grid=(t // tm,),
            in_specs=[pl.BlockSpec((tm, d), lambda i, *_: (i, 0)),
                      pl.BlockSpec((tm, LANES), lambda i, *_: (i, 0))],
            out_specs=pl.BlockSpec(memory_space=pl.ANY),
            scratch_shapes=[pltpu.VMEM((SEG_STAGE, d), BF16), pltpu.VMEM((MOE_TILE, d), BF16),
                            pltpu.SemaphoreType.DMA((1,))]),
        compiler_params=pltpu.CompilerParams(dimension_semantics=("arbitrary",),
                                             vmem_limit_bytes=VMEM_LIMIT, has_side_effects=True),
        name="moe_scatter",
    )(seg, n16, loc, tails, hn, mi)


def _seg_expert_kernel(te_ref, tb_ref, nv_ref, xs_ref, wg_ref, wu_ref, wd_ref, ys_ref, acc_ref):
    i, f = pl.program_id(0), pl.program_id(1)

    @pl.when(i < nv_ref[0])
    def _():
        @pl.when(f == 0)
        def _():
            acc_ref[...] = jnp.zeros_like(acc_ref)

        h = xs_ref[...]
        gate = jnp.dot(h, wg_ref[...].astype(BF16), preferred_element_type=F32)
        up = jnp.dot(h, wu_ref[...].astype(BF16), preferred_element_type=F32)
        a = _silu_mul(gate, up).astype(BF16)
        acc_ref[...] += jnp.dot(a, wd_ref[...].astype(BF16), preferred_element_type=F32)

        @pl.when(f == pl.num_programs(1) - 1)
        def _():
            ys_ref[...] = acc_ref[...].astype(ys_ref.dtype)


def _seg_experts(xs, w_gate_up, w_down, slot, tile_expert, tile_block, n_valid, n_tiles):
    n_rows, d = xs.shape
    ff = w_down.shape[-2]
    tg, tf = MOE_TILE, FF_TILE
    nf = ff // tf

    def fidx(i, f, nv):
        return jnp.where(i < nv[0], f, nf - 1)

    return pl.pallas_call(
        _seg_expert_kernel,
        out_shape=jax.ShapeDtypeStruct((n_rows, d), BF16),
        grid_spec=pltpu.PrefetchScalarGridSpec(
            num_scalar_prefetch=3,
            grid=(n_tiles, nf),
            in_specs=[pl.BlockSpec((tg, d), lambda i, f, te, tb, nv: (tb[i], 0)),
                      pl.BlockSpec((None, None, d, tf),
                                   lambda i, f, te, tb, nv: (slot, te[i], 0, fidx(i, f, nv))),
                      pl.BlockSpec((None, None, d, tf),
                                   lambda i, f, te, tb, nv: (slot, te[i], 0, nf + fidx(i, f, nv))),
                      pl.BlockSpec((None, None, tf, d),
                                   lambda i, f, te, tb, nv: (slot, te[i], fidx(i, f, nv), 0))],
            out_specs=pl.BlockSpec((tg, d), lambda i, f, te, tb, nv: (tb[i], 0)),
            scratch_shapes=[pltpu.VMEM((tg, d), F32)]),
        compiler_params=_cparams(2),
        name="moe_experts",
    )(tile_expert, tile_block, n_valid, xs, w_gate_up, w_gate_up, w_down)


def _gather_kernel(seg_ref, n16_ref, loc_ref, x_ref, mi_ref, mf_ref, gf_ref, ys_ref, o_ref,
                   ybuf_ref, sem, *, final_norm):
    i = pl.program_id(0)
    tm = x_ref.shape[0]
    rows = ybuf_ref.shape[0]

    @pl.when(i == 0)
    def _():
        ybuf_ref[...] = jnp.zeros_like(ybuf_ref)

    base = i * N_EXPERTS
    kw = dict(to_hbm=False)
    _segment_copies(base, seg_ref, n16_ref, loc_ref, ys_ref, ybuf_ref, sem, wait=False, **kw)
    mi = mi_ref[...]
    d1, d2 = _slot_rows([mi[:, k:k + 1] for k in range(4)], loc_ref, base)
    col = lax.broadcasted_iota(I32, (tm, rows), 1)
    gates = mf_ref[...]
    _segment_copies(base, seg_ref, n16_ref, loc_ref, ys_ref, ybuf_ref, sem, wait=True, **kw)
    ybuf = ybuf_ref[...]
    out = x_ref[...]
    for k, dk in enumerate((d1, d2)):
        yk = jnp.dot((col == dk).astype(BF16), ybuf, preferred_element_type=F32)
        out = out + gates[:, k:k + 1] * yk
    if final_norm:
        out = _rms_rows(out, gf_ref[...])
    o_ref[...] = out


def _gather(x, mi, mf, ys, seg, n16, loc, final_g):
    t, d = x.shape
    tm = SEG_TILE
    final_norm = final_g is not None
    gf = (final_g if final_norm else jnp.ones((d,), F32)).reshape(1, d).astype(F32)
    return pl.pallas_call(
        functools.partial(_gather_kernel, final_norm=final_norm),
        out_shape=jax.ShapeDtypeStruct((t, d), F32),
        grid_spec=pltpu.PrefetchScalarGridSpec(
            num_scalar_prefetch=3,
            grid=(t // tm,),
            in_specs=[pl.BlockSpec((tm, d), lambda i, *_: (i, 0)),
                      pl.BlockSpec((tm, LANES), lambda i, *_: (i, 0)),
                      pl.BlockSpec((tm, LANES), lambda i, *_: (i, 0)),
                      pl.BlockSpec((1, d), lambda i, *_: (0, 0)),
                      pl.BlockSpec(memory_space=pl.ANY)],
            out_specs=pl.BlockSpec((tm, d), lambda i, *_: (i, 0)),
            scratch_shapes=[pltpu.VMEM((SEG_STAGE, d), BF16), pltpu.SemaphoreType.DMA((1,))]),
        compiler_params=_cparams(1),
        name="moe_gather",
    )(seg, n16, loc, x, mi, mf, gf, ys)


def _moe_seg(x, g, router, w_gate_up, w_down, slot, final_g=None):
    t, d = x.shape
    tg = MOE_TILE
    hn, mi, mf, cnt = _route(x, g, router)

    counts = cnt[:, 0, :N_EXPERTS]
    n16 = (counts + SEG_ALIGN - 1) // SEG_ALIGN * SEG_ALIGN
    rows_e = jnp.sum(n16, axis=0)
    tiles_e = (rows_e + tg - 1) // tg
    tile_start = jnp.cumsum(tiles_e) - tiles_e
    row_start = tile_start * tg
    seg = row_start[None, :] + jnp.cumsum(n16, axis=0) - n16
    loc = jnp.cumsum(n16, axis=1) - n16
    n_valid = jnp.sum(tiles_e).astype(I32)
    n_tiles = (2 * t + counts.size * (SEG_ALIGN - 1)) // tg + N_EXPERTS
    n_rows = (n_tiles + 1) * tg
    tidx = jnp.minimum(jnp.arange(n_tiles, dtype=I32), n_valid - 1)
    tile_expert = (jnp.sum(tidx[:, None] >= tile_start[None, :], axis=1) - 1).astype(I32)
    tails = (row_start + rows_e).astype(I32)
    seg, n16, loc = (a.reshape(-1).astype(I32) for a in (seg, n16, loc))

    xs = _scatter(hn, mi, seg, n16, loc, tails, n_rows)
    ys = _seg_experts(xs, w_gate_up, w_down, slot, tile_expert, tidx, n_valid.reshape(1), n_tiles)
    return _gather(x, mi, mf, ys, seg, n16, loc, final_g)


def kernel(x, rel_bias, norm_mix, norm_ffn, norm_final, a_w_in, a_w_group, a_scale, a_w_out,
           b_w_in, b_w_out, c_w_in, c_lambda, c_subln, c_w_out, d_w_in, d_sink, d_w_out,
           f_w_gate_up, f_w_down, m_router, m_w_gate_up, m_w_down):
    batch, seq, d = x.shape
    h = x.reshape(batch * seq, d)

    u = _norm_matmul(h, norm_mix[0], a_w_in, 0, F32)
    y = _pool_mix(u, a_w_group, a_scale, 0, batch, seq)
    h = _matmul_res(y, a_w_out, 0, h)
    h = _ffn(h, norm_ffn[0], f_w_gate_up, f_w_down, 0)

    qkv = _dil_proj(h.reshape(batch, seq, d), norm_mix[1], b_w_in, 0, tuple(p[1] for p in DIL_PAIRS))
    o = _dilated_attention(qkv, rel_bias, batch, seq)
    h = _matmul_res(o, b_w_out, 0, h)
    h = _moe_seg(h, norm_ffn[1], m_router[0], m_w_gate_up, m_w_down, 0)

    qkv = _norm_matmul(h, norm_mix[2], c_w_in, 0, BF16)
    o = _diff_attention(qkv, c_lambda, c_subln, 0, rel_bias, batch, seq)
    h = _matmul_res(o, c_w_out, 0, h)
    h = _ffn(h, norm_ffn[2], f_w_gate_up, f_w_down, 1)

    qkv = _norm_matmul(h, norm_mix[3], d_w_in, 0, BF16)
    o = _gqa_attention(qkv, d_sink[0], rel_bias, batch, seq)
    h = _matmul_res(o, d_w_out, 0, h)
    h = _moe_seg(h, norm_ffn[3], m_router[1], m_w_gate_up, m_w_down, 1, final_g=norm_final)
    return h.reshape(batch, seq, d)
```

```python
import collections
import functools
import math

import numpy as np
import jax
import jax.numpy as jnp
from jax import lax
from jax.experimental import pallas as pl
from jax.experimental.pallas import tpu as pltpu

F32 = jnp.float32
BF16 = jnp.bfloat16
I32 = jnp.int32

RMS_EPS = 1e-6
HEAD_DIM = 64
N_BUCKETS = 32
MAX_DISTANCE = 1024
POOL_WINDOWS = (2, 4, 8, 16)
DIL_PAIRS = ((128, 1), (512, 4), (2048, 16))
Q_SCALE = HEAD_DIM ** -0.5
GQA_KV_HEADS = 4
GQA_WINDOW = 128
N_EXPERTS = 8
DIFF_LAYER = 2

LANES = 128
SUBLANES = 8
VMEM_LIMIT = 56 * 1024 * 1024
ROW_TILE = 1024
FF_TILE = 512
MOE_TILE = 1024
SEG_TILE = 512
ATT_TQ = 128
DIL_UNITS = 4
DIFF_ROW_SPLIT = 4
DIL_HEAD_SPLIT = 2
DIFF_TQ = 512
MASK_VALUE = -1e30


def _cparams(n_axes, vmem=VMEM_LIMIT):
    return pltpu.CompilerParams(dimension_semantics=("arbitrary",) * n_axes,
                                vmem_limit_bytes=vmem)


def _next_pow2(n):
    return 1 << (n - 1).bit_length()


def _rms_rows(x, g):
    ms = jnp.mean(x * x, axis=-1, keepdims=True)
    return x * lax.rsqrt(ms + RMS_EPS) * g


def _norm_matmul_kernel(x_ref, g_ref, w_ref, o_ref, hn_ref):
    @pl.when(pl.program_id(1) == 0)
    def _():
        hn_ref[...] = _rms_rows(x_ref[...], g_ref[...]).astype(BF16)

    o_ref[...] = jnp.dot(hn_ref[...], w_ref[...].astype(BF16),
                         preferred_element_type=F32).astype(o_ref.dtype)


def _norm_matmul(x, g, w, slot, out_dtype):
    t, d = x.shape
    n = w.shape[-1]
    tm = ROW_TILE
    tn = next(c for c in (1024, 768, 512, 256, 128) if n % c == 0)
    return pl.pallas_call(
        _norm_matmul_kernel,
        out_shape=jax.ShapeDtypeStruct((t, n), out_dtype),
        grid=(t // tm, n // tn),
        in_specs=[pl.BlockSpec((tm, d), lambda i, j: (i, 0)),
                  pl.BlockSpec((1, d), lambda i, j: (0, 0)),
                  pl.BlockSpec((None, d, tn), lambda i, j: (slot, 0, j))],
        out_specs=pl.BlockSpec((tm, tn), lambda i, j: (i, j)),
        scratch_shapes=[pltpu.VMEM((tm, d), BF16)],
        compiler_params=_cparams(2),
        name="norm_matmul",
    )(x, g.reshape(1, d), w)


def _matmul_res_kernel(a_ref, w_ref, x_ref, o_ref, wb_ref):
    @pl.when(pl.program_id(0) == 0)
    def _():
        wb_ref[...] = w_ref[...].astype(BF16)

    o_ref[...] = x_ref[...] + jnp.dot(a_ref[...], wb_ref[...], preferred_element_type=F32)


def _matmul_res(a, w, slot, x):
    t, k = a.shape
    d = w.shape[-1]
    tm = ROW_TILE
    return pl.pallas_call(
        _matmul_res_kernel,
        out_shape=jax.ShapeDtypeStruct((t, d), F32),
        grid=(t // tm,),
        in_specs=[pl.BlockSpec((tm, k), lambda i: (i, 0)),
                  pl.BlockSpec((None, k, d), lambda i: (slot, 0, 0)),
                  pl.BlockSpec((tm, d), lambda i: (i, 0))],
        out_specs=pl.BlockSpec((tm, d), lambda i: (i, 0)),
        scratch_shapes=[pltpu.VMEM((k, d), BF16)],
        compiler_params=_cparams(1),
        name="matmul_res",
    )(a, w, x)


def _silu_mul(g, u):
    return g * (1.0 / (1.0 + jnp.exp(-g))) * u


def _ffn_kernel(x_ref, g_ref, wg_ref, wu_ref, wd_ref, o_ref, hn_ref, acc_ref):
    f = pl.program_id(1)

    @pl.when(f == 0)
    def _():
        hn_ref[...] = _rms_rows(x_ref[...], g_ref[...]).astype(BF16)
        acc_ref[...] = jnp.zeros_like(acc_ref)

    h = hn_ref[...]
    gate = jnp.dot(h, wg_ref[...].astype(BF16), preferred_element_type=F32)
    up = jnp.dot(h, wu_ref[...].astype(BF16), preferred_element_type=F32)
    a = _silu_mul(gate, up).astype(BF16)
    acc_ref[...] += jnp.dot(a, wd_ref[...].astype(BF16), preferred_element_type=F32)

    @pl.when(f == pl.num_programs(1) - 1)
    def _():
        o_ref[...] = x_ref[...] + acc_ref[...]


def _ffn(x, g, w_gate_up, w_down, slot):
    t, d = x.shape
    ff = w_down.shape[-2]
    tm, tf = ROW_TILE, FF_TILE
    nf = ff // tf
    return pl.pallas_call(
        _ffn_kernel,
        out_shape=jax.ShapeDtypeStruct((t, d), F32),
        grid=(t // tm, nf),
        in_specs=[pl.BlockSpec((tm, d), lambda i, f: (i, 0)),
                  pl.BlockSpec((1, d), lambda i, f: (0, 0)),
                  pl.BlockSpec((None, d, tf), lambda i, f: (slot, 0, f)),
                  pl.BlockSpec((None, d, tf), lambda i, f: (slot, 0, nf + f)),
                  pl.BlockSpec((None, tf, d), lambda i, f: (slot, f, 0))],
        out_specs=pl.BlockSpec((tm, d), lambda i, f: (i, 0)),
        scratch_shapes=[pltpu.VMEM((tm, d), BF16), pltpu.VMEM((tm, d), F32)],
        compiler_params=_cparams(2),
        name="ffn",
    )(x, g.reshape(1, d), w_gate_up, w_gate_up, w_down)


POOL_PAD = 16


def _pool_kernel(u_ref, wg_ref, sc_ref, o_ref, pad_ref):
    s, c = u_ref.shape
    grp = pl.program_id(1)
    pad_ref[pl.ds(0, POOL_PAD), :] = jnp.zeros((POOL_PAD, c), F32)
    pad_ref[pl.ds(POOL_PAD + s, POOL_PAD), :] = jnp.zeros((POOL_PAD, c), F32)
    pad_ref[pl.ds(POOL_PAD, s), :] = u_ref[...]
    wb = wg_ref[...].astype(BF16)
    chunk = 256

    for gi, w in enumerate(POOL_WINDOWS):
        @pl.when(grp == gi)
        def _(w=w):
            for c0 in range(0, s, chunk):
                acc = None
                for dlt in range(-(w // 2), w - w // 2):
                    piece = pad_ref[pl.ds(POOL_PAD + c0 + dlt, chunk), :]
                    acc = piece if acc is None else acc + piece
                pos = c0 + lax.broadcasted_iota(I32, (chunk, 1), 0)
                lo = jnp.maximum(pos - w // 2, 0)
                hi = jnp.minimum(pos + (w - w // 2), s)
                cnt = (hi - lo).astype(F32)
                mixed = acc / cnt - u_ref[pl.ds(c0, chunk), :]
                y = jnp.dot(mixed.astype(BF16), wb, preferred_element_type=F32)
                o_ref[pl.ds(c0, chunk), :] = (y * sc_ref[...]).astype(o_ref.dtype)


def _pool_mix(u, w_group, scale, slot, batch, seq):
    d = u.shape[1]
    ng = len(POOL_WINDOWS)
    c = d // ng
    u3 = u.reshape(batch, seq, d)
    out = pl.pallas_call(
        _pool_kernel,
        out_shape=jax.ShapeDtypeStruct((batch, seq, d), BF16),
        grid=(batch, ng),
        in_specs=[pl.BlockSpec((None, seq, c), lambda b, g: (b, 0, g)),
                  pl.BlockSpec((None, None, c, c), lambda b, g: (slot, g, 0, 0)),
                  pl.BlockSpec((1, c), lambda b, g: (slot, g))],
        out_specs=pl.BlockSpec((None, seq, c), lambda b, g: (b, 0, g)),
        scratch_shapes=[pltpu.VMEM((seq + 2 * POOL_PAD, c), F32)],
        compiler_params=_cparams(2),
        name="pool_mix",
    )(u3, w_group, scale)
    return out.reshape(batch * seq, d)


def _rel_bucket_np(rel):
    half = N_BUCKETS // 2
    max_exact = half // 2
    n = np.abs(rel)
    ratio = np.log(np.maximum(n, 1).astype(np.float32) / np.float32(max_exact))
    big = max_exact + (ratio / np.float32(math.log(MAX_DISTANCE / max_exact))
                       * np.float32(half - max_exact)).astype(np.int32)
    big = np.minimum(big, half - 1)
    return np.where(rel > 0, half, 0) + np.where(n < max_exact, n, big)


def _seg_table(rel_bias, rel, valid):
    bucket = jnp.asarray(_rel_bucket_np(rel).astype(np.int32))
    vals = jnp.take(rel_bias.astype(F32), bucket, axis=0)
    vals = jnp.where(jnp.asarray(valid)[..., None], vals, MASK_VALUE)
    return jnp.transpose(vals, (2, 0, 1))


def _toeplitz(seg_row, rows, cols):
    w = seg_row.shape[1]
    full = jnp.broadcast_to(seg_row, (rows, w))
    rolled = pltpu.roll(full, w - (rows - 1), 1, stride=1, stride_axis=0)
    return rolled[:, :cols]


def _band_variants(hw, n_tiles):
    if n_tiles == 1:
        return (0,)
    return (0, -hw, -2 * hw)


def _band_segs(rel_bias, tq, kw, half, dil, variants, width):
    c = np.arange(width)
    rel = np.stack([r0 + c - (tq - 1) for r0 in variants])
    valid = (np.abs(rel) <= half) & (c[None, :] < tq + kw - 1)
    return _seg_table(rel_bias, rel * dil, valid)


def _band_window(qi, nq, tq, hw, kw, ln):
    if nq == 1:
        return 0, 0
    var = jnp.where(qi == 0, 0, jnp.where(qi == nq - 1, 2, 1))
    return var, jnp.clip(qi * tq - hw, 0, ln - kw)


DilCfg = collections.namedtuple("DilCfg", "dil ln nq hw kw n_var")


def _dil_proj_kernel(x_ref, g_ref, w_ref, o_ref, slab_ref, hn_ref, *, dils):
    grp, c = pl.program_id(1), pl.program_id(2)
    seq, d = x_ref.shape
    n_slabs = d // LANES
    chunk = 256

    @pl.when((grp == 0) & (c == 0))
    def _():
        for c0 in range(0, seq, chunk):
            hn = _rms_rows(x_ref[pl.ds(c0, chunk), :], g_ref[...])
            for s in range(n_slabs):
                slab_ref[s, pl.ds(c0, chunk), :] = hn[:, s * LANES:(s + 1) * LANES]

    for gi, dil in enumerate(dils):
        @pl.when((grp == gi) & (c == 0))
        def _(dil=dil):
            ln = seq // dil
            for r in range(dil):
                for c0 in range(0, ln, chunk):
                    n = min(chunk, ln)
                    rows = [slab_ref[s, pl.ds(r + c0 * dil, n, stride=dil), :] for s in range(n_slabs)]
                    hn_ref[pl.ds(r * ln + c0, n), :] = jnp.concatenate(rows, axis=1).astype(BF16)

    res = jnp.dot(hn_ref[...], w_ref[...].astype(BF16), preferred_element_type=F32)
    res = res * jnp.where(c < pl.num_programs(2) // 3, Q_SCALE, 1.0)
    for s in range(o_ref.shape[0]):
        o_ref[s] = res[:, s * LANES:(s + 1) * LANES].astype(o_ref.dtype)


def _dil_proj(x3, g, w_in, slot, dils):
    batch, seq, d = x3.shape
    n = w_in.shape[-1]
    tn = FF_TILE
    per_group = n // len(dils) // tn
    spt = tn // LANES
    return pl.pallas_call(
        functools.partial(_dil_proj_kernel, dils=dils),
        out_shape=jax.ShapeDtypeStruct((batch, n // LANES, seq, LANES), BF16),
        grid=(batch, len(dils), per_group),
        in_specs=[pl.BlockSpec((None, seq, d), lambda b, g, c: (b, 0, 0)),
                  pl.BlockSpec((1, d), lambda b, g, c: (0, 0)),
                  pl.BlockSpec((None, d, tn), lambda b, g, c: (slot, 0, g * per_group + c))],
        out_specs=pl.BlockSpec((None, spt, seq, LANES), lambda b, g, c: (b, g * per_group + c, 0, 0)),
        scratch_shapes=[pltpu.VMEM((d // LANES, seq, LANES), F32), pltpu.VMEM((seq, d), BF16)],
        compiler_params=_cparams(3),
        name="dil_proj",
    )(x3, g.reshape(1, d), w_in)


def _dil_unit(cfg, unit, q4, k_ref, v_ref, bias_ref, acc_ref, m_ref, l_ref, *, tq, first, last):
    dil, ln, nq, kw = cfg.dil, cfg.ln, cfg.nq, cfg.kw
    n_pairs = q4.shape[0]
    if nq == 1:
        r, qi = unit, 0
    elif dil == 1:
        r, qi = 0, unit
    else:
        r, qi = unit // nq, unit % nq
    var, ks = _band_window(qi, nq, tq, cfg.hw, kw, ln)
    krow = pl.multiple_of(r * ln + ks, cfg.hw)
    if dil == 1:
        rows = pl.ds(pl.multiple_of(qi * tq, tq), tq)
    else:
        rows = pl.ds(qi * tq * dil + r, tq, stride=dil)

    even = lax.broadcasted_iota(I32, (n_pairs, tq, LANES), 2) < HEAD_DIM
    zero = jnp.zeros_like(q4)
    q8 = jnp.concatenate([jnp.where(even, q4, zero), jnp.where(even, zero, q4)], axis=0)
    k4 = k_ref[:, pl.ds(krow, kw), :]
    v4 = v_ref[:, pl.ds(krow, kw), :]
    k8 = jnp.concatenate([k4, k4], axis=0)
    va = jnp.concatenate([v4, jnp.ones_like(v4)], axis=2)
    v8 = jnp.concatenate([va, va], axis=0)
    s = lax.dot_general(q8, k8, (((2,), (2,)), ((0,), (0,))), preferred_element_type=F32)
    s = s + bias_ref[var]
    m8 = jnp.max(s, axis=-1, keepdims=True)
    p = jnp.exp(s - m8).astype(BF16)
    ov = lax.dot_general(p, v8, (((2,), (1,)), ((0,), (0,))), preferred_element_type=F32)
    o_c = jnp.where(even, ov[:n_pairs, :, :LANES], ov[n_pairs:, :, :LANES])
    l_c = jnp.where(even, ov[:n_pairs, :, LANES:], ov[n_pairs:, :, LANES:])
    m_c = jnp.where(even, m8[:n_pairs], m8[n_pairs:])
    if first:
        m_n, l_n, acc_n = m_c, l_c, o_c
    else:
        m_o = jnp.stack([m_ref[pp, rows, :] for pp in range(n_pairs)])
        l_o = jnp.stack([l_ref[pp, rows, :] for pp in range(n_pairs)])
        acc_o = jnp.stack([acc_ref[pp, rows, :] for pp in range(n_pairs)])
        m_n = jnp.maximum(m_o, m_c)
        a_o = jnp.exp(m_o - m_n)
        a_c = jnp.exp(m_c - m_n)
        l_n = a_o * l_o + a_c * l_c
        acc_n = a_o * acc_o + a_c * o_c
    if last:
        acc_n = acc_n / l_n
    for pp in range(n_pairs):
        acc_ref[pp, rows, :] = acc_n[pp]
        if not last:
            m_ref[pp, rows, :] = m_n[pp]
            l_ref[pp, rows, :] = l_n[pp]


def _dil_attn_kernel(*refs, cfgs, tq):
    ng = len(cfgs)
    seg_refs = refs[:ng]
    q_ref, k_ref, v_ref, o_ref = refs[ng:ng + 4]
    bias_refs = refs[ng + 4:2 * ng + 4]
    acc_ref, m_ref, l_ref = refs[2 * ng + 4:]
    hf, b, grp, step = (pl.program_id(i) for i in range(4))
    n_pairs = q_ref.shape[0]

    @pl.when((b == 0) & (grp == 0) & (step == 0))
    def _():
        for gi, cfg in enumerate(cfgs):
            for v in range(cfg.n_var):
                for odd in range(2):
                    for pp in range(n_pairs):
                        head = hf * 2 * n_pairs + 2 * pp + odd
                        row = seg_refs[gi][head, pl.ds(v, 1), :]
                        bias_refs[gi][v, odd * n_pairs + pp] = _toeplitz(row, tq, cfg.kw)

    for gi, cfg in enumerate(cfgs):
        @pl.when(grp == gi)
        def _(gi=gi, cfg=cfg):
            for uu in range(DIL_UNITS):
                _dil_unit(cfg, step * DIL_UNITS + uu, q_ref[:, pl.ds(uu * tq, tq), :], k_ref, v_ref,
                          bias_refs[gi], acc_ref, m_ref, l_ref, tq=tq, first=gi == 0, last=gi == ng - 1)

    @pl.when((grp == ng - 1) & (step == pl.num_programs(3) - 1))
    def _():
        slabs = [acc_ref[s] for s in range(n_pairs)]
        o_ref[...] = jnp.concatenate(slabs, axis=1).astype(o_ref.dtype)


def _dilated_attention(qkv, rel_bias, batch, seq):
    ng = len(DIL_PAIRS)
    d = qkv.shape[1] * LANES // (3 * ng)
    hd = d // DIL_HEAD_SPLIT
    n_pairs = hd // LANES
    tq = ATT_TQ
    cfgs, segs = [], []
    for win, dil in DIL_PAIRS:
        half = win // (2 * dil)
        ln = seq // dil
        nq = ln // tq
        kw = min(tq + 2 * half, ln)
        variants = _band_variants(half, nq)
        cfgs.append(DilCfg(dil, ln, nq, half, kw, len(variants)))
        segs.append(_band_segs(rel_bias, tq, kw, half, dil, variants, _next_pow2(tq + kw - 1)))
    steps = seq // tq // DIL_UNITS
    cb = DIL_HEAD_SPLIT

    in_specs = [pl.BlockSpec(sg.shape, lambda hf, b, g, s: (0, 0, 0)) for sg in segs]
    in_specs += [pl.BlockSpec((None, n_pairs, DIL_UNITS * tq, LANES),
                              lambda hf, b, g, s: (b, (g * 3) * cb + hf, s, 0)),
                 pl.BlockSpec((None, n_pairs, seq, LANES),
                              lambda hf, b, g, s: (b, (g * 3 + 1) * cb + hf, 0, 0)),
                 pl.BlockSpec((None, n_pairs, seq, LANES),
                              lambda hf, b, g, s: (b, (g * 3 + 2) * cb + hf, 0, 0))]
    scratch = [pltpu.VMEM((c.n_var, 2 * n_pairs, tq, c.kw), F32) for c in cfgs]
    scratch += [pltpu.VMEM((n_pairs, seq, LANES), F32) for _ in range(3)]
    out = pl.pallas_call(
        functools.partial(_dil_attn_kernel, cfgs=tuple(cfgs), tq=tq),
        out_shape=jax.ShapeDtypeStruct((batch, seq, d), BF16),
        grid=(DIL_HEAD_SPLIT, batch, ng, steps),
        in_specs=in_specs,
        out_specs=pl.BlockSpec((None, seq, hd), lambda hf, b, g, s: (b, 0, hf)),
        scratch_shapes=scratch,
        compiler_params=_cparams(4),
        name="dil_attn",
    )(*segs, qkv, qkv, qkv)
    return out.reshape(batch * seq, d)


def _gqa_kernel(sink_ref, seg_ref, q_ref, k_ref, v_ref, o_ref, bias_ref, *, tq, kw, hw, grp):
    b, qi = pl.program_id(0), pl.program_id(1)
    nq = pl.num_programs(1)
    seq_len = k_ref.shape[0]
    n_kv = k_ref.shape[1] // HEAD_DIM

    @pl.when((b == 0) & (qi == 0))
    def _():
        for v in range(3):
            for kh in range(n_kv):
                for gq in range(grp):
                    bias_ref[v, kh, pl.ds(gq * tq, tq), :] = _toeplitz(
                        seg_ref[kh * grp + gq, pl.ds(v, 1), :], tq, kw)

    var = jnp.where(qi == 0, 0, jnp.where(qi == nq - 1, 2, 1))
    ks = pl.multiple_of(jnp.clip(qi * tq - hw, 0, seq_len - kw), hw)
    row = lax.broadcasted_iota(I32, (grp * tq, 1), 0)
    outs = [None] * (n_kv * grp)
    for kh in range(n_kv):
        cs = slice(kh * HEAD_DIM, (kh + 1) * HEAD_DIM)
        qs = jnp.concatenate(
            [q_ref[:, (kh * grp + gq) * HEAD_DIM:(kh * grp + gq + 1) * HEAD_DIM] for gq in range(grp)],
            axis=0) * Q_SCALE
        kk = k_ref[pl.ds(ks, kw), cs]
        vv = v_ref[pl.ds(ks, kw), cs]
        s = lax.dot_general(qs, kk, (((1,), (1,)), ((), ())), preferred_element_type=F32)
        s = s + bias_ref[var, kh]
        sk = jnp.zeros((grp * tq, 1), F32)
        for gq in range(grp):
            sk = jnp.where((row >= gq * tq) & (row < (gq + 1) * tq), sink_ref[kh * grp + gq], sk)
        m = jnp.maximum(jnp.max(s, axis=-1, keepdims=True), sk)
        e = jnp.exp(s - m).astype(BF16)
        va = jnp.concatenate([vv, jnp.ones_like(vv)], axis=1)
        ov = jnp.dot(e, va, preferred_element_type=F32)
        o = ov[:, :HEAD_DIM] / (ov[:, HEAD_DIM:] + jnp.exp(sk - m))
        for gq in range(grp):
            outs[kh * grp + gq] = o[gq * tq:(gq + 1) * tq, :]
    o_ref[...] = jnp.concatenate(outs, axis=1).astype(o_ref.dtype)


def _gqa_attention(qkv, sink, rel_bias, batch, seq):
    t, ncol = qkv.shape
    n_q = sink.shape[0]
    d = n_q * HEAD_DIM
    kvw = GQA_KV_HEADS * HEAD_DIM
    grp = n_q // GQA_KV_HEADS
    tq = hw = GQA_WINDOW
    kw = 3 * GQA_WINDOW
    nq = seq // tq
    variants = _band_variants(hw, nq)
    width = _next_pow2(tq + kw - 1)
    seg = _band_segs(rel_bias, tq, kw, GQA_WINDOW, 1, variants, width)
    qkv_v = qkv.reshape(batch, seq, ncol)
    out = pl.pallas_call(
        functools.partial(_gqa_kernel, tq=tq, kw=kw, hw=hw, grp=grp),
        out_shape=jax.ShapeDtypeStruct((batch, seq, d), BF16),
        grid=(batch, nq),
        in_specs=[pl.BlockSpec(memory_space=pltpu.SMEM),
                  pl.BlockSpec((n_q, 3, width), lambda b, qi: (0, 0, 0)),
                  pl.BlockSpec((None, tq, d), lambda b, qi: (b, qi, 0)),
                  pl.BlockSpec((None, seq, kvw), lambda b, qi: (b, 0, d // kvw)),
                  pl.BlockSpec((None, seq, kvw), lambda b, qi: (b, 0, d // kvw + 1))],
        out_specs=pl.BlockSpec((None, tq, d), lambda b, qi: (b, qi, 0)),
        scratch_shapes=[pltpu.VMEM((3, GQA_KV_HEADS, grp * tq, kw), F32)],
        compiler_params=_cparams(2),
        name="gqa_attn",
    )(sink.astype(F32), seg, qkv_v, qkv_v, qkv_v)
    return out.reshape(t, d)


def _diff_kernel(seg_ref, lam_ref, sub_ref, q_ref, k_ref, v_ref, o_ref, bias_ref, *, tq, lam_init):
    h, qi, b = pl.program_id(0), pl.program_id(1), pl.program_id(2)
    seq_len = k_ref.shape[0]
    nk = seq_len // tq

    @pl.when(b == 0)
    def _():
        for j in range(2):
            for ki in range(nk):
                row = seg_ref[h * 2 + j, pl.ds(ki - qi + nk - 1, 1), :]
                bias_ref[j, :, pl.ds(ki * tq, tq)] = _toeplitz(row, tq, tq)

    lv = lam_ref[...]
    s01 = jnp.sum(lv[0:1, :] * lv[1:2, :], axis=-1, keepdims=True)
    s23 = jnp.sum(lv[2:3, :] * lv[3:4, :], axis=-1, keepdims=True)
    lam = jnp.exp(s01) - jnp.exp(s23) + lam_init

    q = q_ref[...] * Q_SCALE
    k = k_ref[...]
    v = v_ref[...]
    va = jnp.concatenate([v, jnp.ones_like(v)], axis=1)
    map0 = lax.broadcasted_iota(I32, q.shape, 1) < HEAD_DIM
    zero = jnp.zeros_like(q)
    qm = (jnp.where(map0, q, zero), jnp.where(map0, zero, q))
    rc = tq // DIFF_ROW_SPLIT
    for c0 in range(0, tq, rc):
        outs = []
        for j in range(2):
            s = lax.dot_general(qm[j][c0:c0 + rc], k, (((1,), (1,)), ((), ())),
                                preferred_element_type=F32)
            s = s + bias_ref[j, pl.ds(c0, rc), :]
            m = jnp.max(s, axis=-1, keepdims=True)
            p = jnp.exp(s - m).astype(BF16)
            ov = jnp.dot(p, va, preferred_element_type=F32)
            outs.append(ov[:, :2 * HEAD_DIM] / ov[:, 2 * HEAD_DIM:])
        o = outs[0] - lam * outs[1]
        o = _rms_rows(o, sub_ref[...]) * (1.0 - lam_init)
        o_ref[pl.ds(c0, rc), :] = o.astype(o_ref.dtype)


def _diff_attention(qkv, lam_vecs, subln, slot, rel_bias, batch, seq):
    t, ncol = qkv.shape
    d = ncol // 3
    hd2 = 2 * HEAD_DIM
    n_heads = d // hd2
    tq = DIFF_TQ
    nk = seq // tq
    width = _next_pow2(2 * tq - 1)
    c = np.arange(width)
    rel = np.stack([(dl - (nk - 1)) * tq + c - (tq - 1) for dl in range(2 * nk - 1)])
    seg = _seg_table(rel_bias, rel, np.ones_like(rel, dtype=bool))
    lam_init = 0.8 - 0.6 * math.exp(-0.3 * DIFF_LAYER)
    qkv_v = qkv.reshape(batch, seq, ncol)
    out = pl.pallas_call(
        functools.partial(_diff_kernel, tq=tq, lam_init=lam_init),
        out_shape=jax.ShapeDtypeStruct((batch, seq, d), BF16),
        grid=(n_heads, seq // tq, batch),
        in_specs=[pl.BlockSpec(seg.shape, lambda h, qi, b: (0, 0, 0)),
                  pl.BlockSpec((None,) + lam_vecs.shape[1:], lambda h, qi, b: (slot, 0, 0)),
                  pl.BlockSpec((1, hd2), lambda h, qi, b: (slot, 0)),
                  pl.BlockSpec((None, tq, hd2), lambda h, qi, b: (b, qi, h)),
                  pl.BlockSpec((None, seq, hd2), lambda h, qi, b: (b, 0, n_heads + h)),
                  pl.BlockSpec((None, seq, hd2), lambda h, qi, b: (b, 0, 2 * n_heads + h))],
        out_specs=pl.BlockSpec((None, tq, hd2), lambda h, qi, b: (b, qi, h)),
        scratch_shapes=[pltpu.VMEM((2, tq, seq), F32)],
        compiler_params=_cparams(3),
        name="diff_attn",
    )(seg, lam_vecs, subln, qkv_v, qkv_v, qkv_v)
    return out.reshape(t, d)


SEG_ALIGN = 16
SEG_SIZES = (512, 256, 128, 64, 32, 16)
SEG_STAGE = 2 * SEG_TILE + N_EXPERTS * SEG_ALIGN


def _route_kernel(a_ref, wo_ref, x_ref, g_ref, r_ref, xn_ref, hn_ref, mi_ref, mf_ref, cnt_ref,
                  tri_ref, wb_ref):
    tm, d = x_ref.shape

    @pl.when(pl.program_id(0) == 0)
    def _():
        rr = lax.broadcasted_iota(I32, (tm, tm), 0)
        cc = lax.broadcasted_iota(I32, (tm, tm), 1)
        tri_ref[...] = (cc < rr).astype(BF16)
        wb_ref[...] = wo_ref[...].astype(BF16)

    xn = x_ref[...] + jnp.dot(a_ref[...], wb_ref[...], preferred_element_type=F32)
    xn_ref[...] = xn
    hn = _rms_rows(xn, g_ref[...])
    hn_ref[...] = hn.astype(BF16)
    h_hi = hn.astype(BF16)
    h_lo = (hn - h_hi.astype(F32)).astype(BF16)
    r = r_ref[...]
    r_hi = r.astype(BF16)
    r_lo = (r - r_hi.astype(F32)).astype(BF16)
    logits = (jnp.dot(h_hi, r_hi, preferred_element_type=F32)
              + jnp.dot(h_hi, r_lo, preferred_element_type=F32)
              + jnp.dot(h_lo, r_hi, preferred_element_type=F32))
    lane = lax.broadcasted_iota(I32, (tm, LANES), 1)
    logits = jnp.where(lane < N_EXPERTS, logits, -jnp.inf)
    v1 = jnp.max(logits, axis=-1, keepdims=True)
    i1 = jnp.min(jnp.where(logits == v1, lane, LANES), axis=-1, keepdims=True)
    oh1 = lane == i1
    rest = jnp.where(oh1, -jnp.inf, logits)
    v2 = jnp.max(rest, axis=-1, keepdims=True)
    i2 = jnp.min(jnp.where(rest == v2, lane, LANES), axis=-1, keepdims=True)
    oh2 = lane == i2
    e2 = jnp.exp(v2 - v1)
    g1 = 1.0 / (1.0 + e2)
    g2 = e2 / (1.0 + e2)

    sel = (oh1 | oh2)
    before = jnp.dot(tri_ref[...], sel.astype(BF16), preferred_element_type=F32)
    rank1 = jnp.sum(jnp.where(oh1, before, 0.0), axis=-1, keepdims=True).astype(I32)
    rank2 = jnp.sum(jnp.where(oh2, before, 0.0), axis=-1, keepdims=True).astype(I32)
    counts = jnp.sum(sel.astype(F32), axis=0, keepdims=True).astype(I32)

    mi_ref[...] = jnp.where(lane == 0, i1, jnp.where(lane == 1, i2,
                            jnp.where(lane == 2, rank1, jnp.where(lane == 3, rank2, 0))))
    mf_ref[...] = jnp.where(lane == 0, g1, jnp.where(lane == 1, g2, 0.0))
    cnt_ref[...] = jnp.broadcast_to(counts, cnt_ref.shape)


def _route(a, w_out, out_slot, x, g, router):
    t, d = x.shape
    k = a.shape[1]
    tm = SEG_TILE
    r_pad = jnp.zeros((d, LANES), F32).at[:, :N_EXPERTS].set(router.astype(F32))
    return pl.pallas_call(
        _route_kernel,
        out_shape=(jax.ShapeDtypeStruct((t, d), F32),
                   jax.ShapeDtypeStruct((t, d), BF16),
                   jax.ShapeDtypeStruct((t, LANES), I32),
                   jax.ShapeDtypeStruct((t, LANES), F32),
                   jax.ShapeDtypeStruct((t // tm, SUBLANES, LANES), I32)),
        grid=(t // tm,),
        in_specs=[pl.BlockSpec((tm, k), lambda i: (i, 0)),
                  pl.BlockSpec((None, k, d), lambda i: (out_slot, 0, 0)),
                  pl.BlockSpec((tm, d), lambda i: (i, 0)),
                  pl.BlockSpec((1, d), lambda i: (0, 0)),
                  pl.BlockSpec((d, LANES), lambda i: (0, 0))],
        out_specs=(pl.BlockSpec((tm, d), lambda i: (i, 0)),
                   pl.BlockSpec((tm, d), lambda i: (i, 0)),
                   pl.BlockSpec((tm, LANES), lambda i: (i, 0)),
                   pl.BlockSpec((tm, LANES), lambda i: (i, 0)),
                   pl.BlockSpec((None, SUBLANES, LANES), lambda i: (i, 0, 0))),
        scratch_shapes=[pltpu.VMEM((tm, tm), BF16), pltpu.VMEM((k, d), BF16)],
        compiler_params=_cparams(1),
        name="moe_route",
    )(a, w_out, x, g.reshape(1, d), r_pad)


def _slot_rows(mi, loc_ref, base):
    e1, e2, d1, d2 = mi[0], mi[1], mi[2], mi[3]
    for e in range(N_EXPERTS):
        off = loc_ref[base + e]
        d1 = d1 + jnp.where(e1 == e, off, 0)
        d2 = d2 + jnp.where(e2 == e, off, 0)
    return d1, d2


def _segment_copies(base, seg_ref, n16_ref, loc_ref, hbm_ref, vmem_ref, sem, *, to_hbm, wait):
    for e in range(N_EXPERTS):
        n16 = n16_ref[base + e]
        hbm0 = seg_ref[base + e]
        vmem0 = loc_ref[base + e]
        off = 0
        for size in SEG_SIZES:
            @pl.when((n16 & size) != 0)
            def _(off=off, size=size):
                h = hbm_ref.at[pl.ds(pl.multiple_of(hbm0 + off, SEG_ALIGN), size)]
                v = vmem_ref.at[pl.ds(pl.multiple_of(vmem0 + off, SEG_ALIGN), size)]
                cp = pltpu.make_async_copy(v, h, sem.at[0]) if to_hbm else pltpu.make_async_copy(h, v, sem.at[0])
                if wait:
                    cp.wait()
                else:
                    cp.start()
            off = off + (n16 & size)


def _scatter_kernel(seg_ref, n16_ref, loc_ref, tail_ref, hn_ref, mi_ref, xs_ref, stage_ref, zero_ref, sem):
    i = pl.program_id(0)
    tm = hn_ref.shape[0]
    rows = stage_ref.shape[0]
    tg = zero_ref.shape[0]

    @pl.when(i == 0)
    def _():
        zero_ref[...] = jnp.zeros_like(zero_ref)
        for e in range(N_EXPERTS):
            tail = pl.multiple_of(tail_ref[e], SEG_ALIGN)
            pltpu.make_async_copy(zero_ref, xs_ref.at[pl.ds(tail, tg)], sem.at[0]).start()
        for e in range(N_EXPERTS):
            tail = pl.multiple_of(tail_ref[e], SEG_ALIGN)
            pltpu.make_async_copy(zero_ref, xs_ref.at[pl.ds(tail, tg)], sem.at[0]).wait()

    base = i * N_EXPERTS
    mi_t = mi_ref[...].T
    d1, d2 = _slot_rows([mi_t[k:k + 1, :] for k in range(4)], loc_ref, base)
    row = lax.broadcasted_iota(I32, (rows, tm), 0)
    onehot = ((row == d1) | (row == d2)).astype(BF16)
    stage_ref[...] = jnp.dot(onehot, hn_ref[...], preferred_element_type=F32).astype(BF16)
    kw = dict(to_hbm=True)
    _segment_copies(base, seg_ref, n16_ref, loc_ref, xs_ref, stage_ref, sem, wait=False, **kw)
    _segment_copies(base, seg_ref, n16_ref, loc_ref, xs_ref, stage_ref, sem, wait=True, **kw)


def _scatter(hn, mi, seg, n16, loc, tails, n_rows):
    t, d = hn.shape
    tm = SEG_TILE
    return pl.pallas_call(
        _scatter_kernel,
        out_shape=jax.ShapeDtypeStruct((n_rows, d), BF16),
        grid_spec=pltpu.PrefetchScalarGridSpec(
            num_scalar_prefetch=4,
            grid=(t // tm,),
            in_specs=[pl.BlockSpec((tm, d), lambda i, *_: (i, 0)),
                      pl.BlockSpec((tm, LANES), lambda i, *_: (i, 0))],
            out_specs=pl.BlockSpec(memory_space=pl.ANY),
            scratch_shapes=[pltpu.VMEM((SEG_STAGE, d), BF16), pltpu.VMEM((MOE_TILE, d), BF16),
                            pltpu.SemaphoreType.DMA((1,))]),
        compiler_params=pltpu.CompilerParams(dimension_semantics=("arbitrary",),
                                             vmem_limit_bytes=VMEM_LIMIT, has_side_effects=True),
        name="moe_scatter",
    )(seg, n16, loc, tails, hn, mi)


def _seg_expert_kernel(te_ref, tb_ref, nv_ref, xs_ref, wg_ref, wu_ref, wd_ref, ys_ref, acc_ref):
    i, f = pl.program_id(0), pl.program_id(1)

    @pl.when(i < nv_ref[0])
    def _():
        @pl.when(f == 0)
        def _():
            acc_ref[...] = jnp.zeros_like(acc_ref)

        h = xs_ref[...]
        gate = jnp.dot(h, wg_ref[...].astype(BF16), preferred_element_type=F32)
        up = jnp.dot(h, wu_ref[...].astype(BF16), preferred_element_type=F32)
        a = _silu_mul(gate, up).astype(BF16)
        acc_ref[...] += jnp.dot(a, wd_ref[...].astype(BF16), preferred_element_type=F32)

        @pl.when(f == pl.num_programs(1) - 1)
        def _():
            ys_ref[...] = acc_ref[...].astype(ys_ref.dtype)


def _seg_experts(xs, w_gate_up, w_down, slot, tile_expert, tile_block, n_valid, n_tiles):
    n_rows, d = xs.shape
    ff = w_down.shape[-2]
    tg, tf = MOE_TILE, FF_TILE
    nf = ff // tf

    def fidx(i, f, nv):
        return jnp.where(i < nv[0], f, nf - 1)

    return pl.pallas_call(
        _seg_expert_kernel,
        out_shape=jax.ShapeDtypeStruct((n_rows, d), BF16),
        grid_spec=pltpu.PrefetchScalarGridSpec(
            num_scalar_prefetch=3,
            grid=(n_tiles, nf),
            in_specs=[pl.BlockSpec((tg, d), lambda i, f, te, tb, nv: (tb[i], 0)),
                      pl.BlockSpec((None, None, d, tf),
                                   lambda i, f, te, tb, nv: (slot, te[i], 0, fidx(i, f, nv))),
                      pl.BlockSpec((None, None, d, tf),
                                   lambda i, f, te, tb, nv: (slot, te[i], 0, nf + fidx(i, f, nv))),
                      pl.BlockSpec((None, None, tf, d),
                                   lambda i, f, te, tb, nv: (slot, te[i], fidx(i, f, nv), 0))],
            out_specs=pl.BlockSpec((tg, d), lambda i, f, te, tb, nv: (tb[i], 0)),
            scratch_shapes=[pltpu.VMEM((tg, d), F32)]),
        compiler_params=_cparams(2),
        name="moe_experts",
    )(tile_expert, tile_block, n_valid, xs, w_gate_up, w_gate_up, w_down)


def _gather_kernel(seg_ref, n16_ref, loc_ref, x_ref, mi_ref, mf_ref, gf_ref, ys_ref, o_ref,
                   ybuf_ref, sem, *, final_norm):
    i = pl.program_id(0)
    tm = x_ref.shape[0]
    rows = ybuf_ref.shape[0]

    @pl.when(i == 0)
    def _():
        ybuf_ref[...] = jnp.zeros_like(ybuf_ref)

    base = i * N_EXPERTS
    kw = dict(to_hbm=False)
    _segment_copies(base, seg_ref, n16_ref, loc_ref, ys_ref, ybuf_ref, sem, wait=False, **kw)
    mi = mi_ref[...]
    d1, d2 = _slot_rows([mi[:, k:k + 1] for k in range(4)], loc_ref, base)
    col = lax.broadcasted_iota(I32, (tm, rows), 1)
    gates = mf_ref[...]
    _segment_copies(base, seg_ref, n16_ref, loc_ref, ys_ref, ybuf_ref, sem, wait=True, **kw)
    ybuf = ybuf_ref[...]
    out = x_ref[...]
    for k, dk in enumerate((d1, d2)):
        yk = jnp.dot((col == dk).astype(BF16), ybuf, preferred_element_type=F32)
        out = out + gates[:, k:k + 1] * yk
    if final_norm:
        out = _rms_rows(out, gf_ref[...])
    o_ref[...] = out


def _gather(x, mi, mf, ys, seg, n16, loc, final_g):
    t, d = x.shape
    tm = SEG_TILE
    final_norm = final_g is not None
    gf = (final_g if final_norm else jnp.ones((d,), F32)).reshape(1, d).astype(F32)
    return pl.pallas_call(
        functools.partial(_gather_kernel, final_norm=final_norm),
        out_shape=jax.ShapeDtypeStruct((t, d), F32),
        grid_spec=pltpu.PrefetchScalarGridSpec(
            num_scalar_prefetch=3,
            grid=(t // tm,),
            in_specs=[pl.BlockSpec((tm, d), lambda i, *_: (i, 0)),
                      pl.BlockSpec((tm, LANES), lambda i, *_: (i, 0)),
                      pl.BlockSpec((tm, LANES), lambda i, *_: (i, 0)),
                      pl.BlockSpec((1, d), lambda i, *_: (0, 0)),
                      pl.BlockSpec(memory_space=pl.ANY)],
            out_specs=pl.BlockSpec((tm, d), lambda i, *_: (i, 0)),
            scratch_shapes=[pltpu.VMEM((SEG_STAGE, d), BF16), pltpu.SemaphoreType.DMA((1,))]),
        compiler_params=_cparams(1),
        name="moe_gather",
    )(seg, n16, loc, x, mi, mf, gf, ys)


def _moe_seg(a, w_out, out_slot, x, g, router, w_gate_up, w_down, slot, final_g=None):
    t, d = x.shape
    tg = MOE_TILE
    x, hn, mi, mf, cnt = _route(a, w_out, out_slot, x, g, router)

    counts = cnt[:, 0, :N_EXPERTS]
    n16 = (counts + SEG_ALIGN - 1) // SEG_ALIGN * SEG_ALIGN
    rows_e = jnp.sum(n16, axis=0)
    tiles_e = (rows_e + tg - 1) // tg
    tile_start = jnp.cumsum(tiles_e) - tiles_e
    row_start = tile_start * tg
    seg = row_start[None, :] + jnp.cumsum(n16, axis=0) - n16
    loc = jnp.cumsum(n16, axis=1) - n16
    n_valid = jnp.sum(tiles_e).astype(I32)
    n_tiles = (2 * t + counts.size * (SEG_ALIGN - 1)) // tg + N_EXPERTS
    n_rows = (n_tiles + 1) * tg
    tidx = jnp.minimum(jnp.arange(n_tiles, dtype=I32), n_valid - 1)
    tile_expert = (jnp.sum(tidx[:, None] >= tile_start[None, :], axis=1) - 1).astype(I32)
    tails = (row_start + rows_e).astype(I32)
    seg, n16, loc = (a.reshape(-1).astype(I32) for a in (seg, n16, loc))

    xs = _scatter(hn, mi, seg, n16, loc, tails, n_rows)
    ys = _seg_experts(xs, w_gate_up, w_down, slot, tile_expert, tidx, n_valid.reshape(1), n_tiles)
    return _gather(x, mi, mf, ys, seg, n16, loc, final_g)


def kernel(x, rel_bias, norm_mix, norm_ffn, norm_final, a_w_in, a_w_group, a_scale, a_w_out,
           b_w_in, b_w_out, c_w_in, c_lambda, c_subln, c_w_out, d_w_in, d_sink, d_w_out,
           f_w_gate_up, f_w_down, m_router, m_w_gate_up, m_w_down):
    batch, seq, d = x.shape
    h = x.reshape(batch * seq, d)

    u = _norm_matmul(h, norm_mix[0], a_w_in, 0, F32)
    y = _pool_mix(u, a_w_group, a_scale, 0, batch, seq)
    h = _matmul_res(y, a_w_out, 0, h)
    h = _ffn(h, norm_ffn[0], f_w_gate_up, f_w_down, 0)

    qkv = _dil_proj(h.reshape(batch, seq, d), norm_mix[1], b_w_in, 0, tuple(p[1] for p in DIL_PAIRS))
    o = _dilated_attention(qkv, rel_bias, batch, seq)
    h = _moe_seg(o, b_w_out, 0, h, norm_ffn[1], m_router[0], m_w_gate_up, m_w_down, 0)

    qkv = _norm_matmul(h, norm_mix[2], c_w_in, 0, BF16)
    o = _diff_attention(qkv, c_lambda, c_subln, 0, rel_bias, batch, seq)
    h = _matmul_res(o, c_w_out, 0, h)
    h = _ffn(h, norm_ffn[2], f_w_gate_up, f_w_down, 1)

    qkv = _norm_matmul(h, norm_mix[3], d_w_in, 0, BF16)
    o = _gqa_attention(qkv, d_sink[0], rel_bias, batch, seq)
    h = _moe_seg(o, d_w_out, 0, h, norm_ffn[3], m_router[1], m_w_gate_up, m_w_down, 1, final_g=norm_final)
    return h.reshape(batch, seq, d)
```

```python
import collections
import functools
import math

import numpy as np
import jax
import jax.numpy as jnp
from jax import lax
from jax.experimental import pallas as pl
from jax.experimental.pallas import tpu as pltpu

F32 = jnp.float32
BF16 = jnp.bfloat16
I32 = jnp.int32

RMS_EPS = 1e-6
HEAD_DIM = 64
N_BUCKETS = 32
MAX_DISTANCE = 1024
POOL_WINDOWS = (2, 4, 8, 16)
DIL_PAIRS = ((128, 1), (512, 4), (2048, 16))
Q_SCALE = HEAD_DIM ** -0.5
GQA_KV_HEADS = 4
GQA_WINDOW = 128
N_EXPERTS = 8
DIFF_LAYER = 2

LANES = 128
SUBLANES = 8
VMEM_LIMIT = 56 * 1024 * 1024
ROW_TILE = 1024
FF_TILE = 512
FFN_ROW_SPLIT = 1
MOE_TILE = 1024
SEG_TILE = 512
ATT_TQ = 128
DIL_UNITS = 4
DIFF_ROW_SPLIT = 4
DIL_HEAD_SPLIT = 2
DIFF_TQ = 512
MASK_VALUE = -1e30


def _cparams(n_axes, vmem=VMEM_LIMIT):
    return pltpu.CompilerParams(dimension_semantics=("arbitrary",) * n_axes,
                                vmem_limit_bytes=vmem)


def _next_pow2(n):
    return 1 << (n - 1).bit_length()


def _rms_rows(x, g):
    ms = jnp.mean(x * x, axis=-1, keepdims=True)
    return x * lax.rsqrt(ms + RMS_EPS) * g


def _norm_matmul_kernel(x_ref, g_ref, w_ref, o_ref, hn_ref):
    @pl.when(pl.program_id(1) == 0)
    def _():
        hn_ref[...] = _rms_rows(x_ref[...], g_ref[...]).astype(BF16)

    o_ref[...] = jnp.dot(hn_ref[...], w_ref[...].astype(BF16),
                         preferred_element_type=F32).astype(o_ref.dtype)


def _norm_matmul(x, g, w, slot, out_dtype):
    t, d = x.shape
    n = w.shape[-1]
    tm = ROW_TILE
    tn = next(c for c in (1024, 768, 512, 256, 128) if n % c == 0)
    return pl.pallas_call(
        _norm_matmul_kernel,
        out_shape=jax.ShapeDtypeStruct((t, n), out_dtype),
        grid=(t // tm, n // tn),
        in_specs=[pl.BlockSpec((tm, d), lambda i, j: (i, 0)),
                  pl.BlockSpec((1, d), lambda i, j: (0, 0)),
                  pl.BlockSpec((None, d, tn), lambda i, j: (slot, 0, j))],
        out_specs=pl.BlockSpec((tm, tn), lambda i, j: (i, j)),
        scratch_shapes=[pltpu.VMEM((tm, d), BF16)],
        compiler_params=_cparams(2),
        name="norm_matmul",
    )(x, g.reshape(1, d), w)


def _matmul_res_kernel(a_ref, w_ref, x_ref, o_ref, wb_ref):
    @pl.when(pl.program_id(0) == 0)
    def _():
        wb_ref[...] = w_ref[...].astype(BF16)

    o_ref[...] = x_ref[...] + jnp.dot(a_ref[...], wb_ref[...], preferred_element_type=F32)


def _matmul_res(a, w, slot, x):
    t, k = a.shape
    d = w.shape[-1]
    tm = ROW_TILE
    return pl.pallas_call(
        _matmul_res_kernel,
        out_shape=jax.ShapeDtypeStruct((t, d), F32),
        grid=(t // tm,),
        in_specs=[pl.BlockSpec((tm, k), lambda i: (i, 0)),
                  pl.BlockSpec((None, k, d), lambda i: (slot, 0, 0)),
                  pl.BlockSpec((tm, d), lambda i: (i, 0))],
        out_specs=pl.BlockSpec((tm, d), lambda i: (i, 0)),
        scratch_shapes=[pltpu.VMEM((k, d), BF16)],
        compiler_params=_cparams(1),
        name="matmul_res",
    )(a, w, x)


def _silu_mul(g, u):
    return g * (1.0 / (1.0 + jnp.exp(-g))) * u


def _swiglu_accumulate(h_ref, wg_ref, wu_ref, wd_ref, acc_ref):
    wg = wg_ref[...].astype(BF16)
    wu = wu_ref[...].astype(BF16)
    wd = wd_ref[...].astype(BF16)
    rows = h_ref.shape[0] // FFN_ROW_SPLIT
    for r0 in range(0, h_ref.shape[0], rows):
        h = h_ref[pl.ds(r0, rows), :]
        gate = jnp.dot(h, wg, preferred_element_type=F32)
        up = jnp.dot(h, wu, preferred_element_type=F32)
        a = _silu_mul(gate, up).astype(BF16)
        acc_ref[pl.ds(r0, rows), :] += jnp.dot(a, wd, preferred_element_type=F32)


def _ffn_kernel(x_ref, g_ref, wg_ref, wu_ref, wd_ref, o_ref, hn_ref, acc_ref):
    f = pl.program_id(1)

    @pl.when(f == 0)
    def _():
        hn_ref[...] = _rms_rows(x_ref[...], g_ref[...]).astype(BF16)
        acc_ref[...] = jnp.zeros_like(acc_ref)

    _swiglu_accumulate(hn_ref, wg_ref, wu_ref, wd_ref, acc_ref)

    @pl.when(f == pl.num_programs(1) - 1)
    def _():
        o_ref[...] = x_ref[...] + acc_ref[...]


def _ffn(x, g, w_gate_up, w_down, slot):
    t, d = x.shape
    ff = w_down.shape[-2]
    tm, tf = ROW_TILE, FF_TILE
    nf = ff // tf
    return pl.pallas_call(
        _ffn_kernel,
        out_shape=jax.ShapeDtypeStruct((t, d), F32),
        grid=(t // tm, nf),
        in_specs=[pl.BlockSpec((tm, d), lambda i, f: (i, 0)),
                  pl.BlockSpec((1, d), lambda i, f: (0, 0)),
                  pl.BlockSpec((None, d, tf), lambda i, f: (slot, 0, f)),
                  pl.BlockSpec((None, d, tf), lambda i, f: (slot, 0, nf + f)),
                  pl.BlockSpec((None, tf, d), lambda i, f: (slot, f, 0))],
        out_specs=pl.BlockSpec((tm, d), lambda i, f: (i, 0)),
        scratch_shapes=[pltpu.VMEM((tm, d), BF16), pltpu.VMEM((tm, d), F32)],
        compiler_params=_cparams(2),
        name="ffn",
    )(x, g.reshape(1, d), w_gate_up, w_gate_up, w_down)


POOL_PAD = 16


def _pool_kernel(u_ref, wg_ref, sc_ref, o_ref, pad_ref):
    s, c = u_ref.shape
    grp = pl.program_id(1)
    pad_ref[pl.ds(0, POOL_PAD), :] = jnp.zeros((POOL_PAD, c), F32)
    pad_ref[pl.ds(POOL_PAD + s, POOL_PAD), :] = jnp.zeros((POOL_PAD, c), F32)
    pad_ref[pl.ds(POOL_PAD, s), :] = u_ref[...]
    wb = wg_ref[...].astype(BF16)
    chunk = 256

    for gi, w in enumerate(POOL_WINDOWS):
        @pl.when(grp == gi)
        def _(w=w):
            for c0 in range(0, s, chunk):
                acc = None
                for dlt in range(-(w // 2), w - w // 2):
                    piece = pad_ref[pl.ds(POOL_PAD + c0 + dlt, chunk), :]
                    acc = piece if acc is None else acc + piece
                pos = c0 + lax.broadcasted_iota(I32, (chunk, 1), 0)
                lo = jnp.maximum(pos - w // 2, 0)
                hi = jnp.minimum(pos + (w - w // 2), s)
                cnt = (hi - lo).astype(F32)
                mixed = acc / cnt - u_ref[pl.ds(c0, chunk), :]
                y = jnp.dot(mixed.astype(BF16), wb, preferred_element_type=F32)
                o_ref[pl.ds(c0, chunk), :] = (y * sc_ref[...]).astype(o_ref.dtype)


def _pool_mix(u, w_group, scale, slot, batch, seq):
    d = u.shape[1]
    ng = len(POOL_WINDOWS)
    c = d // ng
    u3 = u.reshape(batch, seq, d)
    out = pl.pallas_call(
        _pool_kernel,
        out_shape=jax.ShapeDtypeStruct((batch, seq, d), BF16),
        grid=(batch, ng),
        in_specs=[pl.BlockSpec((None, seq, c), lambda b, g: (b, 0, g)),
                  pl.BlockSpec((None, None, c, c), lambda b, g: (slot, g, 0, 0)),
                  pl.BlockSpec((1, c), lambda b, g: (slot, g))],
        out_specs=pl.BlockSpec((None, seq, c), lambda b, g: (b, 0, g)),
        scratch_shapes=[pltpu.VMEM((seq + 2 * POOL_PAD, c), F32)],
        compiler_params=_cparams(2),
        name="pool_mix",
    )(u3, w_group, scale)
    return out.reshape(batch * seq, d)


def _rel_bucket_np(rel):
    half = N_BUCKETS // 2
    max_exact = half // 2
    n = np.abs(rel)
    ratio = np.log(np.maximum(n, 1).astype(np.float32) / np.float32(max_exact))
    big = max_exact + (ratio / np.float32(math.log(MAX_DISTANCE / max_exact))
                       * np.float32(half - max_exact)).astype(np.int32)
    big = np.minimum(big, half - 1)
    return np.where(rel > 0, half, 0) + np.where(n < max_exact, n, big)


def _seg_table(rel_bias, rel, valid):
    bucket = jnp.asarray(_rel_bucket_np(rel).astype(np.int32))
    vals = jnp.take(rel_bias.astype(F32), bucket, axis=0)
    vals = jnp.where(jnp.asarray(valid)[..., None], vals, MASK_VALUE)
    return jnp.transpose(vals, (2, 0, 1))


def _toeplitz(seg_row, rows, cols):
    w = seg_row.shape[1]
    full = jnp.broadcast_to(seg_row, (rows, w))
    rolled = pltpu.roll(full, w - (rows - 1), 1, stride=1, stride_axis=0)
    return rolled[:, :cols]


def _band_variants(hw, n_tiles):
    if n_tiles == 1:
        return (0,)
    return (0, -hw, -2 * hw)


def _band_segs(rel_bias, tq, kw, half, dil, variants, width):
    c = np.arange(width)
    rel = np.stack([r0 + c - (tq - 1) for r0 in variants])
    valid = (np.abs(rel) <= half) & (c[None, :] < tq + kw - 1)
    return _seg_table(rel_bias, rel * dil, valid)


def _band_window(qi, nq, tq, hw, kw, ln):
    if nq == 1:
        return 0, 0
    var = jnp.where(qi == 0, 0, jnp.where(qi == nq - 1, 2, 1))
    return var, jnp.clip(qi * tq - hw, 0, ln - kw)


DilCfg = collections.namedtuple("DilCfg", "dil ln nq hw kw n_var")


def _dil_proj_kernel(x_ref, g_ref, w_ref, o_ref, slab_ref, hn_ref, *, dils):
    grp, c = pl.program_id(1), pl.program_id(2)
    seq, d = x_ref.shape
    n_slabs = d // LANES
    chunk = 256

    @pl.when((grp == 0) & (c == 0))
    def _():
        for c0 in range(0, seq, chunk):
            hn = _rms_rows(x_ref[pl.ds(c0, chunk), :], g_ref[...])
            for s in range(n_slabs):
                slab_ref[s, pl.ds(c0, chunk), :] = hn[:, s * LANES:(s + 1) * LANES]

    for gi, dil in enumerate(dils):
        @pl.when((grp == gi) & (c == 0))
        def _(dil=dil):
            ln = seq // dil
            for r in range(dil):
                for c0 in range(0, ln, chunk):
                    n = min(chunk, ln)
                    rows = [slab_ref[s, pl.ds(r + c0 * dil, n, stride=dil), :] for s in range(n_slabs)]
                    hn_ref[pl.ds(r * ln + c0, n), :] = jnp.concatenate(rows, axis=1).astype(BF16)

    res = jnp.dot(hn_ref[...], w_ref[...].astype(BF16), preferred_element_type=F32)
    res = res * jnp.where(c < pl.num_programs(2) // 3, Q_SCALE, 1.0)
    for s in range(o_ref.shape[0]):
        o_ref[s] = res[:, s * LANES:(s + 1) * LANES].astype(o_ref.dtype)


def _dil_proj(x3, g, w_in, slot, dils):
    batch, seq, d = x3.shape
    n = w_in.shape[-1]
    tn = FF_TILE
    per_group = n // len(dils) // tn
    spt = tn // LANES
    return pl.pallas_call(
        functools.partial(_dil_proj_kernel, dils=dils),
        out_shape=jax.ShapeDtypeStruct((batch, n // LANES, seq, LANES), BF16),
        grid=(batch, len(dils), per_group),
        in_specs=[pl.BlockSpec((None, seq, d), lambda b, g, c: (b, 0, 0)),
                  pl.BlockSpec((1, d), lambda b, g, c: (0, 0)),
                  pl.BlockSpec((None, d, tn), lambda b, g, c: (slot, 0, g * per_group + c))],
        out_specs=pl.BlockSpec((None, spt, seq, LANES), lambda b, g, c: (b, g * per_group + c, 0, 0)),
        scratch_shapes=[pltpu.VMEM((d // LANES, seq, LANES), F32), pltpu.VMEM((seq, d), BF16)],
        compiler_params=_cparams(3),
        name="dil_proj",
    )(x3, g.reshape(1, d), w_in)


def _dil_unit(cfg, unit, q4, k_ref, v_ref, bias_ref, acc_ref, m_ref, l_ref, *, tq, first, last):
    dil, ln, nq, kw = cfg.dil, cfg.ln, cfg.nq, cfg.kw
    n_pairs = q4.shape[0]
    if nq == 1:
        r, qi = unit, 0
    elif dil == 1:
        r, qi = 0, unit
    else:
        r, qi = unit // nq, unit % nq
    var, ks = _band_window(qi, nq, tq, cfg.hw, kw, ln)
    krow = pl.multiple_of(r * ln + ks, cfg.hw)
    if dil == 1:
        rows = pl.ds(pl.multiple_of(qi * tq, tq), tq)
    else:
        rows = pl.ds(qi * tq * dil + r, tq, stride=dil)

    even = lax.broadcasted_iota(I32, (n_pairs, tq, LANES), 2) < HEAD_DIM
    zero = jnp.zeros_like(q4)
    q8 = jnp.concatenate([jnp.where(even, q4, zero), jnp.where(even, zero, q4)], axis=0)
    k4 = k_ref[:, pl.ds(krow, kw), :]
    v4 = v_ref[:, pl.ds(krow, kw), :]
    k8 = jnp.concatenate([k4, k4], axis=0)
    va = jnp.concatenate([v4, jnp.ones_like(v4)], axis=2)
    v8 = jnp.concatenate([va, va], axis=0)
    s = lax.dot_general(q8, k8, (((2,), (2,)), ((0,), (0,))), preferred_element_type=F32)
    s = s + bias_ref[var]
    m8 = jnp.max(s, axis=-1, keepdims=True)
    p = jnp.exp(s - m8).astype(BF16)
    ov = lax.dot_general(p, v8, (((2,), (1,)), ((0,), (0,))), preferred_element_type=F32)
    o_c = jnp.where(even, ov[:n_pairs, :, :LANES], ov[n_pairs:, :, :LANES])
    l_c = jnp.where(even, ov[:n_pairs, :, LANES:], ov[n_pairs:, :, LANES:])
    m_c = jnp.where(even, m8[:n_pairs], m8[n_pairs:])
    if first:
        m_n, l_n, acc_n = m_c, l_c, o_c
    else:
        m_o = jnp.stack([m_ref[pp, rows, :] for pp in range(n_pairs)])
        l_o = jnp.stack([l_ref[pp, rows, :] for pp in range(n_pairs)])
        acc_o = jnp.stack([acc_ref[pp, rows, :] for pp in range(n_pairs)])
        m_n = jnp.maximum(m_o, m_c)
        a_o = jnp.exp(m_o - m_n)
        a_c = jnp.exp(m_c - m_n)
        l_n = a_o * l_o + a_c * l_c
        acc_n = a_o * acc_o + a_c * o_c
    if last:
        acc_n = acc_n / l_n
    for pp in range(n_pairs):
        acc_ref[pp, rows, :] = acc_n[pp]
        if not last:
            m_ref[pp, rows, :] = m_n[pp]
            l_ref[pp, rows, :] = l_n[pp]


def _dil_attn_kernel(*refs, cfgs, tq):
    ng = len(cfgs)
    seg_refs = refs[:ng]
    q_ref, k_ref, v_ref, o_ref = refs[ng:ng + 4]
    bias_refs = refs[ng + 4:2 * ng + 4]
    acc_ref, m_ref, l_ref = refs[2 * ng + 4:]
    hf, b, grp, step = (pl.program_id(i) for i in range(4))
    n_pairs = q_ref.shape[0]

    @pl.when((b == 0) & (grp == 0) & (step == 0))
    def _():
        for gi, cfg in enumerate(cfgs):
            for v in range(cfg.n_var):
                for odd in range(2):
                    for pp in range(n_pairs):
                        head = hf * 2 * n_pairs + 2 * pp + odd
                        row = seg_refs[gi][head, pl.ds(v, 1), :]
                        bias_refs[gi][v, odd * n_pairs + pp] = _toeplitz(row, tq, cfg.kw)

    for gi, cfg in enumerate(cfgs):
        @pl.when(grp == gi)
        def _(gi=gi, cfg=cfg):
            for uu in range(DIL_UNITS):
                _dil_unit(cfg, step * DIL_UNITS + uu, q_ref[:, pl.ds(uu * tq, tq), :], k_ref, v_ref,
                          bias_refs[gi], acc_ref, m_ref, l_ref, tq=tq, first=gi == 0, last=gi == ng - 1)

    @pl.when((grp == ng - 1) & (step == pl.num_programs(3) - 1))
    def _():
        slabs = [acc_ref[s] for s in range(n_pairs)]
        o_ref[...] = jnp.concatenate(slabs, axis=1).astype(o_ref.dtype)


def _dilated_attention(qkv, rel_bias, batch, seq):
    ng = len(DIL_PAIRS)
    d = qkv.shape[1] * LANES // (3 * ng)
    hd = d // DIL_HEAD_SPLIT
    n_pairs = hd // LANES
    tq = ATT_TQ
    cfgs, segs = [], []
    for win, dil in DIL_PAIRS:
        half = win // (2 * dil)
        ln = seq // dil
        nq = ln // tq
        kw = min(tq + 2 * half, ln)
        variants = _band_variants(half, nq)
        cfgs.append(DilCfg(dil, ln, nq, half, kw, len(variants)))
        segs.append(_band_segs(rel_bias, tq, kw, half, dil, variants, _next_pow2(tq + kw - 1)))
    steps = seq // tq // DIL_UNITS
    cb = DIL_HEAD_SPLIT

    in_specs = [pl.BlockSpec(sg.shape, lambda hf, b, g, s: (0, 0, 0)) for sg in segs]
    in_specs += [pl.BlockSpec((None, n_pairs, DIL_UNITS * tq, LANES),
                              lambda hf, b, g, s: (b, (g * 3) * cb + hf, s, 0)),
                 pl.BlockSpec((None, n_pairs, seq, LANES),
                              lambda hf, b, g, s: (b, (g * 3 + 1) * cb + hf, 0, 0)),
                 pl.BlockSpec((None, n_pairs, seq, LANES),
                              lambda hf, b, g, s: (b, (g * 3 + 2) * cb + hf, 0, 0))]
    scratch = [pltpu.VMEM((c.n_var, 2 * n_pairs, tq, c.kw), F32) for c in cfgs]
    scratch += [pltpu.VMEM((n_pairs, seq, LANES), F32) for _ in range(3)]
    out = pl.pallas_call(
        functools.partial(_dil_attn_kernel, cfgs=tuple(cfgs), tq=tq),
        out_shape=jax.ShapeDtypeStruct((batch, seq, d), BF16),
        grid=(DIL_HEAD_SPLIT, batch, ng, steps),
        in_specs=in_specs,
        out_specs=pl.BlockSpec((None, seq, hd), lambda hf, b, g, s: (b, 0, hf)),
        scratch_shapes=scratch,
        compiler_params=_cparams(4),
        name="dil_attn",
    )(*segs, qkv, qkv, qkv)
    return out.reshape(batch * seq, d)


def _gqa_kernel(sink_ref, seg_ref, q_ref, k_ref, v_ref, o_ref, bias_ref, *, tq, kw, hw, grp):
    b, qi = pl.program_id(0), pl.program_id(1)
    nq = pl.num_programs(1)
    seq_len = k_ref.shape[0]
    n_kv = k_ref.shape[1] // HEAD_DIM

    @pl.when((b == 0) & (qi == 0))
    def _():
        for v in range(3):
            for kh in range(n_kv):
                for gq in range(grp):
                    bias_ref[v, kh, pl.ds(gq * tq, tq), :] = _toeplitz(
                        seg_ref[kh * grp + gq, pl.ds(v, 1), :], tq, kw)

    var = jnp.where(qi == 0, 0, jnp.where(qi == nq - 1, 2, 1))
    ks = pl.multiple_of(jnp.clip(qi * tq - hw, 0, seq_len - kw), hw)
    row = lax.broadcasted_iota(I32, (grp * tq, 1), 0)
    outs = [None] * (n_kv * grp)
    for kh in range(n_kv):
        cs = slice(kh * HEAD_DIM, (kh + 1) * HEAD_DIM)
        qs = jnp.concatenate(
            [q_ref[:, (kh * grp + gq) * HEAD_DIM:(kh * grp + gq + 1) * HEAD_DIM] for gq in range(grp)],
            axis=0) * Q_SCALE
        kk = k_ref[pl.ds(ks, kw), cs]
        vv = v_ref[pl.ds(ks, kw), cs]
        s = lax.dot_general(qs, kk, (((1,), (1,)), ((), ())), preferred_element_type=F32)
        s = s + bias_ref[var, kh]
        sk = jnp.zeros((grp * tq, 1), F32)
        for gq in range(grp):
            sk = jnp.where((row >= gq * tq) & (row < (gq + 1) * tq), sink_ref[kh * grp + gq], sk)
        m = jnp.maximum(jnp.max(s, axis=-1, keepdims=True), sk)
        e = jnp.exp(s - m).astype(BF16)
        va = jnp.concatenate([vv, jnp.ones_like(vv)], axis=1)
        ov = jnp.dot(e, va, preferred_element_type=F32)
        o = ov[:, :HEAD_DIM] / (ov[:, HEAD_DIM:] + jnp.exp(sk - m))
        for gq in range(grp):
            outs[kh * grp + gq] = o[gq * tq:(gq + 1) * tq, :]
    o_ref[...] = jnp.concatenate(outs, axis=1).astype(o_ref.dtype)


def _gqa_attention(qkv, sink, rel_bias, batch, seq):
    t, ncol = qkv.shape
    n_q = sink.shape[0]
    d = n_q * HEAD_DIM
    kvw = GQA_KV_HEADS * HEAD_DIM
    grp = n_q // GQA_KV_HEADS
    tq = hw = GQA_WINDOW
    kw = 3 * GQA_WINDOW
    nq = seq // tq
    variants = _band_variants(hw, nq)
    width = _next_pow2(tq + kw - 1)
    seg = _band_segs(rel_bias, tq, kw, GQA_WINDOW, 1, variants, width)
    qkv_v = qkv.reshape(batch, seq, ncol)
    out = pl.pallas_call(
        functools.partial(_gqa_kernel, tq=tq, kw=kw, hw=hw, grp=grp),
        out_shape=jax.ShapeDtypeStruct((batch, seq, d), BF16),
        grid=(batch, nq),
        in_specs=[pl.BlockSpec(memory_space=pltpu.SMEM),
                  pl.BlockSpec((n_q, 3, width), lambda b, qi: (0, 0, 0)),
                  pl.BlockSpec((None, tq, d), lambda b, qi: (b, qi, 0)),
                  pl.BlockSpec((None, seq, kvw), lambda b, qi: (b, 0, d // kvw)),
                  pl.BlockSpec((None, seq, kvw), lambda b, qi: (b, 0, d // kvw + 1))],
        out_specs=pl.BlockSpec((None, tq, d), lambda b, qi: (b, qi, 0)),
        scratch_shapes=[pltpu.VMEM((3, GQA_KV_HEADS, grp * tq, kw), F32)],
        compiler_params=_cparams(2),
        name="gqa_attn",
    )(sink.astype(F32), seg, qkv_v, qkv_v, qkv_v)
    return out.reshape(t, d)


def _diff_kernel(seg_ref, lam_ref, sub_ref, q_ref, k_ref, v_ref, o_ref, bias_ref, *, tq, lam_init):
    h, qi, b = pl.program_id(0), pl.program_id(1), pl.program_id(2)
    seq_len = k_ref.shape[0]
    nk = seq_len // tq

    @pl.when(b == 0)
    def _():
        for j in range(2):
            for ki in range(nk):
                row = seg_ref[h * 2 + j, pl.ds(ki - qi + nk - 1, 1), :]
                bias_ref[j, :, pl.ds(ki * tq, tq)] = _toeplitz(row, tq, tq)

    lv = lam_ref[...]
    s01 = jnp.sum(lv[0:1, :] * lv[1:2, :], axis=-1, keepdims=True)
    s23 = jnp.sum(lv[2:3, :] * lv[3:4, :], axis=-1, keepdims=True)
    lam = jnp.exp(s01) - jnp.exp(s23) + lam_init

    q = q_ref[...] * Q_SCALE
    k = k_ref[...]
    v = v_ref[...]
    va = jnp.concatenate([v, jnp.ones_like(v)], axis=1)
    map0 = lax.broadcasted_iota(I32, q.shape, 1) < HEAD_DIM
    zero = jnp.zeros_like(q)
    qm = (jnp.where(map0, q, zero), jnp.where(map0, zero, q))
    rc = tq // DIFF_ROW_SPLIT
    for c0 in range(0, tq, rc):
        outs = []
        for j in range(2):
            s = lax.dot_general(qm[j][c0:c0 + rc], k, (((1,), (1,)), ((), ())),
                                preferred_element_type=F32)
            s = s + bias_ref[j, pl.ds(c0, rc), :]
            m = jnp.max(s, axis=-1, keepdims=True)
            p = jnp.exp(s - m).astype(BF16)
            ov = jnp.dot(p, va, preferred_element_type=F32)
            outs.append(ov[:, :2 * HEAD_DIM] / ov[:, 2 * HEAD_DIM:])
        o = outs[0] - lam * outs[1]
        o = _rms_rows(o, sub_ref[...]) * (1.0 - lam_init)
        o_ref[pl.ds(c0, rc), :] = o.astype(o_ref.dtype)


def _diff_attention(qkv, lam_vecs, subln, slot, rel_bias, batch, seq):
    t, ncol = qkv.shape
    d = ncol // 3
    hd2 = 2 * HEAD_DIM
    n_heads = d // hd2
    tq = DIFF_TQ
    nk = seq // tq
    width = _next_pow2(2 * tq - 1)
    c = np.arange(width)
    rel = np.stack([(dl - (nk - 1)) * tq + c - (tq - 1) for dl in range(2 * nk - 1)])
    seg = _seg_table(rel_bias, rel, np.ones_like(rel, dtype=bool))
    lam_init = 0.8 - 0.6 * math.exp(-0.3 * DIFF_LAYER)
    qkv_v = qkv.reshape(batch, seq, ncol)
    out = pl.pallas_call(
        functools.partial(_diff_kernel, tq=tq, lam_init=lam_init),
        out_shape=jax.ShapeDtypeStruct((batch, seq, d), BF16),
        grid=(n_heads, seq // tq, batch),
        in_specs=[pl.BlockSpec(seg.shape, lambda h, qi, b: (0, 0, 0)),
                  pl.BlockSpec((None,) + lam_vecs.shape[1:], lambda h, qi, b: (slot, 0, 0)),
                  pl.BlockSpec((1, hd2), lambda h, qi, b: (slot, 0)),
                  pl.BlockSpec((None, tq, hd2), lambda h, qi, b: (b, qi, h)),
                  pl.BlockSpec((None, seq, hd2), lambda h, qi, b: (b, 0, n_heads + h)),
                  pl.BlockSpec((None, seq, hd2), lambda h, qi, b: (b, 0, 2 * n_heads + h))],
        out_specs=pl.BlockSpec((None, tq, hd2), lambda h, qi, b: (b, qi, h)),
        scratch_shapes=[pltpu.VMEM((2, tq, seq), F32)],
        compiler_params=_cparams(3),
        name="diff_attn",
    )(seg, lam_vecs, subln, qkv_v, qkv_v, qkv_v)
    return out.reshape(t, d)


SEG_ALIGN = 16
SEG_SIZES = (512, 256, 128, 64, 32, 16)
SEG_STAGE = 2 * SEG_TILE + N_EXPERTS * SEG_ALIGN


def _route_kernel(a_ref, wo_ref, x_ref, g_ref, r_ref, xn_ref, hn_ref, mi_ref, mf_ref, cnt_ref,
                  tri_ref, wb_ref):
    tm, d = x_ref.shape

    @pl.when(pl.program_id(0) == 0)
    def _():
        rr = lax.broadcasted_iota(I32, (tm, tm), 0)
        cc = lax.broadcasted_iota(I32, (tm, tm), 1)
        tri_ref[...] = (cc < rr).astype(BF16)
        wb_ref[...] = wo_ref[...].astype(BF16)

    xn = x_ref[...] + jnp.dot(a_ref[...], wb_ref[...], preferred_element_type=F32)
    xn_ref[...] = xn
    hn = _rms_rows(xn, g_ref[...])
    hn_ref[...] = hn.astype(BF16)
    h_hi = hn.astype(BF16)
    h_lo = (hn - h_hi.astype(F32)).astype(BF16)
    r = r_ref[...]
    r_hi = r.astype(BF16)
    r_lo = (r - r_hi.astype(F32)).astype(BF16)
    logits = (jnp.dot(h_hi, r_hi, preferred_element_type=F32)
              + jnp.dot(h_hi, r_lo, preferred_element_type=F32)
              + jnp.dot(h_lo, r_hi, preferred_element_type=F32))
    lane = lax.broadcasted_iota(I32, (tm, LANES), 1)
    logits = jnp.where(lane < N_EXPERTS, logits, -jnp.inf)
    v1 = jnp.max(logits, axis=-1, keepdims=True)
    i1 = jnp.min(jnp.where(logits == v1, lane, LANES), axis=-1, keepdims=True)
    oh1 = lane == i1
    rest = jnp.where(oh1, -jnp.inf, logits)
    v2 = jnp.max(rest, axis=-1, keepdims=True)
    i2 = jnp.min(jnp.where(rest == v2, lane, LANES), axis=-1, keepdims=True)
    oh2 = lane == i2
    e2 = jnp.exp(v2 - v1)
    g1 = 1.0 / (1.0 + e2)
    g2 = e2 / (1.0 + e2)

    sel = (oh1 | oh2)
    before = jnp.dot(tri_ref[...], sel.astype(BF16), preferred_element_type=F32)
    rank1 = jnp.sum(jnp.where(oh1, before, 0.0), axis=-1, keepdims=True).astype(I32)
    rank2 = jnp.sum(jnp.where(oh2, before, 0.0), axis=-1, keepdims=True).astype(I32)
    counts = jnp.sum(sel.astype(F32), axis=0, keepdims=True).astype(I32)

    mi_ref[...] = jnp.where(lane == 0, i1, jnp.where(lane == 1, i2,
                            jnp.where(lane == 2, rank1, jnp.where(lane == 3, rank2, 0))))
    mf_ref[...] = jnp.where(lane == 0, g1, jnp.where(lane == 1, g2, 0.0))
    cnt_ref[...] = jnp.broadcast_to(counts, cnt_ref.shape)


def _route(a, w_out, out_slot, x, g, router):
    t, d = x.shape
    k = a.shape[1]
    tm = SEG_TILE
    r_pad = jnp.zeros((d, LANES), F32).at[:, :N_EXPERTS].set(router.astype(F32))
    return pl.pallas_call(
        _route_kernel,
        out_shape=(jax.ShapeDtypeStruct((t, d), F32),
                   jax.ShapeDtypeStruct((t, d), BF16),
                   jax.ShapeDtypeStruct((t, LANES), I32),
                   jax.ShapeDtypeStruct((t, LANES), F32),
                   jax.ShapeDtypeStruct((t // tm, SUBLANES, LANES), I32)),
        grid=(t // tm,),
        in_specs=[pl.BlockSpec((tm, k), lambda i: (i, 0)),
                  pl.BlockSpec((None, k, d), lambda i: (out_slot, 0, 0)),
                  pl.BlockSpec((tm, d), lambda i: (i, 0)),
                  pl.BlockSpec((1, d), lambda i: (0, 0)),
                  pl.BlockSpec((d, LANES), lambda i: (0, 0))],
        out_specs=(pl.BlockSpec((tm, d), lambda i: (i, 0)),
                   pl.BlockSpec((tm, d), lambda i: (i, 0)),
                   pl.BlockSpec((tm, LANES), lambda i: (i, 0)),
                   pl.BlockSpec((tm, LANES), lambda i: (i, 0)),
                   pl.BlockSpec((None, SUBLANES, LANES), lambda i: (i, 0, 0))),
        scratch_shapes=[pltpu.VMEM((tm, tm), BF16), pltpu.VMEM((k, d), BF16)],
        compiler_params=_cparams(1),
        name="moe_route",
    )(a, w_out, x, g.reshape(1, d), r_pad)


def _slot_rows(mi, loc_ref, base):
    e1, e2, d1, d2 = mi[0], mi[1], mi[2], mi[3]
    for e in range(N_EXPERTS):
        off = loc_ref[base + e]
        d1 = d1 + jnp.where(e1 == e, off, 0)
        d2 = d2 + jnp.where(e2 == e, off, 0)
    return d1, d2


def _segment_copies(base, seg_ref, n16_ref, loc_ref, hbm_ref, vmem_ref, sem, *, to_hbm, wait):
    for e in range(N_EXPERTS):
        n16 = n16_ref[base + e]
        hbm0 = seg_ref[base + e]
        vmem0 = loc_ref[base + e]
        off = 0
        for size in SEG_SIZES:
            @pl.when((n16 & size) != 0)
            def _(off=off, size=size):
                h = hbm_ref.at[pl.ds(pl.multiple_of(hbm0 + off, SEG_ALIGN), size)]
                v = vmem_ref.at[pl.ds(pl.multiple_of(vmem0 + off, SEG_ALIGN), size)]
                cp = pltpu.make_async_copy(v, h, sem) if to_hbm else pltpu.make_async_copy(h, v, sem)
                if wait:
                    cp.wait()
                else:
                    cp.start()
            off = off + (n16 & size)


def _scatter_kernel(seg_ref, n16_ref, loc_ref, tail_ref, hn_ref, mi_ref, xs_ref, stage_ref, zero_ref, sem):
    i = pl.program_id(0)
    tm = hn_ref.shape[0]
    rows = stage_ref.shape[1]
    tg = zero_ref.shape[0]
    slot = lax.rem(i, 2)

    @pl.when(i == 0)
    def _():
        zero_ref[...] = jnp.zeros_like(zero_ref)
        for e in range(N_EXPERTS):
            tail = pl.multiple_of(tail_ref[e], SEG_ALIGN)
            pltpu.make_async_copy(zero_ref, xs_ref.at[pl.ds(tail, tg)], sem.at[2]).start()
        for e in range(N_EXPERTS):
            tail = pl.multiple_of(tail_ref[e], SEG_ALIGN)
            pltpu.make_async_copy(zero_ref, xs_ref.at[pl.ds(tail, tg)], sem.at[2]).wait()

    base = i * N_EXPERTS
    mi_t = mi_ref[...].T
    d1, d2 = _slot_rows([mi_t[k:k + 1, :] for k in range(4)], loc_ref, base)
    row = lax.broadcasted_iota(I32, (rows, tm), 0)
    onehot = ((row == d1) | (row == d2)).astype(BF16)
    stage_ref[slot] = jnp.dot(onehot, hn_ref[...], preferred_element_type=F32).astype(BF16)
    tabs = (seg_ref, n16_ref, loc_ref, xs_ref)
    _segment_copies(base, *tabs, stage_ref.at[slot], sem.at[slot], to_hbm=True, wait=False)

    @pl.when(i > 0)
    def _():
        _segment_copies(base - N_EXPERTS, *tabs, stage_ref.at[1 - slot], sem.at[1 - slot],
                        to_hbm=True, wait=True)

    @pl.when(i == pl.num_programs(0) - 1)
    def _():
        _segment_copies(base, *tabs, stage_ref.at[slot], sem.at[slot], to_hbm=True, wait=True)


def _scatter(hn, mi, seg, n16, loc, tails, n_rows):
    t, d = hn.shape
    tm = SEG_TILE
    return pl.pallas_call(
        _scatter_kernel,
        out_shape=jax.ShapeDtypeStruct((n_rows, d), BF16),
        grid_spec=pltpu.PrefetchScalarGridSpec(
            num_scalar_prefetch=4,
            grid=(t // tm,),
            in_specs=[pl.BlockSpec((tm, d), lambda i, *_: (i, 0)),
                      pl.BlockSpec((tm, LANES), lambda i, *_: (i, 0))],
            out_specs=pl.BlockSpec(memory_space=pl.ANY),
            scratch_shapes=[pltpu.VMEM((2, SEG_STAGE, d), BF16), pltpu.VMEM((MOE_TILE, d), BF16),
                            pltpu.SemaphoreType.DMA((3,))]),
        compiler_params=pltpu.CompilerParams(dimension_semantics=("arbitrary",),
                                             vmem_limit_bytes=VMEM_LIMIT, has_side_effects=True),
        name="moe_scatter",
    )(seg, n16, loc, tails, hn, mi)


def _seg_expert_kernel(te_ref, tb_ref, nv_ref, xs_ref, wg_ref, wu_ref, wd_ref, ys_ref, acc_ref):
    i, f = pl.program_id(0), pl.program_id(1)

    @pl.when(i < nv_ref[0])
    def _():
        @pl.when(f == 0)
        def _():
            acc_ref[...] = jnp.zeros_like(acc_ref)

        _swiglu_accumulate(xs_ref, wg_ref, wu_ref, wd_ref, acc_ref)

        @pl.when(f == pl.num_programs(1) - 1)
        def _():
            ys_ref[...] = acc_ref[...].astype(ys_ref.dtype)


def _seg_experts(xs, w_gate_up, w_down, slot, tile_expert, tile_block, n_valid, n_tiles):
    n_rows, d = xs.shape
    ff = w_down.shape[-2]
    tg, tf = MOE_TILE, FF_TILE
    nf = ff // tf

    def fidx(i, f, nv):
        return jnp.where(i < nv[0], f, nf - 1)

    return pl.pallas_call(
        _seg_expert_kernel,
        out_shape=jax.ShapeDtypeStruct((n_rows, d), BF16),
        grid_spec=pltpu.PrefetchScalarGridSpec(
            num_scalar_prefetch=3,
            grid=(n_tiles, nf),
            in_specs=[pl.BlockSpec((tg, d), lambda i, f, te, tb, nv: (tb[i], 0)),
                      pl.BlockSpec((None, None, d, tf),
                                   lambda i, f, te, tb, nv: (slot, te[i], 0, fidx(i, f, nv))),
                      pl.BlockSpec((None, None, d, tf),
                                   lambda i, f, te, tb, nv: (slot, te[i], 0, nf + fidx(i, f, nv))),
                      pl.BlockSpec((None, None, tf, d),
                                   lambda i, f, te, tb, nv: (slot, te[i], fidx(i, f, nv), 0))],
            out_specs=pl.BlockSpec((tg, d), lambda i, f, te, tb, nv: (tb[i], 0)),
            scratch_shapes=[pltpu.VMEM((tg, d), F32)]),
        compiler_params=_cparams(2),
        name="moe_experts",
    )(tile_expert, tile_block, n_valid, xs, w_gate_up, w_gate_up, w_down)


def _gather_kernel(seg_ref, n16_ref, loc_ref, x_ref, mi_ref, mf_ref, gf_ref, ys_ref, o_ref,
                   ybuf_ref, sem, *, final_norm):
    i = pl.program_id(0)
    tm = x_ref.shape[0]
    rows = ybuf_ref.shape[1]
    slot = lax.rem(i, 2)
    base = i * N_EXPERTS
    tabs = (seg_ref, n16_ref, loc_ref, ys_ref)

    @pl.when(i == 0)
    def _():
        ybuf_ref[...] = jnp.zeros_like(ybuf_ref)
        _segment_copies(base, *tabs, ybuf_ref.at[0], sem.at[0], to_hbm=False, wait=False)

    @pl.when(i + 1 < pl.num_programs(0))
    def _():
        _segment_copies(base + N_EXPERTS, *tabs, ybuf_ref.at[1 - slot], sem.at[1 - slot],
                        to_hbm=False, wait=False)

    mi = mi_ref[...]
    d1, d2 = _slot_rows([mi[:, k:k + 1] for k in range(4)], loc_ref, base)
    col = lax.broadcasted_iota(I32, (tm, rows), 1)
    gates = mf_ref[...]
    _segment_copies(base, *tabs, ybuf_ref.at[slot], sem.at[slot], to_hbm=False, wait=True)
    ybuf = ybuf_ref[slot]
    out = x_ref[...]
    for k, dk in enumerate((d1, d2)):
        yk = jnp.dot((col == dk).astype(BF16), ybuf, preferred_element_type=F32)
        out = out + gates[:, k:k + 1] * yk
    if final_norm:
        out = _rms_rows(out, gf_ref[...])
    o_ref[...] = out


def _gather(x, mi, mf, ys, seg, n16, loc, final_g):
    t, d = x.shape
    tm = SEG_TILE
    final_norm = final_g is not None
    gf = (final_g if final_norm else jnp.ones((d,), F32)).reshape(1, d).astype(F32)
    return pl.pallas_call(
        functools.partial(_gather_kernel, final_norm=final_norm),
        out_shape=jax.ShapeDtypeStruct((t, d), F32),
        grid_spec=pltpu.PrefetchScalarGridSpec(
            num_scalar_prefetch=3,
            grid=(t // tm,),
            in_specs=[pl.BlockSpec((tm, d), lambda i, *_: (i, 0)),
                      pl.BlockSpec((tm, LANES), lambda i, *_: (i, 0)),
                      pl.BlockSpec((tm, LANES), lambda i, *_: (i, 0)),
                      pl.BlockSpec((1, d), lambda i, *_: (0, 0)),
                      pl.BlockSpec(memory_space=pl.ANY)],
            out_specs=pl.BlockSpec((tm, d), lambda i, *_: (i, 0)),
            scratch_shapes=[pltpu.VMEM((2, SEG_STAGE, d), BF16), pltpu.SemaphoreType.DMA((2,))]),
        compiler_params=_cparams(1),
        name="moe_gather",
    )(seg, n16, loc, x, mi, mf, gf, ys)


def _moe_seg(a, w_out, out_slot, x, g, router, w_gate_up, w_down, slot, final_g=None):
    t, d = x.shape
    tg = MOE_TILE
    x, hn, mi, mf, cnt = _route(a, w_out, out_slot, x, g, router)

    counts = cnt[:, 0, :N_EXPERTS]
    n16 = (counts + SEG_ALIGN - 1) // SEG_ALIGN * SEG_ALIGN
    rows_e = jnp.sum(n16, axis=0)
    tiles_e = (rows_e + tg - 1) // tg
    tile_start = jnp.cumsum(tiles_e) - tiles_e
    row_start = tile_start * tg
    seg = row_start[None, :] + jnp.cumsum(n16, axis=0) - n16
    loc = jnp.cumsum(n16, axis=1) - n16
    n_valid = jnp.sum(tiles_e).astype(I32)
    n_tiles = (2 * t + counts.size * (SEG_ALIGN - 1)) // tg + N_EXPERTS
    n_rows = (n_tiles + 1) * tg
    tidx = jnp.minimum(jnp.arange(n_tiles, dtype=I32), n_valid - 1)
    tile_expert = (jnp.sum(tidx[:, None] >= tile_start[None, :], axis=1) - 1).astype(I32)
    tails = (row_start + rows_e).astype(I32)
    seg, n16, loc = (a.reshape(-1).astype(I32) for a in (seg, n16, loc))

    xs = _scatter(hn, mi, seg, n16, loc, tails, n_rows)
    ys = _seg_experts(xs, w_gate_up, w_down, slot, tile_expert, tidx, n_valid.reshape(1), n_tiles)
    return _gather(x, mi, mf, ys, seg, n16, loc, final_g)


def kernel(x, rel_bias, norm_mix, norm_ffn, norm_final, a_w_in, a_w_group, a_scale, a_w_out,
           b_w_in, b_w_out, c_w_in, c_lambda, c_subln, c_w_out, d_w_in, d_sink, d_w_out,
           f_w_gate_up, f_w_down, m_router, m_w_gate_up, m_w_down):
    batch, seq, d = x.shape
    h = x.reshape(batch * seq, d)

    u = _norm_matmul(h, norm_mix[0], a_w_in, 0, F32)
    y = _pool_mix(u, a_w_group, a_scale, 0, batch, seq)
    h = _matmul_res(y, a_w_out, 0, h)
    h = _ffn(h, norm_ffn[0], f_w_gate_up, f_w_down, 0)

    qkv = _dil_proj(h.reshape(batch, seq, d), norm_mix[1], b_w_in, 0, tuple(p[1] for p in DIL_PAIRS))
    o = _dilated_attention(qkv, rel_bias, batch, seq)
    h = _moe_seg(o, b_w_out, 0, h, norm_ffn[1], m_router[0], m_w_gate_up, m_w_down, 0)

    qkv = _norm_matmul(h, norm_mix[2], c_w_in, 0, BF16)
    o = _diff_attention(qkv, c_lambda, c_subln, 0, rel_bias, batch, seq)
    h = _matmul_res(o, c_w_out, 0, h)
    h = _ffn(h, norm_ffn[2], f_w_gate_up, f_w_down, 1)

    qkv = _norm_matmul(h, norm_mix[3], d_w_in, 0, BF16)
    o = _gqa_attention(qkv, d_sink[0], rel_bias, batch, seq)
    h = _moe_seg(o, d_w_out, 0, h, norm_ffn[3], m_router[1], m_w_gate_up, m_w_down, 1, final_g=norm_final)
    return h.reshape(batch, seq, d)
```

```python
import collections
import functools
import math

import numpy as np
import jax
import jax.numpy as jnp
from jax import lax
from jax.experimental import pallas as pl
from jax.experimental.pallas import tpu as pltpu

F32 = jnp.float32
BF16 = jnp.bfloat16
I32 = jnp.int32

RMS_EPS = 1e-6
HEAD_DIM = 64
N_BUCKETS = 32
MAX_DISTANCE = 1024
POOL_WINDOWS = (2, 4, 8, 16)
DIL_PAIRS = ((128, 1), (512, 4), (2048, 16))
Q_SCALE = HEAD_DIM ** -0.5
GQA_KV_HEADS = 4
GQA_WINDOW = 128
N_EXPERTS = 8
DIFF_LAYER = 2

LANES = 128
SUBLANES = 8
VMEM_LIMIT = 56 * 1024 * 1024
ROW_TILE = 1024
FF_TILE = 512
FFN_ROW_SPLIT = 1
MOE_TILE = 1024
SEG_TILE = 512
ATT_TQ = 128
DIL_UNITS = 4
DIFF_ROW_SPLIT = 4
DIL_HEAD_SPLIT = 2
DIFF_TQ = 512
MASK_VALUE = -1e30


def _cparams(n_axes, vmem=VMEM_LIMIT):
    return pltpu.CompilerParams(dimension_semantics=("arbitrary",) * n_axes,
                                vmem_limit_bytes=vmem)


def _next_pow2(n):
    return 1 << (n - 1).bit_length()


def _rms_rows(x, g):
    ms = jnp.mean(x * x, axis=-1, keepdims=True)
    return x * lax.rsqrt(ms + RMS_EPS) * g


def _norm_matmul_kernel(x_ref, g_ref, w_ref, o_ref, hn_ref):
    @pl.when(pl.program_id(1) == 0)
    def _():
        hn_ref[...] = _rms_rows(x_ref[...], g_ref[...]).astype(BF16)

    o_ref[...] = jnp.dot(hn_ref[...], w_ref[...].astype(BF16),
                         preferred_element_type=F32).astype(o_ref.dtype)


def _norm_matmul(x, g, w, slot, out_dtype):
    t, d = x.shape
    n = w.shape[-1]
    tm = ROW_TILE
    tn = next(c for c in (1024, 768, 512, 256, 128) if n % c == 0)
    return pl.pallas_call(
        _norm_matmul_kernel,
        out_shape=jax.ShapeDtypeStruct((t, n), out_dtype),
        grid=(t // tm, n // tn),
        in_specs=[pl.BlockSpec((tm, d), lambda i, j: (i, 0)),
                  pl.BlockSpec((1, d), lambda i, j: (0, 0)),
                  pl.BlockSpec((None, d, tn), lambda i, j: (slot, 0, j))],
        out_specs=pl.BlockSpec((tm, tn), lambda i, j: (i, j)),
        scratch_shapes=[pltpu.VMEM((tm, d), BF16)],
        compiler_params=_cparams(2),
        name="norm_matmul",
    )(x, g.reshape(1, d), w)


def _matmul_kernel(a_ref, w_ref, o_ref):
    o_ref[...] = jnp.dot(a_ref[...], w_ref[...].astype(BF16),
                         preferred_element_type=F32).astype(o_ref.dtype)


def _matmul(a, w, slot, out_dtype):
    t, d = a.shape
    n = w.shape[-1]
    tm = ROW_TILE
    tn = next(c for c in (1024, 768, 512, 256, 128) if n % c == 0)
    return pl.pallas_call(
        _matmul_kernel,
        out_shape=jax.ShapeDtypeStruct((t, n), out_dtype),
        grid=(t // tm, n // tn),
        in_specs=[pl.BlockSpec((tm, d), lambda i, j: (i, 0)),
                  pl.BlockSpec((None, d, tn), lambda i, j: (slot, 0, j))],
        out_specs=pl.BlockSpec((tm, tn), lambda i, j: (i, j)),
        compiler_params=_cparams(2),
        name="matmul",
    )(a, w)


def _matmul_res_kernel(a_ref, w_ref, x_ref, o_ref, wb_ref):
    @pl.when(pl.program_id(0) == 0)
    def _():
        wb_ref[...] = w_ref[...].astype(BF16)

    o_ref[...] = x_ref[...] + jnp.dot(a_ref[...], wb_ref[...], preferred_element_type=F32)


def _matmul_res(a, w, slot, x):
    t, k = a.shape
    d = w.shape[-1]
    tm = ROW_TILE
    return pl.pallas_call(
        _matmul_res_kernel,
        out_shape=jax.ShapeDtypeStruct((t, d), F32),
        grid=(t // tm,),
        in_specs=[pl.BlockSpec((tm, k), lambda i: (i, 0)),
                  pl.BlockSpec((None, k, d), lambda i: (slot, 0, 0)),
                  pl.BlockSpec((tm, d), lambda i: (i, 0))],
        out_specs=pl.BlockSpec((tm, d), lambda i: (i, 0)),
        scratch_shapes=[pltpu.VMEM((k, d), BF16)],
        compiler_params=_cparams(1),
        name="matmul_res",
    )(a, w, x)


def _silu_mul(g, u):
    return g * (1.0 / (1.0 + jnp.exp(-g))) * u


def _swiglu_accumulate(h_ref, wg_ref, wu_ref, wd_ref, acc_ref):
    wg = wg_ref[...].astype(BF16)
    wu = wu_ref[...].astype(BF16)
    wd = wd_ref[...].astype(BF16)
    rows = h_ref.shape[0] // FFN_ROW_SPLIT
    for r0 in range(0, h_ref.shape[0], rows):
        h = h_ref[pl.ds(r0, rows), :]
        gate = jnp.dot(h, wg, preferred_element_type=F32)
        up = jnp.dot(h, wu, preferred_element_type=F32)
        a = _silu_mul(gate, up).astype(BF16)
        acc_ref[pl.ds(r0, rows), :] += jnp.dot(a, wd, preferred_element_type=F32)


def _ffn_kernel(x_ref, g_ref, gn_ref, wg_ref, wu_ref, wd_ref, o_ref, *rest, emit_next):
    if emit_next:
        nxt_ref, hn_ref, acc_ref = rest
    else:
        hn_ref, acc_ref = rest
    f = pl.program_id(1)

    @pl.when(f == 0)
    def _():
        hn_ref[...] = _rms_rows(x_ref[...], g_ref[...]).astype(BF16)
        acc_ref[...] = jnp.zeros_like(acc_ref)

    _swiglu_accumulate(hn_ref, wg_ref, wu_ref, wd_ref, acc_ref)

    @pl.when(f == pl.num_programs(1) - 1)
    def _():
        out = x_ref[...] + acc_ref[...]
        o_ref[...] = out
        if emit_next:
            nxt_ref[...] = _rms_rows(out, gn_ref[...]).astype(BF16)


def _ffn(x, g, w_gate_up, w_down, slot, g_next=None):
    t, d = x.shape
    ff = w_down.shape[-2]
    tm, tf = ROW_TILE, FF_TILE
    nf = ff // tf
    emit_next = g_next is not None
    gn = (g_next if emit_next else g).reshape(1, d)
    row_spec = pl.BlockSpec((tm, d), lambda i, f: (i, 0))
    vec_spec = pl.BlockSpec((1, d), lambda i, f: (0, 0))
    out_shape = jax.ShapeDtypeStruct((t, d), F32)
    return pl.pallas_call(
        functools.partial(_ffn_kernel, emit_next=emit_next),
        out_shape=(out_shape, jax.ShapeDtypeStruct((t, d), BF16)) if emit_next else out_shape,
        grid=(t // tm, nf),
        in_specs=[row_spec, vec_spec, vec_spec,
                  pl.BlockSpec((None, d, tf), lambda i, f: (slot, 0, f)),
                  pl.BlockSpec((None, d, tf), lambda i, f: (slot, 0, nf + f)),
                  pl.BlockSpec((None, tf, d), lambda i, f: (slot, f, 0))],
        out_specs=(row_spec, row_spec) if emit_next else row_spec,
        scratch_shapes=[pltpu.VMEM((tm, d), BF16), pltpu.VMEM((tm, d), F32)],
        compiler_params=_cparams(2),
        name="ffn",
    )(x, g.reshape(1, d), gn, w_gate_up, w_gate_up, w_down)


POOL_PAD = 16


def _pool_kernel(u_ref, wg_ref, sc_ref, o_ref, pad_ref):
    s, c = u_ref.shape
    grp = pl.program_id(1)
    pad_ref[pl.ds(0, POOL_PAD), :] = jnp.zeros((POOL_PAD, c), F32)
    pad_ref[pl.ds(POOL_PAD + s, POOL_PAD), :] = jnp.zeros((POOL_PAD, c), F32)
    pad_ref[pl.ds(POOL_PAD, s), :] = u_ref[...]
    wb = wg_ref[...].astype(BF16)
    chunk = 256

    for gi, w in enumerate(POOL_WINDOWS):
        @pl.when(grp == gi)
        def _(w=w):
            for c0 in range(0, s, chunk):
                acc = None
                for dlt in range(-(w // 2), w - w // 2):
                    piece = pad_ref[pl.ds(POOL_PAD + c0 + dlt, chunk), :]
                    acc = piece if acc is None else acc + piece
                pos = c0 + lax.broadcasted_iota(I32, (chunk, 1), 0)
                lo = jnp.maximum(pos - w // 2, 0)
                hi = jnp.minimum(pos + (w - w // 2), s)
                cnt = (hi - lo).astype(F32)
                mixed = acc / cnt - u_ref[pl.ds(c0, chunk), :]
                y = jnp.dot(mixed.astype(BF16), wb, preferred_element_type=F32)
                o_ref[pl.ds(c0, chunk), :] = (y * sc_ref[...]).astype(o_ref.dtype)


def _pool_mix(u, w_group, scale, slot, batch, seq):
    d = u.shape[1]
    ng = len(POOL_WINDOWS)
    c = d // ng
    u3 = u.reshape(batch, seq, d)
    out = pl.pallas_call(
        _pool_kernel,
        out_shape=jax.ShapeDtypeStruct((batch, seq, d), BF16),
        grid=(batch, ng),
        in_specs=[pl.BlockSpec((None, seq, c), lambda b, g: (b, 0, g)),
                  pl.BlockSpec((None, None, c, c), lambda b, g: (slot, g, 0, 0)),
                  pl.BlockSpec((1, c), lambda b, g: (slot, g))],
        out_specs=pl.BlockSpec((None, seq, c), lambda b, g: (b, 0, g)),
        scratch_shapes=[pltpu.VMEM((seq + 2 * POOL_PAD, c), F32)],
        compiler_params=_cparams(2),
        name="pool_mix",
    )(u3, w_group, scale)
    return out.reshape(batch * seq, d)


def _rel_bucket_np(rel):
    half = N_BUCKETS // 2
    max_exact = half // 2
    n = np.abs(rel)
    ratio = np.log(np.maximum(n, 1).astype(np.float32) / np.float32(max_exact))
    big = max_exact + (ratio / np.float32(math.log(MAX_DISTANCE / max_exact))
                       * np.float32(half - max_exact)).astype(np.int32)
    big = np.minimum(big, half - 1)
    return np.where(rel > 0, half, 0) + np.where(n < max_exact, n, big)


def _seg_table(rel_bias, rel, valid):
    bucket = jnp.asarray(_rel_bucket_np(rel).astype(np.int32))
    vals = jnp.take(rel_bias.astype(F32), bucket, axis=0)
    vals = jnp.where(jnp.asarray(valid)[..., None], vals, MASK_VALUE)
    return jnp.transpose(vals, (2, 0, 1))


def _toeplitz(seg_row, rows, cols):
    w = seg_row.shape[1]
    full = jnp.broadcast_to(seg_row, (rows, w))
    rolled = pltpu.roll(full, w - (rows - 1), 1, stride=1, stride_axis=0)
    return rolled[:, :cols]


def _band_variants(hw, n_tiles):
    if n_tiles == 1:
        return (0,)
    return (0, -hw, -2 * hw)


def _band_segs(rel_bias, tq, kw, half, dil, variants, width):
    c = np.arange(width)
    rel = np.stack([r0 + c - (tq - 1) for r0 in variants])
    valid = (np.abs(rel) <= half) & (c[None, :] < tq + kw - 1)
    return _seg_table(rel_bias, rel * dil, valid)


def _band_window(qi, nq, tq, hw, kw, ln):
    if nq == 1:
        return 0, 0
    var = jnp.where(qi == 0, 0, jnp.where(qi == nq - 1, 2, 1))
    return var, jnp.clip(qi * tq - hw, 0, ln - kw)


DilCfg = collections.namedtuple("DilCfg", "dil ln nq hw kw n_var")


def _dil_proj_kernel(x_ref, g_ref, w_ref, o_ref, slab_ref, hn_ref, *, dils):
    grp, c = pl.program_id(1), pl.program_id(2)
    seq, d = x_ref.shape
    n_slabs = d // LANES
    chunk = 256

    @pl.when((grp == 0) & (c == 0))
    def _():
        for c0 in range(0, seq, chunk):
            hn = _rms_rows(x_ref[pl.ds(c0, chunk), :], g_ref[...])
            for s in range(n_slabs):
                slab_ref[s, pl.ds(c0, chunk), :] = hn[:, s * LANES:(s + 1) * LANES]

    for gi, dil in enumerate(dils):
        @pl.when((grp == gi) & (c == 0))
        def _(dil=dil):
            ln = seq // dil
            for r in range(dil):
                for c0 in range(0, ln, chunk):
                    n = min(chunk, ln)
                    rows = [slab_ref[s, pl.ds(r + c0 * dil, n, stride=dil), :] for s in range(n_slabs)]
                    hn_ref[pl.ds(r * ln + c0, n), :] = jnp.concatenate(rows, axis=1).astype(BF16)

    res = jnp.dot(hn_ref[...], w_ref[...].astype(BF16), preferred_element_type=F32)
    res = res * jnp.where(c < pl.num_programs(2) // 3, Q_SCALE, 1.0)
    for s in range(o_ref.shape[0]):
        o_ref[s] = res[:, s * LANES:(s + 1) * LANES].astype(o_ref.dtype)


def _dil_proj(x3, g, w_in, slot, dils):
    batch, seq, d = x3.shape
    n = w_in.shape[-1]
    tn = FF_TILE
    per_group = n // len(dils) // tn
    spt = tn // LANES
    return pl.pallas_call(
        functools.partial(_dil_proj_kernel, dils=dils),
        out_shape=jax.ShapeDtypeStruct((batch, n // LANES, seq, LANES), BF16),
        grid=(batch, len(dils), per_group),
        in_specs=[pl.BlockSpec((None, seq, d), lambda b, g, c: (b, 0, 0)),
                  pl.BlockSpec((1, d), lambda b, g, c: (0, 0)),
                  pl.BlockSpec((None, d, tn), lambda b, g, c: (slot, 0, g * per_group + c))],
        out_specs=pl.BlockSpec((None, spt, seq, LANES), lambda b, g, c: (b, g * per_group + c, 0, 0)),
        scratch_shapes=[pltpu.VMEM((d // LANES, seq, LANES), F32), pltpu.VMEM((seq, d), BF16)],
        compiler_params=_cparams(3),
        name="dil_proj",
    )(x3, g.reshape(1, d), w_in)


def _dil_unit(cfg, unit, q4, k_ref, v_ref, bias_ref, acc_ref, m_ref, l_ref, *, tq, first, last):
    dil, ln, nq, kw = cfg.dil, cfg.ln, cfg.nq, cfg.kw
    n_pairs = q4.shape[0]
    if nq == 1:
        r, qi = unit, 0
    elif dil == 1:
        r, qi = 0, unit
    else:
        r, qi = unit // nq, unit % nq
    var, ks = _band_window(qi, nq, tq, cfg.hw, kw, ln)
    krow = pl.multiple_of(r * ln + ks, cfg.hw)
    if dil == 1:
        rows = pl.ds(pl.multiple_of(qi * tq, tq), tq)
    else:
        rows = pl.ds(qi * tq * dil + r, tq, stride=dil)

    even = lax.broadcasted_iota(I32, (n_pairs, tq, LANES), 2) < HEAD_DIM
    zero = jnp.zeros_like(q4)
    q8 = jnp.concatenate([jnp.where(even, q4, zero), jnp.where(even, zero, q4)], axis=0)
    k4 = k_ref[:, pl.ds(krow, kw), :]
    v4 = v_ref[:, pl.ds(krow, kw), :]
    k8 = jnp.concatenate([k4, k4], axis=0)
    va = jnp.concatenate([v4, jnp.ones_like(v4)], axis=2)
    v8 = jnp.concatenate([va, va], axis=0)
    s = lax.dot_general(q8, k8, (((2,), (2,)), ((0,), (0,))), preferred_element_type=F32)
    s = s + bias_ref[var]
    m8 = jnp.max(s, axis=-1, keepdims=True)
    p = jnp.exp(s - m8).astype(BF16)
    ov = lax.dot_general(p, v8, (((2,), (1,)), ((0,), (0,))), preferred_element_type=F32)
    o_c = jnp.where(even, ov[:n_pairs, :, :LANES], ov[n_pairs:, :, :LANES])
    l_c = jnp.where(even, ov[:n_pairs, :, LANES:], ov[n_pairs:, :, LANES:])
    m_c = jnp.where(even, m8[:n_pairs], m8[n_pairs:])
    if first:
        m_n, l_n, acc_n = m_c, l_c, o_c
    else:
        m_o = jnp.stack([m_ref[pp, rows, :] for pp in range(n_pairs)])
        l_o = jnp.stack([l_ref[pp, rows, :] for pp in range(n_pairs)])
        acc_o = jnp.stack([acc_ref[pp, rows, :] for pp in range(n_pairs)])
        m_n = jnp.maximum(m_o, m_c)
        a_o = jnp.exp(m_o - m_n)
        a_c = jnp.exp(m_c - m_n)
        l_n = a_o * l_o + a_c * l_c
        acc_n = a_o * acc_o + a_c * o_c
    if last:
        acc_n = acc_n / l_n
    for pp in range(n_pairs):
        acc_ref[pp, rows, :] = acc_n[pp]
        if not last:
            m_ref[pp, rows, :] = m_n[pp]
            l_ref[pp, rows, :] = l_n[pp]


def _dil_attn_kernel(*refs, cfgs, tq):
    ng = len(cfgs)
    seg_refs = refs[:ng]
    q_ref, k_ref, v_ref, o_ref = refs[ng:ng + 4]
    bias_refs = refs[ng + 4:2 * ng + 4]
    acc_ref, m_ref, l_ref = refs[2 * ng + 4:]
    hf, b, grp, step = (pl.program_id(i) for i in range(4))
    n_pairs = q_ref.shape[0]

    @pl.when((b == 0) & (grp == 0) & (step == 0))
    def _():
        for gi, cfg in enumerate(cfgs):
            for v in range(cfg.n_var):
                for odd in range(2):
                    for pp in range(n_pairs):
                        head = hf * 2 * n_pairs + 2 * pp + odd
                        row = seg_refs[gi][head, pl.ds(v, 1), :]
                        bias_refs[gi][v, odd * n_pairs + pp] = _toeplitz(row, tq, cfg.kw)

    for gi, cfg in enumerate(cfgs):
        @pl.when(grp == gi)
        def _(gi=gi, cfg=cfg):
            for uu in range(DIL_UNITS):
                _dil_unit(cfg, step * DIL_UNITS + uu, q_ref[:, pl.ds(uu * tq, tq), :], k_ref, v_ref,
                          bias_refs[gi], acc_ref, m_ref, l_ref, tq=tq, first=gi == 0, last=gi == ng - 1)

    @pl.when((grp == ng - 1) & (step == pl.num_programs(3) - 1))
    def _():
        slabs = [acc_ref[s] for s in range(n_pairs)]
        o_ref[...] = jnp.concatenate(slabs, axis=1).astype(o_ref.dtype)


def _dilated_attention(qkv, rel_bias, batch, seq):
    ng = len(DIL_PAIRS)
    d = qkv.shape[1] * LANES // (3 * ng)
    hd = d // DIL_HEAD_SPLIT
    n_pairs = hd // LANES
    tq = ATT_TQ
    cfgs, segs = [], []
    for win, dil in DIL_PAIRS:
        half = win // (2 * dil)
        ln = seq // dil
        nq = ln // tq
        kw = min(tq + 2 * half, ln)
        variants = _band_variants(half, nq)
        cfgs.append(DilCfg(dil, ln, nq, half, kw, len(variants)))
        segs.append(_band_segs(rel_bias, tq, kw, half, dil, variants, _next_pow2(tq + kw - 1)))
    steps = seq // tq // DIL_UNITS
    cb = DIL_HEAD_SPLIT

    in_specs = [pl.BlockSpec(sg.shape, lambda hf, b, g, s: (0, 0, 0)) for sg in segs]
    in_specs += [pl.BlockSpec((None, n_pairs, DIL_UNITS * tq, LANES),
                              lambda hf, b, g, s: (b, (g * 3) * cb + hf, s, 0)),
                 pl.BlockSpec((None, n_pairs, seq, LANES),
                              lambda hf, b, g, s: (b, (g * 3 + 1) * cb + hf, 0, 0)),
                 pl.BlockSpec((None, n_pairs, seq, LANES),
                              lambda hf, b, g, s: (b, (g * 3 + 2) * cb + hf, 0, 0))]
    scratch = [pltpu.VMEM((c.n_var, 2 * n_pairs, tq, c.kw), F32) for c in cfgs]
    scratch += [pltpu.VMEM((n_pairs, seq, LANES), F32) for _ in range(3)]
    out = pl.pallas_call(
        functools.partial(_dil_attn_kernel, cfgs=tuple(cfgs), tq=tq),
        out_shape=jax.ShapeDtypeStruct((batch, seq, d), BF16),
        grid=(DIL_HEAD_SPLIT, batch, ng, steps),
        in_specs=in_specs,
        out_specs=pl.BlockSpec((None, seq, hd), lambda hf, b, g, s: (b, 0, hf)),
        scratch_shapes=scratch,
        compiler_params=_cparams(4),
        name="dil_attn",
    )(*segs, qkv, qkv, qkv)
    return out.reshape(batch * seq, d)


def _gqa_kernel(sink_ref, seg_ref, q_ref, k_ref, v_ref, o_ref, bias_ref, *, tq, kw, hw, grp):
    b, qi = pl.program_id(0), pl.program_id(1)
    nq = pl.num_programs(1)
    seq_len = k_ref.shape[0]
    n_kv = k_ref.shape[1] // HEAD_DIM

    @pl.when((b == 0) & (qi == 0))
    def _():
        for v in range(3):
            for kh in range(n_kv):
                for gq in range(grp):
                    bias_ref[v, kh, pl.ds(gq * tq, tq), :] = _toeplitz(
                        seg_ref[kh * grp + gq, pl.ds(v, 1), :], tq, kw)

    var = jnp.where(qi == 0, 0, jnp.where(qi == nq - 1, 2, 1))
    ks = pl.multiple_of(jnp.clip(qi * tq - hw, 0, seq_len - kw), hw)
    row = lax.broadcasted_iota(I32, (grp * tq, 1), 0)
    outs = [None] * (n_kv * grp)
    for kh in range(n_kv):
        cs = slice(kh * HEAD_DIM, (kh + 1) * HEAD_DIM)
        qs = jnp.concatenate(
            [q_ref[:, (kh * grp + gq) * HEAD_DIM:(kh * grp + gq + 1) * HEAD_DIM] for gq in range(grp)],
            axis=0) * Q_SCALE
        kk = k_ref[pl.ds(ks, kw), cs]
        vv = v_ref[pl.ds(ks, kw), cs]
        s = lax.dot_general(qs, kk, (((1,), (1,)), ((), ())), preferred_element_type=F32)
        s = s + bias_ref[var, kh]
        sk = jnp.zeros((grp * tq, 1), F32)
        for gq in range(grp):
            sk = jnp.where((row >= gq * tq) & (row < (gq + 1) * tq), sink_ref[kh * grp + gq], sk)
        m = jnp.maximum(jnp.max(s, axis=-1, keepdims=True), sk)
        e = jnp.exp(s - m).astype(BF16)
        va = jnp.concatenate([vv, jnp.ones_like(vv)], axis=1)
        ov = jnp.dot(e, va, preferred_element_type=F32)
        o = ov[:, :HEAD_DIM] / (ov[:, HEAD_DIM:] + jnp.exp(sk - m))
        for gq in range(grp):
            outs[kh * grp + gq] = o[gq * tq:(gq + 1) * tq, :]
    o_ref[...] = jnp.concatenate(outs, axis=1).astype(o_ref.dtype)


def _gqa_attention(qkv, sink, rel_bias, batch, seq):
    t, ncol = qkv.shape
    n_q = sink.shape[0]
    d = n_q * HEAD_DIM
    kvw = GQA_KV_HEADS * HEAD_DIM
    grp = n_q // GQA_KV_HEADS
    tq = hw = GQA_WINDOW
    kw = 3 * GQA_WINDOW
    nq = seq // tq
    variants = _band_variants(hw, nq)
    width = _next_pow2(tq + kw - 1)
    seg = _band_segs(rel_bias, tq, kw, GQA_WINDOW, 1, variants, width)
    qkv_v = qkv.reshape(batch, seq, ncol)
    out = pl.pallas_call(
        functools.partial(_gqa_kernel, tq=tq, kw=kw, hw=hw, grp=grp),
        out_shape=jax.ShapeDtypeStruct((batch, seq, d), BF16),
        grid=(batch, nq),
        in_specs=[pl.BlockSpec(memory_space=pltpu.SMEM),
                  pl.BlockSpec((n_q, 3, width), lambda b, qi: (0, 0, 0)),
                  pl.BlockSpec((None, tq, d), lambda b, qi: (b, qi, 0)),
                  pl.BlockSpec((None, seq, kvw), lambda b, qi: (b, 0, d // kvw)),
                  pl.BlockSpec((None, seq, kvw), lambda b, qi: (b, 0, d // kvw + 1))],
        out_specs=pl.BlockSpec((None, tq, d), lambda b, qi: (b, qi, 0)),
        scratch_shapes=[pltpu.VMEM((3, GQA_KV_HEADS, grp * tq, kw), F32)],
        compiler_params=_cparams(2),
        name="gqa_attn",
    )(sink.astype(F32), seg, qkv_v, qkv_v, qkv_v)
    return out.reshape(t, d)


def _diff_kernel(seg_ref, lam_ref, sub_ref, q_ref, k_ref, v_ref, o_ref, bias_ref, *, tq, lam_init):
    h, qi, b = pl.program_id(0), pl.program_id(1), pl.program_id(2)
    seq_len = k_ref.shape[0]
    nk = seq_len // tq

    @pl.when(b == 0)
    def _():
        for j in range(2):
            for ki in range(nk):
                row = seg_ref[h * 2 + j, pl.ds(ki - qi + nk - 1, 1), :]
                bias_ref[j, :, pl.ds(ki * tq, tq)] = _toeplitz(row, tq, tq)

    lv = lam_ref[...]
    s01 = jnp.sum(lv[0:1, :] * lv[1:2, :], axis=-1, keepdims=True)
    s23 = jnp.sum(lv[2:3, :] * lv[3:4, :], axis=-1, keepdims=True)
    lam = jnp.exp(s01) - jnp.exp(s23) + lam_init

    q = q_ref[...] * Q_SCALE
    k = k_ref[...]
    v = v_ref[...]
    va = jnp.concatenate([v, jnp.ones_like(v)], axis=1)
    map0 = lax.broadcasted_iota(I32, q.shape, 1) < HEAD_DIM
    zero = jnp.zeros_like(q)
    qm = (jnp.where(map0, q, zero), jnp.where(map0, zero, q))
    rc = tq // DIFF_ROW_SPLIT
    for c0 in range(0, tq, rc):
        outs = []
        for j in range(2):
            s = lax.dot_general(qm[j][c0:c0 + rc], k, (((1,), (1,)), ((), ())),
                                preferred_element_type=F32)
            s = s + bias_ref[j, pl.ds(c0, rc), :]
            m = jnp.max(s, axis=-1, keepdims=True)
            p = jnp.exp(s - m).astype(BF16)
            ov = jnp.dot(p, va, preferred_element_type=F32)
            outs.append(ov[:, :2 * HEAD_DIM] / ov[:, 2 * HEAD_DIM:])
        o = outs[0] - lam * outs[1]
        o = _rms_rows(o, sub_ref[...]) * (1.0 - lam_init)
        o_ref[pl.ds(c0, rc), :] = o.astype(o_ref.dtype)


def _diff_attention(qkv, lam_vecs, subln, slot, rel_bias, batch, seq):
    t, ncol = qkv.shape
    d = ncol // 3
    hd2 = 2 * HEAD_DIM
    n_heads = d // hd2
    tq = DIFF_TQ
    nk = seq // tq
    width = _next_pow2(2 * tq - 1)
    c = np.arange(width)
    rel = np.stack([(dl - (nk - 1)) * tq + c - (tq - 1) for dl in range(2 * nk - 1)])
    seg = _seg_table(rel_bias, rel, np.ones_like(rel, dtype=bool))
    lam_init = 0.8 - 0.6 * math.exp(-0.3 * DIFF_LAYER)
    qkv_v = qkv.reshape(batch, seq, ncol)
    out = pl.pallas_call(
        functools.partial(_diff_kernel, tq=tq, lam_init=lam_init),
        out_shape=jax.ShapeDtypeStruct((batch, seq, d), BF16),
        grid=(n_heads, seq // tq, batch),
        in_specs=[pl.BlockSpec(seg.shape, lambda h, qi, b: (0, 0, 0)),
                  pl.BlockSpec((None,) + lam_vecs.shape[1:], lambda h, qi, b: (slot, 0, 0)),
                  pl.BlockSpec((1, hd2), lambda h, qi, b: (slot, 0)),
                  pl.BlockSpec((None, tq, hd2), lambda h, qi, b: (b, qi, h)),
                  pl.BlockSpec((None, seq, hd2), lambda h, qi, b: (b, 0, n_heads + h)),
                  pl.BlockSpec((None, seq, hd2), lambda h, qi, b: (b, 0, 2 * n_heads + h))],
        out_specs=pl.BlockSpec((None, tq, hd2), lambda h, qi, b: (b, qi, h)),
        scratch_shapes=[pltpu.VMEM((2, tq, seq), F32)],
        compiler_params=_cparams(3),
        name="diff_attn",
    )(seg, lam_vecs, subln, qkv_v, qkv_v, qkv_v)
    return out.reshape(t, d)


SEG_ALIGN = 16
SEG_SIZES = (512, 256, 128, 64, 32, 16)
SEG_STAGE = 2 * SEG_TILE + N_EXPERTS * SEG_ALIGN


def _route_kernel(a_ref, wo_ref, x_ref, g_ref, r_ref, xn_ref, hn_ref, mi_ref, mf_ref, cnt_ref,
                  tri_ref, wb_ref):
    tm, d = x_ref.shape

    @pl.when(pl.program_id(0) == 0)
    def _():
        rr = lax.broadcasted_iota(I32, (tm, tm), 0)
        cc = lax.broadcasted_iota(I32, (tm, tm), 1)
        tri_ref[...] = (cc < rr).astype(BF16)
        wb_ref[...] = wo_ref[...].astype(BF16)

    xn = x_ref[...] + jnp.dot(a_ref[...], wb_ref[...], preferred_element_type=F32)
    xn_ref[...] = xn
    hn = _rms_rows(xn, g_ref[...])
    hn_ref[...] = hn.astype(BF16)
    h_hi = hn.astype(BF16)
    h_lo = (hn - h_hi.astype(F32)).astype(BF16)
    r = r_ref[...]
    r_hi = r.astype(BF16)
    r_lo = (r - r_hi.astype(F32)).astype(BF16)
    logits = (jnp.dot(h_hi, r_hi, preferred_element_type=F32)
              + jnp.dot(h_hi, r_lo, preferred_element_type=F32)
              + jnp.dot(h_lo, r_hi, preferred_element_type=F32))
    lane = lax.broadcasted_iota(I32, (tm, LANES), 1)
    logits = jnp.where(lane < N_EXPERTS, logits, -jnp.inf)
    v1 = jnp.max(logits, axis=-1, keepdims=True)
    i1 = jnp.min(jnp.where(logits == v1, lane, LANES), axis=-1, keepdims=True)
    oh1 = lane == i1
    rest = jnp.where(oh1, -jnp.inf, logits)
    v2 = jnp.max(rest, axis=-1, keepdims=True)
    i2 = jnp.min(jnp.where(rest == v2, lane, LANES), axis=-1, keepdims=True)
    oh2 = lane == i2
    e2 = jnp.exp(v2 - v1)
    g1 = 1.0 / (1.0 + e2)
    g2 = e2 / (1.0 + e2)

    sel = (oh1 | oh2)
    before = jnp.dot(tri_ref[...], sel.astype(BF16), preferred_element_type=F32)
    rank1 = jnp.sum(jnp.where(oh1, before, 0.0), axis=-1, keepdims=True).astype(I32)
    rank2 = jnp.sum(jnp.where(oh2, before, 0.0), axis=-1, keepdims=True).astype(I32)
    counts = jnp.sum(sel.astype(F32), axis=0, keepdims=True).astype(I32)

    mi_ref[...] = jnp.where(lane == 0, i1, jnp.where(lane == 1, i2,
                            jnp.where(lane == 2, rank1, jnp.where(lane == 3, rank2, 0))))
    mf_ref[...] = jnp.where(lane == 0, g1, jnp.where(lane == 1, g2, 0.0))
    cnt_ref[...] = jnp.broadcast_to(counts, cnt_ref.shape)


def _route(a, w_out, out_slot, x, g, router):
    t, d = x.shape
    k = a.shape[1]
    tm = SEG_TILE
    r_pad = jnp.zeros((d, LANES), F32).at[:, :N_EXPERTS].set(router.astype(F32))
    return pl.pallas_call(
        _route_kernel,
        out_shape=(jax.ShapeDtypeStruct((t, d), F32),
                   jax.ShapeDtypeStruct((t, d), BF16),
                   jax.ShapeDtypeStruct((t, LANES), I32),
                   jax.ShapeDtypeStruct((t, LANES), F32),
                   jax.ShapeDtypeStruct((t // tm, SUBLANES, LANES), I32)),
        grid=(t // tm,),
        in_specs=[pl.BlockSpec((tm, k), lambda i: (i, 0)),
                  pl.BlockSpec((None, k, d), lambda i: (out_slot, 0, 0)),
                  pl.BlockSpec((tm, d), lambda i: (i, 0)),
                  pl.BlockSpec((1, d), lambda i: (0, 0)),
                  pl.BlockSpec((d, LANES), lambda i: (0, 0))],
        out_specs=(pl.BlockSpec((tm, d), lambda i: (i, 0)),
                   pl.BlockSpec((tm, d), lambda i: (i, 0)),
                   pl.BlockSpec((tm, LANES), lambda i: (i, 0)),
                   pl.BlockSpec((tm, LANES), lambda i: (i, 0)),
                   pl.BlockSpec((None, SUBLANES, LANES), lambda i: (i, 0, 0))),
        scratch_shapes=[pltpu.VMEM((tm, tm), BF16), pltpu.VMEM((k, d), BF16)],
        compiler_params=_cparams(1),
        name="moe_route",
    )(a, w_out, x, g.reshape(1, d), r_pad)


def _slot_rows(mi, loc_ref, base):
    e1, e2, d1, d2 = mi[0], mi[1], mi[2], mi[3]
    for e in range(N_EXPERTS):
        off = loc_ref[base + e]
        d1 = d1 + jnp.where(e1 == e, off, 0)
        d2 = d2 + jnp.where(e2 == e, off, 0)
    return d1, d2


def _segment_copies(base, seg_ref, n16_ref, loc_ref, hbm_ref, vmem_ref, sem, *, to_hbm, wait):
    for e in range(N_EXPERTS):
        n16 = n16_ref[base + e]
        hbm0 = seg_ref[base + e]
        vmem0 = loc_ref[base + e]
        off = 0
        for size in SEG_SIZES:
            @pl.when((n16 & size) != 0)
            def _(off=off, size=size):
                h = hbm_ref.at[pl.ds(pl.multiple_of(hbm0 + off, SEG_ALIGN), size)]
                v = vmem_ref.at[pl.ds(pl.multiple_of(vmem0 + off, SEG_ALIGN), size)]
                cp = pltpu.make_async_copy(v, h, sem) if to_hbm else pltpu.make_async_copy(h, v, sem)
                if wait:
                    cp.wait()
                else:
                    cp.start()
            off = off + (n16 & size)


def _scatter_kernel(seg_ref, n16_ref, loc_ref, tail_ref, hn_ref, mi_ref, xs_ref, stage_ref, zero_ref, sem):
    i = pl.program_id(0)
    tm = hn_ref.shape[0]
    rows = stage_ref.shape[1]
    tg = zero_ref.shape[0]
    slot = lax.rem(i, 2)

    @pl.when(i == 0)
    def _():
        zero_ref[...] = jnp.zeros_like(zero_ref)
        for e in range(N_EXPERTS):
            tail = pl.multiple_of(tail_ref[e], SEG_ALIGN)
            pltpu.make_async_copy(zero_ref, xs_ref.at[pl.ds(tail, tg)], sem.at[2]).start()
        for e in range(N_EXPERTS):
            tail = pl.multiple_of(tail_ref[e], SEG_ALIGN)
            pltpu.make_async_copy(zero_ref, xs_ref.at[pl.ds(tail, tg)], sem.at[2]).wait()

    base = i * N_EXPERTS
    mi_t = mi_ref[...].T
    d1, d2 = _slot_rows([mi_t[k:k + 1, :] for k in range(4)], loc_ref, base)
    row = lax.broadcasted_iota(I32, (rows, tm), 0)
    onehot = ((row == d1) | (row == d2)).astype(BF16)
    stage_ref[slot] = jnp.dot(onehot, hn_ref[...], preferred_element_type=F32).astype(BF16)
    tabs = (seg_ref, n16_ref, loc_ref, xs_ref)
    _segment_copies(base, *tabs, stage_ref.at[slot], sem.at[slot], to_hbm=True, wait=False)

    @pl.when(i > 0)
    def _():
        _segment_copies(base - N_EXPERTS, *tabs, stage_ref.at[1 - slot], sem.at[1 - slot],
                        to_hbm=True, wait=True)

    @pl.when(i == pl.num_programs(0) - 1)
    def _():
        _segment_copies(base, *tabs, stage_ref.at[slot], sem.at[slot], to_hbm=True, wait=True)


def _scatter(hn, mi, seg, n16, loc, tails, n_rows):
    t, d = hn.shape
    tm = SEG_TILE
    return pl.pallas_call(
        _scatter_kernel,
        out_shape=jax.ShapeDtypeStruct((n_rows, d), BF16),
        grid_spec=pltpu.PrefetchScalarGridSpec(
            num_scalar_prefetch=4,
            grid=(t // tm,),
            in_specs=[pl.BlockSpec((tm, d), lambda i, *_: (i, 0)),
                      pl.BlockSpec((tm, LANES), lambda i, *_: (i, 0))],
            out_specs=pl.BlockSpec(memory_space=pl.ANY),
            scratch_shapes=[pltpu.VMEM((2, SEG_STAGE, d), BF16), pltpu.VMEM((MOE_TILE, d), BF16),
                            pltpu.SemaphoreType.DMA((3,))]),
        compiler_params=pltpu.CompilerParams(dimension_semantics=("arbitrary",),
                                             vmem_limit_bytes=VMEM_LIMIT, has_side_effects=True),
        name="moe_scatter",
    )(seg, n16, loc, tails, hn, mi)


def _seg_expert_kernel(te_ref, tb_ref, nv_ref, xs_ref, wg_ref, wu_ref, wd_ref, ys_ref, acc_ref):
    i, f = pl.program_id(0), pl.program_id(1)

    @pl.when(i < nv_ref[0])
    def _():
        @pl.when(f == 0)
        def _():
            acc_ref[...] = jnp.zeros_like(acc_ref)

        _swiglu_accumulate(xs_ref, wg_ref, wu_ref, wd_ref, acc_ref)

        @pl.when(f == pl.num_programs(1) - 1)
        def _():
            ys_ref[...] = acc_ref[...].astype(ys_ref.dtype)


def _seg_experts(xs, w_gate_up, w_down, slot, tile_expert, tile_block, n_valid, n_tiles):
    n_rows, d = xs.shape
    ff = w_down.shape[-2]
    tg, tf = MOE_TILE, FF_TILE
    nf = ff // tf

    def fidx(i, f, nv):
        return jnp.where(i < nv[0], f, nf - 1)

    return pl.pallas_call(
        _seg_expert_kernel,
        out_shape=jax.ShapeDtypeStruct((n_rows, d), BF16),
        grid_spec=pltpu.PrefetchScalarGridSpec(
            num_scalar_prefetch=3,
            grid=(n_tiles, nf),
            in_specs=[pl.BlockSpec((tg, d), lambda i, f, te, tb, nv: (tb[i], 0)),
                      pl.BlockSpec((None, None, d, tf),
                                   lambda i, f, te, tb, nv: (slot, te[i], 0, fidx(i, f, nv))),
                      pl.BlockSpec((None, None, d, tf),
                                   lambda i, f, te, tb, nv: (slot, te[i], 0, nf + fidx(i, f, nv))),
                      pl.BlockSpec((None, None, tf, d),
                                   lambda i, f, te, tb, nv: (slot, te[i], fidx(i, f, nv), 0))],
            out_specs=pl.BlockSpec((tg, d), lambda i, f, te, tb, nv: (tb[i], 0)),
            scratch_shapes=[pltpu.VMEM((tg, d), F32)]),
        compiler_params=_cparams(2),
        name="moe_experts",
    )(tile_expert, tile_block, n_valid, xs, w_gate_up, w_gate_up, w_down)


def _gather_kernel(seg_ref, n16_ref, loc_ref, x_ref, mi_ref, mf_ref, gf_ref, ys_ref, o_ref,
                   *rest, norm):
    if norm == "next":
        nxt_ref, ybuf_ref, sem = rest
    else:
        ybuf_ref, sem = rest
    i = pl.program_id(0)
    tm = x_ref.shape[0]
    rows = ybuf_ref.shape[1]
    slot = lax.rem(i, 2)
    base = i * N_EXPERTS
    tabs = (seg_ref, n16_ref, loc_ref, ys_ref)

    @pl.when(i == 0)
    def _():
        ybuf_ref[...] = jnp.zeros_like(ybuf_ref)
        _segment_copies(base, *tabs, ybuf_ref.at[0], sem.at[0], to_hbm=False, wait=False)

    @pl.when(i + 1 < pl.num_programs(0))
    def _():
        _segment_copies(base + N_EXPERTS, *tabs, ybuf_ref.at[1 - slot], sem.at[1 - slot],
                        to_hbm=False, wait=False)

    mi = mi_ref[...]
    d1, d2 = _slot_rows([mi[:, k:k + 1] for k in range(4)], loc_ref, base)
    col = lax.broadcasted_iota(I32, (tm, rows), 1)
    gates = mf_ref[...]
    _segment_copies(base, *tabs, ybuf_ref.at[slot], sem.at[slot], to_hbm=False, wait=True)
    ybuf = ybuf_ref[slot]
    out = x_ref[...]
    for k, dk in enumerate((d1, d2)):
        yk = jnp.dot((col == dk).astype(BF16), ybuf, preferred_element_type=F32)
        out = out + gates[:, k:k + 1] * yk
    if norm == "final":
        out = _rms_rows(out, gf_ref[...])
    o_ref[...] = out
    if norm == "next":
        nxt_ref[...] = _rms_rows(out, gf_ref[...]).astype(BF16)


def _gather(x, mi, mf, ys, seg, n16, loc, g_norm, norm):
    t, d = x.shape
    tm = SEG_TILE
    row_spec = pl.BlockSpec((tm, d), lambda i, *_: (i, 0))
    out_shape = jax.ShapeDtypeStruct((t, d), F32)
    emit_next = norm == "next"
    return pl.pallas_call(
        functools.partial(_gather_kernel, norm=norm),
        out_shape=(out_shape, jax.ShapeDtypeStruct((t, d), BF16)) if emit_next else out_shape,
        grid_spec=pltpu.PrefetchScalarGridSpec(
            num_scalar_prefetch=3,
            grid=(t // tm,),
            in_specs=[row_spec,
                      pl.BlockSpec((tm, LANES), lambda i, *_: (i, 0)),
                      pl.BlockSpec((tm, LANES), lambda i, *_: (i, 0)),
                      pl.BlockSpec((1, d), lambda i, *_: (0, 0)),
                      pl.BlockSpec(memory_space=pl.ANY)],
            out_specs=(row_spec, row_spec) if emit_next else row_spec,
            scratch_shapes=[pltpu.VMEM((2, SEG_STAGE, d), BF16), pltpu.SemaphoreType.DMA((2,))]),
        compiler_params=_cparams(1),
        name="moe_gather",
    )(seg, n16, loc, x, mi, mf, g_norm.reshape(1, d).astype(F32), ys)


def _moe_seg(a, w_out, out_slot, x, g, router, w_gate_up, w_down, slot, g_norm, norm):
    t, d = x.shape
    tg = MOE_TILE
    x, hn, mi, mf, cnt = _route(a, w_out, out_slot, x, g, router)

    counts = cnt[:, 0, :N_EXPERTS]
    n16 = (counts + SEG_ALIGN - 1) // SEG_ALIGN * SEG_ALIGN
    rows_e = jnp.sum(n16, axis=0)
    tiles_e = (rows_e + tg - 1) // tg
    tile_start = jnp.cumsum(tiles_e) - tiles_e
    row_start = tile_start * tg
    seg = row_start[None, :] + jnp.cumsum(n16, axis=0) - n16
    loc = jnp.cumsum(n16, axis=1) - n16
    n_valid = jnp.sum(tiles_e).astype(I32)
    n_tiles = (2 * t + counts.size * (SEG_ALIGN - 1)) // tg + N_EXPERTS
    n_rows = (n_tiles + 1) * tg
    tidx = jnp.minimum(jnp.arange(n_tiles, dtype=I32), n_valid - 1)
    tile_expert = (jnp.sum(tidx[:, None] >= tile_start[None, :], axis=1) - 1).astype(I32)
    tails = (row_start + rows_e).astype(I32)
    seg, n16, loc = (a.reshape(-1).astype(I32) for a in (seg, n16, loc))

    xs = _scatter(hn, mi, seg, n16, loc, tails, n_rows)
    ys = _seg_experts(xs, w_gate_up, w_down, slot, tile_expert, tidx, n_valid.reshape(1), n_tiles)
    return _gather(x, mi, mf, ys, seg, n16, loc, g_norm, norm)


def kernel(x, rel_bias, norm_mix, norm_ffn, norm_final, a_w_in, a_w_group, a_scale, a_w_out,
           b_w_in, b_w_out, c_w_in, c_lambda, c_subln, c_w_out, d_w_in, d_sink, d_w_out,
           f_w_gate_up, f_w_down, m_router, m_w_gate_up, m_w_down):
    batch, seq, d = x.shape
    h = x.reshape(batch * seq, d)
    f_w_gate_up = f_w_gate_up.astype(BF16)
    f_w_down = f_w_down.astype(BF16)

    u = _norm_matmul(h, norm_mix[0], a_w_in, 0, F32)
    y = _pool_mix(u, a_w_group, a_scale, 0, batch, seq)
    h = _matmul_res(y, a_w_out, 0, h)
    h = _ffn(h, norm_ffn[0], f_w_gate_up, f_w_down, 0)

    qkv = _dil_proj(h.reshape(batch, seq, d), norm_mix[1], b_w_in, 0, tuple(p[1] for p in DIL_PAIRS))
    o = _dilated_attention(qkv, rel_bias, batch, seq)
    h, hn = _moe_seg(o, b_w_out, 0, h, norm_ffn[1], m_router[0], m_w_gate_up, m_w_down, 0,
                     norm_mix[2], "next")

    qkv = _matmul(hn, c_w_in, 0, BF16)
    o = _diff_attention(qkv, c_lambda, c_subln, 0, rel_bias, batch, seq)
    h = _matmul_res(o, c_w_out, 0, h)
    h, hn = _ffn(h, norm_ffn[2], f_w_gate_up, f_w_down, 1, g_next=norm_mix[3])

    qkv = _matmul(hn, d_w_in, 0, BF16)
    o = _gqa_attention(qkv, d_sink[0], rel_bias, batch, seq)
    h = _moe_seg(o, d_w_out, 0, h, norm_ffn[3], m_router[1], m_w_gate_up, m_w_down, 1,
                 norm_final, "final")
    return h.reshape(batch, seq, d)
```

```python
import collections
import functools
import math

import numpy as np
import jax
import jax.numpy as jnp
from jax import lax
from jax.experimental import pallas as pl
from jax.experimental.pallas import tpu as pltpu

F32 = jnp.float32
BF16 = jnp.bfloat16
I32 = jnp.int32

RMS_EPS = 1e-6
HEAD_DIM = 64
N_BUCKETS = 32
MAX_DISTANCE = 1024
POOL_WINDOWS = (2, 4, 8, 16)
DIL_PAIRS = ((128, 1), (512, 4), (2048, 16))
Q_SCALE = HEAD_DIM ** -0.5
GQA_KV_HEADS = 4
GQA_WINDOW = 128
N_EXPERTS = 8
DIFF_LAYER = 2

LANES = 128
SUBLANES = 8
VMEM_LIMIT = 56 * 1024 * 1024
ROW_TILE = 1024
FF_TILE = 512
MOE_TILE = 1024
SEG_TILE = 512
ATT_TQ = 128
DIL_UNITS = 4
DIFF_ROW_SPLIT = 4
DIL_HEAD_SPLIT = 2
DIFF_TQ = 512
MASK_VALUE = -1e30


def _cparams(n_axes, vmem=VMEM_LIMIT):
    return pltpu.CompilerParams(dimension_semantics=("arbitrary",) * n_axes,
                                vmem_limit_bytes=vmem)


def _next_pow2(n):
    return 1 << (n - 1).bit_length()


def _rms_rows(x, g):
    ms = jnp.mean(x * x, axis=-1, keepdims=True)
    return x * lax.rsqrt(ms + RMS_EPS) * g


def _norm_matmul_kernel(x_ref, g_ref, w_ref, o_ref, hn_ref):
    @pl.when(pl.program_id(1) == 0)
    def _():
        hn_ref[...] = _rms_rows(x_ref[...], g_ref[...]).astype(BF16)

    o_ref[...] = jnp.dot(hn_ref[...], w_ref[...].astype(BF16),
                         preferred_element_type=F32).astype(o_ref.dtype)


def _norm_matmul(x, g, w, slot, out_dtype):
    t, d = x.shape
    n = w.shape[-1]
    tm = ROW_TILE
    tn = next(c for c in (1024, 768, 512, 256, 128) if n % c == 0)
    return pl.pallas_call(
        _norm_matmul_kernel,
        out_shape=jax.ShapeDtypeStruct((t, n), out_dtype),
        grid=(t // tm, n // tn),
        in_specs=[pl.BlockSpec((tm, d), lambda i, j: (i, 0)),
                  pl.BlockSpec((1, d), lambda i, j: (0, 0)),
                  pl.BlockSpec((None, d, tn), lambda i, j: (slot, 0, j))],
        out_specs=pl.BlockSpec((tm, tn), lambda i, j: (i, j)),
        scratch_shapes=[pltpu.VMEM((tm, d), BF16)],
        compiler_params=_cparams(2),
        name="norm_matmul",
    )(x, g.reshape(1, d), w)


def _matmul_kernel(a_ref, w_ref, o_ref):
    o_ref[...] = jnp.dot(a_ref[...], w_ref[...].astype(BF16),
                         preferred_element_type=F32).astype(o_ref.dtype)


def _matmul(a, w, slot, out_dtype):
    t, d = a.shape
    n = w.shape[-1]
    tm = ROW_TILE
    tn = next(c for c in (1024, 768, 512, 256, 128) if n % c == 0)
    return pl.pallas_call(
        _matmul_kernel,
        out_shape=jax.ShapeDtypeStruct((t, n), out_dtype),
        grid=(t // tm, n // tn),
        in_specs=[pl.BlockSpec((tm, d), lambda i, j: (i, 0)),
                  pl.BlockSpec((None, d, tn), lambda i, j: (slot, 0, j))],
        out_specs=pl.BlockSpec((tm, tn), lambda i, j: (i, j)),
        compiler_params=_cparams(2),
        name="matmul",
    )(a, w)


def _matmul_res_kernel(a_ref, w_ref, x_ref, o_ref, wb_ref):
    @pl.when(pl.program_id(0) == 0)
    def _():
        wb_ref[...] = w_ref[...].astype(BF16)

    o_ref[...] = x_ref[...] + jnp.dot(a_ref[...], wb_ref[...], preferred_element_type=F32)


def _matmul_res(a, w, slot, x):
    t, k = a.shape
    d = w.shape[-1]
    tm = ROW_TILE
    return pl.pallas_call(
        _matmul_res_kernel,
        out_shape=jax.ShapeDtypeStruct((t, d), F32),
        grid=(t // tm,),
        in_specs=[pl.BlockSpec((tm, k), lambda i: (i, 0)),
                  pl.BlockSpec((None, k, d), lambda i: (slot, 0, 0)),
                  pl.BlockSpec((tm, d), lambda i: (i, 0))],
        out_specs=pl.BlockSpec((tm, d), lambda i: (i, 0)),
        scratch_shapes=[pltpu.VMEM((k, d), BF16)],
        compiler_params=_cparams(1),
        name="matmul_res",
    )(a, w, x)


def _silu_mul(g, u):
    return g * (1.0 / (1.0 + jnp.exp(-g))) * u


def _swiglu_accumulate(h_ref, wg_ref, wu_ref, wd_ref, acc_ref, rows=None):
    rows = h_ref.shape[0] if rows is None else rows
    h = h_ref[pl.ds(0, rows), :]
    gate = jnp.dot(h, wg_ref[...].astype(BF16), preferred_element_type=F32)
    up = jnp.dot(h, wu_ref[...].astype(BF16), preferred_element_type=F32)
    a = _silu_mul(gate, up).astype(BF16)
    acc_ref[pl.ds(0, rows), :] += jnp.dot(a, wd_ref[...].astype(BF16), preferred_element_type=F32)


def _ffn_kernel(x_ref, g_ref, gn_ref, wg_ref, wu_ref, wd_ref, o_ref, *rest, emit_next):
    if emit_next:
        nxt_ref, hn_ref, acc_ref = rest
    else:
        hn_ref, acc_ref = rest
    f = pl.program_id(1)

    @pl.when(f == 0)
    def _():
        hn_ref[...] = _rms_rows(x_ref[...], g_ref[...]).astype(BF16)
        acc_ref[...] = jnp.zeros_like(acc_ref)

    _swiglu_accumulate(hn_ref, wg_ref, wu_ref, wd_ref, acc_ref)

    @pl.when(f == pl.num_programs(1) - 1)
    def _():
        out = x_ref[...] + acc_ref[...]
        o_ref[...] = out
        if emit_next:
            nxt_ref[...] = _rms_rows(out, gn_ref[...]).astype(BF16)


def _ffn(x, g, w_gate_up, w_down, slot, g_next=None):
    t, d = x.shape
    ff = w_down.shape[-2]
    tm, tf = ROW_TILE, FF_TILE
    nf = ff // tf
    emit_next = g_next is not None
    gn = (g_next if emit_next else g).reshape(1, d)
    row_spec = pl.BlockSpec((tm, d), lambda i, f: (i, 0))
    vec_spec = pl.BlockSpec((1, d), lambda i, f: (0, 0))
    out_shape = jax.ShapeDtypeStruct((t, d), F32)
    return pl.pallas_call(
        functools.partial(_ffn_kernel, emit_next=emit_next),
        out_shape=(out_shape, jax.ShapeDtypeStruct((t, d), BF16)) if emit_next else out_shape,
        grid=(t // tm, nf),
        in_specs=[row_spec, vec_spec, vec_spec,
                  pl.BlockSpec((None, d, tf), lambda i, f: (slot, 0, f)),
                  pl.BlockSpec((None, d, tf), lambda i, f: (slot, 0, nf + f)),
                  pl.BlockSpec((None, tf, d), lambda i, f: (slot, f, 0))],
        out_specs=(row_spec, row_spec) if emit_next else row_spec,
        scratch_shapes=[pltpu.VMEM((tm, d), BF16), pltpu.VMEM((tm, d), F32)],
        compiler_params=_cparams(2),
        name="ffn",
    )(x, g.reshape(1, d), gn, w_gate_up, w_gate_up, w_down)


POOL_PAD = 16


def _pool_kernel(u_ref, wg_ref, sc_ref, o_ref, pad_ref):
    s, c = u_ref.shape
    grp = pl.program_id(1)
    pad_ref[pl.ds(0, POOL_PAD), :] = jnp.zeros((POOL_PAD, c), F32)
    pad_ref[pl.ds(POOL_PAD + s, POOL_PAD), :] = jnp.zeros((POOL_PAD, c), F32)
    pad_ref[pl.ds(POOL_PAD, s), :] = u_ref[...]
    wb = wg_ref[...].astype(BF16)
    chunk = 256

    for gi, w in enumerate(POOL_WINDOWS):
        @pl.when(grp == gi)
        def _(w=w):
            for c0 in range(0, s, chunk):
                acc = None
                for dlt in range(-(w // 2), w - w // 2):
                    piece = pad_ref[pl.ds(POOL_PAD + c0 + dlt, chunk), :]
                    acc = piece if acc is None else acc + piece
                pos = c0 + lax.broadcasted_iota(I32, (chunk, 1), 0)
                lo = jnp.maximum(pos - w // 2, 0)
                hi = jnp.minimum(pos + (w - w // 2), s)
                cnt = (hi - lo).astype(F32)
                mixed = acc / cnt - u_ref[pl.ds(c0, chunk), :]
                y = jnp.dot(mixed.astype(BF16), wb, preferred_element_type=F32)
                o_ref[pl.ds(c0, chunk), :] = (y * sc_ref[...]).astype(o_ref.dtype)


def _pool_mix(u, w_group, scale, slot, batch, seq):
    d = u.shape[1]
    ng = len(POOL_WINDOWS)
    c = d // ng
    u3 = u.reshape(batch, seq, d)
    out = pl.pallas_call(
        _pool_kernel,
        out_shape=jax.ShapeDtypeStruct((batch, seq, d), BF16),
        grid=(batch, ng),
        in_specs=[pl.BlockSpec((None, seq, c), lambda b, g: (b, 0, g)),
                  pl.BlockSpec((None, None, c, c), lambda b, g: (slot, g, 0, 0)),
                  pl.BlockSpec((1, c), lambda b, g: (slot, g))],
        out_specs=pl.BlockSpec((None, seq, c), lambda b, g: (b, 0, g)),
        scratch_shapes=[pltpu.VMEM((seq + 2 * POOL_PAD, c), F32)],
        compiler_params=_cparams(2),
        name="pool_mix",
    )(u3, w_group, scale)
    return out.reshape(batch * seq, d)


def _rel_bucket_np(rel):
    half = N_BUCKETS // 2
    max_exact = half // 2
    n = np.abs(rel)
    ratio = np.log(np.maximum(n, 1).astype(np.float32) / np.float32(max_exact))
    big = max_exact + (ratio / np.float32(math.log(MAX_DISTANCE / max_exact))
                       * np.float32(half - max_exact)).astype(np.int32)
    big = np.minimum(big, half - 1)
    return np.where(rel > 0, half, 0) + np.where(n < max_exact, n, big)


def _seg_table(rel_bias, rel, valid):
    bucket = jnp.asarray(_rel_bucket_np(rel).astype(np.int32))
    vals = jnp.take(rel_bias.astype(F32), bucket, axis=0)
    vals = jnp.where(jnp.asarray(valid)[..., None], vals, MASK_VALUE)
    return jnp.transpose(vals, (2, 0, 1))


def _toeplitz(seg_row, rows, cols):
    w = seg_row.shape[1]
    full = jnp.broadcast_to(seg_row, (rows, w))
    rolled = pltpu.roll(full, w - (rows - 1), 1, stride=1, stride_axis=0)
    return rolled[:, :cols]


def _band_variants(hw, n_tiles):
    if n_tiles == 1:
        return (0,)
    return (0, -hw, -2 * hw)


def _band_segs(rel_bias, tq, kw, half, dil, variants, width):
    c = np.arange(width)
    rel = np.stack([r0 + c - (tq - 1) for r0 in variants])
    valid = (np.abs(rel) <= half) & (c[None, :] < tq + kw - 1)
    return _seg_table(rel_bias, rel * dil, valid)


def _band_window(qi, nq, tq, hw, kw, ln):
    if nq == 1:
        return 0, 0
    var = jnp.where(qi == 0, 0, jnp.where(qi == nq - 1, 2, 1))
    return var, jnp.clip(qi * tq - hw, 0, ln - kw)


DilCfg = collections.namedtuple("DilCfg", "dil ln nq hw kw n_var")


def _dil_proj_kernel(x_ref, g_ref, w_ref, o_ref, slab_ref, hn_ref, *, dils):
    grp, c = pl.program_id(1), pl.program_id(2)
    seq, d = x_ref.shape
    n_slabs = d // LANES
    chunk = 256

    @pl.when((grp == 0) & (c == 0))
    def _():
        for c0 in range(0, seq, chunk):
            hn = _rms_rows(x_ref[pl.ds(c0, chunk), :], g_ref[...])
            for s in range(n_slabs):
                slab_ref[s, pl.ds(c0, chunk), :] = hn[:, s * LANES:(s + 1) * LANES]

    for gi, dil in enumerate(dils):
        @pl.when((grp == gi) & (c == 0))
        def _(dil=dil):
            ln = seq // dil
            for r in range(dil):
                for c0 in range(0, ln, chunk):
                    n = min(chunk, ln)
                    rows = [slab_ref[s, pl.ds(r + c0 * dil, n, stride=dil), :] for s in range(n_slabs)]
                    hn_ref[pl.ds(r * ln + c0, n), :] = jnp.concatenate(rows, axis=1).astype(BF16)

    res = jnp.dot(hn_ref[...], w_ref[...].astype(BF16), preferred_element_type=F32)
    res = res * jnp.where(c < pl.num_programs(2) // 3, Q_SCALE, 1.0)
    for s in range(o_ref.shape[0]):
        o_ref[s] = res[:, s * LANES:(s + 1) * LANES].astype(o_ref.dtype)


def _dil_proj(x3, g, w_in, slot, dils):
    batch, seq, d = x3.shape
    n = w_in.shape[-1]
    tn = FF_TILE
    per_group = n // len(dils) // tn
    spt = tn // LANES
    return pl.pallas_call(
        functools.partial(_dil_proj_kernel, dils=dils),
        out_shape=jax.ShapeDtypeStruct((batch, n // LANES, seq, LANES), BF16),
        grid=(batch, len(dils), per_group),
        in_specs=[pl.BlockSpec((None, seq, d), lambda b, g, c: (b, 0, 0)),
                  pl.BlockSpec((1, d), lambda b, g, c: (0, 0)),
                  pl.BlockSpec((None, d, tn), lambda b, g, c: (slot, 0, g * per_group + c))],
        out_specs=pl.BlockSpec((None, spt, seq, LANES), lambda b, g, c: (b, g * per_group + c, 0, 0)),
        scratch_shapes=[pltpu.VMEM((d // LANES, seq, LANES), F32), pltpu.VMEM((seq, d), BF16)],
        compiler_params=_cparams(3),
        name="dil_proj",
    )(x3, g.reshape(1, d), w_in)


def _dil_unit(cfg, unit, q4, k_ref, v_ref, bias_ref, acc_ref, m_ref, l_ref, *, tq, first, last):
    dil, ln, nq, kw = cfg.dil, cfg.ln, cfg.nq, cfg.kw
    n_pairs = q4.shape[0]
    if nq == 1:
        r, qi = unit, 0
    elif dil == 1:
        r, qi = 0, unit
    else:
        r, qi = unit // nq, unit % nq
    var, ks = _band_window(qi, nq, tq, cfg.hw, kw, ln)
    krow = pl.multiple_of(r * ln + ks, cfg.hw)
    if dil == 1:
        rows = pl.ds(pl.multiple_of(qi * tq, tq), tq)
    else:
        rows = pl.ds(qi * tq * dil + r, tq, stride=dil)

    even = lax.broadcasted_iota(I32, (n_pairs, tq, LANES), 2) < HEAD_DIM
    zero = jnp.zeros_like(q4)
    q8 = jnp.concatenate([jnp.where(even, q4, zero), jnp.where(even, zero, q4)], axis=0)
    k4 = k_ref[:, pl.ds(krow, kw), :]
    v4 = v_ref[:, pl.ds(krow, kw), :]
    k8 = jnp.concatenate([k4, k4], axis=0)
    va = jnp.concatenate([v4, jnp.ones_like(v4)], axis=2)
    v8 = jnp.concatenate([va, va], axis=0)
    s = lax.dot_general(q8, k8, (((2,), (2,)), ((0,), (0,))), preferred_element_type=F32)
    s = s + bias_ref[var]
    m8 = jnp.max(s, axis=-1, keepdims=True)
    p = jnp.exp(s - m8).astype(BF16)
    ov = lax.dot_general(p, v8, (((2,), (1,)), ((0,), (0,))), preferred_element_type=F32)
    o_c = jnp.where(even, ov[:n_pairs, :, :LANES], ov[n_pairs:, :, :LANES])
    l_c = jnp.where(even, ov[:n_pairs, :, LANES:], ov[n_pairs:, :, LANES:])
    m_c = jnp.where(even, m8[:n_pairs], m8[n_pairs:])
    if first:
        m_n, l_n, acc_n = m_c, l_c, o_c
    else:
        m_o = jnp.stack([m_ref[pp, rows, :] for pp in range(n_pairs)])
        l_o = jnp.stack([l_ref[pp, rows, :] for pp in range(n_pairs)])
        acc_o = jnp.stack([acc_ref[pp, rows, :] for pp in range(n_pairs)])
        m_n = jnp.maximum(m_o, m_c)
        a_o = jnp.exp(m_o - m_n)
        a_c = jnp.exp(m_c - m_n)
        l_n = a_o * l_o + a_c * l_c
        acc_n = a_o * acc_o + a_c * o_c
    if last:
        acc_n = acc_n / l_n
    for pp in range(n_pairs):
        acc_ref[pp, rows, :] = acc_n[pp]
        if not last:
            m_ref[pp, rows, :] = m_n[pp]
            l_ref[pp, rows, :] = l_n[pp]


def _dil_attn_kernel(*refs, cfgs, tq):
    ng = len(cfgs)
    seg_refs = refs[:ng]
    q_ref, k_ref, v_ref, o_ref = refs[ng:ng + 4]
    bias_refs = refs[ng + 4:2 * ng + 4]
    acc_ref, m_ref, l_ref = refs[2 * ng + 4:]
    hf, b, grp, step = (pl.program_id(i) for i in range(4))
    n_pairs = q_ref.shape[0]

    @pl.when((b == 0) & (grp == 0) & (step == 0))
    def _():
        for gi, cfg in enumerate(cfgs):
            for v in range(cfg.n_var):
                for odd in range(2):
                    for pp in range(n_pairs):
                        head = hf * 2 * n_pairs + 2 * pp + odd
                        row = seg_refs[gi][head, pl.ds(v, 1), :]
                        bias_refs[gi][v, odd * n_pairs + pp] = _toeplitz(row, tq, cfg.kw)

    for gi, cfg in enumerate(cfgs):
        @pl.when(grp == gi)
        def _(gi=gi, cfg=cfg):
            for uu in range(DIL_UNITS):
                _dil_unit(cfg, step * DIL_UNITS + uu, q_ref[:, pl.ds(uu * tq, tq), :], k_ref, v_ref,
                          bias_refs[gi], acc_ref, m_ref, l_ref, tq=tq, first=gi == 0, last=gi == ng - 1)

    @pl.when((grp == ng - 1) & (step == pl.num_programs(3) - 1))
    def _():
        slabs = [acc_ref[s] for s in range(n_pairs)]
        o_ref[...] = jnp.concatenate(slabs, axis=1).astype(o_ref.dtype)


def _dilated_attention(qkv, rel_bias, batch, seq):
    ng = len(DIL_PAIRS)
    d = qkv.shape[1] * LANES // (3 * ng)
    hd = d // DIL_HEAD_SPLIT
    n_pairs = hd // LANES
    tq = ATT_TQ
    cfgs, segs = [], []
    for win, dil in DIL_PAIRS:
        half = win // (2 * dil)
        ln = seq // dil
        nq = ln // tq
        kw = min(tq + 2 * half, ln)
        variants = _band_variants(half, nq)
        cfgs.append(DilCfg(dil, ln, nq, half, kw, len(variants)))
        segs.append(_band_segs(rel_bias, tq, kw, half, dil, variants, _next_pow2(tq + kw - 1)))
    steps = seq // tq // DIL_UNITS
    cb = DIL_HEAD_SPLIT

    in_specs = [pl.BlockSpec(sg.shape, lambda hf, b, g, s: (0, 0, 0)) for sg in segs]
    in_specs += [pl.BlockSpec((None, n_pairs, DIL_UNITS * tq, LANES),
                              lambda hf, b, g, s: (b, (g * 3) * cb + hf, s, 0)),
                 pl.BlockSpec((None, n_pairs, seq, LANES),
                              lambda hf, b, g, s: (b, (g * 3 + 1) * cb + hf, 0, 0)),
                 pl.BlockSpec((None, n_pairs, seq, LANES),
                              lambda hf, b, g, s: (b, (g * 3 + 2) * cb + hf, 0, 0))]
    scratch = [pltpu.VMEM((c.n_var, 2 * n_pairs, tq, c.kw), F32) for c in cfgs]
    scratch += [pltpu.VMEM((n_pairs, seq, LANES), F32) for _ in range(3)]
    out = pl.pallas_call(
        functools.partial(_dil_attn_kernel, cfgs=tuple(cfgs), tq=tq),
        out_shape=jax.ShapeDtypeStruct((batch, seq, d), BF16),
        grid=(DIL_HEAD_SPLIT, batch, ng, steps),
        in_specs=in_specs,
        out_specs=pl.BlockSpec((None, seq, hd), lambda hf, b, g, s: (b, 0, hf)),
        scratch_shapes=scratch,
        compiler_params=_cparams(4),
        name="dil_attn",
    )(*segs, qkv, qkv, qkv)
    return out.reshape(batch * seq, d)


def _gqa_kernel(sink_ref, seg_ref, q_ref, k_ref, v_ref, o_ref, bias_ref, *, tq, kw, hw, grp):
    b, qi = pl.program_id(0), pl.program_id(1)
    nq = pl.num_programs(1)
    seq_len = k_ref.shape[0]
    n_kv = k_ref.shape[1] // HEAD_DIM

    @pl.when((b == 0) & (qi == 0))
    def _():
        for v in range(3):
            for kh in range(n_kv):
                for gq in range(grp):
                    bias_ref[v, kh, pl.ds(gq * tq, tq), :] = _toeplitz(
                        seg_ref[kh * grp + gq, pl.ds(v, 1), :], tq, kw)

    var = jnp.where(qi == 0, 0, jnp.where(qi == nq - 1, 2, 1))
    ks = pl.multiple_of(jnp.clip(qi * tq - hw, 0, seq_len - kw), hw)
    row = lax.broadcasted_iota(I32, (grp * tq, 1), 0)
    outs = [None] * (n_kv * grp)
    for kh in range(n_kv):
        cs = slice(kh * HEAD_DIM, (kh + 1) * HEAD_DIM)
        qs = jnp.concatenate(
            [q_ref[:, (kh * grp + gq) * HEAD_DIM:(kh * grp + gq + 1) * HEAD_DIM] for gq in range(grp)],
            axis=0) * Q_SCALE
        kk = k_ref[pl.ds(ks, kw), cs]
        vv = v_ref[pl.ds(ks, kw), cs]
        s = lax.dot_general(qs, kk, (((1,), (1,)), ((), ())), preferred_element_type=F32)
        s = s + bias_ref[var, kh]
        sk = jnp.zeros((grp * tq, 1), F32)
        for gq in range(grp):
            sk = jnp.where((row >= gq * tq) & (row < (gq + 1) * tq), sink_ref[kh * grp + gq], sk)
        m = jnp.maximum(jnp.max(s, axis=-1, keepdims=True), sk)
        e = jnp.exp(s - m).astype(BF16)
        va = jnp.concatenate([vv, jnp.ones_like(vv)], axis=1)
        ov = jnp.dot(e, va, preferred_element_type=F32)
        o = ov[:, :HEAD_DIM] / (ov[:, HEAD_DIM:] + jnp.exp(sk - m))
        for gq in range(grp):
            outs[kh * grp + gq] = o[gq * tq:(gq + 1) * tq, :]
    o_ref[...] = jnp.concatenate(outs, axis=1).astype(o_ref.dtype)


def _gqa_attention(qkv, sink, rel_bias, batch, seq):
    t, ncol = qkv.shape
    n_q = sink.shape[0]
    d = n_q * HEAD_DIM
    kvw = GQA_KV_HEADS * HEAD_DIM
    grp = n_q // GQA_KV_HEADS
    tq = hw = GQA_WINDOW
    kw = 3 * GQA_WINDOW
    nq = seq // tq
    variants = _band_variants(hw, nq)
    width = _next_pow2(tq + kw - 1)
    seg = _band_segs(rel_bias, tq, kw, GQA_WINDOW, 1, variants, width)
    qkv_v = qkv.reshape(batch, seq, ncol)
    out = pl.pallas_call(
        functools.partial(_gqa_kernel, tq=tq, kw=kw, hw=hw, grp=grp),
        out_shape=jax.ShapeDtypeStruct((batch, seq, d), BF16),
        grid=(batch, nq),
        in_specs=[pl.BlockSpec(memory_space=pltpu.SMEM),
                  pl.BlockSpec((n_q, 3, width), lambda b, qi: (0, 0, 0)),
                  pl.BlockSpec((None, tq, d), lambda b, qi: (b, qi, 0)),
                  pl.BlockSpec((None, seq, kvw), lambda b, qi: (b, 0, d // kvw)),
                  pl.BlockSpec((None, seq, kvw), lambda b, qi: (b, 0, d // kvw + 1))],
        out_specs=pl.BlockSpec((None, tq, d), lambda b, qi: (b, qi, 0)),
        scratch_shapes=[pltpu.VMEM((3, GQA_KV_HEADS, grp * tq, kw), F32)],
        compiler_params=_cparams(2),
        name="gqa_attn",
    )(sink.astype(F32), seg, qkv_v, qkv_v, qkv_v)
    return out.reshape(t, d)


def _diff_kernel(seg_ref, lam_ref, sub_ref, q_ref, k_ref, v_ref, o_ref, bias_ref, *, tq, lam_init):
    h, qi, b = pl.program_id(0), pl.program_id(1), pl.program_id(2)
    seq_len = k_ref.shape[0]
    nk = seq_len // tq

    @pl.when(b == 0)
    def _():
        for j in range(2):
            for ki in range(nk):
                row = seg_ref[h * 2 + j, pl.ds(ki - qi + nk - 1, 1), :]
                bias_ref[j, :, pl.ds(ki * tq, tq)] = _toeplitz(row, tq, tq)

    lv = lam_ref[...]
    s01 = jnp.sum(lv[0:1, :] * lv[1:2, :], axis=-1, keepdims=True)
    s23 = jnp.sum(lv[2:3, :] * lv[3:4, :], axis=-1, keepdims=True)
    lam = jnp.exp(s01) - jnp.exp(s23) + lam_init

    q = q_ref[...] * Q_SCALE
    k = k_ref[...]
    v = v_ref[...]
    va = jnp.concatenate([v, jnp.ones_like(v)], axis=1)
    map0 = lax.broadcasted_iota(I32, q.shape, 1) < HEAD_DIM
    zero = jnp.zeros_like(q)
    qm = (jnp.where(map0, q, zero), jnp.where(map0, zero, q))
    rc = tq // DIFF_ROW_SPLIT
    for c0 in range(0, tq, rc):
        outs = []
        for j in range(2):
            s = lax.dot_general(qm[j][c0:c0 + rc], k, (((1,), (1,)), ((), ())),
                                preferred_element_type=F32)
            s = s + bias_ref[j, pl.ds(c0, rc), :]
            m = jnp.max(s, axis=-1, keepdims=True)
            p = jnp.exp(s - m).astype(BF16)
            ov = jnp.dot(p, va, preferred_element_type=F32)
            outs.append(ov[:, :2 * HEAD_DIM] / ov[:, 2 * HEAD_DIM:])
        o = outs[0] - lam * outs[1]
        o = _rms_rows(o, sub_ref[...]) * (1.0 - lam_init)
        o_ref[pl.ds(c0, rc), :] = o.astype(o_ref.dtype)


def _diff_attention(qkv, lam_vecs, subln, slot, rel_bias, batch, seq):
    t, ncol = qkv.shape
    d = ncol // 3
    hd2 = 2 * HEAD_DIM
    n_heads = d // hd2
    tq = DIFF_TQ
    nk = seq // tq
    width = _next_pow2(2 * tq - 1)
    c = np.arange(width)
    rel = np.stack([(dl - (nk - 1)) * tq + c - (tq - 1) for dl in range(2 * nk - 1)])
    seg = _seg_table(rel_bias, rel, np.ones_like(rel, dtype=bool))
    lam_init = 0.8 - 0.6 * math.exp(-0.3 * DIFF_LAYER)
    qkv_v = qkv.reshape(batch, seq, ncol)
    out = pl.pallas_call(
        functools.partial(_diff_kernel, tq=tq, lam_init=lam_init),
        out_shape=jax.ShapeDtypeStruct((batch, seq, d), BF16),
        grid=(n_heads, seq // tq, batch),
        in_specs=[pl.BlockSpec(seg.shape, lambda h, qi, b: (0, 0, 0)),
                  pl.BlockSpec((None,) + lam_vecs.shape[1:], lambda h, qi, b: (slot, 0, 0)),
                  pl.BlockSpec((1, hd2), lambda h, qi, b: (slot, 0)),
                  pl.BlockSpec((None, tq, hd2), lambda h, qi, b: (b, qi, h)),
                  pl.BlockSpec((None, seq, hd2), lambda h, qi, b: (b, 0, n_heads + h)),
                  pl.BlockSpec((None, seq, hd2), lambda h, qi, b: (b, 0, 2 * n_heads + h))],
        out_specs=pl.BlockSpec((None, tq, hd2), lambda h, qi, b: (b, qi, h)),
        scratch_shapes=[pltpu.VMEM((2, tq, seq), F32)],
        compiler_params=_cparams(3),
        name="diff_attn",
    )(seg, lam_vecs, subln, qkv_v, qkv_v, qkv_v)
    return out.reshape(t, d)


SEG_ALIGN = 16
SEG_SIZES = (512, 256, 128, 64, 32, 16)
SEG_STAGE = 2 * SEG_TILE + N_EXPERTS * SEG_ALIGN


def _route_kernel(a_ref, wo_ref, x_ref, g_ref, r_ref, xn_ref, hn_ref, mi_ref, mf_ref, cnt_ref,
                  tri_ref, wb_ref):
    tm, d = x_ref.shape

    @pl.when(pl.program_id(0) == 0)
    def _():
        rr = lax.broadcasted_iota(I32, (tm, tm), 0)
        cc = lax.broadcasted_iota(I32, (tm, tm), 1)
        tri_ref[...] = (cc < rr).astype(BF16)
        wb_ref[...] = wo_ref[...].astype(BF16)

    xn = x_ref[...] + jnp.dot(a_ref[...], wb_ref[...], preferred_element_type=F32)
    xn_ref[...] = xn
    hn = _rms_rows(xn, g_ref[...])
    hn_ref[...] = hn.astype(BF16)
    h_hi = hn.astype(BF16)
    h_lo = (hn - h_hi.astype(F32)).astype(BF16)
    r = r_ref[...]
    r_hi = r.astype(BF16)
    r_lo = (r - r_hi.astype(F32)).astype(BF16)
    logits = (jnp.dot(h_hi, r_hi, preferred_element_type=F32)
              + jnp.dot(h_hi, r_lo, preferred_element_type=F32)
              + jnp.dot(h_lo, r_hi, preferred_element_type=F32))
    lane = lax.broadcasted_iota(I32, (tm, LANES), 1)
    logits = jnp.where(lane < N_EXPERTS, logits, -jnp.inf)
    v1 = jnp.max(logits, axis=-1, keepdims=True)
    i1 = jnp.min(jnp.where(logits == v1, lane, LANES), axis=-1, keepdims=True)
    oh1 = lane == i1
    rest = jnp.where(oh1, -jnp.inf, logits)
    v2 = jnp.max(rest, axis=-1, keepdims=True)
    i2 = jnp.min(jnp.where(rest == v2, lane, LANES), axis=-1, keepdims=True)
    oh2 = lane == i2
    e2 = jnp.exp(v2 - v1)
    g1 = 1.0 / (1.0 + e2)
    g2 = e2 / (1.0 + e2)

    sel = (oh1 | oh2)
    before = jnp.dot(tri_ref[...], sel.astype(BF16), preferred_element_type=F32)
    rank1 = jnp.sum(jnp.where(oh1, before, 0.0), axis=-1, keepdims=True).astype(I32)
    rank2 = jnp.sum(jnp.where(oh2, before, 0.0), axis=-1, keepdims=True).astype(I32)
    counts = jnp.sum(sel.astype(F32), axis=0, keepdims=True).astype(I32)

    mi_ref[...] = jnp.where(lane == 0, i1, jnp.where(lane == 1, i2,
                            jnp.where(lane == 2, rank1, jnp.where(lane == 3, rank2, 0))))
    mf_ref[...] = jnp.where(lane == 0, g1, jnp.where(lane == 1, g2, 0.0))
    cnt_ref[...] = jnp.broadcast_to(counts, cnt_ref.shape)


def _route(a, w_out, out_slot, x, g, router):
    t, d = x.shape
    k = a.shape[1]
    tm = SEG_TILE
    r_pad = jnp.zeros((d, LANES), F32).at[:, :N_EXPERTS].set(router.astype(F32))
    return pl.pallas_call(
        _route_kernel,
        out_shape=(jax.ShapeDtypeStruct((t, d), F32),
                   jax.ShapeDtypeStruct((t, d), BF16),
                   jax.ShapeDtypeStruct((t, LANES), I32),
                   jax.ShapeDtypeStruct((t, LANES), F32),
                   jax.ShapeDtypeStruct((t // tm, SUBLANES, LANES), I32)),
        grid=(t // tm,),
        in_specs=[pl.BlockSpec((tm, k), lambda i: (i, 0)),
                  pl.BlockSpec((None, k, d), lambda i: (out_slot, 0, 0)),
                  pl.BlockSpec((tm, d), lambda i: (i, 0)),
                  pl.BlockSpec((1, d), lambda i: (0, 0)),
                  pl.BlockSpec((d, LANES), lambda i: (0, 0))],
        out_specs=(pl.BlockSpec((tm, d), lambda i: (i, 0)),
                   pl.BlockSpec((tm, d), lambda i: (i, 0)),
                   pl.BlockSpec((tm, LANES), lambda i: (i, 0)),
                   pl.BlockSpec((tm, LANES), lambda i: (i, 0)),
                   pl.BlockSpec((None, SUBLANES, LANES), lambda i: (i, 0, 0))),
        scratch_shapes=[pltpu.VMEM((tm, tm), BF16), pltpu.VMEM((k, d), BF16)],
        compiler_params=_cparams(1),
        name="moe_route",
    )(a, w_out, x, g.reshape(1, d), r_pad)


def _slot_rows(mi, loc_ref, base):
    e1, e2, d1, d2 = mi[0], mi[1], mi[2], mi[3]
    for e in range(N_EXPERTS):
        off = loc_ref[base + e]
        d1 = d1 + jnp.where(e1 == e, off, 0)
        d2 = d2 + jnp.where(e2 == e, off, 0)
    return d1, d2


def _segment_copies(base, seg_ref, n16_ref, loc_ref, hbm_ref, vmem_ref, sem, *, to_hbm, wait):
    for e in range(N_EXPERTS):
        n16 = n16_ref[base + e]
        hbm0 = seg_ref[base + e]
        vmem0 = loc_ref[base + e]
        off = 0
        for size in SEG_SIZES:
            @pl.when((n16 & size) != 0)
            def _(off=off, size=size):
                h = hbm_ref.at[pl.ds(pl.multiple_of(hbm0 + off, SEG_ALIGN), size)]
                v = vmem_ref.at[pl.ds(pl.multiple_of(vmem0 + off, SEG_ALIGN), size)]
                cp = pltpu.make_async_copy(v, h, sem) if to_hbm else pltpu.make_async_copy(h, v, sem)
                if wait:
                    cp.wait()
                else:
                    cp.start()
            off = off + (n16 & size)


def _scatter_kernel(seg_ref, n16_ref, loc_ref, tail_ref, hn_ref, mi_ref, xs_ref, stage_ref, zero_ref, sem):
    i = pl.program_id(0)
    tm = hn_ref.shape[0]
    rows = stage_ref.shape[1]
    tg = zero_ref.shape[0]
    slot = lax.rem(i, 2)

    @pl.when(i == 0)
    def _():
        zero_ref[...] = jnp.zeros_like(zero_ref)
        for e in range(N_EXPERTS):
            tail = pl.multiple_of(tail_ref[e], SEG_ALIGN)
            pltpu.make_async_copy(zero_ref, xs_ref.at[pl.ds(tail, tg)], sem.at[2]).start()
        for e in range(N_EXPERTS):
            tail = pl.multiple_of(tail_ref[e], SEG_ALIGN)
            pltpu.make_async_copy(zero_ref, xs_ref.at[pl.ds(tail, tg)], sem.at[2]).wait()

    base = i * N_EXPERTS
    mi_t = mi_ref[...].T
    d1, d2 = _slot_rows([mi_t[k:k + 1, :] for k in range(4)], loc_ref, base)
    row = lax.broadcasted_iota(I32, (rows, tm), 0)
    onehot = ((row == d1) | (row == d2)).astype(BF16)
    stage_ref[slot] = jnp.dot(onehot, hn_ref[...], preferred_element_type=F32).astype(BF16)
    tabs = (seg_ref, n16_ref, loc_ref, xs_ref)
    _segment_copies(base, *tabs, stage_ref.at[slot], sem.at[slot], to_hbm=True, wait=False)

    @pl.when(i > 0)
    def _():
        _segment_copies(base - N_EXPERTS, *tabs, stage_ref.at[1 - slot], sem.at[1 - slot],
                        to_hbm=True, wait=True)

    @pl.when(i == pl.num_programs(0) - 1)
    def _():
        _segment_copies(base, *tabs, stage_ref.at[slot], sem.at[slot], to_hbm=True, wait=True)


def _scatter(hn, mi, seg, n16, loc, tails, n_rows):
    t, d = hn.shape
    tm = SEG_TILE
    return pl.pallas_call(
        _scatter_kernel,
        out_shape=jax.ShapeDtypeStruct((n_rows, d), BF16),
        grid_spec=pltpu.PrefetchScalarGridSpec(
            num_scalar_prefetch=4,
            grid=(t // tm,),
            in_specs=[pl.BlockSpec((tm, d), lambda i, *_: (i, 0)),
                      pl.BlockSpec((tm, LANES), lambda i, *_: (i, 0))],
            out_specs=pl.BlockSpec(memory_space=pl.ANY),
            scratch_shapes=[pltpu.VMEM((2, SEG_STAGE, d), BF16), pltpu.VMEM((MOE_TILE, d), BF16),
                            pltpu.SemaphoreType.DMA((3,))]),
        compiler_params=pltpu.CompilerParams(dimension_semantics=("arbitrary",),
                                             vmem_limit_bytes=VMEM_LIMIT, has_side_effects=True),
        name="moe_scatter",
    )(seg, n16, loc, tails, hn, mi)


def _seg_expert_kernel(te_ref, tb_ref, nr_ref, nv_ref, xs_ref, wg_ref, wu_ref, wd_ref, ys_ref, acc_ref):
    i, f = pl.program_id(0), pl.program_id(1)
    half = xs_ref.shape[0] // 2

    @pl.when(i < nv_ref[0])
    def _():
        @pl.when(f == 0)
        def _():
            acc_ref[...] = jnp.zeros_like(acc_ref)

        @pl.when(nr_ref[i] > half)
        def _():
            _swiglu_accumulate(xs_ref, wg_ref, wu_ref, wd_ref, acc_ref)

        @pl.when(nr_ref[i] <= half)
        def _():
            _swiglu_accumulate(xs_ref, wg_ref, wu_ref, wd_ref, acc_ref, rows=half)

        @pl.when(f == pl.num_programs(1) - 1)
        def _():
            ys_ref[...] = acc_ref[...].astype(ys_ref.dtype)


def _seg_experts(xs, w_gate_up, w_down, slot, tile_expert, tile_block, tile_rows, n_valid, n_tiles):
    n_rows, d = xs.shape
    ff = w_down.shape[-2]
    tg, tf = MOE_TILE, FF_TILE
    nf = ff // tf

    def fidx(i, f, nv):
        return jnp.where(i < nv[0], f, nf - 1)

    return pl.pallas_call(
        _seg_expert_kernel,
        out_shape=jax.ShapeDtypeStruct((n_rows, d), BF16),
        grid_spec=pltpu.PrefetchScalarGridSpec(
            num_scalar_prefetch=4,
            grid=(n_tiles, nf),
            in_specs=[pl.BlockSpec((tg, d), lambda i, f, te, tb, nr, nv: (tb[i], 0)),
                      pl.BlockSpec((None, None, d, tf),
                                   lambda i, f, te, tb, nr, nv: (slot, te[i], 0, fidx(i, f, nv))),
                      pl.BlockSpec((None, None, d, tf),
                                   lambda i, f, te, tb, nr, nv: (slot, te[i], 0, nf + fidx(i, f, nv))),
                      pl.BlockSpec((None, None, tf, d),
                                   lambda i, f, te, tb, nr, nv: (slot, te[i], fidx(i, f, nv), 0))],
            out_specs=pl.BlockSpec((tg, d), lambda i, f, te, tb, nr, nv: (tb[i], 0)),
            scratch_shapes=[pltpu.VMEM((tg, d), F32)]),
        compiler_params=_cparams(2),
        name="moe_experts",
    )(tile_expert, tile_block, tile_rows, n_valid, xs, w_gate_up, w_gate_up, w_down)


def _gather_kernel(seg_ref, n16_ref, loc_ref, x_ref, mi_ref, mf_ref, gf_ref, ys_ref, o_ref,
                   *rest, norm):
    if norm == "next":
        nxt_ref, ybuf_ref, sem = rest
    else:
        ybuf_ref, sem = rest
    i = pl.program_id(0)
    tm = x_ref.shape[0]
    rows = ybuf_ref.shape[1]
    slot = lax.rem(i, 2)
    base = i * N_EXPERTS
    tabs = (seg_ref, n16_ref, loc_ref, ys_ref)

    @pl.when(i == 0)
    def _():
        ybuf_ref[...] = jnp.zeros_like(ybuf_ref)
        _segment_copies(base, *tabs, ybuf_ref.at[0], sem.at[0], to_hbm=False, wait=False)

    @pl.when(i + 1 < pl.num_programs(0))
    def _():
        _segment_copies(base + N_EXPERTS, *tabs, ybuf_ref.at[1 - slot], sem.at[1 - slot],
                        to_hbm=False, wait=False)

    mi = mi_ref[...]
    d1, d2 = _slot_rows([mi[:, k:k + 1] for k in range(4)], loc_ref, base)
    col = lax.broadcasted_iota(I32, (tm, rows), 1)
    gates = mf_ref[...]
    _segment_copies(base, *tabs, ybuf_ref.at[slot], sem.at[slot], to_hbm=False, wait=True)
    ybuf = ybuf_ref[slot]
    out = x_ref[...]
    for k, dk in enumerate((d1, d2)):
        yk = jnp.dot((col == dk).astype(BF16), ybuf, preferred_element_type=F32)
        out = out + gates[:, k:k + 1] * yk
    if norm == "final":
        out = _rms_rows(out, gf_ref[...])
    o_ref[...] = out
    if norm == "next":
        nxt_ref[...] = _rms_rows(out, gf_ref[...]).astype(BF16)


def _gather(x, mi, mf, ys, seg, n16, loc, g_norm, norm):
    t, d = x.shape
    tm = SEG_TILE
    row_spec = pl.BlockSpec((tm, d), lambda i, *_: (i, 0))
    out_shape = jax.ShapeDtypeStruct((t, d), F32)
    emit_next = norm == "next"
    return pl.pallas_call(
        functools.partial(_gather_kernel, norm=norm),
        out_shape=(out_shape, jax.ShapeDtypeStruct((t, d), BF16)) if emit_next else out_shape,
        grid_spec=pltpu.PrefetchScalarGridSpec(
            num_scalar_prefetch=3,
            grid=(t // tm,),
            in_specs=[row_spec,
                      pl.BlockSpec((tm, LANES), lambda i, *_: (i, 0)),
                      pl.BlockSpec((tm, LANES), lambda i, *_: (i, 0)),
                      pl.BlockSpec((1, d), lambda i, *_: (0, 0)),
                      pl.BlockSpec(memory_space=pl.ANY)],
            out_specs=(row_spec, row_spec) if emit_next else row_spec,
            scratch_shapes=[pltpu.VMEM((2, SEG_STAGE, d), BF16), pltpu.SemaphoreType.DMA((2,))]),
        compiler_params=_cparams(1),
        name="moe_gather",
    )(seg, n16, loc, x, mi, mf, g_norm.reshape(1, d).astype(F32), ys)


def _moe_seg(a, w_out, out_slot, x, g, router, w_gate_up, w_down, slot, g_norm, norm):
    t, d = x.shape
    tg = MOE_TILE
    x, hn, mi, mf, cnt = _route(a, w_out, out_slot, x, g, router)

    counts = cnt[:, 0, :N_EXPERTS]
    n16 = (counts + SEG_ALIGN - 1) // SEG_ALIGN * SEG_ALIGN
    rows_e = jnp.sum(n16, axis=0)
    tiles_e = (rows_e + tg - 1) // tg
    tile_start = jnp.cumsum(tiles_e) - tiles_e
    row_start = tile_start * tg
    seg = row_start[None, :] + jnp.cumsum(n16, axis=0) - n16
    loc = jnp.cumsum(n16, axis=1) - n16
    n_valid = jnp.sum(tiles_e).astype(I32)
    n_tiles = (2 * t + counts.size * (SEG_ALIGN - 1)) // tg + N_EXPERTS
    n_rows = (n_tiles + 1) * tg
    tidx = jnp.minimum(jnp.arange(n_tiles, dtype=I32), n_valid - 1)
    tile_expert = (jnp.sum(tidx[:, None] >= tile_start[None, :], axis=1) - 1).astype(I32)
    tails = (row_start + rows_e).astype(I32)
    seg, n16, loc = (a.reshape(-1).astype(I32) for a in (seg, n16, loc))

    xs = _scatter(hn, mi, seg, n16, loc, tails, n_rows)
    tile_rows = (tails[tile_expert] - tidx * tg).astype(I32)
    ys = _seg_experts(xs, w_gate_up, w_down, slot, tile_expert, tidx, tile_rows, n_valid.reshape(1), n_tiles)
    return _gather(x, mi, mf, ys, seg, n16, loc, g_norm, norm)


def kernel(x, rel_bias, norm_mix, norm_ffn, norm_final, a_w_in, a_w_group, a_scale, a_w_out,
           b_w_in, b_w_out, c_w_in, c_lambda, c_subln, c_w_out, d_w_in, d_sink, d_w_out,
           f_w_gate_up, f_w_down, m_router, m_w_gate_up, m_w_down):
    batch, seq, d = x.shape
    h = x.reshape(batch * seq, d)

    u = _norm_matmul(h, norm_mix[0], a_w_in, 0, F32)
    y = _pool_mix(u, a_w_group, a_scale, 0, batch, seq)
    h = _matmul_res(y, a_w_out, 0, h)
    h = _ffn(h, norm_ffn[0], f_w_gate_up, f_w_down, 0)

    qkv = _dil_proj(h.reshape(batch, seq, d), norm_mix[1], b_w_in, 0, tuple(p[1] for p in DIL_PAIRS))
    o = _dilated_attention(qkv, rel_bias, batch, seq)
    h, hn = _moe_seg(o, b_w_out, 0, h, norm_ffn[1], m_router[0], m_w_gate_up, m_w_down, 0,
                     norm_mix[2], "next")

    qkv = _matmul(hn, c_w_in, 0, BF16)
    o = _diff_attention(qkv, c_lambda, c_subln, 0, rel_bias, batch, seq)
    h = _matmul_res(o, c_w_out, 0, h)
    h, hn = _ffn(h, norm_ffn[2], f_w_gate_up, f_w_down, 1, g_next=norm_mix[3])

    qkv = _matmul(hn, d_w_in, 0, BF16)
    o = _gqa_attention(qkv, d_sink[0], rel_bias, batch, seq)
    h = _moe_seg(o, d_w_out, 0, h, norm_ffn[3], m_router[1], m_w_gate_up, m_w_down, 1,
                 norm_final, "final")
    return h.reshape(batch, seq, d)
```

```python
import collections
import functools
import math

import numpy as np
import jax
import jax.numpy as jnp
from jax import lax
from jax.experimental import pallas as pl
from jax.experimental.pallas import tpu as pltpu

F32 = jnp.float32
BF16 = jnp.bfloat16
I32 = jnp.int32

RMS_EPS = 1e-6
HEAD_DIM = 64
N_BUCKETS = 32
MAX_DISTANCE = 1024
POOL_WINDOWS = (2, 4, 8, 16)
DIL_PAIRS = ((128, 1), (512, 4), (2048, 16))
Q_SCALE = HEAD_DIM ** -0.5
GQA_KV_HEADS = 4
GQA_WINDOW = 128
N_EXPERTS = 8
DIFF_LAYER = 2

LANES = 128
SUBLANES = 8
VMEM_LIMIT = 56 * 1024 * 1024
ROW_TILE = 1024
FF_TILE = 512
MOE_TILE = 1024
MOE_HEIGHTS = 4
ROUTE_ROW_SPLIT = 2
SEG_TILE = 512
ATT_TQ = 128
DIL_UNITS = 4
DIFF_ROW_SPLIT = 4
DIL_HEAD_SPLIT = 2
DIFF_TQ = 512
MASK_VALUE = -1e30


def _cparams(n_axes, vmem=VMEM_LIMIT):
    return pltpu.CompilerParams(dimension_semantics=("arbitrary",) * n_axes,
                                vmem_limit_bytes=vmem)


def _next_pow2(n):
    return 1 << (n - 1).bit_length()


def _rms_rows(x, g):
    ms = jnp.mean(x * x, axis=-1, keepdims=True)
    return x * lax.rsqrt(ms + RMS_EPS) * g


def _norm_matmul_kernel(x_ref, g_ref, w_ref, o_ref, hn_ref):
    @pl.when(pl.program_id(1) == 0)
    def _():
        hn_ref[...] = _rms_rows(x_ref[...], g_ref[...]).astype(BF16)

    o_ref[...] = jnp.dot(hn_ref[...], w_ref[...].astype(BF16),
                         preferred_element_type=F32).astype(o_ref.dtype)


def _norm_matmul(x, g, w, slot, out_dtype):
    t, d = x.shape
    n = w.shape[-1]
    tm = ROW_TILE
    tn = next(c for c in (1024, 768, 512, 256, 128) if n % c == 0)
    return pl.pallas_call(
        _norm_matmul_kernel,
        out_shape=jax.ShapeDtypeStruct((t, n), out_dtype),
        grid=(t // tm, n // tn),
        in_specs=[pl.BlockSpec((tm, d), lambda i, j: (i, 0)),
                  pl.BlockSpec((1, d), lambda i, j: (0, 0)),
                  pl.BlockSpec((None, d, tn), lambda i, j: (slot, 0, j))],
        out_specs=pl.BlockSpec((tm, tn), lambda i, j: (i, j)),
        scratch_shapes=[pltpu.VMEM((tm, d), BF16)],
        compiler_params=_cparams(2),
        name="norm_matmul",
    )(x, g.reshape(1, d), w)


def _matmul_kernel(a_ref, w_ref, o_ref):
    o_ref[...] = jnp.dot(a_ref[...], w_ref[...].astype(BF16),
                         preferred_element_type=F32).astype(o_ref.dtype)


def _matmul(a, w, slot, out_dtype):
    t, d = a.shape
    n = w.shape[-1]
    tm = ROW_TILE
    tn = next(c for c in (1024, 768, 512, 256, 128) if n % c == 0)
    return pl.pallas_call(
        _matmul_kernel,
        out_shape=jax.ShapeDtypeStruct((t, n), out_dtype),
        grid=(t // tm, n // tn),
        in_specs=[pl.BlockSpec((tm, d), lambda i, j: (i, 0)),
                  pl.BlockSpec((None, d, tn), lambda i, j: (slot, 0, j))],
        out_specs=pl.BlockSpec((tm, tn), lambda i, j: (i, j)),
        compiler_params=_cparams(2),
        name="matmul",
    )(a, w)


def _matmul_res_kernel(a_ref, w_ref, x_ref, o_ref, wb_ref):
    @pl.when(pl.program_id(0) == 0)
    def _():
        wb_ref[...] = w_ref[...].astype(BF16)

    o_ref[...] = x_ref[...] + jnp.dot(a_ref[...], wb_ref[...], preferred_element_type=F32)


def _matmul_res(a, w, slot, x):
    t, k = a.shape
    d = w.shape[-1]
    tm = ROW_TILE
    return pl.pallas_call(
        _matmul_res_kernel,
        out_shape=jax.ShapeDtypeStruct((t, d), F32),
        grid=(t // tm,),
        in_specs=[pl.BlockSpec((tm, k), lambda i: (i, 0)),
                  pl.BlockSpec((None, k, d), lambda i: (slot, 0, 0)),
                  pl.BlockSpec((tm, d), lambda i: (i, 0))],
        out_specs=pl.BlockSpec((tm, d), lambda i: (i, 0)),
        scratch_shapes=[pltpu.VMEM((k, d), BF16)],
        compiler_params=_cparams(1),
        name="matmul_res",
    )(a, w, x)


def _silu_mul(g, u):
    return g * (1.0 / (1.0 + jnp.exp(-g))) * u


def _swiglu_accumulate(h_ref, wg_ref, wu_ref, wd_ref, acc_ref, rows=None):
    rows = h_ref.shape[0] if rows is None else rows
    h = h_ref[pl.ds(0, rows), :]
    gate = jnp.dot(h, wg_ref[...].astype(BF16), preferred_element_type=F32)
    up = jnp.dot(h, wu_ref[...].astype(BF16), preferred_element_type=F32)
    a = _silu_mul(gate, up).astype(BF16)
    acc_ref[pl.ds(0, rows), :] += jnp.dot(a, wd_ref[...].astype(BF16), preferred_element_type=F32)


def _ffn_kernel(x_ref, g_ref, gn_ref, wg_ref, wu_ref, wd_ref, o_ref, *rest, emit_next):
    if emit_next:
        nxt_ref, hn_ref, acc_ref = rest
    else:
        hn_ref, acc_ref = rest
    f = pl.program_id(1)

    @pl.when(f == 0)
    def _():
        hn_ref[...] = _rms_rows(x_ref[...], g_ref[...]).astype(BF16)
        acc_ref[...] = jnp.zeros_like(acc_ref)

    _swiglu_accumulate(hn_ref, wg_ref, wu_ref, wd_ref, acc_ref)

    @pl.when(f == pl.num_programs(1) - 1)
    def _():
        out = x_ref[...] + acc_ref[...]
        o_ref[...] = out
        if emit_next:
            nxt_ref[...] = _rms_rows(out, gn_ref[...]).astype(BF16)


def _ffn(x, g, w_gate_up, w_down, slot, g_next=None):
    t, d = x.shape
    ff = w_down.shape[-2]
    tm, tf = ROW_TILE, FF_TILE
    nf = ff // tf
    emit_next = g_next is not None
    gn = (g_next if emit_next else g).reshape(1, d)
    row_spec = pl.BlockSpec((tm, d), lambda i, f: (i, 0))
    vec_spec = pl.BlockSpec((1, d), lambda i, f: (0, 0))
    out_shape = jax.ShapeDtypeStruct((t, d), F32)
    return pl.pallas_call(
        functools.partial(_ffn_kernel, emit_next=emit_next),
        out_shape=(out_shape, jax.ShapeDtypeStruct((t, d), BF16)) if emit_next else out_shape,
        grid=(t // tm, nf),
        in_specs=[row_spec, vec_spec, vec_spec,
                  pl.BlockSpec((None, d, tf), lambda i, f: (slot, 0, f)),
                  pl.BlockSpec((None, d, tf), lambda i, f: (slot, 0, nf + f)),
                  pl.BlockSpec((None, tf, d), lambda i, f: (slot, f, 0))],
        out_specs=(row_spec, row_spec) if emit_next else row_spec,
        scratch_shapes=[pltpu.VMEM((tm, d), BF16), pltpu.VMEM((tm, d), F32)],
        compiler_params=_cparams(2),
        name="ffn",
    )(x, g.reshape(1, d), gn, w_gate_up, w_gate_up, w_down)


POOL_PAD = 16


def _pool_kernel(u_ref, wg_ref, sc_ref, o_ref, pad_ref):
    s, c = u_ref.shape
    grp = pl.program_id(1)
    pad_ref[pl.ds(0, POOL_PAD), :] = jnp.zeros((POOL_PAD, c), F32)
    pad_ref[pl.ds(POOL_PAD + s, POOL_PAD), :] = jnp.zeros((POOL_PAD, c), F32)
    pad_ref[pl.ds(POOL_PAD, s), :] = u_ref[...]
    wb = wg_ref[...].astype(BF16)
    chunk = 256

    for gi, w in enumerate(POOL_WINDOWS):
        @pl.when(grp == gi)
        def _(w=w):
            for c0 in range(0, s, chunk):
                acc = None
                for dlt in range(-(w // 2), w - w // 2):
                    piece = pad_ref[pl.ds(POOL_PAD + c0 + dlt, chunk), :]
                    acc = piece if acc is None else acc + piece
                pos = c0 + lax.broadcasted_iota(I32, (chunk, 1), 0)
                lo = jnp.maximum(pos - w // 2, 0)
                hi = jnp.minimum(pos + (w - w // 2), s)
                cnt = (hi - lo).astype(F32)
                mixed = acc / cnt - u_ref[pl.ds(c0, chunk), :]
                y = jnp.dot(mixed.astype(BF16), wb, preferred_element_type=F32)
                o_ref[pl.ds(c0, chunk), :] = (y * sc_ref[...]).astype(o_ref.dtype)


def _pool_mix(u, w_group, scale, slot, batch, seq):
    d = u.shape[1]
    ng = len(POOL_WINDOWS)
    c = d // ng
    u3 = u.reshape(batch, seq, d)
    out = pl.pallas_call(
        _pool_kernel,
        out_shape=jax.ShapeDtypeStruct((batch, seq, d), BF16),
        grid=(batch, ng),
        in_specs=[pl.BlockSpec((None, seq, c), lambda b, g: (b, 0, g)),
                  pl.BlockSpec((None, None, c, c), lambda b, g: (slot, g, 0, 0)),
                  pl.BlockSpec((1, c), lambda b, g: (slot, g))],
        out_specs=pl.BlockSpec((None, seq, c), lambda b, g: (b, 0, g)),
        scratch_shapes=[pltpu.VMEM((seq + 2 * POOL_PAD, c), F32)],
        compiler_params=_cparams(2),
        name="pool_mix",
    )(u3, w_group, scale)
    return out.reshape(batch * seq, d)


def _rel_bucket_np(rel):
    half = N_BUCKETS // 2
    max_exact = half // 2
    n = np.abs(rel)
    ratio = np.log(np.maximum(n, 1).astype(np.float32) / np.float32(max_exact))
    big = max_exact + (ratio / np.float32(math.log(MAX_DISTANCE / max_exact))
                       * np.float32(half - max_exact)).astype(np.int32)
    big = np.minimum(big, half - 1)
    return np.where(rel > 0, half, 0) + np.where(n < max_exact, n, big)


def _seg_table(rel_bias, rel, valid):
    bucket = jnp.asarray(_rel_bucket_np(rel).astype(np.int32))
    vals = jnp.take(rel_bias.astype(F32), bucket, axis=0)
    vals = jnp.where(jnp.asarray(valid)[..., None], vals, MASK_VALUE)
    return jnp.transpose(vals, (2, 0, 1))


def _toeplitz(seg_row, rows, cols):
    w = seg_row.shape[1]
    full = jnp.broadcast_to(seg_row, (rows, w))
    rolled = pltpu.roll(full, w - (rows - 1), 1, stride=1, stride_axis=0)
    return rolled[:, :cols]


def _band_variants(hw, n_tiles):
    if n_tiles == 1:
        return (0,)
    return (0, -hw, -2 * hw)


def _band_segs(rel_bias, tq, kw, half, dil, variants, width):
    c = np.arange(width)
    rel = np.stack([r0 + c - (tq - 1) for r0 in variants])
    valid = (np.abs(rel) <= half) & (c[None, :] < tq + kw - 1)
    return _seg_table(rel_bias, rel * dil, valid)


def _band_window(qi, nq, tq, hw, kw, ln):
    if nq == 1:
        return 0, 0
    var = jnp.where(qi == 0, 0, jnp.where(qi == nq - 1, 2, 1))
    return var, jnp.clip(qi * tq - hw, 0, ln - kw)


DilCfg = collections.namedtuple("DilCfg", "dil ln nq hw kw n_var")


def _dil_proj_kernel(x_ref, g_ref, w_ref, o_ref, slab_ref, hn_ref, *, dils):
    grp, c = pl.program_id(1), pl.program_id(2)
    seq, d = x_ref.shape
    n_slabs = d // LANES
    chunk = 256

    @pl.when((grp == 0) & (c == 0))
    def _():
        for c0 in range(0, seq, chunk):
            hn = _rms_rows(x_ref[pl.ds(c0, chunk), :], g_ref[...])
            for s in range(n_slabs):
                slab_ref[s, pl.ds(c0, chunk), :] = hn[:, s * LANES:(s + 1) * LANES]

    for gi, dil in enumerate(dils):
        @pl.when((grp == gi) & (c == 0))
        def _(dil=dil):
            ln = seq // dil
            for r in range(dil):
                for c0 in range(0, ln, chunk):
                    n = min(chunk, ln)
                    rows = [slab_ref[s, pl.ds(r + c0 * dil, n, stride=dil), :] for s in range(n_slabs)]
                    hn_ref[pl.ds(r * ln + c0, n), :] = jnp.concatenate(rows, axis=1).astype(BF16)

    res = jnp.dot(hn_ref[...], w_ref[...].astype(BF16), preferred_element_type=F32)
    res = res * jnp.where(c < pl.num_programs(2) // 3, Q_SCALE, 1.0)
    for s in range(o_ref.shape[0]):
        o_ref[s] = res[:, s * LANES:(s + 1) * LANES].astype(o_ref.dtype)


def _dil_proj(x3, g, w_in, slot, dils):
    batch, seq, d = x3.shape
    n = w_in.shape[-1]
    tn = FF_TILE
    per_group = n // len(dils) // tn
    spt = tn // LANES
    return pl.pallas_call(
        functools.partial(_dil_proj_kernel, dils=dils),
        out_shape=jax.ShapeDtypeStruct((batch, n // LANES, seq, LANES), BF16),
        grid=(batch, len(dils), per_group),
        in_specs=[pl.BlockSpec((None, seq, d), lambda b, g, c: (b, 0, 0)),
                  pl.BlockSpec((1, d), lambda b, g, c: (0, 0)),
                  pl.BlockSpec((None, d, tn), lambda b, g, c: (slot, 0, g * per_group + c))],
        out_specs=pl.BlockSpec((None, spt, seq, LANES), lambda b, g, c: (b, g * per_group + c, 0, 0)),
        scratch_shapes=[pltpu.VMEM((d // LANES, seq, LANES), F32), pltpu.VMEM((seq, d), BF16)],
        compiler_params=_cparams(3),
        name="dil_proj",
    )(x3, g.reshape(1, d), w_in)


def _dil_unit(cfg, unit, q4, k_ref, v_ref, bias_ref, acc_ref, m_ref, l_ref, *, tq, first, last):
    dil, ln, nq, kw = cfg.dil, cfg.ln, cfg.nq, cfg.kw
    n_pairs = q4.shape[0]
    if nq == 1:
        r, qi = unit, 0
    elif dil == 1:
        r, qi = 0, unit
    else:
        r, qi = unit // nq, unit % nq
    var, ks = _band_window(qi, nq, tq, cfg.hw, kw, ln)
    krow = pl.multiple_of(r * ln + ks, cfg.hw)
    if dil == 1:
        rows = pl.ds(pl.multiple_of(qi * tq, tq), tq)
    else:
        rows = pl.ds(qi * tq * dil + r, tq, stride=dil)

    even = lax.broadcasted_iota(I32, (n_pairs, tq, LANES), 2) < HEAD_DIM
    zero = jnp.zeros_like(q4)
    q8 = jnp.concatenate([jnp.where(even, q4, zero), jnp.where(even, zero, q4)], axis=0)
    k4 = k_ref[:, pl.ds(krow, kw), :]
    v4 = v_ref[:, pl.ds(krow, kw), :]
    k8 = jnp.concatenate([k4, k4], axis=0)
    va = jnp.concatenate([v4, jnp.ones_like(v4)], axis=2)
    v8 = jnp.concatenate([va, va], axis=0)
    s = lax.dot_general(q8, k8, (((2,), (2,)), ((0,), (0,))), preferred_element_type=F32)
    s = s + bias_ref[var]
    m8 = jnp.max(s, axis=-1, keepdims=True)
    p = jnp.exp(s - m8).astype(BF16)
    ov = lax.dot_general(p, v8, (((2,), (1,)), ((0,), (0,))), preferred_element_type=F32)
    o_c = jnp.where(even, ov[:n_pairs, :, :LANES], ov[n_pairs:, :, :LANES])
    l_c = jnp.where(even, ov[:n_pairs, :, LANES:], ov[n_pairs:, :, LANES:])
    m_c = jnp.where(even, m8[:n_pairs], m8[n_pairs:])
    if first:
        m_n, l_n, acc_n = m_c, l_c, o_c
    else:
        m_o = jnp.stack([m_ref[pp, rows, :] for pp in range(n_pairs)])
        l_o = jnp.stack([l_ref[pp, rows, :] for pp in range(n_pairs)])
        acc_o = jnp.stack([acc_ref[pp, rows, :] for pp in range(n_pairs)])
        m_n = jnp.maximum(m_o, m_c)
        a_o = jnp.exp(m_o - m_n)
        a_c = jnp.exp(m_c - m_n)
        l_n = a_o * l_o + a_c * l_c
        acc_n = a_o * acc_o + a_c * o_c
    if last:
        acc_n = acc_n / l_n
    for pp in range(n_pairs):
        acc_ref[pp, rows, :] = acc_n[pp]
        if not last:
            m_ref[pp, rows, :] = m_n[pp]
            l_ref[pp, rows, :] = l_n[pp]


def _dil_attn_kernel(*refs, cfgs, tq):
    ng = len(cfgs)
    seg_refs = refs[:ng]
    q_ref, k_ref, v_ref, o_ref = refs[ng:ng + 4]
    bias_refs = refs[ng + 4:2 * ng + 4]
    acc_ref, m_ref, l_ref = refs[2 * ng + 4:]
    hf, b, grp, step = (pl.program_id(i) for i in range(4))
    n_pairs = q_ref.shape[0]

    @pl.when((b == 0) & (grp == 0) & (step == 0))
    def _():
        for gi, cfg in enumerate(cfgs):
            for v in range(cfg.n_var):
                for odd in range(2):
                    for pp in range(n_pairs):
                        head = hf * 2 * n_pairs + 2 * pp + odd
                        row = seg_refs[gi][head, pl.ds(v, 1), :]
                        bias_refs[gi][v, odd * n_pairs + pp] = _toeplitz(row, tq, cfg.kw)

    for gi, cfg in enumerate(cfgs):
        @pl.when(grp == gi)
        def _(gi=gi, cfg=cfg):
            for uu in range(DIL_UNITS):
                _dil_unit(cfg, step * DIL_UNITS + uu, q_ref[:, pl.ds(uu * tq, tq), :], k_ref, v_ref,
                          bias_refs[gi], acc_ref, m_ref, l_ref, tq=tq, first=gi == 0, last=gi == ng - 1)

    @pl.when((grp == ng - 1) & (step == pl.num_programs(3) - 1))
    def _():
        slabs = [acc_ref[s] for s in range(n_pairs)]
        o_ref[...] = jnp.concatenate(slabs, axis=1).astype(o_ref.dtype)


def _dilated_attention(qkv, rel_bias, batch, seq):
    ng = len(DIL_PAIRS)
    d = qkv.shape[1] * LANES // (3 * ng)
    hd = d // DIL_HEAD_SPLIT
    n_pairs = hd // LANES
    tq = ATT_TQ
    cfgs, segs = [], []
    for win, dil in DIL_PAIRS:
        half = win // (2 * dil)
        ln = seq // dil
        nq = ln // tq
        kw = min(tq + 2 * half, ln)
        variants = _band_variants(half, nq)
        cfgs.append(DilCfg(dil, ln, nq, half, kw, len(variants)))
        segs.append(_band_segs(rel_bias, tq, kw, half, dil, variants, _next_pow2(tq + kw - 1)))
    steps = seq // tq // DIL_UNITS
    cb = DIL_HEAD_SPLIT

    in_specs = [pl.BlockSpec(sg.shape, lambda hf, b, g, s: (0, 0, 0)) for sg in segs]
    in_specs += [pl.BlockSpec((None, n_pairs, DIL_UNITS * tq, LANES),
                              lambda hf, b, g, s: (b, (g * 3) * cb + hf, s, 0)),
                 pl.BlockSpec((None, n_pairs, seq, LANES),
                              lambda hf, b, g, s: (b, (g * 3 + 1) * cb + hf, 0, 0)),
                 pl.BlockSpec((None, n_pairs, seq, LANES),
                              lambda hf, b, g, s: (b, (g * 3 + 2) * cb + hf, 0, 0))]
    scratch = [pltpu.VMEM((c.n_var, 2 * n_pairs, tq, c.kw), F32) for c in cfgs]
    scratch += [pltpu.VMEM((n_pairs, seq, LANES), F32) for _ in range(3)]
    out = pl.pallas_call(
        functools.partial(_dil_attn_kernel, cfgs=tuple(cfgs), tq=tq),
        out_shape=jax.ShapeDtypeStruct((batch, seq, d), BF16),
        grid=(DIL_HEAD_SPLIT, batch, ng, steps),
        in_specs=in_specs,
        out_specs=pl.BlockSpec((None, seq, hd), lambda hf, b, g, s: (b, 0, hf)),
        scratch_shapes=scratch,
        compiler_params=_cparams(4),
        name="dil_attn",
    )(*segs, qkv, qkv, qkv)
    return out.reshape(batch * seq, d)


def _gqa_kernel(sink_ref, seg_ref, q_ref, k_ref, v_ref, o_ref, bias_ref, *, tq, kw, hw, grp):
    b, qi = pl.program_id(0), pl.program_id(1)
    nq = pl.num_programs(1)
    seq_len = k_ref.shape[0]
    n_kv = k_ref.shape[1] // HEAD_DIM

    @pl.when((b == 0) & (qi == 0))
    def _():
        for v in range(3):
            for kh in range(n_kv):
                for gq in range(grp):
                    bias_ref[v, kh, pl.ds(gq * tq, tq), :] = _toeplitz(
                        seg_ref[kh * grp + gq, pl.ds(v, 1), :], tq, kw)

    var = jnp.where(qi == 0, 0, jnp.where(qi == nq - 1, 2, 1))
    ks = pl.multiple_of(jnp.clip(qi * tq - hw, 0, seq_len - kw), hw)
    row = lax.broadcasted_iota(I32, (grp * tq, 1), 0)
    outs = [None] * (n_kv * grp)
    for kh in range(n_kv):
        cs = slice(kh * HEAD_DIM, (kh + 1) * HEAD_DIM)
        qs = jnp.concatenate(
            [q_ref[:, (kh * grp + gq) * HEAD_DIM:(kh * grp + gq + 1) * HEAD_DIM] for gq in range(grp)],
            axis=0) * Q_SCALE
        kk = k_ref[pl.ds(ks, kw), cs]
        vv = v_ref[pl.ds(ks, kw), cs]
        s = lax.dot_general(qs, kk, (((1,), (1,)), ((), ())), preferred_element_type=F32)
        s = s + bias_ref[var, kh]
        sk = jnp.zeros((grp * tq, 1), F32)
        for gq in range(grp):
            sk = jnp.where((row >= gq * tq) & (row < (gq + 1) * tq), sink_ref[kh * grp + gq], sk)
        m = jnp.maximum(jnp.max(s, axis=-1, keepdims=True), sk)
        e = jnp.exp(s - m).astype(BF16)
        va = jnp.concatenate([vv, jnp.ones_like(vv)], axis=1)
        ov = jnp.dot(e, va, preferred_element_type=F32)
        o = ov[:, :HEAD_DIM] / (ov[:, HEAD_DIM:] + jnp.exp(sk - m))
        for gq in range(grp):
            outs[kh * grp + gq] = o[gq * tq:(gq + 1) * tq, :]
    o_ref[...] = jnp.concatenate(outs, axis=1).astype(o_ref.dtype)


def _gqa_attention(qkv, sink, rel_bias, batch, seq):
    t, ncol = qkv.shape
    n_q = sink.shape[0]
    d = n_q * HEAD_DIM
    kvw = GQA_KV_HEADS * HEAD_DIM
    grp = n_q // GQA_KV_HEADS
    tq = hw = GQA_WINDOW
    kw = 3 * GQA_WINDOW
    nq = seq // tq
    variants = _band_variants(hw, nq)
    width = _next_pow2(tq + kw - 1)
    seg = _band_segs(rel_bias, tq, kw, GQA_WINDOW, 1, variants, width)
    qkv_v = qkv.reshape(batch, seq, ncol)
    out = pl.pallas_call(
        functools.partial(_gqa_kernel, tq=tq, kw=kw, hw=hw, grp=grp),
        out_shape=jax.ShapeDtypeStruct((batch, seq, d), BF16),
        grid=(batch, nq),
        in_specs=[pl.BlockSpec(memory_space=pltpu.SMEM),
                  pl.BlockSpec((n_q, 3, width), lambda b, qi: (0, 0, 0)),
                  pl.BlockSpec((None, tq, d), lambda b, qi: (b, qi, 0)),
                  pl.BlockSpec((None, seq, kvw), lambda b, qi: (b, 0, d // kvw)),
                  pl.BlockSpec((None, seq, kvw), lambda b, qi: (b, 0, d // kvw + 1))],
        out_specs=pl.BlockSpec((None, tq, d), lambda b, qi: (b, qi, 0)),
        scratch_shapes=[pltpu.VMEM((3, GQA_KV_HEADS, grp * tq, kw), F32)],
        compiler_params=_cparams(2),
        name="gqa_attn",
    )(sink.astype(F32), seg, qkv_v, qkv_v, qkv_v)
    return out.reshape(t, d)


def _diff_kernel(seg_ref, lam_ref, sub_ref, q_ref, k_ref, v_ref, o_ref, bias_ref, *, tq, lam_init):
    h, qi, b = pl.program_id(0), pl.program_id(1), pl.program_id(2)
    seq_len = k_ref.shape[0]
    nk = seq_len // tq

    @pl.when(b == 0)
    def _():
        for j in range(2):
            for ki in range(nk):
                row = seg_ref[h * 2 + j, pl.ds(ki - qi + nk - 1, 1), :]
                bias_ref[j, :, pl.ds(ki * tq, tq)] = _toeplitz(row, tq, tq)

    lv = lam_ref[...]
    s01 = jnp.sum(lv[0:1, :] * lv[1:2, :], axis=-1, keepdims=True)
    s23 = jnp.sum(lv[2:3, :] * lv[3:4, :], axis=-1, keepdims=True)
    lam = jnp.exp(s01) - jnp.exp(s23) + lam_init

    q = q_ref[...] * Q_SCALE
    k = k_ref[...]
    v = v_ref[...]
    va = jnp.concatenate([v, jnp.ones_like(v)], axis=1)
    map0 = lax.broadcasted_iota(I32, q.shape, 1) < HEAD_DIM
    zero = jnp.zeros_like(q)
    qm = (jnp.where(map0, q, zero), jnp.where(map0, zero, q))
    rc = tq // DIFF_ROW_SPLIT
    for c0 in range(0, tq, rc):
        outs = []
        for j in range(2):
            s = lax.dot_general(qm[j][c0:c0 + rc], k, (((1,), (1,)), ((), ())),
                                preferred_element_type=F32)
            s = s + bias_ref[j, pl.ds(c0, rc), :]
            m = jnp.max(s, axis=-1, keepdims=True)
            p = jnp.exp(s - m).astype(BF16)
            ov = jnp.dot(p, va, preferred_element_type=F32)
            outs.append(ov[:, :2 * HEAD_DIM] / ov[:, 2 * HEAD_DIM:])
        o = outs[0] - lam * outs[1]
        o = _rms_rows(o, sub_ref[...]) * (1.0 - lam_init)
        o_ref[pl.ds(c0, rc), :] = o.astype(o_ref.dtype)


def _diff_attention(qkv, lam_vecs, subln, slot, rel_bias, batch, seq):
    t, ncol = qkv.shape
    d = ncol // 3
    hd2 = 2 * HEAD_DIM
    n_heads = d // hd2
    tq = DIFF_TQ
    nk = seq // tq
    width = _next_pow2(2 * tq - 1)
    c = np.arange(width)
    rel = np.stack([(dl - (nk - 1)) * tq + c - (tq - 1) for dl in range(2 * nk - 1)])
    seg = _seg_table(rel_bias, rel, np.ones_like(rel, dtype=bool))
    lam_init = 0.8 - 0.6 * math.exp(-0.3 * DIFF_LAYER)
    qkv_v = qkv.reshape(batch, seq, ncol)
    out = pl.pallas_call(
        functools.partial(_diff_kernel, tq=tq, lam_init=lam_init),
        out_shape=jax.ShapeDtypeStruct((batch, seq, d), BF16),
        grid=(n_heads, seq // tq, batch),
        in_specs=[pl.BlockSpec(seg.shape, lambda h, qi, b: (0, 0, 0)),
                  pl.BlockSpec((None,) + lam_vecs.shape[1:], lambda h, qi, b: (slot, 0, 0)),
                  pl.BlockSpec((1, hd2), lambda h, qi, b: (slot, 0)),
                  pl.BlockSpec((None, tq, hd2), lambda h, qi, b: (b, qi, h)),
                  pl.BlockSpec((None, seq, hd2), lambda h, qi, b: (b, 0, n_heads + h)),
                  pl.BlockSpec((None, seq, hd2), lambda h, qi, b: (b, 0, 2 * n_heads + h))],
        out_specs=pl.BlockSpec((None, tq, hd2), lambda h, qi, b: (b, qi, h)),
        scratch_shapes=[pltpu.VMEM((2, tq, seq), F32)],
        compiler_params=_cparams(3),
        name="diff_attn",
    )(seg, lam_vecs, subln, qkv_v, qkv_v, qkv_v)
    return out.reshape(t, d)


SEG_ALIGN = 16
SEG_SIZES = (512, 256, 128, 64, 32, 16)
SEG_STAGE = 2 * SEG_TILE + N_EXPERTS * SEG_ALIGN


def _route_kernel(a_ref, wo_ref, x_ref, g_ref, r_ref, xn_ref, hn_ref, mi_ref, mf_ref, cnt_ref,
                  tri_ref, wb_ref):
    tm, d = x_ref.shape

    @pl.when(pl.program_id(0) == 0)
    def _():
        rr = lax.broadcasted_iota(I32, (tm, tm), 0)
        cc = lax.broadcasted_iota(I32, (tm, tm), 1)
        tri_ref[...] = (cc < rr).astype(BF16)
        wb_ref[...] = wo_ref[...].astype(BF16)

    r = r_ref[...]
    r_hi = r.astype(BF16)
    r_lo = (r - r_hi.astype(F32)).astype(BF16)
    rows = tm // ROUTE_ROW_SPLIT
    lane = lax.broadcasted_iota(I32, (rows, LANES), 1)

    picks = []
    for r0 in range(0, tm, rows):
        rs = pl.ds(r0, rows)
        xn = x_ref[rs, :] + jnp.dot(a_ref[rs, :], wb_ref[...], preferred_element_type=F32)
        xn_ref[rs, :] = xn
        hn = _rms_rows(xn, g_ref[...])
        h_hi = hn.astype(BF16)
        hn_ref[rs, :] = h_hi
        h_lo = (hn - h_hi.astype(F32)).astype(BF16)
        logits = (jnp.dot(h_hi, r_hi, preferred_element_type=F32)
                  + jnp.dot(h_hi, r_lo, preferred_element_type=F32)
                  + jnp.dot(h_lo, r_hi, preferred_element_type=F32))
        logits = jnp.where(lane < N_EXPERTS, logits, -jnp.inf)
        v1 = jnp.max(logits, axis=-1, keepdims=True)
        i1 = jnp.min(jnp.where(logits == v1, lane, LANES), axis=-1, keepdims=True)
        oh1 = lane == i1
        rest = jnp.where(oh1, -jnp.inf, logits)
        v2 = jnp.max(rest, axis=-1, keepdims=True)
        i2 = jnp.min(jnp.where(rest == v2, lane, LANES), axis=-1, keepdims=True)
        oh2 = lane == i2
        e2 = jnp.exp(v2 - v1)
        mf_ref[rs, :] = jnp.where(lane == 0, 1.0 / (1.0 + e2), jnp.where(lane == 1, e2 / (1.0 + e2), 0.0))
        picks.append((rs, i1, i2, oh1, oh2))

    sel = jnp.concatenate([(oh1 | oh2).astype(BF16) for _, _, _, oh1, oh2 in picks], axis=0)
    for rs, i1, i2, oh1, oh2 in picks:
        before = jnp.dot(tri_ref[rs, :], sel, preferred_element_type=F32)
        rank1 = jnp.sum(jnp.where(oh1, before, 0.0), axis=-1, keepdims=True).astype(I32)
        rank2 = jnp.sum(jnp.where(oh2, before, 0.0), axis=-1, keepdims=True).astype(I32)
        mi_ref[rs, :] = jnp.where(lane == 0, i1, jnp.where(lane == 1, i2,
                                  jnp.where(lane == 2, rank1, jnp.where(lane == 3, rank2, 0))))
    counts = jnp.sum(sel.astype(F32), axis=0, keepdims=True).astype(I32)
    cnt_ref[...] = jnp.broadcast_to(counts, cnt_ref.shape)


def _route(a, w_out, out_slot, x, g, router):
    t, d = x.shape
    k = a.shape[1]
    tm = SEG_TILE
    r_pad = jnp.zeros((d, LANES), F32).at[:, :N_EXPERTS].set(router.astype(F32))
    return pl.pallas_call(
        _route_kernel,
        out_shape=(jax.ShapeDtypeStruct((t, d), F32),
                   jax.ShapeDtypeStruct((t, d), BF16),
                   jax.ShapeDtypeStruct((t, LANES), I32),
                   jax.ShapeDtypeStruct((t, LANES), F32),
                   jax.ShapeDtypeStruct((t // tm, SUBLANES, LANES), I32)),
        grid=(t // tm,),
        in_specs=[pl.BlockSpec((tm, k), lambda i: (i, 0)),
                  pl.BlockSpec((None, k, d), lambda i: (out_slot, 0, 0)),
                  pl.BlockSpec((tm, d), lambda i: (i, 0)),
                  pl.BlockSpec((1, d), lambda i: (0, 0)),
                  pl.BlockSpec((d, LANES), lambda i: (0, 0))],
        out_specs=(pl.BlockSpec((tm, d), lambda i: (i, 0)),
                   pl.BlockSpec((tm, d), lambda i: (i, 0)),
                   pl.BlockSpec((tm, LANES), lambda i: (i, 0)),
                   pl.BlockSpec((tm, LANES), lambda i: (i, 0)),
                   pl.BlockSpec((None, SUBLANES, LANES), lambda i: (i, 0, 0))),
        scratch_shapes=[pltpu.VMEM((tm, tm), BF16), pltpu.VMEM((k, d), BF16)],
        compiler_params=_cparams(1),
        name="moe_route",
    )(a, w_out, x, g.reshape(1, d), r_pad)


def _slot_rows(mi, loc_ref, base):
    e1, e2, d1, d2 = mi[0], mi[1], mi[2], mi[3]
    for e in range(N_EXPERTS):
        off = loc_ref[base + e]
        d1 = d1 + jnp.where(e1 == e, off, 0)
        d2 = d2 + jnp.where(e2 == e, off, 0)
    return d1, d2


def _segment_copies(base, seg_ref, n16_ref, loc_ref, hbm_ref, vmem_ref, sem, *, to_hbm, wait):
    for e in range(N_EXPERTS):
        n16 = n16_ref[base + e]
        hbm0 = seg_ref[base + e]
        vmem0 = loc_ref[base + e]
        off = 0
        for size in SEG_SIZES:
            @pl.when((n16 & size) != 0)
            def _(off=off, size=size):
                h = hbm_ref.at[pl.ds(pl.multiple_of(hbm0 + off, SEG_ALIGN), size)]
                v = vmem_ref.at[pl.ds(pl.multiple_of(vmem0 + off, SEG_ALIGN), size)]
                cp = pltpu.make_async_copy(v, h, sem) if to_hbm else pltpu.make_async_copy(h, v, sem)
                if wait:
                    cp.wait()
                else:
                    cp.start()
            off = off + (n16 & size)


def _scatter_kernel(seg_ref, n16_ref, loc_ref, tail_ref, hn_ref, mi_ref, xs_ref, stage_ref, zero_ref, sem):
    i = pl.program_id(0)
    tm = hn_ref.shape[0]
    rows = stage_ref.shape[1]
    tg = zero_ref.shape[0]
    slot = lax.rem(i, 2)

    @pl.when(i == 0)
    def _():
        zero_ref[...] = jnp.zeros_like(zero_ref)
        for e in range(N_EXPERTS):
            tail = pl.multiple_of(tail_ref[e], SEG_ALIGN)
            pltpu.make_async_copy(zero_ref, xs_ref.at[pl.ds(tail, tg)], sem.at[2]).start()
        for e in range(N_EXPERTS):
            tail = pl.multiple_of(tail_ref[e], SEG_ALIGN)
            pltpu.make_async_copy(zero_ref, xs_ref.at[pl.ds(tail, tg)], sem.at[2]).wait()

    base = i * N_EXPERTS
    mi_t = mi_ref[...].T
    d1, d2 = _slot_rows([mi_t[k:k + 1, :] for k in range(4)], loc_ref, base)
    row = lax.broadcasted_iota(I32, (rows, tm), 0)
    onehot = ((row == d1) | (row == d2)).astype(BF16)
    stage_ref[slot] = jnp.dot(onehot, hn_ref[...], preferred_element_type=F32).astype(BF16)
    tabs = (seg_ref, n16_ref, loc_ref, xs_ref)
    _segment_copies(base, *tabs, stage_ref.at[slot], sem.at[slot], to_hbm=True, wait=False)

    @pl.when(i > 0)
    def _():
        _segment_copies(base - N_EXPERTS, *tabs, stage_ref.at[1 - slot], sem.at[1 - slot],
                        to_hbm=True, wait=True)

    @pl.when(i == pl.num_programs(0) - 1)
    def _():
        _segment_copies(base, *tabs, stage_ref.at[slot], sem.at[slot], to_hbm=True, wait=True)


def _scatter(hn, mi, seg, n16, loc, tails, n_rows):
    t, d = hn.shape
    tm = SEG_TILE
    return pl.pallas_call(
        _scatter_kernel,
        out_shape=jax.ShapeDtypeStruct((n_rows, d), BF16),
        grid_spec=pltpu.PrefetchScalarGridSpec(
            num_scalar_prefetch=4,
            grid=(t // tm,),
            in_specs=[pl.BlockSpec((tm, d), lambda i, *_: (i, 0)),
                      pl.BlockSpec((tm, LANES), lambda i, *_: (i, 0))],
            out_specs=pl.BlockSpec(memory_space=pl.ANY),
            scratch_shapes=[pltpu.VMEM((2, SEG_STAGE, d), BF16), pltpu.VMEM((MOE_TILE, d), BF16),
                            pltpu.SemaphoreType.DMA((3,))]),
        compiler_params=pltpu.CompilerParams(dimension_semantics=("arbitrary",),
                                             vmem_limit_bytes=VMEM_LIMIT, has_side_effects=True),
        name="moe_scatter",
    )(seg, n16, loc, tails, hn, mi)


def _seg_expert_kernel(te_ref, tb_ref, nr_ref, nv_ref, xs_ref, wg_ref, wu_ref, wd_ref, ys_ref, acc_ref):
    i, f = pl.program_id(0), pl.program_id(1)
    step = xs_ref.shape[0] // MOE_HEIGHTS

    @pl.when(i < nv_ref[0])
    def _():
        @pl.when(f == 0)
        def _():
            acc_ref[...] = jnp.zeros_like(acc_ref)

        nrow = nr_ref[i]
        for q in range(1, MOE_HEIGHTS + 1):
            lo_ok = nrow > (q - 1) * step
            in_q = lo_ok if q == MOE_HEIGHTS else lo_ok & (nrow <= q * step)

            @pl.when(in_q)
            def _(q=q):
                _swiglu_accumulate(xs_ref, wg_ref, wu_ref, wd_ref, acc_ref, rows=q * step)

        @pl.when(f == pl.num_programs(1) - 1)
        def _():
            ys_ref[...] = acc_ref[...].astype(ys_ref.dtype)


def _seg_experts(xs, w_gate_up, w_down, slot, tile_expert, tile_block, tile_rows, n_valid, n_tiles):
    n_rows, d = xs.shape
    ff = w_down.shape[-2]
    tg, tf = MOE_TILE, FF_TILE
    nf = ff // tf

    def fidx(i, f, nv):
        return jnp.where(i < nv[0], f, nf - 1)

    return pl.pallas_call(
        _seg_expert_kernel,
        out_shape=jax.ShapeDtypeStruct((n_rows, d), BF16),
        grid_spec=pltpu.PrefetchScalarGridSpec(
            num_scalar_prefetch=4,
            grid=(n_tiles, nf),
            in_specs=[pl.BlockSpec((tg, d), lambda i, f, te, tb, nr, nv: (tb[i], 0)),
                      pl.BlockSpec((None, None, d, tf),
                                   lambda i, f, te, tb, nr, nv: (slot, te[i], 0, fidx(i, f, nv))),
                      pl.BlockSpec((None, None, d, tf),
                                   lambda i, f, te, tb, nr, nv: (slot, te[i], 0, nf + fidx(i, f, nv))),
                      pl.BlockSpec((None, None, tf, d),
                                   lambda i, f, te, tb, nr, nv: (slot, te[i], fidx(i, f, nv), 0))],
            out_specs=pl.BlockSpec((tg, d), lambda i, f, te, tb, nr, nv: (tb[i], 0)),
            scratch_shapes=[pltpu.VMEM((tg, d), F32)]),
        compiler_params=_cparams(2),
        name="moe_experts",
    )(tile_expert, tile_block, tile_rows, n_valid, xs, w_gate_up, w_gate_up, w_down)


def _gather_kernel(seg_ref, n16_ref, loc_ref, x_ref, mi_ref, mf_ref, gf_ref, ys_ref, o_ref,
                   *rest, norm):
    if norm == "next":
        nxt_ref, ybuf_ref, sem = rest
    else:
        ybuf_ref, sem = rest
    i = pl.program_id(0)
    tm = x_ref.shape[0]
    rows = ybuf_ref.shape[1]
    slot = lax.rem(i, 2)
    base = i * N_EXPERTS
    tabs = (seg_ref, n16_ref, loc_ref, ys_ref)

    @pl.when(i == 0)
    def _():
        ybuf_ref[...] = jnp.zeros_like(ybuf_ref)
        _segment_copies(base, *tabs, ybuf_ref.at[0], sem.at[0], to_hbm=False, wait=False)

    @pl.when(i + 1 < pl.num_programs(0))
    def _():
        _segment_copies(base + N_EXPERTS, *tabs, ybuf_ref.at[1 - slot], sem.at[1 - slot],
                        to_hbm=False, wait=False)

    mi = mi_ref[...]
    d1, d2 = _slot_rows([mi[:, k:k + 1] for k in range(4)], loc_ref, base)
    col = lax.broadcasted_iota(I32, (tm, rows), 1)
    gates = mf_ref[...]
    _segment_copies(base, *tabs, ybuf_ref.at[slot], sem.at[slot], to_hbm=False, wait=True)
    ybuf = ybuf_ref[slot]
    out = x_ref[...]
    for k, dk in enumerate((d1, d2)):
        yk = jnp.dot((col == dk).astype(BF16), ybuf, preferred_element_type=F32)
        out = out + gates[:, k:k + 1] * yk
    if norm == "final":
        out = _rms_rows(out, gf_ref[...])
    o_ref[...] = out
    if norm == "next":
        nxt_ref[...] = _rms_rows(out, gf_ref[...]).astype(BF16)


def _gather(x, mi, mf, ys, seg, n16, loc, g_norm, norm):
    t, d = x.shape
    tm = SEG_TILE
    row_spec = pl.BlockSpec((tm, d), lambda i, *_: (i, 0))
    out_shape = jax.ShapeDtypeStruct((t, d), F32)
    emit_next = norm == "next"
    return pl.pallas_call(
        functools.partial(_gather_kernel, norm=norm),
        out_shape=(out_shape, jax.ShapeDtypeStruct((t, d), BF16)) if emit_next else out_shape,
        grid_spec=pltpu.PrefetchScalarGridSpec(
            num_scalar_prefetch=3,
            grid=(t // tm,),
            in_specs=[row_spec,
                      pl.BlockSpec((tm, LANES), lambda i, *_: (i, 0)),
                      pl.BlockSpec((tm, LANES), lambda i, *_: (i, 0)),
                      pl.BlockSpec((1, d), lambda i, *_: (0, 0)),
                      pl.BlockSpec(memory_space=pl.ANY)],
            out_specs=(row_spec, row_spec) if emit_next else row_spec,
            scratch_shapes=[pltpu.VMEM((2, SEG_STAGE, d), BF16), pltpu.SemaphoreType.DMA((2,))]),
        compiler_params=_cparams(1),
        name="moe_gather",
    )(seg, n16, loc, x, mi, mf, g_norm.reshape(1, d).astype(F32), ys)


def _moe_seg(a, w_out, out_slot, x, g, router, w_gate_up, w_down, slot, g_norm, norm):
    t, d = x.shape
    tg = MOE_TILE
    x, hn, mi, mf, cnt = _route(a, w_out, out_slot, x, g, router)

    counts = cnt[:, 0, :N_EXPERTS]
    n16 = (counts + SEG_ALIGN - 1) // SEG_ALIGN * SEG_ALIGN
    rows_e = jnp.sum(n16, axis=0)
    tiles_e = (rows_e + tg - 1) // tg
    tile_start = jnp.cumsum(tiles_e) - tiles_e
    row_start = tile_start * tg
    seg = row_start[None, :] + jnp.cumsum(n16, axis=0) - n16
    loc = jnp.cumsum(n16, axis=1) - n16
    n_valid = jnp.sum(tiles_e).astype(I32)
    n_tiles = (2 * t + counts.size * (SEG_ALIGN - 1)) // tg + N_EXPERTS
    n_rows = (n_tiles + 1) * tg
    tidx = jnp.minimum(jnp.arange(n_tiles, dtype=I32), n_valid - 1)
    tile_expert = (jnp.sum(tidx[:, None] >= tile_start[None, :], axis=1) - 1).astype(I32)
    tails = (row_start + rows_e).astype(I32)
    seg, n16, loc = (a.reshape(-1).astype(I32) for a in (seg, n16, loc))

    xs = _scatter(hn, mi, seg, n16, loc, tails, n_rows)
    tile_rows = (tails[tile_expert] - tidx * tg).astype(I32)
    ys = _seg_experts(xs, w_gate_up, w_down, slot, tile_expert, tidx, tile_rows, n_valid.reshape(1), n_tiles)
    return _gather(x, mi, mf, ys, seg, n16, loc, g_norm, norm)


def kernel(x, rel_bias, norm_mix, norm_ffn, norm_final, a_w_in, a_w_group, a_scale, a_w_out,
           b_w_in, b_w_out, c_w_in, c_lambda, c_subln, c_w_out, d_w_in, d_sink, d_w_out,
           f_w_gate_up, f_w_down, m_router, m_w_gate_up, m_w_down):
    batch, seq, d = x.shape
    h = x.reshape(batch * seq, d)

    u = _norm_matmul(h, norm_mix[0], a_w_in, 0, F32)
    y = _pool_mix(u, a_w_group, a_scale, 0, batch, seq)
    h = _matmul_res(y, a_w_out, 0, h)
    h = _ffn(h, norm_ffn[0], f_w_gate_up, f_w_down, 0)

    qkv = _dil_proj(h.reshape(batch, seq, d), norm_mix[1], b_w_in, 0, tuple(p[1] for p in DIL_PAIRS))
    o = _dilated_attention(qkv, rel_bias, batch, seq)
    h, hn = _moe_seg(o, b_w_out, 0, h, norm_ffn[1], m_router[0], m_w_gate_up, m_w_down, 0,
                     norm_mix[2], "next")

    qkv = _matmul(hn, c_w_in, 0, BF16)
    o = _diff_attention(qkv, c_lambda, c_subln, 0, rel_bias, batch, seq)
    h = _matmul_res(o, c_w_out, 0, h)
    h, hn = _ffn(h, norm_ffn[2], f_w_gate_up, f_w_down, 1, g_next=norm_mix[3])

    qkv = _matmul(hn, d_w_in, 0, BF16)
    o = _gqa_attention(qkv, d_sink[0], rel_bias, batch, seq)
    h = _moe_seg(o, d_w_out, 0, h, norm_ffn[3], m_router[1], m_w_gate_up, m_w_down, 1,
                 norm_final, "final")
    return h.reshape(batch, seq, d)
```

```python
import collections
import functools
import math

import numpy as np
import jax
import jax.numpy as jnp
from jax import lax
from jax.experimental import pallas as pl
from jax.experimental.pallas import tpu as pltpu

F32 = jnp.float32
BF16 = jnp.bfloat16
I32 = jnp.int32

RMS_EPS = 1e-6
HEAD_DIM = 64
N_BUCKETS = 32
MAX_DISTANCE = 1024
POOL_WINDOWS = (2, 4, 8, 16)
DIL_PAIRS = ((128, 1), (512, 4), (2048, 16))
Q_SCALE = HEAD_DIM ** -0.5
GQA_KV_HEADS = 4
GQA_WINDOW = 128
N_EXPERTS = 8
DIFF_LAYER = 2

LANES = 128
SUBLANES = 8
VMEM_LIMIT = 56 * 1024 * 1024
ROW_TILE = 1024
FF_TILE = 512
MOE_TILE = 1024
MOE_HEIGHTS = 4
ROUTE_ROW_SPLIT = 2
SEG_TILE = 512
ATT_TQ = 128
DIL_UNITS = 4
GQA_UNITS = 2
DIFF_ROW_SPLIT = 4
DIL_HEAD_SPLIT = 2
DIFF_TQ = 512
MASK_VALUE = -1e30


def _cparams(n_axes, vmem=VMEM_LIMIT):
    return pltpu.CompilerParams(dimension_semantics=("arbitrary",) * n_axes,
                                vmem_limit_bytes=vmem)


def _next_pow2(n):
    return 1 << (n - 1).bit_length()


def _rms_rows(x, g):
    ms = jnp.mean(x * x, axis=-1, keepdims=True)
    return x * lax.rsqrt(ms + RMS_EPS) * g


def _norm_matmul_kernel(x_ref, g_ref, w_ref, o_ref, hn_ref):
    @pl.when(pl.program_id(1) == 0)
    def _():
        hn_ref[...] = _rms_rows(x_ref[...], g_ref[...]).astype(BF16)

    o_ref[...] = jnp.dot(hn_ref[...], w_ref[...].astype(BF16),
                         preferred_element_type=F32).astype(o_ref.dtype)


def _norm_matmul(x, g, w, slot, out_dtype):
    t, d = x.shape
    n = w.shape[-1]
    tm = ROW_TILE
    tn = next(c for c in (1024, 768, 512, 256, 128) if n % c == 0)
    return pl.pallas_call(
        _norm_matmul_kernel,
        out_shape=jax.ShapeDtypeStruct((t, n), out_dtype),
        grid=(t // tm, n // tn),
        in_specs=[pl.BlockSpec((tm, d), lambda i, j: (i, 0)),
                  pl.BlockSpec((1, d), lambda i, j: (0, 0)),
                  pl.BlockSpec((None, d, tn), lambda i, j: (slot, 0, j))],
        out_specs=pl.BlockSpec((tm, tn), lambda i, j: (i, j)),
        scratch_shapes=[pltpu.VMEM((tm, d), BF16)],
        compiler_params=_cparams(2),
        name="norm_matmul",
    )(x, g.reshape(1, d), w)


def _matmul_kernel(a_ref, w_ref, o_ref):
    o_ref[...] = jnp.dot(a_ref[...], w_ref[...].astype(BF16),
                         preferred_element_type=F32).astype(o_ref.dtype)


def _matmul(a, w, slot, out_dtype):
    t, d = a.shape
    n = w.shape[-1]
    tm = ROW_TILE
    tn = next(c for c in (1024, 768, 512, 256, 128) if n % c == 0)
    return pl.pallas_call(
        _matmul_kernel,
        out_shape=jax.ShapeDtypeStruct((t, n), out_dtype),
        grid=(t // tm, n // tn),
        in_specs=[pl.BlockSpec((tm, d), lambda i, j: (i, 0)),
                  pl.BlockSpec((None, d, tn), lambda i, j: (slot, 0, j))],
        out_specs=pl.BlockSpec((tm, tn), lambda i, j: (i, j)),
        compiler_params=_cparams(2),
        name="matmul",
    )(a, w)


def _matmul_res_kernel(a_ref, w_ref, x_ref, o_ref, wb_ref):
    @pl.when(pl.program_id(0) == 0)
    def _():
        wb_ref[...] = w_ref[...].astype(BF16)

    o_ref[...] = x_ref[...] + jnp.dot(a_ref[...], wb_ref[...], preferred_element_type=F32)


def _matmul_res(a, w, slot, x):
    t, k = a.shape
    d = w.shape[-1]
    tm = ROW_TILE
    return pl.pallas_call(
        _matmul_res_kernel,
        out_shape=jax.ShapeDtypeStruct((t, d), F32),
        grid=(t // tm,),
        in_specs=[pl.BlockSpec((tm, k), lambda i: (i, 0)),
                  pl.BlockSpec((None, k, d), lambda i: (slot, 0, 0)),
                  pl.BlockSpec((tm, d), lambda i: (i, 0))],
        out_specs=pl.BlockSpec((tm, d), lambda i: (i, 0)),
        scratch_shapes=[pltpu.VMEM((k, d), BF16)],
        compiler_params=_cparams(1),
        name="matmul_res",
    )(a, w, x)


def _silu_mul(g, u):
    return g * (1.0 / (1.0 + jnp.exp(-g))) * u


def _swiglu_accumulate(h_ref, wg_ref, wu_ref, wd_ref, acc_ref, rows=None):
    rows = h_ref.shape[0] if rows is None else rows
    h = h_ref[pl.ds(0, rows), :]
    gate = jnp.dot(h, wg_ref[...].astype(BF16), preferred_element_type=F32)
    up = jnp.dot(h, wu_ref[...].astype(BF16), preferred_element_type=F32)
    a = _silu_mul(gate, up).astype(BF16)
    acc_ref[pl.ds(0, rows), :] += jnp.dot(a, wd_ref[...].astype(BF16), preferred_element_type=F32)


def _ffn_kernel(x_ref, g_ref, gn_ref, wg_ref, wu_ref, wd_ref, o_ref, *rest, emit_next):
    if emit_next:
        nxt_ref, hn_ref, acc_ref = rest
    else:
        hn_ref, acc_ref = rest
    f = pl.program_id(1)

    @pl.when(f == 0)
    def _():
        hn_ref[...] = _rms_rows(x_ref[...], g_ref[...]).astype(BF16)
        acc_ref[...] = jnp.zeros_like(acc_ref)

    _swiglu_accumulate(hn_ref, wg_ref, wu_ref, wd_ref, acc_ref)

    @pl.when(f == pl.num_programs(1) - 1)
    def _():
        out = x_ref[...] + acc_ref[...]
        o_ref[...] = out
        if emit_next:
            nxt_ref[...] = _rms_rows(out, gn_ref[...]).astype(BF16)


def _ffn(x, g, w_gate_up, w_down, slot, g_next=None):
    t, d = x.shape
    ff = w_down.shape[-2]
    tm, tf = ROW_TILE, FF_TILE
    nf = ff // tf
    emit_next = g_next is not None
    gn = (g_next if emit_next else g).reshape(1, d)
    row_spec = pl.BlockSpec((tm, d), lambda i, f: (i, 0))
    vec_spec = pl.BlockSpec((1, d), lambda i, f: (0, 0))
    out_shape = jax.ShapeDtypeStruct((t, d), F32)
    return pl.pallas_call(
        functools.partial(_ffn_kernel, emit_next=emit_next),
        out_shape=(out_shape, jax.ShapeDtypeStruct((t, d), BF16)) if emit_next else out_shape,
        grid=(t // tm, nf),
        in_specs=[row_spec, vec_spec, vec_spec,
                  pl.BlockSpec((None, d, tf), lambda i, f: (slot, 0, f)),
                  pl.BlockSpec((None, d, tf), lambda i, f: (slot, 0, nf + f)),
                  pl.BlockSpec((None, tf, d), lambda i, f: (slot, f, 0))],
        out_specs=(row_spec, row_spec) if emit_next else row_spec,
        scratch_shapes=[pltpu.VMEM((tm, d), BF16), pltpu.VMEM((tm, d), F32)],
        compiler_params=_cparams(2),
        name="ffn",
    )(x, g.reshape(1, d), gn, w_gate_up, w_gate_up, w_down)


POOL_PAD = 16


def _pool_kernel(u_ref, wg_ref, sc_ref, o_ref, pad_ref):
    s, c = u_ref.shape
    grp = pl.program_id(1)
    pad_ref[pl.ds(0, POOL_PAD), :] = jnp.zeros((POOL_PAD, c), F32)
    pad_ref[pl.ds(POOL_PAD + s, POOL_PAD), :] = jnp.zeros((POOL_PAD, c), F32)
    pad_ref[pl.ds(POOL_PAD, s), :] = u_ref[...]
    wb = wg_ref[...].astype(BF16)
    chunk = 256

    for gi, w in enumerate(POOL_WINDOWS):
        @pl.when(grp == gi)
        def _(w=w):
            for c0 in range(0, s, chunk):
                acc = None
                for dlt in range(-(w // 2), w - w // 2):
                    piece = pad_ref[pl.ds(POOL_PAD + c0 + dlt, chunk), :]
                    acc = piece if acc is None else acc + piece
                pos = c0 + lax.broadcasted_iota(I32, (chunk, 1), 0)
                lo = jnp.maximum(pos - w // 2, 0)
                hi = jnp.minimum(pos + (w - w // 2), s)
                cnt = (hi - lo).astype(F32)
                mixed = acc / cnt - u_ref[pl.ds(c0, chunk), :]
                y = jnp.dot(mixed.astype(BF16), wb, preferred_element_type=F32)
                o_ref[pl.ds(c0, chunk), :] = (y * sc_ref[...]).astype(o_ref.dtype)


def _pool_mix(u, w_group, scale, slot, batch, seq):
    d = u.shape[1]
    ng = len(POOL_WINDOWS)
    c = d // ng
    u3 = u.reshape(batch, seq, d)
    out = pl.pallas_call(
        _pool_kernel,
        out_shape=jax.ShapeDtypeStruct((batch, seq, d), BF16),
        grid=(batch, ng),
        in_specs=[pl.BlockSpec((None, seq, c), lambda b, g: (b, 0, g)),
                  pl.BlockSpec((None, None, c, c), lambda b, g: (slot, g, 0, 0)),
                  pl.BlockSpec((1, c), lambda b, g: (slot, g))],
        out_specs=pl.BlockSpec((None, seq, c), lambda b, g: (b, 0, g)),
        scratch_shapes=[pltpu.VMEM((seq + 2 * POOL_PAD, c), F32)],
        compiler_params=_cparams(2),
        name="pool_mix",
    )(u3, w_group, scale)
    return out.reshape(batch * seq, d)


def _rel_bucket_np(rel):
    half = N_BUCKETS // 2
    max_exact = half // 2
    n = np.abs(rel)
    ratio = np.log(np.maximum(n, 1).astype(np.float32) / np.float32(max_exact))
    big = max_exact + (ratio / np.float32(math.log(MAX_DISTANCE / max_exact))
                       * np.float32(half - max_exact)).astype(np.int32)
    big = np.minimum(big, half - 1)
    return np.where(rel > 0, half, 0) + np.where(n < max_exact, n, big)


def _seg_table(rel_bias, rel, valid):
    bucket = jnp.asarray(_rel_bucket_np(rel).astype(np.int32))
    vals = jnp.take(rel_bias.astype(F32), bucket, axis=0)
    vals = jnp.where(jnp.asarray(valid)[..., None], vals, MASK_VALUE)
    return jnp.transpose(vals, (2, 0, 1))


def _toeplitz(seg_row, rows, cols):
    w = seg_row.shape[1]
    full = jnp.broadcast_to(seg_row, (rows, w))
    rolled = pltpu.roll(full, w - (rows - 1), 1, stride=1, stride_axis=0)
    return rolled[:, :cols]


def _band_variants(hw, n_tiles):
    if n_tiles == 1:
        return (0,)
    return (0, -hw, -2 * hw)


def _band_segs(rel_bias, tq, kw, half, dil, variants, width):
    c = np.arange(width)
    rel = np.stack([r0 + c - (tq - 1) for r0 in variants])
    valid = (np.abs(rel) <= half) & (c[None, :] < tq + kw - 1)
    return _seg_table(rel_bias, rel * dil, valid)


def _band_window(qi, nq, tq, hw, kw, ln):
    if nq == 1:
        return 0, 0
    var = jnp.where(qi == 0, 0, jnp.where(qi == nq - 1, 2, 1))
    return var, jnp.clip(qi * tq - hw, 0, ln - kw)


DilCfg = collections.namedtuple("DilCfg", "dil ln nq hw kw n_var")


def _dil_proj_kernel(x_ref, g_ref, w_ref, o_ref, slab_ref, hn_ref, *, dils):
    grp, c = pl.program_id(1), pl.program_id(2)
    seq, d = x_ref.shape
    n_slabs = d // LANES
    chunk = 256

    @pl.when((grp == 0) & (c == 0))
    def _():
        for c0 in range(0, seq, chunk):
            hn = _rms_rows(x_ref[pl.ds(c0, chunk), :], g_ref[...])
            for s in range(n_slabs):
                slab_ref[s, pl.ds(c0, chunk), :] = hn[:, s * LANES:(s + 1) * LANES]

    for gi, dil in enumerate(dils):
        @pl.when((grp == gi) & (c == 0))
        def _(dil=dil):
            ln = seq // dil
            for r in range(dil):
                for c0 in range(0, ln, chunk):
                    n = min(chunk, ln)
                    rows = [slab_ref[s, pl.ds(r + c0 * dil, n, stride=dil), :] for s in range(n_slabs)]
                    hn_ref[pl.ds(r * ln + c0, n), :] = jnp.concatenate(rows, axis=1).astype(BF16)

    res = jnp.dot(hn_ref[...], w_ref[...].astype(BF16), preferred_element_type=F32)
    res = res * jnp.where(c < pl.num_programs(2) // 3, Q_SCALE, 1.0)
    for s in range(o_ref.shape[0]):
        o_ref[s] = res[:, s * LANES:(s + 1) * LANES].astype(o_ref.dtype)


def _dil_proj(x3, g, w_in, slot, dils):
    batch, seq, d = x3.shape
    n = w_in.shape[-1]
    tn = FF_TILE
    per_group = n // len(dils) // tn
    spt = tn // LANES
    return pl.pallas_call(
        functools.partial(_dil_proj_kernel, dils=dils),
        out_shape=jax.ShapeDtypeStruct((batch, n // LANES, seq, LANES), BF16),
        grid=(batch, len(dils), per_group),
        in_specs=[pl.BlockSpec((None, seq, d), lambda b, g, c: (b, 0, 0)),
                  pl.BlockSpec((1, d), lambda b, g, c: (0, 0)),
                  pl.BlockSpec((None, d, tn), lambda b, g, c: (slot, 0, g * per_group + c))],
        out_specs=pl.BlockSpec((None, spt, seq, LANES), lambda b, g, c: (b, g * per_group + c, 0, 0)),
        scratch_shapes=[pltpu.VMEM((d // LANES, seq, LANES), F32), pltpu.VMEM((seq, d), BF16)],
        compiler_params=_cparams(3),
        name="dil_proj",
    )(x3, g.reshape(1, d), w_in)


def _dil_unit(cfg, unit, q4, k_ref, v_ref, bias_ref, acc_ref, m_ref, l_ref, *, tq, first, last):
    dil, ln, nq, kw = cfg.dil, cfg.ln, cfg.nq, cfg.kw
    n_pairs = q4.shape[0]
    if nq == 1:
        r, qi = unit, 0
    elif dil == 1:
        r, qi = 0, unit
    else:
        r, qi = unit // nq, unit % nq
    var, ks = _band_window(qi, nq, tq, cfg.hw, kw, ln)
    krow = pl.multiple_of(r * ln + ks, cfg.hw)
    if dil == 1:
        rows = pl.ds(pl.multiple_of(qi * tq, tq), tq)
    else:
        rows = pl.ds(qi * tq * dil + r, tq, stride=dil)

    even = lax.broadcasted_iota(I32, (n_pairs, tq, LANES), 2) < HEAD_DIM
    zero = jnp.zeros_like(q4)
    q8 = jnp.concatenate([jnp.where(even, q4, zero), jnp.where(even, zero, q4)], axis=0)
    k4 = k_ref[:, pl.ds(krow, kw), :]
    v4 = v_ref[:, pl.ds(krow, kw), :]
    k8 = jnp.concatenate([k4, k4], axis=0)
    va = jnp.concatenate([v4, jnp.ones_like(v4)], axis=2)
    v8 = jnp.concatenate([va, va], axis=0)
    s = lax.dot_general(q8, k8, (((2,), (2,)), ((0,), (0,))), preferred_element_type=F32)
    s = s + bias_ref[var]
    m8 = jnp.max(s, axis=-1, keepdims=True)
    p = jnp.exp(s - m8).astype(BF16)
    ov = lax.dot_general(p, v8, (((2,), (1,)), ((0,), (0,))), preferred_element_type=F32)
    o_c = jnp.where(even, ov[:n_pairs, :, :LANES], ov[n_pairs:, :, :LANES])
    l_c = jnp.where(even, ov[:n_pairs, :, LANES:], ov[n_pairs:, :, LANES:])
    m_c = jnp.where(even, m8[:n_pairs], m8[n_pairs:])
    if first:
        m_n, l_n, acc_n = m_c, l_c, o_c
    else:
        m_o = jnp.stack([m_ref[pp, rows, :] for pp in range(n_pairs)])
        l_o = jnp.stack([l_ref[pp, rows, :] for pp in range(n_pairs)])
        acc_o = jnp.stack([acc_ref[pp, rows, :] for pp in range(n_pairs)])
        m_n = jnp.maximum(m_o, m_c)
        a_o = jnp.exp(m_o - m_n)
        a_c = jnp.exp(m_c - m_n)
        l_n = a_o * l_o + a_c * l_c
        acc_n = a_o * acc_o + a_c * o_c
    if last:
        acc_n = acc_n / l_n
    for pp in range(n_pairs):
        acc_ref[pp, rows, :] = acc_n[pp]
        if not last:
            m_ref[pp, rows, :] = m_n[pp]
            l_ref[pp, rows, :] = l_n[pp]


def _dil_attn_kernel(*refs, cfgs, tq):
    ng = len(cfgs)
    seg_refs = refs[:ng]
    q_ref, k_ref, v_ref, o_ref = refs[ng:ng + 4]
    bias_refs = refs[ng + 4:2 * ng + 4]
    acc_ref, m_ref, l_ref = refs[2 * ng + 4:]
    hf, b, grp, step = (pl.program_id(i) for i in range(4))
    n_pairs = q_ref.shape[0]

    @pl.when((b == 0) & (grp == 0) & (step == 0))
    def _():
        for gi, cfg in enumerate(cfgs):
            for v in range(cfg.n_var):
                for odd in range(2):
                    for pp in range(n_pairs):
                        head = hf * 2 * n_pairs + 2 * pp + odd
                        row = seg_refs[gi][head, pl.ds(v, 1), :]
                        bias_refs[gi][v, odd * n_pairs + pp] = _toeplitz(row, tq, cfg.kw)

    for gi, cfg in enumerate(cfgs):
        @pl.when(grp == gi)
        def _(gi=gi, cfg=cfg):
            for uu in range(DIL_UNITS):
                _dil_unit(cfg, step * DIL_UNITS + uu, q_ref[:, pl.ds(uu * tq, tq), :], k_ref, v_ref,
                          bias_refs[gi], acc_ref, m_ref, l_ref, tq=tq, first=gi == 0, last=gi == ng - 1)

    @pl.when((grp == ng - 1) & (step == pl.num_programs(3) - 1))
    def _():
        slabs = [acc_ref[s] for s in range(n_pairs)]
        o_ref[...] = jnp.concatenate(slabs, axis=1).astype(o_ref.dtype)


def _dilated_attention(qkv, rel_bias, batch, seq):
    ng = len(DIL_PAIRS)
    d = qkv.shape[1] * LANES // (3 * ng)
    hd = d // DIL_HEAD_SPLIT
    n_pairs = hd // LANES
    tq = ATT_TQ
    cfgs, segs = [], []
    for win, dil in DIL_PAIRS:
        half = win // (2 * dil)
        ln = seq // dil
        nq = ln // tq
        kw = min(tq + 2 * half, ln)
        variants = _band_variants(half, nq)
        cfgs.append(DilCfg(dil, ln, nq, half, kw, len(variants)))
        segs.append(_band_segs(rel_bias, tq, kw, half, dil, variants, _next_pow2(tq + kw - 1)))
    steps = seq // tq // DIL_UNITS
    cb = DIL_HEAD_SPLIT

    in_specs = [pl.BlockSpec(sg.shape, lambda hf, b, g, s: (0, 0, 0)) for sg in segs]
    in_specs += [pl.BlockSpec((None, n_pairs, DIL_UNITS * tq, LANES),
                              lambda hf, b, g, s: (b, (g * 3) * cb + hf, s, 0)),
                 pl.BlockSpec((None, n_pairs, seq, LANES),
                              lambda hf, b, g, s: (b, (g * 3 + 1) * cb + hf, 0, 0)),
                 pl.BlockSpec((None, n_pairs, seq, LANES),
                              lambda hf, b, g, s: (b, (g * 3 + 2) * cb + hf, 0, 0))]
    scratch = [pltpu.VMEM((c.n_var, 2 * n_pairs, tq, c.kw), F32) for c in cfgs]
    scratch += [pltpu.VMEM((n_pairs, seq, LANES), F32) for _ in range(3)]
    out = pl.pallas_call(
        functools.partial(_dil_attn_kernel, cfgs=tuple(cfgs), tq=tq),
        out_shape=jax.ShapeDtypeStruct((batch, seq, d), BF16),
        grid=(DIL_HEAD_SPLIT, batch, ng, steps),
        in_specs=in_specs,
        out_specs=pl.BlockSpec((None, seq, hd), lambda hf, b, g, s: (b, 0, hf)),
        scratch_shapes=scratch,
        compiler_params=_cparams(4),
        name="dil_attn",
    )(*segs, qkv, qkv, qkv)
    return out.reshape(batch * seq, d)


def _gqa_kernel(sink_ref, seg_ref, q_ref, k_ref, v_ref, o_ref, bias_ref, *, tq, kw, hw, grp):
    b, step = pl.program_id(0), pl.program_id(1)
    seq_len = k_ref.shape[0]
    nq = seq_len // tq
    n_kv = k_ref.shape[1] // HEAD_DIM

    @pl.when((b == 0) & (step == 0))
    def _():
        for v in range(3):
            for kh in range(n_kv):
                for gq in range(grp):
                    bias_ref[v, kh, pl.ds(gq * tq, tq), :] = _toeplitz(
                        seg_ref[kh * grp + gq, pl.ds(v, 1), :], tq, kw)

    row = lax.broadcasted_iota(I32, (grp * tq, 1), 0)
    for u in range(GQA_UNITS):
        qi = step * GQA_UNITS + u
        qrows = pl.ds(u * tq, tq)
        var, ks = _band_window(qi, nq, tq, hw, kw, seq_len)
        ks = pl.multiple_of(ks, hw)
        outs = [None] * (n_kv * grp)
        for kh in range(n_kv):
            cs = slice(kh * HEAD_DIM, (kh + 1) * HEAD_DIM)
            qs = jnp.concatenate(
                [q_ref[qrows, (kh * grp + gq) * HEAD_DIM:(kh * grp + gq + 1) * HEAD_DIM]
                 for gq in range(grp)], axis=0) * Q_SCALE
            kk = k_ref[pl.ds(ks, kw), cs]
            vv = v_ref[pl.ds(ks, kw), cs]
            s = lax.dot_general(qs, kk, (((1,), (1,)), ((), ())), preferred_element_type=F32)
            s = s + bias_ref[var, kh]
            sk = jnp.zeros((grp * tq, 1), F32)
            for gq in range(grp):
                sk = jnp.where((row >= gq * tq) & (row < (gq + 1) * tq), sink_ref[kh * grp + gq], sk)
            m = jnp.maximum(jnp.max(s, axis=-1, keepdims=True), sk)
            e = jnp.exp(s - m).astype(BF16)
            va = jnp.concatenate([vv, jnp.ones_like(vv)], axis=1)
            ov = jnp.dot(e, va, preferred_element_type=F32)
            o = ov[:, :HEAD_DIM] / (ov[:, HEAD_DIM:] + jnp.exp(sk - m))
            for gq in range(grp):
                outs[kh * grp + gq] = o[gq * tq:(gq + 1) * tq, :]
        o_ref[qrows, :] = jnp.concatenate(outs, axis=1).astype(o_ref.dtype)


def _gqa_attention(qkv, sink, rel_bias, batch, seq):
    t, ncol = qkv.shape
    n_q = sink.shape[0]
    d = n_q * HEAD_DIM
    kvw = GQA_KV_HEADS * HEAD_DIM
    grp = n_q // GQA_KV_HEADS
    tq = hw = GQA_WINDOW
    kw = 3 * GQA_WINDOW
    nq = seq // tq
    variants = _band_variants(hw, nq)
    width = _next_pow2(tq + kw - 1)
    seg = _band_segs(rel_bias, tq, kw, GQA_WINDOW, 1, variants, width)
    qkv_v = qkv.reshape(batch, seq, ncol)
    out = pl.pallas_call(
        functools.partial(_gqa_kernel, tq=tq, kw=kw, hw=hw, grp=grp),
        out_shape=jax.ShapeDtypeStruct((batch, seq, d), BF16),
        grid=(batch, nq // GQA_UNITS),
        in_specs=[pl.BlockSpec(memory_space=pltpu.SMEM),
                  pl.BlockSpec((n_q, 3, width), lambda b, qi: (0, 0, 0)),
                  pl.BlockSpec((None, GQA_UNITS * tq, d), lambda b, qi: (b, qi, 0)),
                  pl.BlockSpec((None, seq, kvw), lambda b, qi: (b, 0, d // kvw)),
                  pl.BlockSpec((None, seq, kvw), lambda b, qi: (b, 0, d // kvw + 1))],
        out_specs=pl.BlockSpec((None, GQA_UNITS * tq, d), lambda b, qi: (b, qi, 0)),
        scratch_shapes=[pltpu.VMEM((3, GQA_KV_HEADS, grp * tq, kw), F32)],
        compiler_params=_cparams(2),
        name="gqa_attn",
    )(sink.astype(F32), seg, qkv_v, qkv_v, qkv_v)
    return out.reshape(t, d)


def _diff_kernel(seg_ref, lam_ref, sub_ref, q_ref, k_ref, v_ref, o_ref, bias_ref, *, tq, lam_init):
    h, qi, b = pl.program_id(0), pl.program_id(1), pl.program_id(2)
    seq_len = k_ref.shape[0]
    nk = seq_len // tq

    @pl.when(b == 0)
    def _():
        for j in range(2):
            for ki in range(nk):
                row = seg_ref[h * 2 + j, pl.ds(ki - qi + nk - 1, 1), :]
                bias_ref[j, :, pl.ds(ki * tq, tq)] = _toeplitz(row, tq, tq)

    lv = lam_ref[...]
    s01 = jnp.sum(lv[0:1, :] * lv[1:2, :], axis=-1, keepdims=True)
    s23 = jnp.sum(lv[2:3, :] * lv[3:4, :], axis=-1, keepdims=True)
    lam = jnp.exp(s01) - jnp.exp(s23) + lam_init

    q = q_ref[...] * Q_SCALE
    k = k_ref[...]
    v = v_ref[...]
    va = jnp.concatenate([v, jnp.ones_like(v)], axis=1)
    map0 = lax.broadcasted_iota(I32, q.shape, 1) < HEAD_DIM
    zero = jnp.zeros_like(q)
    qm = (jnp.where(map0, q, zero), jnp.where(map0, zero, q))
    rc = tq // DIFF_ROW_SPLIT
    for c0 in range(0, tq, rc):
        outs = []
        for j in range(2):
            s = lax.dot_general(qm[j][c0:c0 + rc], k, (((1,), (1,)), ((), ())),
                                preferred_element_type=F32)
            s = s + bias_ref[j, pl.ds(c0, rc), :]
            m = jnp.max(s, axis=-1, keepdims=True)
            p = jnp.exp(s - m).astype(BF16)
            ov = jnp.dot(p, va, preferred_element_type=F32)
            outs.append(ov[:, :2 * HEAD_DIM] / ov[:, 2 * HEAD_DIM:])
        o = outs[0] - lam * outs[1]
        o = _rms_rows(o, sub_ref[...]) * (1.0 - lam_init)
        o_ref[pl.ds(c0, rc), :] = o.astype(o_ref.dtype)


def _diff_attention(qkv, lam_vecs, subln, slot, rel_bias, batch, seq):
    t, ncol = qkv.shape
    d = ncol // 3
    hd2 = 2 * HEAD_DIM
    n_heads = d // hd2
    tq = DIFF_TQ
    nk = seq // tq
    width = _next_pow2(2 * tq - 1)
    c = np.arange(width)
    rel = np.stack([(dl - (nk - 1)) * tq + c - (tq - 1) for dl in range(2 * nk - 1)])
    seg = _seg_table(rel_bias, rel, np.ones_like(rel, dtype=bool))
    lam_init = 0.8 - 0.6 * math.exp(-0.3 * DIFF_LAYER)
    qkv_v = qkv.reshape(batch, seq, ncol)
    out = pl.pallas_call(
        functools.partial(_diff_kernel, tq=tq, lam_init=lam_init),
        out_shape=jax.ShapeDtypeStruct((batch, seq, d), BF16),
        grid=(n_heads, seq // tq, batch),
        in_specs=[pl.BlockSpec(seg.shape, lambda h, qi, b: (0, 0, 0)),
                  pl.BlockSpec((None,) + lam_vecs.shape[1:], lambda h, qi, b: (slot, 0, 0)),
                  pl.BlockSpec((1, hd2), lambda h, qi, b: (slot, 0)),
                  pl.BlockSpec((None, tq, hd2), lambda h, qi, b: (b, qi, h)),
                  pl.BlockSpec((None, seq, hd2), lambda h, qi, b: (b, 0, n_heads + h)),
                  pl.BlockSpec((None, seq, hd2), lambda h, qi, b: (b, 0, 2 * n_heads + h))],
        out_specs=pl.BlockSpec((None, tq, hd2), lambda h, qi, b: (b, qi, h)),
        scratch_shapes=[pltpu.VMEM((2, tq, seq), F32)],
        compiler_params=_cparams(3),
        name="diff_attn",
    )(seg, lam_vecs, subln, qkv_v, qkv_v, qkv_v)
    return out.reshape(t, d)


SEG_ALIGN = 16
SEG_SIZES = (512, 256, 128, 64, 32, 16)
SEG_STAGE = 2 * SEG_TILE + N_EXPERTS * SEG_ALIGN


def _route_kernel(a_ref, wo_ref, x_ref, g_ref, r_ref, xn_ref, hn_ref, mi_ref, mf_ref, cnt_ref,
                  tri_ref, wb_ref):
    tm, d = x_ref.shape

    @pl.when(pl.program_id(0) == 0)
    def _():
        rr = lax.broadcasted_iota(I32, (tm, tm), 0)
        cc = lax.broadcasted_iota(I32, (tm, tm), 1)
        tri_ref[...] = (cc < rr).astype(BF16)
        wb_ref[...] = wo_ref[...].astype(BF16)

    r = r_ref[...]
    r_hi = r.astype(BF16)
    r_lo = (r - r_hi.astype(F32)).astype(BF16)
    rows = tm // ROUTE_ROW_SPLIT
    lane = lax.broadcasted_iota(I32, (rows, LANES), 1)

    picks = []
    for r0 in range(0, tm, rows):
        rs = pl.ds(r0, rows)
        xn = x_ref[rs, :] + jnp.dot(a_ref[rs, :], wb_ref[...], preferred_element_type=F32)
        xn_ref[rs, :] = xn
        hn = _rms_rows(xn, g_ref[...])
        h_hi = hn.astype(BF16)
        hn_ref[rs, :] = h_hi
        h_lo = (hn - h_hi.astype(F32)).astype(BF16)
        logits = (jnp.dot(h_hi, r_hi, preferred_element_type=F32)
                  + jnp.dot(h_hi, r_lo, preferred_element_type=F32)
                  + jnp.dot(h_lo, r_hi, preferred_element_type=F32))
        logits = jnp.where(lane < N_EXPERTS, logits, -jnp.inf)
        v1 = jnp.max(logits, axis=-1, keepdims=True)
        i1 = jnp.min(jnp.where(logits == v1, lane, LANES), axis=-1, keepdims=True)
        oh1 = lane == i1
        rest = jnp.where(oh1, -jnp.inf, logits)
        v2 = jnp.max(rest, axis=-1, keepdims=True)
        i2 = jnp.min(jnp.where(rest == v2, lane, LANES), axis=-1, keepdims=True)
        oh2 = lane == i2
        e2 = jnp.exp(v2 - v1)
        mf_ref[rs, :] = jnp.where(lane == 0, 1.0 / (1.0 + e2), jnp.where(lane == 1, e2 / (1.0 + e2), 0.0))
        picks.append((rs, i1, i2, oh1, oh2))

    sel = jnp.concatenate([(oh1 | oh2).astype(BF16) for _, _, _, oh1, oh2 in picks], axis=0)
    for rs, i1, i2, oh1, oh2 in picks:
        before = jnp.dot(tri_ref[rs, :], sel, preferred_element_type=F32)
        rank1 = jnp.sum(jnp.where(oh1, before, 0.0), axis=-1, keepdims=True).astype(I32)
        rank2 = jnp.sum(jnp.where(oh2, before, 0.0), axis=-1, keepdims=True).astype(I32)
        mi_ref[rs, :] = jnp.where(lane == 0, i1, jnp.where(lane == 1, i2,
                                  jnp.where(lane == 2, rank1, jnp.where(lane == 3, rank2, 0))))
    counts = jnp.sum(sel.astype(F32), axis=0, keepdims=True).astype(I32)
    cnt_ref[...] = jnp.broadcast_to(counts, cnt_ref.shape)


def _route(a, w_out, out_slot, x, g, router):
    t, d = x.shape
    k = a.shape[1]
    tm = SEG_TILE
    r_pad = jnp.zeros((d, LANES), F32).at[:, :N_EXPERTS].set(router.astype(F32))
    return pl.pallas_call(
        _route_kernel,
        out_shape=(jax.ShapeDtypeStruct((t, d), F32),
                   jax.ShapeDtypeStruct((t, d), BF16),
                   jax.ShapeDtypeStruct((t, LANES), I32),
                   jax.ShapeDtypeStruct((t, LANES), F32),
                   jax.ShapeDtypeStruct((t // tm, SUBLANES, LANES), I32)),
        grid=(t // tm,),
        in_specs=[pl.BlockSpec((tm, k), lambda i: (i, 0)),
                  pl.BlockSpec((None, k, d), lambda i: (out_slot, 0, 0)),
                  pl.BlockSpec((tm, d), lambda i: (i, 0)),
                  pl.BlockSpec((1, d), lambda i: (0, 0)),
                  pl.BlockSpec((d, LANES), lambda i: (0, 0))],
        out_specs=(pl.BlockSpec((tm, d), lambda i: (i, 0)),
                   pl.BlockSpec((tm, d), lambda i: (i, 0)),
                   pl.BlockSpec((tm, LANES), lambda i: (i, 0)),
                   pl.BlockSpec((tm, LANES), lambda i: (i, 0)),
                   pl.BlockSpec((None, SUBLANES, LANES), lambda i: (i, 0, 0))),
        scratch_shapes=[pltpu.VMEM((tm, tm), BF16), pltpu.VMEM((k, d), BF16)],
        compiler_params=_cparams(1),
        name="moe_route",
    )(a, w_out, x, g.reshape(1, d), r_pad)


def _slot_rows(mi, loc_ref, base):
    e1, e2, d1, d2 = mi[0], mi[1], mi[2], mi[3]
    for e in range(N_EXPERTS):
        off = loc_ref[base + e]
        d1 = d1 + jnp.where(e1 == e, off, 0)
        d2 = d2 + jnp.where(e2 == e, off, 0)
    return d1, d2


def _segment_copies(base, seg_ref, n16_ref, loc_ref, hbm_ref, vmem_ref, sem, *, to_hbm, wait):
    for e in range(N_EXPERTS):
        n16 = n16_ref[base + e]
        hbm0 = seg_ref[base + e]
        vmem0 = loc_ref[base + e]
        off = 0
        for size in SEG_SIZES:
            @pl.when((n16 & size) != 0)
            def _(off=off, size=size):
                h = hbm_ref.at[pl.ds(pl.multiple_of(hbm0 + off, SEG_ALIGN), size)]
                v = vmem_ref.at[pl.ds(pl.multiple_of(vmem0 + off, SEG_ALIGN), size)]
                cp = pltpu.make_async_copy(v, h, sem) if to_hbm else pltpu.make_async_copy(h, v, sem)
                if wait:
                    cp.wait()
                else:
                    cp.start()
            off = off + (n16 & size)


def _scatter_kernel(seg_ref, n16_ref, loc_ref, tail_ref, nv_ref, hn_ref, mi_ref, xs_ref, stage_ref,
                    zero_ref, sem):
    i = pl.program_id(0)
    tm = hn_ref.shape[0]
    rows = stage_ref.shape[1]
    tg = zero_ref.shape[0]
    slot = lax.rem(i, 2)

    @pl.when(i == 0)
    def _():
        zero_ref[...] = jnp.zeros_like(zero_ref)
        for e in range(N_EXPERTS):
            tail = pl.multiple_of(tail_ref[e], SEG_ALIGN)
            pltpu.make_async_copy(zero_ref, xs_ref.at[pl.ds(tail, tg)], sem.at[2]).start()
        for e in range(N_EXPERTS):
            tail = pl.multiple_of(tail_ref[e], SEG_ALIGN)
            pltpu.make_async_copy(zero_ref, xs_ref.at[pl.ds(tail, tg)], sem.at[2]).wait()

        def zero_tile(tile, carry):
            cp = pltpu.make_async_copy(zero_ref, xs_ref.at[pl.ds(pl.multiple_of(tile * tg, tg), tg)],
                                       sem.at[2])
            cp.start()
            cp.wait()
            return carry

        lax.fori_loop(nv_ref[0], xs_ref.shape[0] // tg, zero_tile, 0)

    base = i * N_EXPERTS
    mi_t = mi_ref[...].T
    d1, d2 = _slot_rows([mi_t[k:k + 1, :] for k in range(4)], loc_ref, base)
    row = lax.broadcasted_iota(I32, (rows, tm), 0)
    onehot = ((row == d1) | (row == d2)).astype(BF16)
    stage_ref[slot] = jnp.dot(onehot, hn_ref[...], preferred_element_type=F32).astype(BF16)
    tabs = (seg_ref, n16_ref, loc_ref, xs_ref)
    _segment_copies(base, *tabs, stage_ref.at[slot], sem.at[slot], to_hbm=True, wait=False)

    @pl.when(i > 0)
    def _():
        _segment_copies(base - N_EXPERTS, *tabs, stage_ref.at[1 - slot], sem.at[1 - slot],
                        to_hbm=True, wait=True)

    @pl.when(i == pl.num_programs(0) - 1)
    def _():
        _segment_copies(base, *tabs, stage_ref.at[slot], sem.at[slot], to_hbm=True, wait=True)


def _scatter(hn, mi, seg, n16, loc, tails, n_valid, n_rows):
    t, d = hn.shape
    tm = SEG_TILE
    return pl.pallas_call(
        _scatter_kernel,
        out_shape=jax.ShapeDtypeStruct((n_rows, d), BF16),
        grid_spec=pltpu.PrefetchScalarGridSpec(
            num_scalar_prefetch=5,
            grid=(t // tm,),
            in_specs=[pl.BlockSpec((tm, d), lambda i, *_: (i, 0)),
                      pl.BlockSpec((tm, LANES), lambda i, *_: (i, 0))],
            out_specs=pl.BlockSpec(memory_space=pl.ANY),
            scratch_shapes=[pltpu.VMEM((2, SEG_STAGE, d), BF16), pltpu.VMEM((MOE_TILE, d), BF16),
                            pltpu.SemaphoreType.DMA((3,))]),
        compiler_params=pltpu.CompilerParams(dimension_semantics=("arbitrary",),
                                             vmem_limit_bytes=VMEM_LIMIT, has_side_effects=True),
        name="moe_scatter",
    )(seg, n16, loc, tails, n_valid, hn, mi)


def _seg_expert_kernel(te_ref, tb_ref, nr_ref, nv_ref, xs_ref, wg_ref, wu_ref, wd_ref, ys_ref, acc_ref):
    i, f = pl.program_id(0), pl.program_id(1)
    step = xs_ref.shape[0] // MOE_HEIGHTS

    @pl.when(i < nv_ref[0])
    def _():
        @pl.when(f == 0)
        def _():
            acc_ref[...] = jnp.zeros_like(acc_ref)

        nrow = nr_ref[i]
        for q in range(1, MOE_HEIGHTS + 1):
            lo_ok = nrow > (q - 1) * step
            in_q = lo_ok if q == MOE_HEIGHTS else lo_ok & (nrow <= q * step)

            @pl.when(in_q)
            def _(q=q):
                _swiglu_accumulate(xs_ref, wg_ref, wu_ref, wd_ref, acc_ref, rows=q * step)

        @pl.when(f == pl.num_programs(1) - 1)
        def _():
            ys_ref[...] = acc_ref[...].astype(ys_ref.dtype)

    @pl.when((i >= nv_ref[0]) & (f == pl.num_programs(1) - 1))
    def _():
        ys_ref[...] = jnp.zeros_like(ys_ref)


def _seg_experts(xs, w_gate_up, w_down, slot, tile_expert, tile_block, tile_rows, n_valid, n_tiles):
    n_rows, d = xs.shape
    ff = w_down.shape[-2]
    tg, tf = MOE_TILE, FF_TILE
    nf = ff // tf

    def fidx(i, f, nv):
        return jnp.where(i < nv[0], f, nf - 1)

    return pl.pallas_call(
        _seg_expert_kernel,
        out_shape=jax.ShapeDtypeStruct((n_tiles * tg, d), BF16),
        grid_spec=pltpu.PrefetchScalarGridSpec(
            num_scalar_prefetch=4,
            grid=(n_tiles, nf),
            in_specs=[pl.BlockSpec((tg, d), lambda i, f, te, tb, nr, nv: (tb[i], 0)),
                      pl.BlockSpec((None, None, d, tf),
                                   lambda i, f, te, tb, nr, nv: (slot, te[i], 0, fidx(i, f, nv))),
                      pl.BlockSpec((None, None, d, tf),
                                   lambda i, f, te, tb, nr, nv: (slot, te[i], 0, nf + fidx(i, f, nv))),
                      pl.BlockSpec((None, None, tf, d),
                                   lambda i, f, te, tb, nr, nv: (slot, te[i], fidx(i, f, nv), 0))],
            out_specs=pl.BlockSpec((tg, d), lambda i, f, te, tb, nr, nv: (i, 0)),
            scratch_shapes=[pltpu.VMEM((tg, d), F32)]),
        compiler_params=_cparams(2),
        name="moe_experts",
    )(tile_expert, tile_block, tile_rows, n_valid, xs, w_gate_up, w_gate_up, w_down)


def _gather_kernel(seg_ref, n16_ref, loc_ref, x_ref, mi_ref, mf_ref, gf_ref, ys_ref, o_ref,
                   *rest, norm):
    if norm == "next":
        nxt_ref, ybuf_ref, sem = rest
    else:
        ybuf_ref, sem = rest
    i = pl.program_id(0)
    tm = x_ref.shape[0]
    rows = ybuf_ref.shape[1]
    slot = lax.rem(i, 2)
    base = i * N_EXPERTS
    tabs = (seg_ref, n16_ref, loc_ref, ys_ref)

    @pl.when(i == 0)
    def _():
        ybuf_ref[...] = jnp.zeros_like(ybuf_ref)
        _segment_copies(base, *tabs, ybuf_ref.at[0], sem.at[0], to_hbm=False, wait=False)

    @pl.when(i + 1 < pl.num_programs(0))
    def _():
        _segment_copies(base + N_EXPERTS, *tabs, ybuf_ref.at[1 - slot], sem.at[1 - slot],
                        to_hbm=False, wait=False)

    mi = mi_ref[...]
    d1, d2 = _slot_rows([mi[:, k:k + 1] for k in range(4)], loc_ref, base)
    col = lax.broadcasted_iota(I32, (tm, rows), 1)
    gates = mf_ref[...]
    _segment_copies(base, *tabs, ybuf_ref.at[slot], sem.at[slot], to_hbm=False, wait=True)
    ybuf = ybuf_ref[slot]
    out = x_ref[...]
    for k, dk in enumerate((d1, d2)):
        yk = jnp.dot((col == dk).astype(BF16), ybuf, preferred_element_type=F32)
        out = out + gates[:, k:k + 1] * yk
    if norm == "final":
        out = _rms_rows(out, gf_ref[...])
    o_ref[...] = out
    if norm == "next":
        nxt_ref[...] = _rms_rows(out, gf_ref[...]).astype(BF16)


def _gather(x, mi, mf, ys, seg, n16, loc, g_norm, norm):
    t, d = x.shape
    tm = SEG_TILE
    row_spec = pl.BlockSpec((tm, d), lambda i, *_: (i, 0))
    out_shape = jax.ShapeDtypeStruct((t, d), F32)
    emit_next = norm == "next"
    return pl.pallas_call(
        functools.partial(_gather_kernel, norm=norm),
        out_shape=(out_shape, jax.ShapeDtypeStruct((t, d), BF16)) if emit_next else out_shape,
        grid_spec=pltpu.PrefetchScalarGridSpec(
            num_scalar_prefetch=3,
            grid=(t // tm,),
            in_specs=[row_spec,
                      pl.BlockSpec((tm, LANES), lambda i, *_: (i, 0)),
                      pl.BlockSpec((tm, LANES), lambda i, *_: (i, 0)),
                      pl.BlockSpec((1, d), lambda i, *_: (0, 0)),
                      pl.BlockSpec(memory_space=pl.ANY)],
            out_specs=(row_spec, row_spec) if emit_next else row_spec,
            scratch_shapes=[pltpu.VMEM((2, SEG_STAGE, d), BF16), pltpu.SemaphoreType.DMA((2,))]),
        compiler_params=_cparams(1),
        name="moe_gather",
    )(seg, n16, loc, x, mi, mf, g_norm.reshape(1, d).astype(F32), ys)


def _moe_seg(a, w_out, out_slot, x, g, router, w_gate_up, w_down, slot, g_norm, norm):
    t, d = x.shape
    tg = MOE_TILE
    x, hn, mi, mf, cnt = _route(a, w_out, out_slot, x, g, router)

    counts = cnt[:, 0, :N_EXPERTS]
    n16 = (counts + SEG_ALIGN - 1) // SEG_ALIGN * SEG_ALIGN
    rows_e = jnp.sum(n16, axis=0)
    tiles_e = (rows_e + tg - 1) // tg
    tile_start = jnp.cumsum(tiles_e) - tiles_e
    row_start = tile_start * tg
    seg = row_start[None, :] + jnp.cumsum(n16, axis=0) - n16
    loc = jnp.cumsum(n16, axis=1) - n16
    n_valid = jnp.sum(tiles_e).astype(I32)
    n_tiles = (2 * t + counts.size * (SEG_ALIGN - 1)) // tg + N_EXPERTS
    n_rows = (n_tiles + 1) * tg
    tidx = jnp.minimum(jnp.arange(n_tiles, dtype=I32), n_valid - 1)
    tile_expert = (jnp.sum(tidx[:, None] >= tile_start[None, :], axis=1) - 1).astype(I32)
    tails = (row_start + rows_e).astype(I32)
    seg, n16, loc = (a.reshape(-1).astype(I32) for a in (seg, n16, loc))

    n_valid = n_valid.reshape(1)
    xs = _scatter(hn, mi, seg, n16, loc, tails, n_valid, n_rows)
    tile_rows = (tails[tile_expert] - tidx * tg).astype(I32)
    ys = _seg_experts(xs, w_gate_up, w_down, slot, tile_expert, tidx, tile_rows, n_valid.reshape(1), n_tiles)
    return _gather(x, mi, mf, ys, seg, n16, loc, g_norm, norm)


def kernel(x, rel_bias, norm_mix, norm_ffn, norm_final, a_w_in, a_w_group, a_scale, a_w_out,
           b_w_in, b_w_out, c_w_in, c_lambda, c_subln, c_w_out, d_w_in, d_sink, d_w_out,
           f_w_gate_up, f_w_down, m_router, m_w_gate_up, m_w_down):
    batch, seq, d = x.shape
    h = x.reshape(batch * seq, d)

    u = _norm_matmul(h, norm_mix[0], a_w_in, 0, F32)
    y = _pool_mix(u, a_w_group, a_scale, 0, batch, seq)
    h = _matmul_res(y, a_w_out, 0, h)
    h = _ffn(h, norm_ffn[0], f_w_gate_up, f_w_down, 0)

    qkv = _dil_proj(h.reshape(batch, seq, d), norm_mix[1], b_w_in, 0, tuple(p[1] for p in DIL_PAIRS))
    o = _dilated_attention(qkv, rel_bias, batch, seq)
    h, hn = _moe_seg(o, b_w_out, 0, h, norm_ffn[1], m_router[0], m_w_gate_up, m_w_down, 0,
                     norm_mix[2], "next")

    qkv = _matmul(hn, c_w_in, 0, BF16)
    o = _diff_attention(qkv, c_lambda, c_subln, 0, rel_bias, batch, seq)
    h = _matmul_res(o, c_w_out, 0, h)
    h, hn = _ffn(h, norm_ffn[2], f_w_gate_up, f_w_down, 1, g_next=norm_mix[3])

    qkv = _matmul(hn, d_w_in, 0, BF16)
    o = _gqa_attention(qkv, d_sink[0], rel_bias, batch, seq)
    h = _moe_seg(o, d_w_out, 0, h, norm_ffn[3], m_router[1], m_w_gate_up, m_w_down, 1,
                 norm_final, "final")
    return h.reshape(batch, seq, d)
```

```python
import collections
import functools
import math

import numpy as np
import jax
import jax.numpy as jnp
from jax import lax
from jax.experimental import pallas as pl
from jax.experimental.pallas import tpu as pltpu

F32 = jnp.float32
BF16 = jnp.bfloat16
I32 = jnp.int32

RMS_EPS = 1e-6
HEAD_DIM = 64
N_BUCKETS = 32
MAX_DISTANCE = 1024
POOL_WINDOWS = (2, 4, 8, 16)
DIL_PAIRS = ((128, 1), (512, 4), (2048, 16))
Q_SCALE = HEAD_DIM ** -0.5
GQA_KV_HEADS = 4
GQA_WINDOW = 128
N_EXPERTS = 8
DIFF_LAYER = 2

LANES = 128
SUBLANES = 8
VMEM_LIMIT = 56 * 1024 * 1024
ROW_TILE = 1024
FF_TILE = 512
MOE_TILE = 1024
MOE_HEIGHTS = 4
ROUTE_ROW_SPLIT = 2
SEG_TILE = 512
ATT_TQ = 128
DIL_UNITS = 4
GQA_UNITS = 1
DIFF_ROW_SPLIT = 4
DIL_HEAD_SPLIT = 2
DIFF_TQ = 512
MASK_VALUE = -1e30


def _cparams(n_axes, vmem=VMEM_LIMIT):
    return pltpu.CompilerParams(dimension_semantics=("arbitrary",) * n_axes,
                                vmem_limit_bytes=vmem)


def _next_pow2(n):
    return 1 << (n - 1).bit_length()


def _rms_rows(x, g):
    ms = jnp.mean(x * x, axis=-1, keepdims=True)
    return x * lax.rsqrt(ms + RMS_EPS) * g


def _norm_matmul_kernel(x_ref, g_ref, w_ref, o_ref, hn_ref):
    @pl.when(pl.program_id(1) == 0)
    def _():
        hn_ref[...] = _rms_rows(x_ref[...], g_ref[...]).astype(BF16)

    o_ref[...] = jnp.dot(hn_ref[...], w_ref[...].astype(BF16),
                         preferred_element_type=F32).astype(o_ref.dtype)


def _norm_matmul(x, g, w, slot, out_dtype):
    t, d = x.shape
    n = w.shape[-1]
    tm = ROW_TILE
    tn = next(c for c in (1024, 768, 512, 256, 128) if n % c == 0)
    return pl.pallas_call(
        _norm_matmul_kernel,
        out_shape=jax.ShapeDtypeStruct((t, n), out_dtype),
        grid=(t // tm, n // tn),
        in_specs=[pl.BlockSpec((tm, d), lambda i, j: (i, 0)),
                  pl.BlockSpec((1, d), lambda i, j: (0, 0)),
                  pl.BlockSpec((None, d, tn), lambda i, j: (slot, 0, j))],
        out_specs=pl.BlockSpec((tm, tn), lambda i, j: (i, j)),
        scratch_shapes=[pltpu.VMEM((tm, d), BF16)],
        compiler_params=_cparams(2),
        name="norm_matmul",
    )(x, g.reshape(1, d), w)


def _matmul_kernel(a_ref, w_ref, o_ref):
    o_ref[...] = jnp.dot(a_ref[...], w_ref[...].astype(BF16),
                         preferred_element_type=F32).astype(o_ref.dtype)


def _matmul(a, w, slot, out_dtype):
    t, d = a.shape
    n = w.shape[-1]
    tm = 2 * ROW_TILE
    tn = next(c for c in (1024, 768, 512, 256, 128) if n % c == 0)
    return pl.pallas_call(
        _matmul_kernel,
        out_shape=jax.ShapeDtypeStruct((t, n), out_dtype),
        grid=(t // tm, n // tn),
        in_specs=[pl.BlockSpec((tm, d), lambda i, j: (i, 0)),
                  pl.BlockSpec((None, d, tn), lambda i, j: (slot, 0, j))],
        out_specs=pl.BlockSpec((tm, tn), lambda i, j: (i, j)),
        compiler_params=_cparams(2),
        name="matmul",
    )(a, w)


def _matmul_res_kernel(a_ref, w_ref, x_ref, o_ref, wb_ref):
    @pl.when(pl.program_id(0) == 0)
    def _():
        wb_ref[...] = w_ref[...].astype(BF16)

    o_ref[...] = x_ref[...] + jnp.dot(a_ref[...], wb_ref[...], preferred_element_type=F32)


def _matmul_res(a, w, slot, x):
    t, k = a.shape
    d = w.shape[-1]
    tm = ROW_TILE
    return pl.pallas_call(
        _matmul_res_kernel,
        out_shape=jax.ShapeDtypeStruct((t, d), F32),
        grid=(t // tm,),
        in_specs=[pl.BlockSpec((tm, k), lambda i: (i, 0)),
                  pl.BlockSpec((None, k, d), lambda i: (slot, 0, 0)),
                  pl.BlockSpec((tm, d), lambda i: (i, 0))],
        out_specs=pl.BlockSpec((tm, d), lambda i: (i, 0)),
        scratch_shapes=[pltpu.VMEM((k, d), BF16)],
        compiler_params=_cparams(1),
        name="matmul_res",
    )(a, w, x)


def _silu_mul(g, u):
    return g * (1.0 / (1.0 + jnp.exp(-g))) * u


def _swiglu_accumulate(h_ref, wg_ref, wu_ref, wd_ref, acc_ref, rows=None):
    rows = h_ref.shape[0] if rows is None else rows
    h = h_ref[pl.ds(0, rows), :]
    gate = jnp.dot(h, wg_ref[...].astype(BF16), preferred_element_type=F32)
    up = jnp.dot(h, wu_ref[...].astype(BF16), preferred_element_type=F32)
    a = _silu_mul(gate, up).astype(BF16)
    acc_ref[pl.ds(0, rows), :] += jnp.dot(a, wd_ref[...].astype(BF16), preferred_element_type=F32)


def _ffn_kernel(x_ref, g_ref, gn_ref, wg_ref, wu_ref, wd_ref, o_ref, *rest, emit_next):
    if emit_next:
        nxt_ref, hn_ref, acc_ref = rest
    else:
        hn_ref, acc_ref = rest
    f = pl.program_id(1)

    @pl.when(f == 0)
    def _():
        hn_ref[...] = _rms_rows(x_ref[...], g_ref[...]).astype(BF16)
        acc_ref[...] = jnp.zeros_like(acc_ref)

    _swiglu_accumulate(hn_ref, wg_ref, wu_ref, wd_ref, acc_ref)

    @pl.when(f == pl.num_programs(1) - 1)
    def _():
        out = x_ref[...] + acc_ref[...]
        o_ref[...] = out
        if emit_next:
            nxt_ref[...] = _rms_rows(out, gn_ref[...]).astype(BF16)


def _ffn(x, g, w_gate_up, w_down, slot, g_next=None):
    t, d = x.shape
    ff = w_down.shape[-2]
    tm, tf = ROW_TILE, FF_TILE
    nf = ff // tf
    emit_next = g_next is not None
    gn = (g_next if emit_next else g).reshape(1, d)
    row_spec = pl.BlockSpec((tm, d), lambda i, f: (i, 0))
    vec_spec = pl.BlockSpec((1, d), lambda i, f: (0, 0))
    out_shape = jax.ShapeDtypeStruct((t, d), F32)
    return pl.pallas_call(
        functools.partial(_ffn_kernel, emit_next=emit_next),
        out_shape=(out_shape, jax.ShapeDtypeStruct((t, d), BF16)) if emit_next else out_shape,
        grid=(t // tm, nf),
        in_specs=[row_spec, vec_spec, vec_spec,
                  pl.BlockSpec((None, d, tf), lambda i, f: (slot, 0, f)),
                  pl.BlockSpec((None, d, tf), lambda i, f: (slot, 0, nf + f)),
                  pl.BlockSpec((None, tf, d), lambda i, f: (slot, f, 0))],
        out_specs=(row_spec, row_spec) if emit_next else row_spec,
        scratch_shapes=[pltpu.VMEM((tm, d), BF16), pltpu.VMEM((tm, d), F32)],
        compiler_params=_cparams(2),
        name="ffn",
    )(x, g.reshape(1, d), gn, w_gate_up, w_gate_up, w_down)


POOL_PAD = 16


def _pool_kernel(u_ref, wg_ref, sc_ref, o_ref, pad_ref):
    s, c = u_ref.shape
    grp = pl.program_id(1)
    pad_ref[pl.ds(0, POOL_PAD), :] = jnp.zeros((POOL_PAD, c), F32)
    pad_ref[pl.ds(POOL_PAD + s, POOL_PAD), :] = jnp.zeros((POOL_PAD, c), F32)
    pad_ref[pl.ds(POOL_PAD, s), :] = u_ref[...]
    wb = wg_ref[...].astype(BF16)
    chunk = 256

    for gi, w in enumerate(POOL_WINDOWS):
        @pl.when(grp == gi)
        def _(w=w):
            for c0 in range(0, s, chunk):
                acc = None
                for dlt in range(-(w // 2), w - w // 2):
                    piece = pad_ref[pl.ds(POOL_PAD + c0 + dlt, chunk), :]
                    acc = piece if acc is None else acc + piece
                pos = c0 + lax.broadcasted_iota(I32, (chunk, 1), 0)
                lo = jnp.maximum(pos - w // 2, 0)
                hi = jnp.minimum(pos + (w - w // 2), s)
                cnt = (hi - lo).astype(F32)
                mixed = acc / cnt - u_ref[pl.ds(c0, chunk), :]
                y = jnp.dot(mixed.astype(BF16), wb, preferred_element_type=F32)
                o_ref[pl.ds(c0, chunk), :] = (y * sc_ref[...]).astype(o_ref.dtype)


def _pool_mix(u, w_group, scale, slot, batch, seq):
    d = u.shape[1]
    ng = len(POOL_WINDOWS)
    c = d // ng
    u3 = u.reshape(batch, seq, d)
    out = pl.pallas_call(
        _pool_kernel,
        out_shape=jax.ShapeDtypeStruct((batch, seq, d), BF16),
        grid=(batch, ng),
        in_specs=[pl.BlockSpec((None, seq, c), lambda b, g: (b, 0, g)),
                  pl.BlockSpec((None, None, c, c), lambda b, g: (slot, g, 0, 0)),
                  pl.BlockSpec((1, c), lambda b, g: (slot, g))],
        out_specs=pl.BlockSpec((None, seq, c), lambda b, g: (b, 0, g)),
        scratch_shapes=[pltpu.VMEM((seq + 2 * POOL_PAD, c), F32)],
        compiler_params=_cparams(2),
        name="pool_mix",
    )(u3, w_group, scale)
    return out.reshape(batch * seq, d)


def _rel_bucket_np(rel):
    half = N_BUCKETS // 2
    max_exact = half // 2
    n = np.abs(rel)
    ratio = np.log(np.maximum(n, 1).astype(np.float32) / np.float32(max_exact))
    big = max_exact + (ratio / np.float32(math.log(MAX_DISTANCE / max_exact))
                       * np.float32(half - max_exact)).astype(np.int32)
    big = np.minimum(big, half - 1)
    return np.where(rel > 0, half, 0) + np.where(n < max_exact, n, big)


def _seg_table(rel_bias, rel, valid):
    bucket = jnp.asarray(_rel_bucket_np(rel).astype(np.int32))
    vals = jnp.take(rel_bias.astype(F32), bucket, axis=0)
    vals = jnp.where(jnp.asarray(valid)[..., None], vals, MASK_VALUE)
    return jnp.transpose(vals, (2, 0, 1))


def _toeplitz(seg_row, rows, cols):
    w = seg_row.shape[1]
    full = jnp.broadcast_to(seg_row, (rows, w))
    rolled = pltpu.roll(full, w - (rows - 1), 1, stride=1, stride_axis=0)
    return rolled[:, :cols]


def _band_variants(hw, n_tiles):
    if n_tiles == 1:
        return (0,)
    return (0, -hw, -2 * hw)


def _band_segs(rel_bias, tq, kw, half, dil, variants, width):
    c = np.arange(width)
    rel = np.stack([r0 + c - (tq - 1) for r0 in variants])
    valid = (np.abs(rel) <= half) & (c[None, :] < tq + kw - 1)
    return _seg_table(rel_bias, rel * dil, valid)


def _band_window(qi, nq, tq, hw, kw, ln):
    if nq == 1:
        return 0, 0
    var = jnp.where(qi == 0, 0, jnp.where(qi == nq - 1, 2, 1))
    return var, jnp.clip(qi * tq - hw, 0, ln - kw)


DilCfg = collections.namedtuple("DilCfg", "dil ln nq hw kw n_var")


def _dil_proj_kernel(x_ref, g_ref, w_ref, o_ref, slab_ref, hn_ref, *, dils):
    grp, c = pl.program_id(1), pl.program_id(2)
    seq, d = x_ref.shape
    n_slabs = d // LANES
    chunk = 256

    @pl.when((grp == 0) & (c == 0))
    def _():
        for c0 in range(0, seq, chunk):
            hn = _rms_rows(x_ref[pl.ds(c0, chunk), :], g_ref[...])
            for s in range(n_slabs):
                slab_ref[s, pl.ds(c0, chunk), :] = hn[:, s * LANES:(s + 1) * LANES]

    for gi, dil in enumerate(dils):
        @pl.when((grp == gi) & (c == 0))
        def _(dil=dil):
            ln = seq // dil
            for r in range(dil):
                for c0 in range(0, ln, chunk):
                    n = min(chunk, ln)
                    rows = [slab_ref[s, pl.ds(r + c0 * dil, n, stride=dil), :] for s in range(n_slabs)]
                    hn_ref[pl.ds(r * ln + c0, n), :] = jnp.concatenate(rows, axis=1).astype(BF16)

    res = jnp.dot(hn_ref[...], w_ref[...].astype(BF16), preferred_element_type=F32)
    res = res * jnp.where(c < pl.num_programs(2) // 3, Q_SCALE, 1.0)
    for s in range(o_ref.shape[0]):
        o_ref[s] = res[:, s * LANES:(s + 1) * LANES].astype(o_ref.dtype)


def _dil_proj(x3, g, w_in, slot, dils):
    batch, seq, d = x3.shape
    n = w_in.shape[-1]
    tn = FF_TILE
    per_group = n // len(dils) // tn
    spt = tn // LANES
    return pl.pallas_call(
        functools.partial(_dil_proj_kernel, dils=dils),
        out_shape=jax.ShapeDtypeStruct((batch, n // LANES, seq, LANES), BF16),
        grid=(batch, len(dils), per_group),
        in_specs=[pl.BlockSpec((None, seq, d), lambda b, g, c: (b, 0, 0)),
                  pl.BlockSpec((1, d), lambda b, g, c: (0, 0)),
                  pl.BlockSpec((None, d, tn), lambda b, g, c: (slot, 0, g * per_group + c))],
        out_specs=pl.BlockSpec((None, spt, seq, LANES), lambda b, g, c: (b, g * per_group + c, 0, 0)),
        scratch_shapes=[pltpu.VMEM((d // LANES, seq, LANES), F32), pltpu.VMEM((seq, d), BF16)],
        compiler_params=_cparams(3),
        name="dil_proj",
    )(x3, g.reshape(1, d), w_in)


def _dil_unit(cfg, unit, q4, k_ref, v_ref, bias_ref, acc_ref, m_ref, l_ref, *, tq, first, last):
    dil, ln, nq, kw = cfg.dil, cfg.ln, cfg.nq, cfg.kw
    n_pairs = q4.shape[0]
    if nq == 1:
        r, qi = unit, 0
    elif dil == 1:
        r, qi = 0, unit
    else:
        r, qi = unit // nq, unit % nq
    var, ks = _band_window(qi, nq, tq, cfg.hw, kw, ln)
    krow = pl.multiple_of(r * ln + ks, cfg.hw)
    if dil == 1:
        rows = pl.ds(pl.multiple_of(qi * tq, tq), tq)
    else:
        rows = pl.ds(qi * tq * dil + r, tq, stride=dil)

    even = lax.broadcasted_iota(I32, (n_pairs, tq, LANES), 2) < HEAD_DIM
    zero = jnp.zeros_like(q4)
    q8 = jnp.concatenate([jnp.where(even, q4, zero), jnp.where(even, zero, q4)], axis=0)
    k4 = k_ref[:, pl.ds(krow, kw), :]
    v4 = v_ref[:, pl.ds(krow, kw), :]
    k8 = jnp.concatenate([k4, k4], axis=0)
    va = jnp.concatenate([v4, jnp.ones_like(v4)], axis=2)
    v8 = jnp.concatenate([va, va], axis=0)
    s = lax.dot_general(q8, k8, (((2,), (2,)), ((0,), (0,))), preferred_element_type=F32)
    s = s + bias_ref[var]
    m8 = jnp.max(s, axis=-1, keepdims=True)
    p = jnp.exp(s - m8).astype(BF16)
    ov = lax.dot_general(p, v8, (((2,), (1,)), ((0,), (0,))), preferred_element_type=F32)
    o_c = jnp.where(even, ov[:n_pairs, :, :LANES], ov[n_pairs:, :, :LANES])
    l_c = jnp.where(even, ov[:n_pairs, :, LANES:], ov[n_pairs:, :, LANES:])
    m_c = jnp.where(even, m8[:n_pairs], m8[n_pairs:])
    if first:
        m_n, l_n, acc_n = m_c, l_c, o_c
    else:
        m_o = jnp.stack([m_ref[pp, rows, :] for pp in range(n_pairs)])
        l_o = jnp.stack([l_ref[pp, rows, :] for pp in range(n_pairs)])
        acc_o = jnp.stack([acc_ref[pp, rows, :] for pp in range(n_pairs)])
        m_n = jnp.maximum(m_o, m_c)
        a_o = jnp.exp(m_o - m_n)
        a_c = jnp.exp(m_c - m_n)
        l_n = a_o * l_o + a_c * l_c
        acc_n = a_o * acc_o + a_c * o_c
    if last:
        acc_n = acc_n / l_n
    for pp in range(n_pairs):
        acc_ref[pp, rows, :] = acc_n[pp]
        if not last:
            m_ref[pp, rows, :] = m_n[pp]
            l_ref[pp, rows, :] = l_n[pp]


def _dil_attn_kernel(*refs, cfgs, tq):
    ng = len(cfgs)
    seg_refs = refs[:ng]
    q_ref, k_ref, v_ref, o_ref = refs[ng:ng + 4]
    bias_refs = refs[ng + 4:2 * ng + 4]
    acc_ref, m_ref, l_ref = refs[2 * ng + 4:]
    hf, b, grp, step = (pl.program_id(i) for i in range(4))
    n_pairs = q_ref.shape[0]

    @pl.when((b == 0) & (grp == 0) & (step == 0))
    def _():
        for gi, cfg in enumerate(cfgs):
            for v in range(cfg.n_var):
                for odd in range(2):
                    for pp in range(n_pairs):
                        head = hf * 2 * n_pairs + 2 * pp + odd
                        row = seg_refs[gi][head, pl.ds(v, 1), :]
                        bias_refs[gi][v, odd * n_pairs + pp] = _toeplitz(row, tq, cfg.kw)

    for gi, cfg in enumerate(cfgs):
        @pl.when(grp == ng - 1 - gi)
        def _(gi=gi, cfg=cfg):
            for uu in range(DIL_UNITS):
                _dil_unit(cfg, step * DIL_UNITS + uu, q_ref[:, pl.ds(uu * tq, tq), :], k_ref, v_ref,
                          bias_refs[gi], acc_ref, m_ref, l_ref, tq=tq, first=gi == ng - 1, last=gi == 0)

    @pl.when((grp == ng - 1) & (step == pl.num_programs(3) - 1))
    def _():
        slabs = [acc_ref[s] for s in range(n_pairs)]
        o_ref[...] = jnp.concatenate(slabs, axis=1).astype(o_ref.dtype)


def _dilated_attention(qkv, rel_bias, batch, seq):
    ng = len(DIL_PAIRS)
    d = qkv.shape[1] * LANES // (3 * ng)
    hd = d // DIL_HEAD_SPLIT
    n_pairs = hd // LANES
    tq = ATT_TQ
    cfgs, segs = [], []
    for win, dil in DIL_PAIRS:
        half = win // (2 * dil)
        ln = seq // dil
        nq = ln // tq
        kw = min(tq + 2 * half, ln)
        variants = _band_variants(half, nq)
        cfgs.append(DilCfg(dil, ln, nq, half, kw, len(variants)))
        segs.append(_band_segs(rel_bias, tq, kw, half, dil, variants, _next_pow2(tq + kw - 1)))
    steps = seq // tq // DIL_UNITS
    cb = DIL_HEAD_SPLIT

    in_specs = [pl.BlockSpec(sg.shape, lambda hf, b, g, s: (0, 0, 0)) for sg in segs]
    in_specs += [pl.BlockSpec((None, n_pairs, DIL_UNITS * tq, LANES),
                              lambda hf, b, g, s: (b, ((ng - 1 - g) * 3) * cb + hf, s, 0)),
                 pl.BlockSpec((None, n_pairs, seq, LANES),
                              lambda hf, b, g, s: (b, ((ng - 1 - g) * 3 + 1) * cb + hf, 0, 0)),
                 pl.BlockSpec((None, n_pairs, seq, LANES),
                              lambda hf, b, g, s: (b, ((ng - 1 - g) * 3 + 2) * cb + hf, 0, 0))]
    scratch = [pltpu.VMEM((c.n_var, 2 * n_pairs, tq, c.kw), F32) for c in cfgs]
    scratch += [pltpu.VMEM((n_pairs, seq, LANES), F32) for _ in range(3)]
    out = pl.pallas_call(
        functools.partial(_dil_attn_kernel, cfgs=tuple(cfgs), tq=tq),
        out_shape=jax.ShapeDtypeStruct((batch, seq, d), BF16),
        grid=(DIL_HEAD_SPLIT, batch, ng, steps),
        in_specs=in_specs,
        out_specs=pl.BlockSpec((None, seq, hd), lambda hf, b, g, s: (b, 0, hf)),
        scratch_shapes=scratch,
        compiler_params=_cparams(4),
        name="dil_attn",
    )(*segs, qkv, qkv, qkv)
    return out.reshape(batch * seq, d)


def _gqa_kernel(sink_ref, seg_ref, q_ref, k_ref, v_ref, o_ref, bias_ref, *, tq, kw, hw, grp):
    b, step = pl.program_id(0), pl.program_id(1)
    seq_len = k_ref.shape[0]
    nq = seq_len // tq
    n_kv = k_ref.shape[1] // HEAD_DIM

    @pl.when((b == 0) & (step == 0))
    def _():
        for v in range(3):
            for kh in range(n_kv):
                for gq in range(grp):
                    bias_ref[v, kh, pl.ds(gq * tq, tq), :] = _toeplitz(
                        seg_ref[kh * grp + gq, pl.ds(v, 1), :], tq, kw)

    row = lax.broadcasted_iota(I32, (grp * tq, 1), 0)
    for u in range(GQA_UNITS):
        qi = step * GQA_UNITS + u
        qrows = pl.ds(u * tq, tq)
        var, ks = _band_window(qi, nq, tq, hw, kw, seq_len)
        ks = pl.multiple_of(ks, hw)
        outs = [None] * (n_kv * grp)
        for kh in range(n_kv):
            cs = slice(kh * HEAD_DIM, (kh + 1) * HEAD_DIM)
            qs = jnp.concatenate(
                [q_ref[qrows, (kh * grp + gq) * HEAD_DIM:(kh * grp + gq + 1) * HEAD_DIM]
                 for gq in range(grp)], axis=0) * Q_SCALE
            kk = k_ref[pl.ds(ks, kw), cs]
            vv = v_ref[pl.ds(ks, kw), cs]
            s = lax.dot_general(qs, kk, (((1,), (1,)), ((), ())), preferred_element_type=F32)
            s = s + bias_ref[var, kh]
            sk = jnp.zeros((grp * tq, 1), F32)
            for gq in range(grp):
                sk = jnp.where((row >= gq * tq) & (row < (gq + 1) * tq), sink_ref[kh * grp + gq], sk)
            m = jnp.maximum(jnp.max(s, axis=-1, keepdims=True), sk)
            e = jnp.exp(s - m).astype(BF16)
            va = jnp.concatenate([vv, jnp.ones_like(vv)], axis=1)
            ov = jnp.dot(e, va, preferred_element_type=F32)
            o = ov[:, :HEAD_DIM] / (ov[:, HEAD_DIM:] + jnp.exp(sk - m))
            for gq in range(grp):
                outs[kh * grp + gq] = o[gq * tq:(gq + 1) * tq, :]
        o_ref[qrows, :] = jnp.concatenate(outs, axis=1).astype(o_ref.dtype)


def _gqa_attention(qkv, sink, rel_bias, batch, seq):
    t, ncol = qkv.shape
    n_q = sink.shape[0]
    d = n_q * HEAD_DIM
    kvw = GQA_KV_HEADS * HEAD_DIM
    grp = n_q // GQA_KV_HEADS
    tq = hw = GQA_WINDOW
    kw = 3 * GQA_WINDOW
    nq = seq // tq
    variants = _band_variants(hw, nq)
    width = _next_pow2(tq + kw - 1)
    seg = _band_segs(rel_bias, tq, kw, GQA_WINDOW, 1, variants, width)
    qkv_v = qkv.reshape(batch, seq, ncol)
    out = pl.pallas_call(
        functools.partial(_gqa_kernel, tq=tq, kw=kw, hw=hw, grp=grp),
        out_shape=jax.ShapeDtypeStruct((batch, seq, d), BF16),
        grid=(batch, nq // GQA_UNITS),
        in_specs=[pl.BlockSpec(memory_space=pltpu.SMEM),
                  pl.BlockSpec((n_q, 3, width), lambda b, qi: (0, 0, 0)),
                  pl.BlockSpec((None, GQA_UNITS * tq, d), lambda b, qi: (b, qi, 0)),
                  pl.BlockSpec((None, seq, kvw), lambda b, qi: (b, 0, d // kvw)),
                  pl.BlockSpec((None, seq, kvw), lambda b, qi: (b, 0, d // kvw + 1))],
        out_specs=pl.BlockSpec((None, GQA_UNITS * tq, d), lambda b, qi: (b, qi, 0)),
        scratch_shapes=[pltpu.VMEM((3, GQA_KV_HEADS, grp * tq, kw), F32)],
        compiler_params=_cparams(2),
        name="gqa_attn",
    )(sink.astype(F32), seg, qkv_v, qkv_v, qkv_v)
    return out.reshape(t, d)


def _diff_kernel(seg_ref, lam_ref, sub_ref, q_ref, k_ref, v_ref, o_ref, bias_ref, *, tq, lam_init):
    h, qi, b = pl.program_id(0), pl.program_id(1), pl.program_id(2)
    seq_len = k_ref.shape[0]
    nk = seq_len // tq

    @pl.when(b == 0)
    def _():
        for j in range(2):
            for ki in range(nk):
                row = seg_ref[h * 2 + j, pl.ds(ki - qi + nk - 1, 1), :]
                bias_ref[j, :, pl.ds(ki * tq, tq)] = _toeplitz(row, tq, tq)

    lv = lam_ref[...]
    s01 = jnp.sum(lv[0:1, :] * lv[1:2, :], axis=-1, keepdims=True)
    s23 = jnp.sum(lv[2:3, :] * lv[3:4, :], axis=-1, keepdims=True)
    lam = jnp.exp(s01) - jnp.exp(s23) + lam_init

    q = q_ref[...] * Q_SCALE
    k = k_ref[...]
    v = v_ref[...]
    va = jnp.concatenate([v, jnp.ones_like(v)], axis=1)
    map0 = lax.broadcasted_iota(I32, q.shape, 1) < HEAD_DIM
    zero = jnp.zeros_like(q)
    qm = (jnp.where(map0, q, zero), jnp.where(map0, zero, q))
    rc = tq // DIFF_ROW_SPLIT
    for c0 in range(0, tq, rc):
        outs = []
        for j in range(2):
            s = lax.dot_general(qm[j][c0:c0 + rc], k, (((1,), (1,)), ((), ())),
                                preferred_element_type=F32)
            s = s + bias_ref[j, pl.ds(c0, rc), :]
            m = jnp.max(s, axis=-1, keepdims=True)
            p = jnp.exp(s - m).astype(BF16)
            ov = jnp.dot(p, va, preferred_element_type=F32)
            outs.append(ov[:, :2 * HEAD_DIM] / ov[:, 2 * HEAD_DIM:])
        o = outs[0] - lam * outs[1]
        o = _rms_rows(o, sub_ref[...]) * (1.0 - lam_init)
        o_ref[pl.ds(c0, rc), :] = o.astype(o_ref.dtype)


def _diff_attention(qkv, lam_vecs, subln, slot, rel_bias, batch, seq):
    t, ncol = qkv.shape
    d = ncol // 3
    hd2 = 2 * HEAD_DIM
    n_heads = d // hd2
    tq = DIFF_TQ
    nk = seq // tq
    width = _next_pow2(2 * tq - 1)
    c = np.arange(width)
    rel = np.stack([(dl - (nk - 1)) * tq + c - (tq - 1) for dl in range(2 * nk - 1)])
    seg = _seg_table(rel_bias, rel, np.ones_like(rel, dtype=bool))
    lam_init = 0.8 - 0.6 * math.exp(-0.3 * DIFF_LAYER)
    qkv_v = qkv.reshape(batch, seq, ncol)
    out = pl.pallas_call(
        functools.partial(_diff_kernel, tq=tq, lam_init=lam_init),
        out_shape=jax.ShapeDtypeStruct((batch, seq, d), BF16),
        grid=(n_heads, seq // tq, batch),
        in_specs=[pl.BlockSpec(seg.shape, lambda h, qi, b: (0, 0, 0)),
                  pl.BlockSpec((None,) + lam_vecs.shape[1:], lambda h, qi, b: (slot, 0, 0)),
                  pl.BlockSpec((1, hd2), lambda h, qi, b: (slot, 0)),
                  pl.BlockSpec((None, tq, hd2), lambda h, qi, b: (b, qi, h)),
                  pl.BlockSpec((None, seq, hd2), lambda h, qi, b: (b, 0, n_heads + h)),
                  pl.BlockSpec((None, seq, hd2), lambda h, qi, b: (b, 0, 2 * n_heads + h))],
        out_specs=pl.BlockSpec((None, tq, hd2), lambda h, qi, b: (b, qi, h)),
        scratch_shapes=[pltpu.VMEM((2, tq, seq), F32)],
        compiler_params=_cparams(3),
        name="diff_attn",
    )(seg, lam_vecs, subln, qkv_v, qkv_v, qkv_v)
    return out.reshape(t, d)


SEG_ALIGN = 16
SEG_SIZES = (512, 256, 128, 64, 32, 16)
SEG_STAGE = 2 * SEG_TILE + N_EXPERTS * SEG_ALIGN


def _route_kernel(a_ref, wo_ref, x_ref, g_ref, r_ref, xn_ref, hn_ref, mi_ref, mf_ref, cnt_ref,
                  tri_ref, wb_ref):
    tm, d = x_ref.shape

    @pl.when(pl.program_id(0) == 0)
    def _():
        rr = lax.broadcasted_iota(I32, (tm, tm), 0)
        cc = lax.broadcasted_iota(I32, (tm, tm), 1)
        tri_ref[...] = (cc < rr).astype(BF16)
        wb_ref[...] = wo_ref[...].astype(BF16)

    r = r_ref[...]
    r_hi = r.astype(BF16)
    r_lo = (r - r_hi.astype(F32)).astype(BF16)
    rows = tm // ROUTE_ROW_SPLIT
    lane = lax.broadcasted_iota(I32, (rows, LANES), 1)

    picks = []
    for r0 in range(0, tm, rows):
        rs = pl.ds(r0, rows)
        xn = x_ref[rs, :] + jnp.dot(a_ref[rs, :], wb_ref[...], preferred_element_type=F32)
        xn_ref[rs, :] = xn
        hn = _rms_rows(xn, g_ref[...])
        h_hi = hn.astype(BF16)
        hn_ref[rs, :] = h_hi
        h_lo = (hn - h_hi.astype(F32)).astype(BF16)
        logits = (jnp.dot(h_hi, r_hi, preferred_element_type=F32)
                  + jnp.dot(h_hi, r_lo, preferred_element_type=F32)
                  + jnp.dot(h_lo, r_hi, preferred_element_type=F32))
        logits = jnp.where(lane < N_EXPERTS, logits, -jnp.inf)
        v1 = jnp.max(logits, axis=-1, keepdims=True)
        i1 = jnp.min(jnp.where(logits == v1, lane, LANES), axis=-1, keepdims=True)
        oh1 = lane == i1
        rest = jnp.where(oh1, -jnp.inf, logits)
        v2 = jnp.max(rest, axis=-1, keepdims=True)
        i2 = jnp.min(jnp.where(rest == v2, lane, LANES), axis=-1, keepdims=True)
        oh2 = lane == i2
        e2 = jnp.exp(v2 - v1)
        mf_ref[rs, :] = jnp.where(lane == 0, 1.0 / (1.0 + e2), jnp.where(lane == 1, e2 / (1.0 + e2), 0.0))
        picks.append((rs, i1, i2, oh1, oh2))

    sel = jnp.concatenate([(oh1 | oh2).astype(BF16) for _, _, _, oh1, oh2 in picks], axis=0)
    for rs, i1, i2, oh1, oh2 in picks:
        before = jnp.dot(tri_ref[rs, :], sel, preferred_element_type=F32)
        rank1 = jnp.sum(jnp.where(oh1, before, 0.0), axis=-1, keepdims=True).astype(I32)
        rank2 = jnp.sum(jnp.where(oh2, before, 0.0), axis=-1, keepdims=True).astype(I32)
        mi_ref[rs, :] = jnp.where(lane == 0, i1, jnp.where(lane == 1, i2,
                                  jnp.where(lane == 2, rank1, jnp.where(lane == 3, rank2, 0))))
    counts = jnp.sum(sel.astype(F32), axis=0, keepdims=True).astype(I32)
    cnt_ref[...] = jnp.broadcast_to(counts, cnt_ref.shape)


def _route(a, w_out, out_slot, x, g, router):
    t, d = x.shape
    k = a.shape[1]
    tm = SEG_TILE
    r_pad = jnp.zeros((d, LANES), F32).at[:, :N_EXPERTS].set(router.astype(F32))
    return pl.pallas_call(
        _route_kernel,
        out_shape=(jax.ShapeDtypeStruct((t, d), F32),
                   jax.ShapeDtypeStruct((t, d), BF16),
                   jax.ShapeDtypeStruct((t, LANES), I32),
                   jax.ShapeDtypeStruct((t, LANES), F32),
                   jax.ShapeDtypeStruct((t // tm, SUBLANES, LANES), I32)),
        grid=(t // tm,),
        in_specs=[pl.BlockSpec((tm, k), lambda i: (i, 0)),
                  pl.BlockSpec((None, k, d), lambda i: (out_slot, 0, 0)),
                  pl.BlockSpec((tm, d), lambda i: (i, 0)),
                  pl.BlockSpec((1, d), lambda i: (0, 0)),
                  pl.BlockSpec((d, LANES), lambda i: (0, 0))],
        out_specs=(pl.BlockSpec((tm, d), lambda i: (i, 0)),
                   pl.BlockSpec((tm, d), lambda i: (i, 0)),
                   pl.BlockSpec((tm, LANES), lambda i: (i, 0)),
                   pl.BlockSpec((tm, LANES), lambda i: (i, 0)),
                   pl.BlockSpec((None, SUBLANES, LANES), lambda i: (i, 0, 0))),
        scratch_shapes=[pltpu.VMEM((tm, tm), BF16), pltpu.VMEM((k, d), BF16)],
        compiler_params=_cparams(1),
        name="moe_route",
    )(a, w_out, x, g.reshape(1, d), r_pad)


def _slot_rows(mi, loc_ref, base):
    e1, e2, d1, d2 = mi[0], mi[1], mi[2], mi[3]
    for e in range(N_EXPERTS):
        off = loc_ref[base + e]
        d1 = d1 + jnp.where(e1 == e, off, 0)
        d2 = d2 + jnp.where(e2 == e, off, 0)
    return d1, d2


def _segment_copies(base, seg_ref, n16_ref, loc_ref, hbm_ref, vmem_ref, sem, *, to_hbm, wait):
    for e in range(N_EXPERTS):
        n16 = n16_ref[base + e]
        hbm0 = seg_ref[base + e]
        vmem0 = loc_ref[base + e]
        off = 0
        for size in SEG_SIZES:
            @pl.when((n16 & size) != 0)
            def _(off=off, size=size):
                h = hbm_ref.at[pl.ds(pl.multiple_of(hbm0 + off, SEG_ALIGN), size)]
                v = vmem_ref.at[pl.ds(pl.multiple_of(vmem0 + off, SEG_ALIGN), size)]
                cp = pltpu.make_async_copy(v, h, sem) if to_hbm else pltpu.make_async_copy(h, v, sem)
                if wait:
                    cp.wait()
                else:
                    cp.start()
            off = off + (n16 & size)


def _scatter_kernel(seg_ref, n16_ref, loc_ref, tail_ref, nv_ref, hn_ref, mi_ref, xs_ref, stage_ref,
                    zero_ref, sem):
    i = pl.program_id(0)
    tm = hn_ref.shape[0]
    rows = stage_ref.shape[1]
    tg = zero_ref.shape[0]
    slot = lax.rem(i, 2)

    @pl.when(i == 0)
    def _():
        zero_ref[...] = jnp.zeros_like(zero_ref)
        for e in range(N_EXPERTS):
            tail = pl.multiple_of(tail_ref[e], SEG_ALIGN)
            pltpu.make_async_copy(zero_ref, xs_ref.at[pl.ds(tail, tg)], sem.at[2]).start()
        for e in range(N_EXPERTS):
            tail = pl.multiple_of(tail_ref[e], SEG_ALIGN)
            pltpu.make_async_copy(zero_ref, xs_ref.at[pl.ds(tail, tg)], sem.at[2]).wait()

        def zero_tile(tile, carry):
            cp = pltpu.make_async_copy(zero_ref, xs_ref.at[pl.ds(pl.multiple_of(tile * tg, tg), tg)],
                                       sem.at[2])
            cp.start()
            cp.wait()
            return carry

        lax.fori_loop(nv_ref[0], xs_ref.shape[0] // tg, zero_tile, 0)

    base = i * N_EXPERTS
    mi_t = mi_ref[...].T
    d1, d2 = _slot_rows([mi_t[k:k + 1, :] for k in range(4)], loc_ref, base)
    row = lax.broadcasted_iota(I32, (rows, tm), 0)
    onehot = ((row == d1) | (row == d2)).astype(BF16)
    stage_ref[slot] = jnp.dot(onehot, hn_ref[...], preferred_element_type=F32).astype(BF16)
    tabs = (seg_ref, n16_ref, loc_ref, xs_ref)
    _segment_copies(base, *tabs, stage_ref.at[slot], sem.at[slot], to_hbm=True, wait=False)

    @pl.when(i > 0)
    def _():
        _segment_copies(base - N_EXPERTS, *tabs, stage_ref.at[1 - slot], sem.at[1 - slot],
                        to_hbm=True, wait=True)

    @pl.when(i == pl.num_programs(0) - 1)
    def _():
        _segment_copies(base, *tabs, stage_ref.at[slot], sem.at[slot], to_hbm=True, wait=True)


def _scatter(hn, mi, seg, n16, loc, tails, n_valid, n_rows):
    t, d = hn.shape
    tm = SEG_TILE
    return pl.pallas_call(
        _scatter_kernel,
        out_shape=jax.ShapeDtypeStruct((n_rows, d), BF16),
        grid_spec=pltpu.PrefetchScalarGridSpec(
            num_scalar_prefetch=5,
            grid=(t // tm,),
            in_specs=[pl.BlockSpec((tm, d), lambda i, *_: (i, 0)),
                      pl.BlockSpec((tm, LANES), lambda i, *_: (i, 0))],
            out_specs=pl.BlockSpec(memory_space=pl.ANY),
            scratch_shapes=[pltpu.VMEM((2, SEG_STAGE, d), BF16), pltpu.VMEM((MOE_TILE, d), BF16),
                            pltpu.SemaphoreType.DMA((3,))]),
        compiler_params=pltpu.CompilerParams(dimension_semantics=("arbitrary",),
                                             vmem_limit_bytes=VMEM_LIMIT, has_side_effects=True),
        name="moe_scatter",
    )(seg, n16, loc, tails, n_valid, hn, mi)


def _seg_expert_kernel(te_ref, tb_ref, nr_ref, nv_ref, xs_ref, wg_ref, wu_ref, wd_ref, ys_ref, acc_ref):
    i, f = pl.program_id(0), pl.program_id(1)
    step = xs_ref.shape[0] // MOE_HEIGHTS

    @pl.when(i < nv_ref[0])
    def _():
        @pl.when(f == 0)
        def _():
            acc_ref[...] = jnp.zeros_like(acc_ref)

        nrow = nr_ref[i]
        for q in range(1, MOE_HEIGHTS + 1):
            lo_ok = nrow > (q - 1) * step
            in_q = lo_ok if q == MOE_HEIGHTS else lo_ok & (nrow <= q * step)

            @pl.when(in_q)
            def _(q=q):
                _swiglu_accumulate(xs_ref, wg_ref, wu_ref, wd_ref, acc_ref, rows=q * step)

        @pl.when(f == pl.num_programs(1) - 1)
        def _():
            ys_ref[...] = acc_ref[...].astype(ys_ref.dtype)

    @pl.when((i >= nv_ref[0]) & (f == pl.num_programs(1) - 1))
    def _():
        ys_ref[...] = jnp.zeros_like(ys_ref)


def _seg_experts(xs, w_gate_up, w_down, slot, tile_expert, tile_block, tile_rows, n_valid, n_tiles):
    n_rows, d = xs.shape
    ff = w_down.shape[-2]
    tg, tf = MOE_TILE, FF_TILE
    nf = ff // tf

    def fidx(i, f, nv):
        return jnp.where(i < nv[0], f, nf - 1)

    return pl.pallas_call(
        _seg_expert_kernel,
        out_shape=jax.ShapeDtypeStruct((n_tiles * tg, d), BF16),
        grid_spec=pltpu.PrefetchScalarGridSpec(
            num_scalar_prefetch=4,
            grid=(n_tiles, nf),
            in_specs=[pl.BlockSpec((tg, d), lambda i, f, te, tb, nr, nv: (tb[i], 0)),
                      pl.BlockSpec((None, None, d, tf),
                                   lambda i, f, te, tb, nr, nv: (slot, te[i], 0, fidx(i, f, nv))),
                      pl.BlockSpec((None, None, d, tf),
                                   lambda i, f, te, tb, nr, nv: (slot, te[i], 0, nf + fidx(i, f, nv))),
                      pl.BlockSpec((None, None, tf, d),
                                   lambda i, f, te, tb, nr, nv: (slot, te[i], fidx(i, f, nv), 0))],
            out_specs=pl.BlockSpec((tg, d), lambda i, f, te, tb, nr, nv: (i, 0)),
            scratch_shapes=[pltpu.VMEM((tg, d), F32)]),
        compiler_params=_cparams(2),
        name="moe_experts",
    )(tile_expert, tile_block, tile_rows, n_valid, xs, w_gate_up, w_gate_up, w_down)


def _gather_kernel(seg_ref, n16_ref, loc_ref, x_ref, mi_ref, mf_ref, gf_ref, ys_ref, o_ref,
                   *rest, norm):
    if norm == "next":
        nxt_ref, ybuf_ref, sem = rest
    else:
        ybuf_ref, sem = rest
    i = pl.program_id(0)
    tm = x_ref.shape[0]
    rows = ybuf_ref.shape[1]
    slot = lax.rem(i, 2)
    base = i * N_EXPERTS
    tabs = (seg_ref, n16_ref, loc_ref, ys_ref)

    @pl.when(i == 0)
    def _():
        ybuf_ref[...] = jnp.zeros_like(ybuf_ref)
        _segment_copies(base, *tabs, ybuf_ref.at[0], sem.at[0], to_hbm=False, wait=False)

    @pl.when(i + 1 < pl.num_programs(0))
    def _():
        _segment_copies(base + N_EXPERTS, *tabs, ybuf_ref.at[1 - slot], sem.at[1 - slot],
                        to_hbm=False, wait=False)

    mi = mi_ref[...]
    d1, d2 = _slot_rows([mi[:, k:k + 1] for k in range(4)], loc_ref, base)
    col = lax.broadcasted_iota(I32, (tm, rows), 1)
    gates = mf_ref[...]
    _segment_copies(base, *tabs, ybuf_ref.at[slot], sem.at[slot], to_hbm=False, wait=True)
    ybuf = ybuf_ref[slot]
    out = x_ref[...]
    for k, dk in enumerate((d1, d2)):
        yk = jnp.dot((col == dk).astype(BF16), ybuf, preferred_element_type=F32)
        out = out + gates[:, k:k + 1] * yk
    if norm == "final":
        out = _rms_rows(out, gf_ref[...])
    o_ref[...] = out
    if norm == "next":
        nxt_ref[...] = _rms_rows(out, gf_ref[...]).astype(BF16)


def _gather(x, mi, mf, ys, seg, n16, loc, g_norm, norm):
    t, d = x.shape
    tm = SEG_TILE
    row_spec = pl.BlockSpec((tm, d), lambda i, *_: (i, 0))
    out_shape = jax.ShapeDtypeStruct((t, d), F32)
    emit_next = norm == "next"
    return pl.pallas_call(
        functools.partial(_gather_kernel, norm=norm),
        out_shape=(out_shape, jax.ShapeDtypeStruct((t, d), BF16)) if emit_next else out_shape,
        grid_spec=pltpu.PrefetchScalarGridSpec(
            num_scalar_prefetch=3,
            grid=(t // tm,),
            in_specs=[row_spec,
                      pl.BlockSpec((tm, LANES), lambda i, *_: (i, 0)),
                      pl.BlockSpec((tm, LANES), lambda i, *_: (i, 0)),
                      pl.BlockSpec((1, d), lambda i, *_: (0, 0)),
                      pl.BlockSpec(memory_space=pl.ANY)],
            out_specs=(row_spec, row_spec) if emit_next else row_spec,
            scratch_shapes=[pltpu.VMEM((2, SEG_STAGE, d), BF16), pltpu.SemaphoreType.DMA((2,))]),
        compiler_params=_cparams(1),
        name="moe_gather",
    )(seg, n16, loc, x, mi, mf, g_norm.reshape(1, d).astype(F32), ys)


def _moe_seg(a, w_out, out_slot, x, g, router, w_gate_up, w_down, slot, g_norm, norm):
    t, d = x.shape
    tg = MOE_TILE
    x, hn, mi, mf, cnt = _route(a, w_out, out_slot, x, g, router)

    counts = cnt[:, 0, :N_EXPERTS]
    n16 = (counts + SEG_ALIGN - 1) // SEG_ALIGN * SEG_ALIGN
    rows_e = jnp.sum(n16, axis=0)
    tiles_e = (rows_e + tg - 1) // tg
    tile_start = jnp.cumsum(tiles_e) - tiles_e
    row_start = tile_start * tg
    seg = row_start[None, :] + jnp.cumsum(n16, axis=0) - n16
    loc = jnp.cumsum(n16, axis=1) - n16
    n_valid = jnp.sum(tiles_e).astype(I32)
    n_tiles = (2 * t + counts.size * (SEG_ALIGN - 1)) // tg + N_EXPERTS
    n_rows = (n_tiles + 1) * tg
    tidx = jnp.minimum(jnp.arange(n_tiles, dtype=I32), n_valid - 1)
    tile_expert = (jnp.sum(tidx[:, None] >= tile_start[None, :], axis=1) - 1).astype(I32)
    tails = (row_start + rows_e).astype(I32)
    seg, n16, loc = (a.reshape(-1).astype(I32) for a in (seg, n16, loc))

    n_valid = n_valid.reshape(1)
    xs = _scatter(hn, mi, seg, n16, loc, tails, n_valid, n_rows)
    tile_rows = (tails[tile_expert] - tidx * tg).astype(I32)
    ys = _seg_experts(xs, w_gate_up, w_down, slot, tile_expert, tidx, tile_rows, n_valid.reshape(1), n_tiles)
    return _gather(x, mi, mf, ys, seg, n16, loc, g_norm, norm)


def kernel(x, rel_bias, norm_mix, norm_ffn, norm_final, a_w_in, a_w_group, a_scale, a_w_out,
           b_w_in, b_w_out, c_w_in, c_lambda, c_subln, c_w_out, d_w_in, d_sink, d_w_out,
           f_w_gate_up, f_w_down, m_router, m_w_gate_up, m_w_down):
    batch, seq, d = x.shape
    h = x.reshape(batch * seq, d)

    u = _norm_matmul(h, norm_mix[0], a_w_in, 0, F32)
    y = _pool_mix(u, a_w_group, a_scale, 0, batch, seq)
    h = _matmul_res(y, a_w_out, 0, h)
    h = _ffn(h, norm_ffn[0], f_w_gate_up, f_w_down, 0)

    qkv = _dil_proj(h.reshape(batch, seq, d), norm_mix[1], b_w_in, 0, tuple(p[1] for p in DIL_PAIRS))
    o = _dilated_attention(qkv, rel_bias, batch, seq)
    h, hn = _moe_seg(o, b_w_out, 0, h, norm_ffn[1], m_router[0], m_w_gate_up, m_w_down, 0,
                     norm_mix[2], "next")

    qkv = _matmul(hn, c_w_in, 0, BF16)
    o = _diff_attention(qkv, c_lambda, c_subln, 0, rel_bias, batch, seq)
    h = _matmul_res(o, c_w_out, 0, h)
    h, hn = _ffn(h, norm_ffn[2], f_w_gate_up, f_w_down, 1, g_next=norm_mix[3])

    qkv = _matmul(hn, d_w_in, 0, BF16)
    o = _gqa_attention(qkv, d_sink[0], rel_bias, batch, seq)
    h = _moe_seg(o, d_w_out, 0, h, norm_ffn[3], m_router[1], m_w_gate_up, m_w_down, 1,
                 norm_final, "final")
    return h.reshape(batch, seq, d)
```

```python
import collections
import functools
import math

import numpy as np
import jax
import jax.numpy as jnp
from jax import lax
from jax.experimental import pallas as pl
from jax.experimental.pallas import tpu as pltpu

F32 = jnp.float32
BF16 = jnp.bfloat16
I32 = jnp.int32

RMS_EPS = 1e-6
HEAD_DIM = 64
N_BUCKETS = 32
MAX_DISTANCE = 1024
POOL_WINDOWS = (2, 4, 8, 16)
DIL_PAIRS = ((128, 1), (512, 4), (2048, 16))
Q_SCALE = HEAD_DIM ** -0.5
GQA_KV_HEADS = 4
GQA_WINDOW = 128
N_EXPERTS = 8
DIFF_LAYER = 2

LANES = 128
SUBLANES = 8
VMEM_LIMIT = 56 * 1024 * 1024
ROW_TILE = 1024
FF_TILE = 512
MOE_TILE = 1024
MOE_HEIGHTS = 4
ROUTE_ROW_SPLIT = 2
SEG_TILE = 512
ATT_TQ = 128
DIL_UNITS = 4
GQA_UNITS = 1
DIFF_ROW_SPLIT = 4
DIL_HEAD_SPLIT = 2
DIFF_TQ = 1024
MASK_VALUE = -1e30


def _cparams(n_axes, vmem=VMEM_LIMIT):
    return pltpu.CompilerParams(dimension_semantics=("arbitrary",) * n_axes,
                                vmem_limit_bytes=vmem)


def _next_pow2(n):
    return 1 << (n - 1).bit_length()


def _rms_rows(x, g):
    ms = jnp.mean(x * x, axis=-1, keepdims=True)
    return x * lax.rsqrt(ms + RMS_EPS) * g


def _norm_matmul_kernel(x_ref, g_ref, w_ref, o_ref, hn_ref):
    @pl.when(pl.program_id(1) == 0)
    def _():
        hn_ref[...] = _rms_rows(x_ref[...], g_ref[...]).astype(BF16)

    o_ref[...] = jnp.dot(hn_ref[...], w_ref[...].astype(BF16),
                         preferred_element_type=F32).astype(o_ref.dtype)


def _norm_matmul(x, g, w, slot, out_dtype):
    t, d = x.shape
    n = w.shape[-1]
    tm = ROW_TILE
    tn = next(c for c in (1024, 768, 512, 256, 128) if n % c == 0)
    return pl.pallas_call(
        _norm_matmul_kernel,
        out_shape=jax.ShapeDtypeStruct((t, n), out_dtype),
        grid=(t // tm, n // tn),
        in_specs=[pl.BlockSpec((tm, d), lambda i, j: (i, 0)),
                  pl.BlockSpec((1, d), lambda i, j: (0, 0)),
                  pl.BlockSpec((None, d, tn), lambda i, j: (slot, 0, j))],
        out_specs=pl.BlockSpec((tm, tn), lambda i, j: (i, j)),
        scratch_shapes=[pltpu.VMEM((tm, d), BF16)],
        compiler_params=_cparams(2),
        name="norm_matmul",
    )(x, g.reshape(1, d), w)


def _matmul_kernel(a_ref, w_ref, o_ref):
    o_ref[...] = jnp.dot(a_ref[...], w_ref[...].astype(BF16),
                         preferred_element_type=F32).astype(o_ref.dtype)


def _matmul(a, w, slot, out_dtype):
    t, d = a.shape
    n = w.shape[-1]
    tm = 2 * ROW_TILE
    tn = next(c for c in (1024, 768, 512, 256, 128) if n % c == 0)
    return pl.pallas_call(
        _matmul_kernel,
        out_shape=jax.ShapeDtypeStruct((t, n), out_dtype),
        grid=(t // tm, n // tn),
        in_specs=[pl.BlockSpec((tm, d), lambda i, j: (i, 0)),
                  pl.BlockSpec((None, d, tn), lambda i, j: (slot, 0, j))],
        out_specs=pl.BlockSpec((tm, tn), lambda i, j: (i, j)),
        compiler_params=_cparams(2),
        name="matmul",
    )(a, w)


def _matmul_res_kernel(a_ref, w_ref, x_ref, o_ref, wb_ref):
    @pl.when(pl.program_id(0) == 0)
    def _():
        wb_ref[...] = w_ref[...].astype(BF16)

    o_ref[...] = x_ref[...] + jnp.dot(a_ref[...], wb_ref[...], preferred_element_type=F32)


def _matmul_res(a, w, slot, x):
    t, k = a.shape
    d = w.shape[-1]
    tm = ROW_TILE
    return pl.pallas_call(
        _matmul_res_kernel,
        out_shape=jax.ShapeDtypeStruct((t, d), F32),
        grid=(t // tm,),
        in_specs=[pl.BlockSpec((tm, k), lambda i: (i, 0)),
                  pl.BlockSpec((None, k, d), lambda i: (slot, 0, 0)),
                  pl.BlockSpec((tm, d), lambda i: (i, 0))],
        out_specs=pl.BlockSpec((tm, d), lambda i: (i, 0)),
        scratch_shapes=[pltpu.VMEM((k, d), BF16)],
        compiler_params=_cparams(1),
        name="matmul_res",
    )(a, w, x)


def _silu_mul(g, u):
    return g * (1.0 / (1.0 + jnp.exp(-g))) * u


def _swiglu_accumulate(h_ref, wg_ref, wu_ref, wd_ref, acc_ref, rows=None):
    rows = h_ref.shape[0] if rows is None else rows
    h = h_ref[pl.ds(0, rows), :]
    gate = jnp.dot(h, wg_ref[...].astype(BF16), preferred_element_type=F32)
    up = jnp.dot(h, wu_ref[...].astype(BF16), preferred_element_type=F32)
    a = _silu_mul(gate, up).astype(BF16)
    acc_ref[pl.ds(0, rows), :] += jnp.dot(a, wd_ref[...].astype(BF16), preferred_element_type=F32)


def _ffn_kernel(x_ref, g_ref, gn_ref, wg_ref, wu_ref, wd_ref, o_ref, *rest, emit_next):
    if emit_next:
        nxt_ref, hn_ref, acc_ref = rest
    else:
        hn_ref, acc_ref = rest
    f = pl.program_id(1)

    @pl.when(f == 0)
    def _():
        hn_ref[...] = _rms_rows(x_ref[...], g_ref[...]).astype(BF16)
        acc_ref[...] = jnp.zeros_like(acc_ref)

    _swiglu_accumulate(hn_ref, wg_ref, wu_ref, wd_ref, acc_ref)

    @pl.when(f == pl.num_programs(1) - 1)
    def _():
        out = x_ref[...] + acc_ref[...]
        o_ref[...] = out
        if emit_next:
            nxt_ref[...] = _rms_rows(out, gn_ref[...]).astype(BF16)


def _ffn(x, g, w_gate_up, w_down, slot, g_next=None):
    t, d = x.shape
    ff = w_down.shape[-2]
    tm, tf = ROW_TILE, FF_TILE
    nf = ff // tf
    emit_next = g_next is not None
    gn = (g_next if emit_next else g).reshape(1, d)
    row_spec = pl.BlockSpec((tm, d), lambda i, f: (i, 0))
    vec_spec = pl.BlockSpec((1, d), lambda i, f: (0, 0))
    out_shape = jax.ShapeDtypeStruct((t, d), F32)
    return pl.pallas_call(
        functools.partial(_ffn_kernel, emit_next=emit_next),
        out_shape=(out_shape, jax.ShapeDtypeStruct((t, d), BF16)) if emit_next else out_shape,
        grid=(t // tm, nf),
        in_specs=[row_spec, vec_spec, vec_spec,
                  pl.BlockSpec((None, d, tf), lambda i, f: (slot, 0, f)),
                  pl.BlockSpec((None, d, tf), lambda i, f: (slot, 0, nf + f)),
                  pl.BlockSpec((None, tf, d), lambda i, f: (slot, f, 0))],
        out_specs=(row_spec, row_spec) if emit_next else row_spec,
        scratch_shapes=[pltpu.VMEM((tm, d), BF16), pltpu.VMEM((tm, d), F32)],
        compiler_params=_cparams(2),
        name="ffn",
    )(x, g.reshape(1, d), gn, w_gate_up, w_gate_up, w_down)


POOL_PAD = 16


def _pool_kernel(u_ref, wg_ref, sc_ref, o_ref, pad_ref):
    s, c = u_ref.shape
    grp = pl.program_id(1)
    pad_ref[pl.ds(0, POOL_PAD), :] = jnp.zeros((POOL_PAD, c), F32)
    pad_ref[pl.ds(POOL_PAD + s, POOL_PAD), :] = jnp.zeros((POOL_PAD, c), F32)
    pad_ref[pl.ds(POOL_PAD, s), :] = u_ref[...]
    wb = wg_ref[...].astype(BF16)
    chunk = 256

    for gi, w in enumerate(POOL_WINDOWS):
        @pl.when(grp == gi)
        def _(w=w):
            for c0 in range(0, s, chunk):
                acc = None
                for dlt in range(-(w // 2), w - w // 2):
                    piece = pad_ref[pl.ds(POOL_PAD + c0 + dlt, chunk), :]
                    acc = piece if acc is None else acc + piece
                pos = c0 + lax.broadcasted_iota(I32, (chunk, 1), 0)
                lo = jnp.maximum(pos - w // 2, 0)
                hi = jnp.minimum(pos + (w - w // 2), s)
                cnt = (hi - lo).astype(F32)
                mixed = acc / cnt - u_ref[pl.ds(c0, chunk), :]
                y = jnp.dot(mixed.astype(BF16), wb, preferred_element_type=F32)
                o_ref[pl.ds(c0, chunk), :] = (y * sc_ref[...]).astype(o_ref.dtype)


def _pool_mix(u, w_group, scale, slot, batch, seq):
    d = u.shape[1]
    ng = len(POOL_WINDOWS)
    c = d // ng
    u3 = u.reshape(batch, seq, d)
    out = pl.pallas_call(
        _pool_kernel,
        out_shape=jax.ShapeDtypeStruct((batch, seq, d), BF16),
        grid=(batch, ng),
        in_specs=[pl.BlockSpec((None, seq, c), lambda b, g: (b, 0, g)),
                  pl.BlockSpec((None, None, c, c), lambda b, g: (slot, g, 0, 0)),
                  pl.BlockSpec((1, c), lambda b, g: (slot, g))],
        out_specs=pl.BlockSpec((None, seq, c), lambda b, g: (b, 0, g)),
        scratch_shapes=[pltpu.VMEM((seq + 2 * POOL_PAD, c), F32)],
        compiler_params=_cparams(2),
        name="pool_mix",
    )(u3, w_group, scale)
    return out.reshape(batch * seq, d)


def _rel_bucket_np(rel):
    half = N_BUCKETS // 2
    max_exact = half // 2
    n = np.abs(rel)
    ratio = np.log(np.maximum(n, 1).astype(np.float32) / np.float32(max_exact))
    big = max_exact + (ratio / np.float32(math.log(MAX_DISTANCE / max_exact))
                       * np.float32(half - max_exact)).astype(np.int32)
    big = np.minimum(big, half - 1)
    return np.where(rel > 0, half, 0) + np.where(n < max_exact, n, big)


def _seg_table(rel_bias, rel, valid):
    bucket = jnp.asarray(_rel_bucket_np(rel).astype(np.int32))
    vals = jnp.take(rel_bias.astype(F32), bucket, axis=0)
    vals = jnp.where(jnp.asarray(valid)[..., None], vals, MASK_VALUE)
    return jnp.transpose(vals, (2, 0, 1))


def _toeplitz(seg_row, rows, cols):
    w = seg_row.shape[1]
    full = jnp.broadcast_to(seg_row, (rows, w))
    rolled = pltpu.roll(full, w - (rows - 1), 1, stride=1, stride_axis=0)
    return rolled[:, :cols]


def _band_variants(hw, n_tiles):
    if n_tiles == 1:
        return (0,)
    return (0, -hw, -2 * hw)


def _band_segs(rel_bias, tq, kw, half, dil, variants, width):
    c = np.arange(width)
    rel = np.stack([r0 + c - (tq - 1) for r0 in variants])
    valid = (np.abs(rel) <= half) & (c[None, :] < tq + kw - 1)
    return _seg_table(rel_bias, rel * dil, valid)


def _band_window(qi, nq, tq, hw, kw, ln):
    if nq == 1:
        return 0, 0
    var = jnp.where(qi == 0, 0, jnp.where(qi == nq - 1, 2, 1))
    return var, jnp.clip(qi * tq - hw, 0, ln - kw)


DilCfg = collections.namedtuple("DilCfg", "dil ln nq hw kw n_var")


def _dil_proj_kernel(x_ref, g_ref, w_ref, o_ref, slab_ref, hn_ref, *, dils):
    grp, c = pl.program_id(1), pl.program_id(2)
    seq, d = x_ref.shape
    n_slabs = d // LANES
    chunk = 256

    @pl.when((grp == 0) & (c == 0))
    def _():
        for c0 in range(0, seq, chunk):
            hn = _rms_rows(x_ref[pl.ds(c0, chunk), :], g_ref[...])
            for s in range(n_slabs):
                slab_ref[s, pl.ds(c0, chunk), :] = hn[:, s * LANES:(s + 1) * LANES]

    for gi, dil in enumerate(dils):
        @pl.when((grp == gi) & (c == 0))
        def _(dil=dil):
            ln = seq // dil
            for r in range(dil):
                for c0 in range(0, ln, chunk):
                    n = min(chunk, ln)
                    rows = [slab_ref[s, pl.ds(r + c0 * dil, n, stride=dil), :] for s in range(n_slabs)]
                    hn_ref[pl.ds(r * ln + c0, n), :] = jnp.concatenate(rows, axis=1).astype(BF16)

    res = jnp.dot(hn_ref[...], w_ref[...].astype(BF16), preferred_element_type=F32)
    res = res * jnp.where(c < pl.num_programs(2) // 3, Q_SCALE, 1.0)
    for s in range(o_ref.shape[0]):
        o_ref[s] = res[:, s * LANES:(s + 1) * LANES].astype(o_ref.dtype)


def _dil_proj(x3, g, w_in, slot, dils):
    batch, seq, d = x3.shape
    n = w_in.shape[-1]
    tn = FF_TILE
    per_group = n // len(dils) // tn
    spt = tn // LANES
    return pl.pallas_call(
        functools.partial(_dil_proj_kernel, dils=dils),
        out_shape=jax.ShapeDtypeStruct((batch, n // LANES, seq, LANES), BF16),
        grid=(batch, len(dils), per_group),
        in_specs=[pl.BlockSpec((None, seq, d), lambda b, g, c: (b, 0, 0)),
                  pl.BlockSpec((1, d), lambda b, g, c: (0, 0)),
                  pl.BlockSpec((None, d, tn), lambda b, g, c: (slot, 0, g * per_group + c))],
        out_specs=pl.BlockSpec((None, spt, seq, LANES), lambda b, g, c: (b, g * per_group + c, 0, 0)),
        scratch_shapes=[pltpu.VMEM((d // LANES, seq, LANES), F32), pltpu.VMEM((seq, d), BF16)],
        compiler_params=_cparams(3),
        name="dil_proj",
    )(x3, g.reshape(1, d), w_in)


def _dil_unit(cfg, unit, q4, k_ref, v_ref, bias_ref, acc_ref, m_ref, l_ref, *, tq, first, last):
    dil, ln, nq, kw = cfg.dil, cfg.ln, cfg.nq, cfg.kw
    n_pairs = q4.shape[0]
    if nq == 1:
        r, qi = unit, 0
    elif dil == 1:
        r, qi = 0, unit
    else:
        r, qi = unit // nq, unit % nq
    var, ks = _band_window(qi, nq, tq, cfg.hw, kw, ln)
    krow = pl.multiple_of(r * ln + ks, cfg.hw)
    if dil == 1:
        rows = pl.ds(pl.multiple_of(qi * tq, tq), tq)
    else:
        rows = pl.ds(qi * tq * dil + r, tq, stride=dil)

    even = lax.broadcasted_iota(I32, (n_pairs, tq, LANES), 2) < HEAD_DIM
    zero = jnp.zeros_like(q4)
    q8 = jnp.concatenate([jnp.where(even, q4, zero), jnp.where(even, zero, q4)], axis=0)
    k4 = k_ref[:, pl.ds(krow, kw), :]
    v4 = v_ref[:, pl.ds(krow, kw), :]
    k8 = jnp.concatenate([k4, k4], axis=0)
    va = jnp.concatenate([v4, jnp.ones_like(v4)], axis=2)
    v8 = jnp.concatenate([va, va], axis=0)
    s = lax.dot_general(q8, k8, (((2,), (2,)), ((0,), (0,))), preferred_element_type=F32)
    s = s + bias_ref[var]
    m8 = jnp.max(s, axis=-1, keepdims=True)
    p = jnp.exp(s - m8).astype(BF16)
    ov = lax.dot_general(p, v8, (((2,), (1,)), ((0,), (0,))), preferred_element_type=F32)
    o_c = jnp.where(even, ov[:n_pairs, :, :LANES], ov[n_pairs:, :, :LANES])
    l_c = jnp.where(even, ov[:n_pairs, :, LANES:], ov[n_pairs:, :, LANES:])
    m_c = jnp.where(even, m8[:n_pairs], m8[n_pairs:])
    if first:
        m_n, l_n, acc_n = m_c, l_c, o_c
    else:
        m_o = jnp.stack([m_ref[pp, rows, :] for pp in range(n_pairs)])
        l_o = jnp.stack([l_ref[pp, rows, :] for pp in range(n_pairs)])
        acc_o = jnp.stack([acc_ref[pp, rows, :] for pp in range(n_pairs)])
        m_n = jnp.maximum(m_o, m_c)
        a_o = jnp.exp(m_o - m_n)
        a_c = jnp.exp(m_c - m_n)
        l_n = a_o * l_o + a_c * l_c
        acc_n = a_o * acc_o + a_c * o_c
    if last:
        acc_n = acc_n / l_n
    for pp in range(n_pairs):
        acc_ref[pp, rows, :] = acc_n[pp]
        if not last:
            m_ref[pp, rows, :] = m_n[pp]
            l_ref[pp, rows, :] = l_n[pp]


def _dil_attn_kernel(*refs, cfgs, tq):
    ng = len(cfgs)
    seg_refs = refs[:ng]
    q_ref, k_ref, v_ref, o_ref = refs[ng:ng + 4]
    bias_refs = refs[ng + 4:2 * ng + 4]
    acc_ref, m_ref, l_ref = refs[2 * ng + 4:]
    hf, b, grp, step = (pl.program_id(i) for i in range(4))
    n_pairs = q_ref.shape[0]

    @pl.when((b == 0) & (grp == 0) & (step == 0))
    def _():
        for gi, cfg in enumerate(cfgs):
            for v in range(cfg.n_var):
                for odd in range(2):
                    for pp in range(n_pairs):
                        head = hf * 2 * n_pairs + 2 * pp + odd
                        row = seg_refs[gi][head, pl.ds(v, 1), :]
                        bias_refs[gi][v, odd * n_pairs + pp] = _toeplitz(row, tq, cfg.kw)

    for gi, cfg in enumerate(cfgs):
        @pl.when(grp == gi)
        def _(gi=gi, cfg=cfg):
            for uu in range(DIL_UNITS):
                _dil_unit(cfg, step * DIL_UNITS + uu, q_ref[:, pl.ds(uu * tq, tq), :], k_ref, v_ref,
                          bias_refs[gi], acc_ref, m_ref, l_ref, tq=tq, first=gi == 0, last=gi == ng - 1)

    @pl.when((grp == ng - 1) & (step == pl.num_programs(3) - 1))
    def _():
        slabs = [acc_ref[s] for s in range(n_pairs)]
        o_ref[...] = jnp.concatenate(slabs, axis=1).astype(o_ref.dtype)


def _dilated_attention(qkv, rel_bias, batch, seq):
    ng = len(DIL_PAIRS)
    d = qkv.shape[1] * LANES // (3 * ng)
    hd = d // DIL_HEAD_SPLIT
    n_pairs = hd // LANES
    tq = ATT_TQ
    cfgs, segs = [], []
    for win, dil in DIL_PAIRS:
        half = win // (2 * dil)
        ln = seq // dil
        nq = ln // tq
        kw = min(tq + 2 * half, ln)
        variants = _band_variants(half, nq)
        cfgs.append(DilCfg(dil, ln, nq, half, kw, len(variants)))
        segs.append(_band_segs(rel_bias, tq, kw, half, dil, variants, _next_pow2(tq + kw - 1)))
    steps = seq // tq // DIL_UNITS
    cb = DIL_HEAD_SPLIT

    in_specs = [pl.BlockSpec(sg.shape, lambda hf, b, g, s: (0, 0, 0)) for sg in segs]
    in_specs += [pl.BlockSpec((None, n_pairs, DIL_UNITS * tq, LANES),
                              lambda hf, b, g, s: (b, (g * 3) * cb + hf, s, 0)),
                 pl.BlockSpec((None, n_pairs, seq, LANES),
                              lambda hf, b, g, s: (b, (g * 3 + 1) * cb + hf, 0, 0)),
                 pl.BlockSpec((None, n_pairs, seq, LANES),
                              lambda hf, b, g, s: (b, (g * 3 + 2) * cb + hf, 0, 0))]
    scratch = [pltpu.VMEM((c.n_var, 2 * n_pairs, tq, c.kw), F32) for c in cfgs]
    scratch += [pltpu.VMEM((n_pairs, seq, LANES), F32) for _ in range(3)]
    out = pl.pallas_call(
        functools.partial(_dil_attn_kernel, cfgs=tuple(cfgs), tq=tq),
        out_shape=jax.ShapeDtypeStruct((batch, seq, d), BF16),
        grid=(DIL_HEAD_SPLIT, batch, ng, steps),
        in_specs=in_specs,
        out_specs=pl.BlockSpec((None, seq, hd), lambda hf, b, g, s: (b, 0, hf)),
        scratch_shapes=scratch,
        compiler_params=_cparams(4),
        name="dil_attn",
    )(*segs, qkv, qkv, qkv)
    return out.reshape(batch * seq, d)


def _gqa_kernel(sink_ref, seg_ref, q_ref, k_ref, v_ref, o_ref, bias_ref, *, tq, kw, hw, grp):
    b, step = pl.program_id(0), pl.program_id(1)
    seq_len = k_ref.shape[0]
    nq = seq_len // tq
    n_kv = k_ref.shape[1] // HEAD_DIM

    @pl.when((b == 0) & (step == 0))
    def _():
        for v in range(3):
            for kh in range(n_kv):
                for gq in range(grp):
                    bias_ref[v, kh, pl.ds(gq * tq, tq), :] = _toeplitz(
                        seg_ref[kh * grp + gq, pl.ds(v, 1), :], tq, kw)

    row = lax.broadcasted_iota(I32, (grp * tq, 1), 0)
    for u in range(GQA_UNITS):
        qi = step * GQA_UNITS + u
        qrows = pl.ds(u * tq, tq)
        var, ks = _band_window(qi, nq, tq, hw, kw, seq_len)
        ks = pl.multiple_of(ks, hw)
        outs = [None] * (n_kv * grp)
        for kh in range(n_kv):
            cs = slice(kh * HEAD_DIM, (kh + 1) * HEAD_DIM)
            qs = jnp.concatenate(
                [q_ref[qrows, (kh * grp + gq) * HEAD_DIM:(kh * grp + gq + 1) * HEAD_DIM]
                 for gq in range(grp)], axis=0) * Q_SCALE
            kk = k_ref[pl.ds(ks, kw), cs]
            vv = v_ref[pl.ds(ks, kw), cs]
            s = lax.dot_general(qs, kk, (((1,), (1,)), ((), ())), preferred_element_type=F32)
            s = s + bias_ref[var, kh]
            sk = jnp.zeros((grp * tq, 1), F32)
            for gq in range(grp):
                sk = jnp.where((row >= gq * tq) & (row < (gq + 1) * tq), sink_ref[kh * grp + gq], sk)
            m = jnp.maximum(jnp.max(s, axis=-1, keepdims=True), sk)
            e = jnp.exp(s - m).astype(BF16)
            va = jnp.concatenate([vv, jnp.ones_like(vv)], axis=1)
            ov = jnp.dot(e, va, preferred_element_type=F32)
            o = ov[:, :HEAD_DIM] / (ov[:, HEAD_DIM:] + jnp.exp(sk - m))
            for gq in range(grp):
                outs[kh * grp + gq] = o[gq * tq:(gq + 1) * tq, :]
        o_ref[qrows, :] = jnp.concatenate(outs, axis=1).astype(o_ref.dtype)


def _gqa_attention(qkv, sink, rel_bias, batch, seq):
    t, ncol = qkv.shape
    n_q = sink.shape[0]
    d = n_q * HEAD_DIM
    kvw = GQA_KV_HEADS * HEAD_DIM
    grp = n_q // GQA_KV_HEADS
    tq = hw = GQA_WINDOW
    kw = 3 * GQA_WINDOW
    nq = seq // tq
    variants = _band_variants(hw, nq)
    width = _next_pow2(tq + kw - 1)
    seg = _band_segs(rel_bias, tq, kw, GQA_WINDOW, 1, variants, width)
    qkv_v = qkv.reshape(batch, seq, ncol)
    out = pl.pallas_call(
        functools.partial(_gqa_kernel, tq=tq, kw=kw, hw=hw, grp=grp),
        out_shape=jax.ShapeDtypeStruct((batch, seq, d), BF16),
        grid=(batch, nq // GQA_UNITS),
        in_specs=[pl.BlockSpec(memory_space=pltpu.SMEM),
                  pl.BlockSpec((n_q, 3, width), lambda b, qi: (0, 0, 0)),
                  pl.BlockSpec((None, GQA_UNITS * tq, d), lambda b, qi: (b, qi, 0)),
                  pl.BlockSpec((None, seq, kvw), lambda b, qi: (b, 0, d // kvw)),
                  pl.BlockSpec((None, seq, kvw), lambda b, qi: (b, 0, d // kvw + 1))],
        out_specs=pl.BlockSpec((None, GQA_UNITS * tq, d), lambda b, qi: (b, qi, 0)),
        scratch_shapes=[pltpu.VMEM((3, GQA_KV_HEADS, grp * tq, kw), F32)],
        compiler_params=_cparams(2),
        name="gqa_attn",
    )(sink.astype(F32), seg, qkv_v, qkv_v, qkv_v)
    return out.reshape(t, d)


def _diff_kernel(seg_ref, lam_ref, sub_ref, q_ref, k_ref, v_ref, o_ref, bias_ref, *, tq, lam_init):
    h, qi, b = pl.program_id(0), pl.program_id(1), pl.program_id(2)
    seq_len = k_ref.shape[0]
    nk = seq_len // tq

    @pl.when(b == 0)
    def _():
        for j in range(2):
            for ki in range(nk):
                row = seg_ref[h * 2 + j, pl.ds(ki - qi + nk - 1, 1), :]
                bias_ref[j, :, pl.ds(ki * tq, tq)] = _toeplitz(row, tq, tq)

    lv = lam_ref[...]
    s01 = jnp.sum(lv[0:1, :] * lv[1:2, :], axis=-1, keepdims=True)
    s23 = jnp.sum(lv[2:3, :] * lv[3:4, :], axis=-1, keepdims=True)
    lam = jnp.exp(s01) - jnp.exp(s23) + lam_init

    q = q_ref[...] * Q_SCALE
    k = k_ref[...]
    v = v_ref[...]
    va = jnp.concatenate([v, jnp.ones_like(v)], axis=1)
    map0 = lax.broadcasted_iota(I32, q.shape, 1) < HEAD_DIM
    zero = jnp.zeros_like(q)
    qm = (jnp.where(map0, q, zero), jnp.where(map0, zero, q))
    rc = tq // DIFF_ROW_SPLIT
    for c0 in range(0, tq, rc):
        outs = []
        for j in range(2):
            s = lax.dot_general(qm[j][c0:c0 + rc], k, (((1,), (1,)), ((), ())),
                                preferred_element_type=F32)
            s = s + bias_ref[j, pl.ds(c0, rc), :]
            m = jnp.max(s, axis=-1, keepdims=True)
            p = jnp.exp(s - m).astype(BF16)
            ov = jnp.dot(p, va, preferred_element_type=F32)
            outs.append(ov[:, :2 * HEAD_DIM] / ov[:, 2 * HEAD_DIM:])
        o = outs[0] - lam * outs[1]
        o = _rms_rows(o, sub_ref[...]) * (1.0 - lam_init)
        o_ref[pl.ds(c0, rc), :] = o.astype(o_ref.dtype)


def _diff_attention(qkv, lam_vecs, subln, slot, rel_bias, batch, seq):
    t, ncol = qkv.shape
    d = ncol // 3
    hd2 = 2 * HEAD_DIM
    n_heads = d // hd2
    tq = DIFF_TQ
    nk = seq // tq
    width = _next_pow2(2 * tq - 1)
    c = np.arange(width)
    rel = np.stack([(dl - (nk - 1)) * tq + c - (tq - 1) for dl in range(2 * nk - 1)])
    seg = _seg_table(rel_bias, rel, np.ones_like(rel, dtype=bool))
    lam_init = 0.8 - 0.6 * math.exp(-0.3 * DIFF_LAYER)
    qkv_v = qkv.reshape(batch, seq, ncol)
    out = pl.pallas_call(
        functools.partial(_diff_kernel, tq=tq, lam_init=lam_init),
        out_shape=jax.ShapeDtypeStruct((batch, seq, d), BF16),
        grid=(n_heads, seq // tq, batch),
        in_specs=[pl.BlockSpec(seg.shape, lambda h, qi, b: (0, 0, 0)),
                  pl.BlockSpec((None,) + lam_vecs.shape[1:], lambda h, qi, b: (slot, 0, 0)),
                  pl.BlockSpec((1, hd2), lambda h, qi, b: (slot, 0)),
                  pl.BlockSpec((None, tq, hd2), lambda h, qi, b: (b, qi, h)),
                  pl.BlockSpec((None, seq, hd2), lambda h, qi, b: (b, 0, n_heads + h)),
                  pl.BlockSpec((None, seq, hd2), lambda h, qi, b: (b, 0, 2 * n_heads + h))],
        out_specs=pl.BlockSpec((None, tq, hd2), lambda h, qi, b: (b, qi, h)),
        scratch_shapes=[pltpu.VMEM((2, tq, seq), F32)],
        compiler_params=_cparams(3),
        name="diff_attn",
    )(seg, lam_vecs, subln, qkv_v, qkv_v, qkv_v)
    return out.reshape(t, d)


SEG_ALIGN = 16
SEG_SIZES = (512, 256, 128, 64, 32, 16)
SEG_STAGE = 2 * SEG_TILE + N_EXPERTS * SEG_ALIGN


def _route_kernel(a_ref, wo_ref, x_ref, g_ref, r_ref, xn_ref, hn_ref, mi_ref, mf_ref, cnt_ref,
                  tri_ref, wb_ref):
    tm, d = x_ref.shape

    @pl.when(pl.program_id(0) == 0)
    def _():
        rr = lax.broadcasted_iota(I32, (tm, tm), 0)
        cc = lax.broadcasted_iota(I32, (tm, tm), 1)
        tri_ref[...] = (cc < rr).astype(BF16)
        wb_ref[...] = wo_ref[...].astype(BF16)

    r = r_ref[...]
    r_hi = r.astype(BF16)
    r_lo = (r - r_hi.astype(F32)).astype(BF16)
    rows = tm // ROUTE_ROW_SPLIT
    lane = lax.broadcasted_iota(I32, (rows, LANES), 1)

    picks = []
    for r0 in range(0, tm, rows):
        rs = pl.ds(r0, rows)
        xn = x_ref[rs, :] + jnp.dot(a_ref[rs, :], wb_ref[...], preferred_element_type=F32)
        xn_ref[rs, :] = xn
        hn = _rms_rows(xn, g_ref[...])
        h_hi = hn.astype(BF16)
        hn_ref[rs, :] = h_hi
        h_lo = (hn - h_hi.astype(F32)).astype(BF16)
        logits = (jnp.dot(h_hi, r_hi, preferred_element_type=F32)
                  + jnp.dot(h_hi, r_lo, preferred_element_type=F32)
                  + jnp.dot(h_lo, r_hi, preferred_element_type=F32))
        logits = jnp.where(lane < N_EXPERTS, logits, -jnp.inf)
        v1 = jnp.max(logits, axis=-1, keepdims=True)
        i1 = jnp.min(jnp.where(logits == v1, lane, LANES), axis=-1, keepdims=True)
        oh1 = lane == i1
        rest = jnp.where(oh1, -jnp.inf, logits)
        v2 = jnp.max(rest, axis=-1, keepdims=True)
        i2 = jnp.min(jnp.where(rest == v2, lane, LANES), axis=-1, keepdims=True)
        oh2 = lane == i2
        e2 = jnp.exp(v2 - v1)
        mf_ref[rs, :] = jnp.where(lane == 0, 1.0 / (1.0 + e2), jnp.where(lane == 1, e2 / (1.0 + e2), 0.0))
        picks.append((rs, i1, i2, oh1, oh2))

    sel = jnp.concatenate([(oh1 | oh2).astype(BF16) for _, _, _, oh1, oh2 in picks], axis=0)
    for rs, i1, i2, oh1, oh2 in picks:
        before = jnp.dot(tri_ref[rs, :], sel, preferred_element_type=F32)
        rank1 = jnp.sum(jnp.where(oh1, before, 0.0), axis=-1, keepdims=True).astype(I32)
        rank2 = jnp.sum(jnp.where(oh2, before, 0.0), axis=-1, keepdims=True).astype(I32)
        mi_ref[rs, :] = jnp.where(lane == 0, i1, jnp.where(lane == 1, i2,
                                  jnp.where(lane == 2, rank1, jnp.where(lane == 3, rank2, 0))))
    counts = jnp.sum(sel.astype(F32), axis=0, keepdims=True).astype(I32)
    cnt_ref[...] = jnp.broadcast_to(counts, cnt_ref.shape)


def _route(a, w_out, out_slot, x, g, router):
    t, d = x.shape
    k = a.shape[1]
    tm = SEG_TILE
    r_pad = jnp.zeros((d, LANES), F32).at[:, :N_EXPERTS].set(router.astype(F32))
    return pl.pallas_call(
        _route_kernel,
        out_shape=(jax.ShapeDtypeStruct((t, d), F32),
                   jax.ShapeDtypeStruct((t, d), BF16),
                   jax.ShapeDtypeStruct((t, LANES), I32),
                   jax.ShapeDtypeStruct((t, LANES), F32),
                   jax.ShapeDtypeStruct((t // tm, SUBLANES, LANES), I32)),
        grid=(t // tm,),
        in_specs=[pl.BlockSpec((tm, k), lambda i: (i, 0)),
                  pl.BlockSpec((None, k, d), lambda i: (out_slot, 0, 0)),
                  pl.BlockSpec((tm, d), lambda i: (i, 0)),
                  pl.BlockSpec((1, d), lambda i: (0, 0)),
                  pl.BlockSpec((d, LANES), lambda i: (0, 0))],
        out_specs=(pl.BlockSpec((tm, d), lambda i: (i, 0)),
                   pl.BlockSpec((tm, d), lambda i: (i, 0)),
                   pl.BlockSpec((tm, LANES), lambda i: (i, 0)),
                   pl.BlockSpec((tm, LANES), lambda i: (i, 0)),
                   pl.BlockSpec((None, SUBLANES, LANES), lambda i: (i, 0, 0))),
        scratch_shapes=[pltpu.VMEM((tm, tm), BF16), pltpu.VMEM((k, d), BF16)],
        compiler_params=_cparams(1),
        name="moe_route",
    )(a, w_out, x, g.reshape(1, d), r_pad)


def _slot_rows(mi, loc_ref, base):
    e1, e2, d1, d2 = mi[0], mi[1], mi[2], mi[3]
    for e in range(N_EXPERTS):
        off = loc_ref[base + e]
        d1 = d1 + jnp.where(e1 == e, off, 0)
        d2 = d2 + jnp.where(e2 == e, off, 0)
    return d1, d2


def _segment_copies(base, seg_ref, n16_ref, loc_ref, hbm_ref, vmem_ref, sem, *, to_hbm, wait):
    for e in range(N_EXPERTS):
        n16 = n16_ref[base + e]
        hbm0 = seg_ref[base + e]
        vmem0 = loc_ref[base + e]
        off = 0
        for size in SEG_SIZES:
            @pl.when((n16 & size) != 0)
            def _(off=off, size=size):
                h = hbm_ref.at[pl.ds(pl.multiple_of(hbm0 + off, SEG_ALIGN), size)]
                v = vmem_ref.at[pl.ds(pl.multiple_of(vmem0 + off, SEG_ALIGN), size)]
                cp = pltpu.make_async_copy(v, h, sem) if to_hbm else pltpu.make_async_copy(h, v, sem)
                if wait:
                    cp.wait()
                else:
                    cp.start()
            off = off + (n16 & size)


def _scatter_kernel(seg_ref, n16_ref, loc_ref, tail_ref, nv_ref, hn_ref, mi_ref, xs_ref, stage_ref,
                    zero_ref, sem):
    i = pl.program_id(0)
    tm = hn_ref.shape[0]
    rows = stage_ref.shape[1]
    tg = zero_ref.shape[0]
    slot = lax.rem(i, 2)

    @pl.when(i == 0)
    def _():
        zero_ref[...] = jnp.zeros_like(zero_ref)
        for e in range(N_EXPERTS):
            tail = pl.multiple_of(tail_ref[e], SEG_ALIGN)
            pltpu.make_async_copy(zero_ref, xs_ref.at[pl.ds(tail, tg)], sem.at[2]).start()
        for e in range(N_EXPERTS):
            tail = pl.multiple_of(tail_ref[e], SEG_ALIGN)
            pltpu.make_async_copy(zero_ref, xs_ref.at[pl.ds(tail, tg)], sem.at[2]).wait()

        def zero_tile(tile, carry):
            cp = pltpu.make_async_copy(zero_ref, xs_ref.at[pl.ds(pl.multiple_of(tile * tg, tg), tg)],
                                       sem.at[2])
            cp.start()
            cp.wait()
            return carry

        lax.fori_loop(nv_ref[0], xs_ref.shape[0] // tg, zero_tile, 0)

    base = i * N_EXPERTS
    mi_t = mi_ref[...].T
    d1, d2 = _slot_rows([mi_t[k:k + 1, :] for k in range(4)], loc_ref, base)
    row = lax.broadcasted_iota(I32, (rows, tm), 0)
    onehot = ((row == d1) | (row == d2)).astype(BF16)
    stage_ref[slot] = jnp.dot(onehot, hn_ref[...], preferred_element_type=F32).astype(BF16)
    tabs = (seg_ref, n16_ref, loc_ref, xs_ref)
    _segment_copies(base, *tabs, stage_ref.at[slot], sem.at[slot], to_hbm=True, wait=False)

    @pl.when(i > 0)
    def _():
        _segment_copies(base - N_EXPERTS, *tabs, stage_ref.at[1 - slot], sem.at[1 - slot],
                        to_hbm=True, wait=True)

    @pl.when(i == pl.num_programs(0) - 1)
    def _():
        _segment_copies(base, *tabs, stage_ref.at[slot], sem.at[slot], to_hbm=True, wait=True)


def _scatter(hn, mi, seg, n16, loc, tails, n_valid, n_rows):
    t, d = hn.shape
    tm = SEG_TILE
    return pl.pallas_call(
        _scatter_kernel,
        out_shape=jax.ShapeDtypeStruct((n_rows, d), BF16),
        grid_spec=pltpu.PrefetchScalarGridSpec(
            num_scalar_prefetch=5,
            grid=(t // tm,),
            in_specs=[pl.BlockSpec((tm, d), lambda i, *_: (i, 0)),
                      pl.BlockSpec((tm, LANES), lambda i, *_: (i, 0))],
            out_specs=pl.BlockSpec(memory_space=pl.ANY),
            scratch_shapes=[pltpu.VMEM((2, SEG_STAGE, d), BF16), pltpu.VMEM((MOE_TILE, d), BF16),
                            pltpu.SemaphoreType.DMA((3,))]),
        compiler_params=pltpu.CompilerParams(dimension_semantics=("arbitrary",),
                                             vmem_limit_bytes=VMEM_LIMIT, has_side_effects=True),
        name="moe_scatter",
    )(seg, n16, loc, tails, n_valid, hn, mi)


def _seg_expert_kernel(te_ref, tb_ref, nr_ref, nv_ref, xs_ref, wg_ref, wu_ref, wd_ref, ys_ref, acc_ref):
    i, f = pl.program_id(0), pl.program_id(1)
    step = xs_ref.shape[0] // MOE_HEIGHTS

    @pl.when(i < nv_ref[0])
    def _():
        @pl.when(f == 0)
        def _():
            acc_ref[...] = jnp.zeros_like(acc_ref)

        nrow = nr_ref[i]
        for q in range(1, MOE_HEIGHTS + 1):
            lo_ok = nrow > (q - 1) * step
            in_q = lo_ok if q == MOE_HEIGHTS else lo_ok & (nrow <= q * step)

            @pl.when(in_q)
            def _(q=q):
                _swiglu_accumulate(xs_ref, wg_ref, wu_ref, wd_ref, acc_ref, rows=q * step)

        @pl.when(f == pl.num_programs(1) - 1)
        def _():
            ys_ref[...] = acc_ref[...].astype(ys_ref.dtype)

    @pl.when((i >= nv_ref[0]) & (f == pl.num_programs(1) - 1))
    def _():
        ys_ref[...] = jnp.zeros_like(ys_ref)


def _seg_experts(xs, w_gate_up, w_down, slot, tile_expert, tile_block, tile_rows, n_valid, n_tiles):
    n_rows, d = xs.shape
    ff = w_down.shape[-2]
    tg, tf = MOE_TILE, FF_TILE
    nf = ff // tf

    def fidx(i, f, nv):
        return jnp.where(i < nv[0], f, nf - 1)

    return pl.pallas_call(
        _seg_expert_kernel,
        out_shape=jax.ShapeDtypeStruct((n_tiles * tg, d), BF16),
        grid_spec=pltpu.PrefetchScalarGridSpec(
            num_scalar_prefetch=4,
            grid=(n_tiles, nf),
            in_specs=[pl.BlockSpec((tg, d), lambda i, f, te, tb, nr, nv: (tb[i], 0)),
                      pl.BlockSpec((None, None, d, tf),
                                   lambda i, f, te, tb, nr, nv: (slot, te[i], 0, fidx(i, f, nv))),
                      pl.BlockSpec((None, None, d, tf),
                                   lambda i, f, te, tb, nr, nv: (slot, te[i], 0, nf + fidx(i, f, nv))),
                      pl.BlockSpec((None, None, tf, d),
                                   lambda i, f, te, tb, nr, nv: (slot, te[i], fidx(i, f, nv), 0))],
            out_specs=pl.BlockSpec((tg, d), lambda i, f, te, tb, nr, nv: (i, 0)),
            scratch_shapes=[pltpu.VMEM((tg, d), F32)]),
        compiler_params=_cparams(2),
        name="moe_experts",
    )(tile_expert, tile_block, tile_rows, n_valid, xs, w_gate_up, w_gate_up, w_down)


def _gather_kernel(seg_ref, n16_ref, loc_ref, x_ref, mi_ref, mf_ref, gf_ref, ys_ref, o_ref,
                   *rest, norm):
    if norm == "next":
        nxt_ref, ybuf_ref, sem = rest
    else:
        ybuf_ref, sem = rest
    i = pl.program_id(0)
    tm = x_ref.shape[0]
    rows = ybuf_ref.shape[1]
    slot = lax.rem(i, 2)
    base = i * N_EXPERTS
    tabs = (seg_ref, n16_ref, loc_ref, ys_ref)

    @pl.when(i == 0)
    def _():
        ybuf_ref[...] = jnp.zeros_like(ybuf_ref)
        _segment_copies(base, *tabs, ybuf_ref.at[0], sem.at[0], to_hbm=False, wait=False)

    @pl.when(i + 1 < pl.num_programs(0))
    def _():
        _segment_copies(base + N_EXPERTS, *tabs, ybuf_ref.at[1 - slot], sem.at[1 - slot],
                        to_hbm=False, wait=False)

    mi = mi_ref[...]
    d1, d2 = _slot_rows([mi[:, k:k + 1] for k in range(4)], loc_ref, base)
    col = lax.broadcasted_iota(I32, (tm, rows), 1)
    gates = mf_ref[...]
    _segment_copies(base, *tabs, ybuf_ref.at[slot], sem.at[slot], to_hbm=False, wait=True)
    ybuf = ybuf_ref[slot]
    out = x_ref[...]
    for k, dk in enumerate((d1, d2)):
        yk = jnp.dot((col == dk).astype(BF16), ybuf, preferred_element_type=F32)
        out = out + gates[:, k:k + 1] * yk
    if norm == "final":
        out = _rms_rows(out, gf_ref[...])
    o_ref[...] = out
    if norm == "next":
        nxt_ref[...] = _rms_rows(out, gf_ref[...]).astype(BF16)


def _gather(x, mi, mf, ys, seg, n16, loc, g_norm, norm):
    t, d = x.shape
    tm = SEG_TILE
    row_spec = pl.BlockSpec((tm, d), lambda i, *_: (i, 0))
    out_shape = jax.ShapeDtypeStruct((t, d), F32)
    emit_next = norm == "next"
    return pl.pallas_call(
        functools.partial(_gather_kernel, norm=norm),
        out_shape=(out_shape, jax.ShapeDtypeStruct((t, d), BF16)) if emit_next else out_shape,
        grid_spec=pltpu.PrefetchScalarGridSpec(
            num_scalar_prefetch=3,
            grid=(t // tm,),
            in_specs=[row_spec,
                      pl.BlockSpec((tm, LANES), lambda i, *_: (i, 0)),
                      pl.BlockSpec((tm, LANES), lambda i, *_: (i, 0)),
                      pl.BlockSpec((1, d), lambda i, *_: (0, 0)),
                      pl.BlockSpec(memory_space=pl.ANY)],
            out_specs=(row_spec, row_spec) if emit_next else row_spec,
            scratch_shapes=[pltpu.VMEM((2, SEG_STAGE, d), BF16), pltpu.SemaphoreType.DMA((2,))]),
        compiler_params=_cparams(1),
        name="moe_gather",
    )(seg, n16, loc, x, mi, mf, g_norm.reshape(1, d).astype(F32), ys)


def _moe_seg(a, w_out, out_slot, x, g, router, w_gate_up, w_down, slot, g_norm, norm):
    t, d = x.shape
    tg = MOE_TILE
    x, hn, mi, mf, cnt = _route(a, w_out, out_slot, x, g, router)

    counts = cnt[:, 0, :N_EXPERTS]
    n16 = (counts + SEG_ALIGN - 1) // SEG_ALIGN * SEG_ALIGN
    rows_e = jnp.sum(n16, axis=0)
    tiles_e = (rows_e + tg - 1) // tg
    tile_start = jnp.cumsum(tiles_e) - tiles_e
    row_start = tile_start * tg
    seg = row_start[None, :] + jnp.cumsum(n16, axis=0) - n16
    loc = jnp.cumsum(n16, axis=1) - n16
    n_valid = jnp.sum(tiles_e).astype(I32)
    n_tiles = (2 * t + counts.size * (SEG_ALIGN - 1)) // tg + N_EXPERTS
    n_rows = (n_tiles + 1) * tg
    tidx = jnp.minimum(jnp.arange(n_tiles, dtype=I32), n_valid - 1)
    tile_expert = (jnp.sum(tidx[:, None] >= tile_start[None, :], axis=1) - 1).astype(I32)
    tails = (row_start + rows_e).astype(I32)
    seg, n16, loc = (a.reshape(-1).astype(I32) for a in (seg, n16, loc))

    n_valid = n_valid.reshape(1)
    xs = _scatter(hn, mi, seg, n16, loc, tails, n_valid, n_rows)
    tile_rows = (tails[tile_expert] - tidx * tg).astype(I32)
    ys = _seg_experts(xs, w_gate_up, w_down, slot, tile_expert, tidx, tile_rows, n_valid.reshape(1), n_tiles)
    return _gather(x, mi, mf, ys, seg, n16, loc, g_norm, norm)


def kernel(x, rel_bias, norm_mix, norm_ffn, norm_final, a_w_in, a_w_group, a_scale, a_w_out,
           b_w_in, b_w_out, c_w_in, c_lambda, c_subln, c_w_out, d_w_in, d_sink, d_w_out,
           f_w_gate_up, f_w_down, m_router, m_w_gate_up, m_w_down):
    batch, seq, d = x.shape
    h = x.reshape(batch * seq, d)

    u = _norm_matmul(h, norm_mix[0], a_w_in, 0, F32)
    y = _pool_mix(u, a_w_group, a_scale, 0, batch, seq)
    h = _matmul_res(y, a_w_out, 0, h)
    h = _ffn(h, norm_ffn[0], f_w_gate_up, f_w_down, 0)

    qkv = _dil_proj(h.reshape(batch, seq, d), norm_mix[1], b_w_in, 0, tuple(p[1] for p in DIL_PAIRS))
    o = _dilated_attention(qkv, rel_bias, batch, seq)
    h, hn = _moe_seg(o, b_w_out, 0, h, norm_ffn[1], m_router[0], m_w_gate_up, m_w_down, 0,
                     norm_mix[2], "next")

    qkv = _matmul(hn, c_w_in, 0, BF16)
    o = _diff_attention(qkv, c_lambda, c_subln, 0, rel_bias, batch, seq)
    h = _matmul_res(o, c_w_out, 0, h)
    h, hn = _ffn(h, norm_ffn[2], f_w_gate_up, f_w_down, 1, g_next=norm_mix[3])

    qkv = _matmul(hn, d_w_in, 0, BF16)
    o = _gqa_attention(qkv, d_sink[0], rel_bias, batch, seq)
    h = _moe_seg(o, d_w_out, 0, h, norm_ffn[3], m_router[1], m_w_gate_up, m_w_down, 1,
                 norm_final, "final")
    return h.reshape(batch, seq, d)
```

```python
import collections
import functools
import math

import numpy as np
import jax
import jax.numpy as jnp
from jax import lax
from jax.experimental import pallas as pl
from jax.experimental.pallas import tpu as pltpu

F32 = jnp.float32
BF16 = jnp.bfloat16
I32 = jnp.int32

RMS_EPS = 1e-6
HEAD_DIM = 64
N_BUCKETS = 32
MAX_DISTANCE = 1024
POOL_WINDOWS = (2, 4, 8, 16)
DIL_PAIRS = ((128, 1), (512, 4), (2048, 16))
Q_SCALE = HEAD_DIM ** -0.5
GQA_KV_HEADS = 4
GQA_WINDOW = 128
N_EXPERTS = 8
DIFF_LAYER = 2

LANES = 128
SUBLANES = 8
VMEM_LIMIT = 56 * 1024 * 1024
ROW_TILE = 1024
FF_TILE = 512
MOE_TILE = 1024
MOE_HEIGHTS = 4
ROUTE_ROW_SPLIT = 2
SEG_TILE = 512
ATT_TQ = 128
DIL_UNITS = 8
GQA_UNITS = 1
DIFF_ROW_SPLIT = 4
DIL_HEAD_SPLIT = 2
DIFF_TQ = 1024
MASK_VALUE = -1e30


def _cparams(n_axes, vmem=VMEM_LIMIT):
    return pltpu.CompilerParams(dimension_semantics=("arbitrary",) * n_axes,
                                vmem_limit_bytes=vmem)


def _next_pow2(n):
    return 1 << (n - 1).bit_length()


def _rms_rows(x, g):
    ms = jnp.mean(x * x, axis=-1, keepdims=True)
    return x * lax.rsqrt(ms + RMS_EPS) * g


def _norm_matmul_kernel(x_ref, g_ref, w_ref, o_ref, hn_ref):
    @pl.when(pl.program_id(1) == 0)
    def _():
        hn_ref[...] = _rms_rows(x_ref[...], g_ref[...]).astype(BF16)

    o_ref[...] = jnp.dot(hn_ref[...], w_ref[...].astype(BF16),
                         preferred_element_type=F32).astype(o_ref.dtype)


def _norm_matmul(x, g, w, slot, out_dtype):
    t, d = x.shape
    n = w.shape[-1]
    tm = ROW_TILE
    tn = next(c for c in (1024, 768, 512, 256, 128) if n % c == 0)
    return pl.pallas_call(
        _norm_matmul_kernel,
        out_shape=jax.ShapeDtypeStruct((t, n), out_dtype),
        grid=(t // tm, n // tn),
        in_specs=[pl.BlockSpec((tm, d), lambda i, j: (i, 0)),
                  pl.BlockSpec((1, d), lambda i, j: (0, 0)),
                  pl.BlockSpec((None, d, tn), lambda i, j: (slot, 0, j))],
        out_specs=pl.BlockSpec((tm, tn), lambda i, j: (i, j)),
        scratch_shapes=[pltpu.VMEM((tm, d), BF16)],
        compiler_params=_cparams(2),
        name="norm_matmul",
    )(x, g.reshape(1, d), w)


def _matmul_kernel(a_ref, w_ref, o_ref):
    o_ref[...] = jnp.dot(a_ref[...], w_ref[...].astype(BF16),
                         preferred_element_type=F32).astype(o_ref.dtype)


def _matmul(a, w, slot, out_dtype):
    t, d = a.shape
    n = w.shape[-1]
    tm = 2 * ROW_TILE
    tn = next(c for c in (1024, 768, 512, 256, 128) if n % c == 0)
    return pl.pallas_call(
        _matmul_kernel,
        out_shape=jax.ShapeDtypeStruct((t, n), out_dtype),
        grid=(t // tm, n // tn),
        in_specs=[pl.BlockSpec((tm, d), lambda i, j: (i, 0)),
                  pl.BlockSpec((None, d, tn), lambda i, j: (slot, 0, j))],
        out_specs=pl.BlockSpec((tm, tn), lambda i, j: (i, j)),
        compiler_params=_cparams(2),
        name="matmul",
    )(a, w)


def _matmul_res_kernel(a_ref, w_ref, x_ref, o_ref, wb_ref):
    @pl.when(pl.program_id(0) == 0)
    def _():
        wb_ref[...] = w_ref[...].astype(BF16)

    o_ref[...] = x_ref[...] + jnp.dot(a_ref[...], wb_ref[...], preferred_element_type=F32)


def _matmul_res(a, w, slot, x):
    t, k = a.shape
    d = w.shape[-1]
    tm = ROW_TILE
    return pl.pallas_call(
        _matmul_res_kernel,
        out_shape=jax.ShapeDtypeStruct((t, d), F32),
        grid=(t // tm,),
        in_specs=[pl.BlockSpec((tm, k), lambda i: (i, 0)),
                  pl.BlockSpec((None, k, d), lambda i: (slot, 0, 0)),
                  pl.BlockSpec((tm, d), lambda i: (i, 0))],
        out_specs=pl.BlockSpec((tm, d), lambda i: (i, 0)),
        scratch_shapes=[pltpu.VMEM((k, d), BF16)],
        compiler_params=_cparams(1),
        name="matmul_res",
    )(a, w, x)


def _silu_mul(g, u):
    return g * (1.0 / (1.0 + jnp.exp(-g))) * u


def _swiglu_accumulate(h_ref, wg_ref, wu_ref, wd_ref, acc_ref, rows=None):
    rows = h_ref.shape[0] if rows is None else rows
    h = h_ref[pl.ds(0, rows), :]
    gate = jnp.dot(h, wg_ref[...].astype(BF16), preferred_element_type=F32)
    up = jnp.dot(h, wu_ref[...].astype(BF16), preferred_element_type=F32)
    a = _silu_mul(gate, up).astype(BF16)
    acc_ref[pl.ds(0, rows), :] += jnp.dot(a, wd_ref[...].astype(BF16), preferred_element_type=F32)


def _ffn_kernel(x_ref, g_ref, gn_ref, wg_ref, wu_ref, wd_ref, o_ref, *rest, emit_next):
    if emit_next:
        nxt_ref, hn_ref, acc_ref = rest
    else:
        hn_ref, acc_ref = rest
    f = pl.program_id(1)

    @pl.when(f == 0)
    def _():
        hn_ref[...] = _rms_rows(x_ref[...], g_ref[...]).astype(BF16)
        acc_ref[...] = jnp.zeros_like(acc_ref)

    _swiglu_accumulate(hn_ref, wg_ref, wu_ref, wd_ref, acc_ref)

    @pl.when(f == pl.num_programs(1) - 1)
    def _():
        out = x_ref[...] + acc_ref[...]
        o_ref[...] = out
        if emit_next:
            nxt_ref[...] = _rms_rows(out, gn_ref[...]).astype(BF16)


def _ffn(x, g, w_gate_up, w_down, slot, g_next=None):
    t, d = x.shape
    ff = w_down.shape[-2]
    tm, tf = ROW_TILE, FF_TILE
    nf = ff // tf
    emit_next = g_next is not None
    gn = (g_next if emit_next else g).reshape(1, d)
    row_spec = pl.BlockSpec((tm, d), lambda i, f: (i, 0))
    vec_spec = pl.BlockSpec((1, d), lambda i, f: (0, 0))
    out_shape = jax.ShapeDtypeStruct((t, d), F32)
    return pl.pallas_call(
        functools.partial(_ffn_kernel, emit_next=emit_next),
        out_shape=(out_shape, jax.ShapeDtypeStruct((t, d), BF16)) if emit_next else out_shape,
        grid=(t // tm, nf),
        in_specs=[row_spec, vec_spec, vec_spec,
                  pl.BlockSpec((None, d, tf), lambda i, f: (slot, 0, f)),
                  pl.BlockSpec((None, d, tf), lambda i, f: (slot, 0, nf + f)),
                  pl.BlockSpec((None, tf, d), lambda i, f: (slot, f, 0))],
        out_specs=(row_spec, row_spec) if emit_next else row_spec,
        scratch_shapes=[pltpu.VMEM((tm, d), BF16), pltpu.VMEM((tm, d), F32)],
        compiler_params=_cparams(2),
        name="ffn",
    )(x, g.reshape(1, d), gn, w_gate_up, w_gate_up, w_down)


POOL_PAD = 16


def _pool_kernel(u_ref, wg_ref, sc_ref, o_ref, pad_ref):
    s, c = u_ref.shape
    grp = pl.program_id(1)
    pad_ref[pl.ds(0, POOL_PAD), :] = jnp.zeros((POOL_PAD, c), F32)
    pad_ref[pl.ds(POOL_PAD + s, POOL_PAD), :] = jnp.zeros((POOL_PAD, c), F32)
    pad_ref[pl.ds(POOL_PAD, s), :] = u_ref[...]
    wb = wg_ref[...].astype(BF16)
    chunk = 256

    for gi, w in enumerate(POOL_WINDOWS):
        @pl.when(grp == gi)
        def _(w=w):
            for c0 in range(0, s, chunk):
                acc = None
                for dlt in range(-(w // 2), w - w // 2):
                    piece = pad_ref[pl.ds(POOL_PAD + c0 + dlt, chunk), :]
                    acc = piece if acc is None else acc + piece
                pos = c0 + lax.broadcasted_iota(I32, (chunk, 1), 0)
                lo = jnp.maximum(pos - w // 2, 0)
                hi = jnp.minimum(pos + (w - w // 2), s)
                cnt = (hi - lo).astype(F32)
                mixed = acc / cnt - u_ref[pl.ds(c0, chunk), :]
                y = jnp.dot(mixed.astype(BF16), wb, preferred_element_type=F32)
                o_ref[pl.ds(c0, chunk), :] = (y * sc_ref[...]).astype(o_ref.dtype)


def _pool_mix(u, w_group, scale, slot, batch, seq):
    d = u.shape[1]
    ng = len(POOL_WINDOWS)
    c = d // ng
    u3 = u.reshape(batch, seq, d)
    out = pl.pallas_call(
        _pool_kernel,
        out_shape=jax.ShapeDtypeStruct((batch, seq, d), BF16),
        grid=(batch, ng),
        in_specs=[pl.BlockSpec((None, seq, c), lambda b, g: (b, 0, g)),
                  pl.BlockSpec((None, None, c, c), lambda b, g: (slot, g, 0, 0)),
                  pl.BlockSpec((1, c), lambda b, g: (slot, g))],
        out_specs=pl.BlockSpec((None, seq, c), lambda b, g: (b, 0, g)),
        scratch_shapes=[pltpu.VMEM((seq + 2 * POOL_PAD, c), F32)],
        compiler_params=_cparams(2),
        name="pool_mix",
    )(u3, w_group, scale)
    return out.reshape(batch * seq, d)


def _rel_bucket_np(rel):
    half = N_BUCKETS // 2
    max_exact = half // 2
    n = np.abs(rel)
    ratio = np.log(np.maximum(n, 1).astype(np.float32) / np.float32(max_exact))
    big = max_exact + (ratio / np.float32(math.log(MAX_DISTANCE / max_exact))
                       * np.float32(half - max_exact)).astype(np.int32)
    big = np.minimum(big, half - 1)
    return np.where(rel > 0, half, 0) + np.where(n < max_exact, n, big)


def _seg_table(rel_bias, rel, valid):
    bucket = jnp.asarray(_rel_bucket_np(rel).astype(np.int32))
    vals = jnp.take(rel_bias.astype(F32), bucket, axis=0)
    vals = jnp.where(jnp.asarray(valid)[..., None], vals, MASK_VALUE)
    return jnp.transpose(vals, (2, 0, 1))


def _toeplitz(seg_row, rows, cols):
    w = seg_row.shape[1]
    full = jnp.broadcast_to(seg_row, (rows, w))
    rolled = pltpu.roll(full, w - (rows - 1), 1, stride=1, stride_axis=0)
    return rolled[:, :cols]


def _band_variants(hw, n_tiles):
    if n_tiles == 1:
        return (0,)
    return (0, -hw, -2 * hw)


def _band_segs(rel_bias, tq, kw, half, dil, variants, width):
    c = np.arange(width)
    rel = np.stack([r0 + c - (tq - 1) for r0 in variants])
    valid = (np.abs(rel) <= half) & (c[None, :] < tq + kw - 1)
    return _seg_table(rel_bias, rel * dil, valid)


def _band_window(qi, nq, tq, hw, kw, ln):
    if nq == 1:
        return 0, 0
    var = jnp.where(qi == 0, 0, jnp.where(qi == nq - 1, 2, 1))
    return var, jnp.clip(qi * tq - hw, 0, ln - kw)


DilCfg = collections.namedtuple("DilCfg", "dil ln nq hw kw n_var")


def _dil_proj_kernel(x_ref, g_ref, w_ref, o_ref, slab_ref, hn_ref, *, dils):
    grp, c = pl.program_id(1), pl.program_id(2)
    seq, d = x_ref.shape
    n_slabs = d // LANES
    chunk = 256

    @pl.when((grp == 0) & (c == 0))
    def _():
        for c0 in range(0, seq, chunk):
            hn = _rms_rows(x_ref[pl.ds(c0, chunk), :], g_ref[...])
            for s in range(n_slabs):
                slab_ref[s, pl.ds(c0, chunk), :] = hn[:, s * LANES:(s + 1) * LANES]

    for gi, dil in enumerate(dils):
        @pl.when((grp == gi) & (c == 0))
        def _(dil=dil):
            ln = seq // dil
            for r in range(dil):
                for c0 in range(0, ln, chunk):
                    n = min(chunk, ln)
                    rows = [slab_ref[s, pl.ds(r + c0 * dil, n, stride=dil), :] for s in range(n_slabs)]
                    hn_ref[pl.ds(r * ln + c0, n), :] = jnp.concatenate(rows, axis=1).astype(BF16)

    res = jnp.dot(hn_ref[...], w_ref[...].astype(BF16), preferred_element_type=F32)
    res = res * jnp.where(c < pl.num_programs(2) // 3, Q_SCALE, 1.0)
    for s in range(o_ref.shape[0]):
        o_ref[s] = res[:, s * LANES:(s + 1) * LANES].astype(o_ref.dtype)


def _dil_proj(x3, g, w_in, slot, dils):
    batch, seq, d = x3.shape
    n = w_in.shape[-1]
    tn = FF_TILE
    per_group = n // len(dils) // tn
    spt = tn // LANES
    return pl.pallas_call(
        functools.partial(_dil_proj_kernel, dils=dils),
        out_shape=jax.ShapeDtypeStruct((batch, n // LANES, seq, LANES), BF16),
        grid=(batch, len(dils), per_group),
        in_specs=[pl.BlockSpec((None, seq, d), lambda b, g, c: (b, 0, 0)),
                  pl.BlockSpec((1, d), lambda b, g, c: (0, 0)),
                  pl.BlockSpec((None, d, tn), lambda b, g, c: (slot, 0, g * per_group + c))],
        out_specs=pl.BlockSpec((None, spt, seq, LANES), lambda b, g, c: (b, g * per_group + c, 0, 0)),
        scratch_shapes=[pltpu.VMEM((d // LANES, seq, LANES), F32), pltpu.VMEM((seq, d), BF16)],
        compiler_params=_cparams(3),
        name="dil_proj",
    )(x3, g.reshape(1, d), w_in)


def _dil_unit(cfg, unit, q4, k_ref, v_ref, bias_ref, acc_ref, m_ref, l_ref, *, tq, first, last):
    dil, ln, nq, kw = cfg.dil, cfg.ln, cfg.nq, cfg.kw
    n_pairs = q4.shape[0]
    if nq == 1:
        r, qi = unit, 0
    elif dil == 1:
        r, qi = 0, unit
    else:
        r, qi = unit // nq, unit % nq
    var, ks = _band_window(qi, nq, tq, cfg.hw, kw, ln)
    krow = pl.multiple_of(r * ln + ks, cfg.hw)
    if dil == 1:
        rows = pl.ds(pl.multiple_of(qi * tq, tq), tq)
    else:
        rows = pl.ds(qi * tq * dil + r, tq, stride=dil)

    even = lax.broadcasted_iota(I32, (n_pairs, tq, LANES), 2) < HEAD_DIM
    zero = jnp.zeros_like(q4)
    q8 = jnp.concatenate([jnp.where(even, q4, zero), jnp.where(even, zero, q4)], axis=0)
    k4 = k_ref[:, pl.ds(krow, kw), :]
    v4 = v_ref[:, pl.ds(krow, kw), :]
    k8 = jnp.concatenate([k4, k4], axis=0)
    va = jnp.concatenate([v4, jnp.ones_like(v4)], axis=2)
    v8 = jnp.concatenate([va, va], axis=0)
    s = lax.dot_general(q8, k8, (((2,), (2,)), ((0,), (0,))), preferred_element_type=F32)
    s = s + bias_ref[var]
    m8 = jnp.max(s, axis=-1, keepdims=True)
    p = jnp.exp(s - m8).astype(BF16)
    ov = lax.dot_general(p, v8, (((2,), (1,)), ((0,), (0,))), preferred_element_type=F32)
    o_c = jnp.where(even, ov[:n_pairs, :, :LANES], ov[n_pairs:, :, :LANES])
    l_c = jnp.where(even, ov[:n_pairs, :, LANES:], ov[n_pairs:, :, LANES:])
    m_c = jnp.where(even, m8[:n_pairs], m8[n_pairs:])
    if first:
        m_n, l_n, acc_n = m_c, l_c, o_c
    else:
        m_o = jnp.stack([m_ref[pp, rows, :] for pp in range(n_pairs)])
        l_o = jnp.stack([l_ref[pp, rows, :] for pp in range(n_pairs)])
        acc_o = jnp.stack([acc_ref[pp, rows, :] for pp in range(n_pairs)])
        m_n = jnp.maximum(m_o, m_c)
        a_o = jnp.exp(m_o - m_n)
        a_c = jnp.exp(m_c - m_n)
        l_n = a_o * l_o + a_c * l_c
        acc_n = a_o * acc_o + a_c * o_c
    if last:
        acc_n = acc_n / l_n
    for pp in range(n_pairs):
        acc_ref[pp, rows, :] = acc_n[pp]
        if not last:
            m_ref[pp, rows, :] = m_n[pp]
            l_ref[pp, rows, :] = l_n[pp]


def _dil_attn_kernel(*refs, cfgs, tq):
    ng = len(cfgs)
    seg_refs = refs[:ng]
    q_ref, k_ref, v_ref, o_ref = refs[ng:ng + 4]
    bias_refs = refs[ng + 4:2 * ng + 4]
    acc_ref, m_ref, l_ref = refs[2 * ng + 4:]
    hf, b, grp, step = (pl.program_id(i) for i in range(4))
    n_pairs = q_ref.shape[0]

    @pl.when((b == 0) & (grp == 0) & (step == 0))
    def _():
        for gi, cfg in enumerate(cfgs):
            for v in range(cfg.n_var):
                for odd in range(2):
                    for pp in range(n_pairs):
                        head = hf * 2 * n_pairs + 2 * pp + odd
                        row = seg_refs[gi][head, pl.ds(v, 1), :]
                        bias_refs[gi][v, odd * n_pairs + pp] = _toeplitz(row, tq, cfg.kw)

    for gi, cfg in enumerate(cfgs):
        @pl.when(grp == gi)
        def _(gi=gi, cfg=cfg):
            for uu in range(DIL_UNITS):
                _dil_unit(cfg, step * DIL_UNITS + uu, q_ref[:, pl.ds(uu * tq, tq), :], k_ref, v_ref,
                          bias_refs[gi], acc_ref, m_ref, l_ref, tq=tq, first=gi == 0, last=gi == ng - 1)

    @pl.when((grp == ng - 1) & (step == pl.num_programs(3) - 1))
    def _():
        slabs = [acc_ref[s] for s in range(n_pairs)]
        o_ref[...] = jnp.concatenate(slabs, axis=1).astype(o_ref.dtype)


def _dilated_attention(qkv, rel_bias, batch, seq):
    ng = len(DIL_PAIRS)
    d = qkv.shape[1] * LANES // (3 * ng)
    hd = d // DIL_HEAD_SPLIT
    n_pairs = hd // LANES
    tq = ATT_TQ
    cfgs, segs = [], []
    for win, dil in DIL_PAIRS:
        half = win // (2 * dil)
        ln = seq // dil
        nq = ln // tq
        kw = min(tq + 2 * half, ln)
        variants = _band_variants(half, nq)
        cfgs.append(DilCfg(dil, ln, nq, half, kw, len(variants)))
        segs.append(_band_segs(rel_bias, tq, kw, half, dil, variants, _next_pow2(tq + kw - 1)))
    steps = seq // tq // DIL_UNITS
    cb = DIL_HEAD_SPLIT

    in_specs = [pl.BlockSpec(sg.shape, lambda hf, b, g, s: (0, 0, 0)) for sg in segs]
    in_specs += [pl.BlockSpec((None, n_pairs, DIL_UNITS * tq, LANES),
                              lambda hf, b, g, s: (b, (g * 3) * cb + hf, s, 0)),
                 pl.BlockSpec((None, n_pairs, seq, LANES),
                              lambda hf, b, g, s: (b, (g * 3 + 1) * cb + hf, 0, 0)),
                 pl.BlockSpec((None, n_pairs, seq, LANES),
                              lambda hf, b, g, s: (b, (g * 3 + 2) * cb + hf, 0, 0))]
    scratch = [pltpu.VMEM((c.n_var, 2 * n_pairs, tq, c.kw), F32) for c in cfgs]
    scratch += [pltpu.VMEM((n_pairs, seq, LANES), F32) for _ in range(3)]
    out = pl.pallas_call(
        functools.partial(_dil_attn_kernel, cfgs=tuple(cfgs), tq=tq),
        out_shape=jax.ShapeDtypeStruct((batch, seq, d), BF16),
        grid=(DIL_HEAD_SPLIT, batch, ng, steps),
        in_specs=in_specs,
        out_specs=pl.BlockSpec((None, seq, hd), lambda hf, b, g, s: (b, 0, hf)),
        scratch_shapes=scratch,
        compiler_params=_cparams(4),
        name="dil_attn",
    )(*segs, qkv, qkv, qkv)
    return out.reshape(batch * seq, d)


def _gqa_kernel(sink_ref, seg_ref, q_ref, k_ref, v_ref, o_ref, bias_ref, *, tq, kw, hw, grp):
    b, step = pl.program_id(0), pl.program_id(1)
    seq_len = k_ref.shape[0]
    nq = seq_len // tq
    n_kv = k_ref.shape[1] // HEAD_DIM

    @pl.when((b == 0) & (step == 0))
    def _():
        for v in range(3):
            for kh in range(n_kv):
                for gq in range(grp):
                    bias_ref[v, kh, pl.ds(gq * tq, tq), :] = _toeplitz(
                        seg_ref[kh * grp + gq, pl.ds(v, 1), :], tq, kw)

    row = lax.broadcasted_iota(I32, (grp * tq, 1), 0)
    for u in range(GQA_UNITS):
        qi = step * GQA_UNITS + u
        qrows = pl.ds(u * tq, tq)
        var, ks = _band_window(qi, nq, tq, hw, kw, seq_len)
        ks = pl.multiple_of(ks, hw)
        outs = [None] * (n_kv * grp)
        for kh in range(n_kv):
            cs = slice(kh * HEAD_DIM, (kh + 1) * HEAD_DIM)
            qs = jnp.concatenate(
                [q_ref[qrows, (kh * grp + gq) * HEAD_DIM:(kh * grp + gq + 1) * HEAD_DIM]
                 for gq in range(grp)], axis=0) * Q_SCALE
            kk = k_ref[pl.ds(ks, kw), cs]
            vv = v_ref[pl.ds(ks, kw), cs]
            s = lax.dot_general(qs, kk, (((1,), (1,)), ((), ())), preferred_element_type=F32)
            s = s + bias_ref[var, kh]
            sk = jnp.zeros((grp * tq, 1), F32)
            for gq in range(grp):
                sk = jnp.where((row >= gq * tq) & (row < (gq + 1) * tq), sink_ref[kh * grp + gq], sk)
            m = jnp.maximum(jnp.max(s, axis=-1, keepdims=True), sk)
            e = jnp.exp(s - m).astype(BF16)
            va = jnp.concatenate([vv, jnp.ones_like(vv)], axis=1)
            ov = jnp.dot(e, va, preferred_element_type=F32)
            o = ov[:, :HEAD_DIM] / (ov[:, HEAD_DIM:] + jnp.exp(sk - m))
            for gq in range(grp):
                outs[kh * grp + gq] = o[gq * tq:(gq + 1) * tq, :]
        o_ref[qrows, :] = jnp.concatenate(outs, axis=1).astype(o_ref.dtype)


def _gqa_attention(qkv, sink, rel_bias, batch, seq):
    t, ncol = qkv.shape
    n_q = sink.shape[0]
    d = n_q * HEAD_DIM
    kvw = GQA_KV_HEADS * HEAD_DIM
    grp = n_q // GQA_KV_HEADS
    tq = hw = GQA_WINDOW
    kw = 3 * GQA_WINDOW
    nq = seq // tq
    variants = _band_variants(hw, nq)
    width = _next_pow2(tq + kw - 1)
    seg = _band_segs(rel_bias, tq, kw, GQA_WINDOW, 1, variants, width)
    qkv_v = qkv.reshape(batch, seq, ncol)
    out = pl.pallas_call(
        functools.partial(_gqa_kernel, tq=tq, kw=kw, hw=hw, grp=grp),
        out_shape=jax.ShapeDtypeStruct((batch, seq, d), BF16),
        grid=(batch, nq // GQA_UNITS),
        in_specs=[pl.BlockSpec(memory_space=pltpu.SMEM),
                  pl.BlockSpec((n_q, 3, width), lambda b, qi: (0, 0, 0)),
                  pl.BlockSpec((None, GQA_UNITS * tq, d), lambda b, qi: (b, qi, 0)),
                  pl.BlockSpec((None, seq, kvw), lambda b, qi: (b, 0, d // kvw)),
                  pl.BlockSpec((None, seq, kvw), lambda b, qi: (b, 0, d // kvw + 1))],
        out_specs=pl.BlockSpec((None, GQA_UNITS * tq, d), lambda b, qi: (b, qi, 0)),
        scratch_shapes=[pltpu.VMEM((3, GQA_KV_HEADS, grp * tq, kw), F32)],
        compiler_params=_cparams(2),
        name="gqa_attn",
    )(sink.astype(F32), seg, qkv_v, qkv_v, qkv_v)
    return out.reshape(t, d)


def _diff_kernel(seg_ref, lam_ref, sub_ref, q_ref, k_ref, v_ref, o_ref, bias_ref, *, tq, lam_init):
    h, qi, b = pl.program_id(0), pl.program_id(1), pl.program_id(2)
    seq_len = k_ref.shape[0]
    nk = seq_len // tq

    @pl.when(b == 0)
    def _():
        for j in range(2):
            for ki in range(nk):
                row = seg_ref[h * 2 + j, pl.ds(ki - qi + nk - 1, 1), :]
                bias_ref[j, :, pl.ds(ki * tq, tq)] = _toeplitz(row, tq, tq)

    lv = lam_ref[...]
    s01 = jnp.sum(lv[0:1, :] * lv[1:2, :], axis=-1, keepdims=True)
    s23 = jnp.sum(lv[2:3, :] * lv[3:4, :], axis=-1, keepdims=True)
    lam = jnp.exp(s01) - jnp.exp(s23) + lam_init

    q = q_ref[...] * Q_SCALE
    k = k_ref[...]
    v = v_ref[...]
    va = jnp.concatenate([v, jnp.ones_like(v)], axis=1)
    map0 = lax.broadcasted_iota(I32, q.shape, 1) < HEAD_DIM
    zero = jnp.zeros_like(q)
    qm = (jnp.where(map0, q, zero), jnp.where(map0, zero, q))
    rc = tq // DIFF_ROW_SPLIT
    for c0 in range(0, tq, rc):
        outs = []
        for j in range(2):
            s = lax.dot_general(qm[j][c0:c0 + rc], k, (((1,), (1,)), ((), ())),
                                preferred_element_type=F32)
            s = s + bias_ref[j, pl.ds(c0, rc), :]
            m = jnp.max(s, axis=-1, keepdims=True)
            p = jnp.exp(s - m).astype(BF16)
            ov = jnp.dot(p, va, preferred_element_type=F32)
            outs.append(ov[:, :2 * HEAD_DIM] / ov[:, 2 * HEAD_DIM:])
        o = outs[0] - lam * outs[1]
        o = _rms_rows(o, sub_ref[...]) * (1.0 - lam_init)
        o_ref[pl.ds(c0, rc), :] = o.astype(o_ref.dtype)


def _diff_attention(qkv, lam_vecs, subln, slot, rel_bias, batch, seq):
    t, ncol = qkv.shape
    d = ncol // 3
    hd2 = 2 * HEAD_DIM
    n_heads = d // hd2
    tq = DIFF_TQ
    nk = seq // tq
    width = _next_pow2(2 * tq - 1)
    c = np.arange(width)
    rel = np.stack([(dl - (nk - 1)) * tq + c - (tq - 1) for dl in range(2 * nk - 1)])
    seg = _seg_table(rel_bias, rel, np.ones_like(rel, dtype=bool))
    lam_init = 0.8 - 0.6 * math.exp(-0.3 * DIFF_LAYER)
    qkv_v = qkv.reshape(batch, seq, ncol)
    out = pl.pallas_call(
        functools.partial(_diff_kernel, tq=tq, lam_init=lam_init),
        out_shape=jax.ShapeDtypeStruct((batch, seq, d), BF16),
        grid=(n_heads, seq // tq, batch),
        in_specs=[pl.BlockSpec(seg.shape, lambda h, qi, b: (0, 0, 0)),
                  pl.BlockSpec((None,) + lam_vecs.shape[1:], lambda h, qi, b: (slot, 0, 0)),
                  pl.BlockSpec((1, hd2), lambda h, qi, b: (slot, 0)),
                  pl.BlockSpec((None, tq, hd2), lambda h, qi, b: (b, qi, h)),
                  pl.BlockSpec((None, seq, hd2), lambda h, qi, b: (b, 0, n_heads + h)),
                  pl.BlockSpec((None, seq, hd2), lambda h, qi, b: (b, 0, 2 * n_heads + h))],
        out_specs=pl.BlockSpec((None, tq, hd2), lambda h, qi, b: (b, qi, h)),
        scratch_shapes=[pltpu.VMEM((2, tq, seq), F32)],
        compiler_params=_cparams(3),
        name="diff_attn",
    )(seg, lam_vecs, subln, qkv_v, qkv_v, qkv_v)
    return out.reshape(t, d)


SEG_ALIGN = 16
SEG_SIZES = (512, 256, 128, 64, 32, 16)
SEG_STAGE = 2 * SEG_TILE + N_EXPERTS * SEG_ALIGN


def _route_kernel(a_ref, wo_ref, x_ref, g_ref, r_ref, xn_ref, hn_ref, mi_ref, mf_ref, cnt_ref,
                  tri_ref, wb_ref):
    tm, d = x_ref.shape

    @pl.when(pl.program_id(0) == 0)
    def _():
        rr = lax.broadcasted_iota(I32, (tm, tm), 0)
        cc = lax.broadcasted_iota(I32, (tm, tm), 1)
        tri_ref[...] = (cc < rr).astype(BF16)
        wb_ref[...] = wo_ref[...].astype(BF16)

    r = r_ref[...]
    r_hi = r.astype(BF16)
    r_lo = (r - r_hi.astype(F32)).astype(BF16)
    rows = tm // ROUTE_ROW_SPLIT
    lane = lax.broadcasted_iota(I32, (rows, LANES), 1)

    picks = []
    for r0 in range(0, tm, rows):
        rs = pl.ds(r0, rows)
        xn = x_ref[rs, :] + jnp.dot(a_ref[rs, :], wb_ref[...], preferred_element_type=F32)
        xn_ref[rs, :] = xn
        hn = _rms_rows(xn, g_ref[...])
        h_hi = hn.astype(BF16)
        hn_ref[rs, :] = h_hi
        h_lo = (hn - h_hi.astype(F32)).astype(BF16)
        logits = (jnp.dot(h_hi, r_hi, preferred_element_type=F32)
                  + jnp.dot(h_hi, r_lo, preferred_element_type=F32)
                  + jnp.dot(h_lo, r_hi, preferred_element_type=F32))
        logits = jnp.where(lane < N_EXPERTS, logits, -jnp.inf)
        v1 = jnp.max(logits, axis=-1, keepdims=True)
        i1 = jnp.min(jnp.where(logits == v1, lane, LANES), axis=-1, keepdims=True)
        oh1 = lane == i1
        rest = jnp.where(oh1, -jnp.inf, logits)
        v2 = jnp.max(rest, axis=-1, keepdims=True)
        i2 = jnp.min(jnp.where(rest == v2, lane, LANES), axis=-1, keepdims=True)
        oh2 = lane == i2
        e2 = jnp.exp(v2 - v1)
        mf_ref[rs, :] = jnp.where(lane == 0, 1.0 / (1.0 + e2), jnp.where(lane == 1, e2 / (1.0 + e2), 0.0))
        picks.append((rs, i1, i2, oh1, oh2))

    sel = jnp.concatenate([(oh1 | oh2).astype(BF16) for _, _, _, oh1, oh2 in picks], axis=0)
    for rs, i1, i2, oh1, oh2 in picks:
        before = jnp.dot(tri_ref[rs, :], sel, preferred_element_type=F32)
        rank1 = jnp.sum(jnp.where(oh1, before, 0.0), axis=-1, keepdims=True).astype(I32)
        rank2 = jnp.sum(jnp.where(oh2, before, 0.0), axis=-1, keepdims=True).astype(I32)
        mi_ref[rs, :] = jnp.where(lane == 0, i1, jnp.where(lane == 1, i2,
                                  jnp.where(lane == 2, rank1, jnp.where(lane == 3, rank2, 0))))
    counts = jnp.sum(sel.astype(F32), axis=0, keepdims=True).astype(I32)
    cnt_ref[...] = jnp.broadcast_to(counts, cnt_ref.shape)


def _route(a, w_out, out_slot, x, g, router):
    t, d = x.shape
    k = a.shape[1]
    tm = SEG_TILE
    r_pad = jnp.zeros((d, LANES), F32).at[:, :N_EXPERTS].set(router.astype(F32))
    return pl.pallas_call(
        _route_kernel,
        out_shape=(jax.ShapeDtypeStruct((t, d), F32),
                   jax.ShapeDtypeStruct((t, d), BF16),
                   jax.ShapeDtypeStruct((t, LANES), I32),
                   jax.ShapeDtypeStruct((t, LANES), F32),
                   jax.ShapeDtypeStruct((t // tm, SUBLANES, LANES), I32)),
        grid=(t // tm,),
        in_specs=[pl.BlockSpec((tm, k), lambda i: (i, 0)),
                  pl.BlockSpec((None, k, d), lambda i: (out_slot, 0, 0)),
                  pl.BlockSpec((tm, d), lambda i: (i, 0)),
                  pl.BlockSpec((1, d), lambda i: (0, 0)),
                  pl.BlockSpec((d, LANES), lambda i: (0, 0))],
        out_specs=(pl.BlockSpec((tm, d), lambda i: (i, 0)),
                   pl.BlockSpec((tm, d), lambda i: (i, 0)),
                   pl.BlockSpec((tm, LANES), lambda i: (i, 0)),
                   pl.BlockSpec((tm, LANES), lambda i: (i, 0)),
                   pl.BlockSpec((None, SUBLANES, LANES), lambda i: (i, 0, 0))),
        scratch_shapes=[pltpu.VMEM((tm, tm), BF16), pltpu.VMEM((k, d), BF16)],
        compiler_params=_cparams(1),
        name="moe_route",
    )(a, w_out, x, g.reshape(1, d), r_pad)


def _slot_rows(mi, loc_ref, base):
    e1, e2, d1, d2 = mi[0], mi[1], mi[2], mi[3]
    for e in range(N_EXPERTS):
        off = loc_ref[base + e]
        d1 = d1 + jnp.where(e1 == e, off, 0)
        d2 = d2 + jnp.where(e2 == e, off, 0)
    return d1, d2


def _segment_copies(base, seg_ref, n16_ref, loc_ref, hbm_ref, vmem_ref, sem, *, to_hbm, wait):
    for e in range(N_EXPERTS):
        n16 = n16_ref[base + e]
        hbm0 = seg_ref[base + e]
        vmem0 = loc_ref[base + e]
        off = 0
        for size in SEG_SIZES:
            @pl.when((n16 & size) != 0)
            def _(off=off, size=size):
                h = hbm_ref.at[pl.ds(pl.multiple_of(hbm0 + off, SEG_ALIGN), size)]
                v = vmem_ref.at[pl.ds(pl.multiple_of(vmem0 + off, SEG_ALIGN), size)]
                cp = pltpu.make_async_copy(v, h, sem) if to_hbm else pltpu.make_async_copy(h, v, sem)
                if wait:
                    cp.wait()
                else:
                    cp.start()
            off = off + (n16 & size)


def _scatter_kernel(seg_ref, n16_ref, loc_ref, tail_ref, nv_ref, hn_ref, mi_ref, xs_ref, stage_ref,
                    zero_ref, sem):
    i = pl.program_id(0)
    tm = hn_ref.shape[0]
    rows = stage_ref.shape[1]
    tg = zero_ref.shape[0]
    slot = lax.rem(i, 2)

    @pl.when(i == 0)
    def _():
        zero_ref[...] = jnp.zeros_like(zero_ref)
        for e in range(N_EXPERTS):
            tail = pl.multiple_of(tail_ref[e], SEG_ALIGN)
            pltpu.make_async_copy(zero_ref, xs_ref.at[pl.ds(tail, tg)], sem.at[2]).start()
        for e in range(N_EXPERTS):
            tail = pl.multiple_of(tail_ref[e], SEG_ALIGN)
            pltpu.make_async_copy(zero_ref, xs_ref.at[pl.ds(tail, tg)], sem.at[2]).wait()

        def zero_tile(tile, carry):
            cp = pltpu.make_async_copy(zero_ref, xs_ref.at[pl.ds(pl.multiple_of(tile * tg, tg), tg)],
                                       sem.at[2])
            cp.start()
            cp.wait()
            return carry

        lax.fori_loop(nv_ref[0], xs_ref.shape[0] // tg, zero_tile, 0)

    base = i * N_EXPERTS
    mi_t = mi_ref[...].T
    d1, d2 = _slot_rows([mi_t[k:k + 1, :] for k in range(4)], loc_ref, base)
    row = lax.broadcasted_iota(I32, (rows, tm), 0)
    onehot = ((row == d1) | (row == d2)).astype(BF16)
    stage_ref[slot] = jnp.dot(onehot, hn_ref[...], preferred_element_type=F32).astype(BF16)
    tabs = (seg_ref, n16_ref, loc_ref, xs_ref)
    _segment_copies(base, *tabs, stage_ref.at[slot], sem.at[slot], to_hbm=True, wait=False)

    @pl.when(i > 0)
    def _():
        _segment_copies(base - N_EXPERTS, *tabs, stage_ref.at[1 - slot], sem.at[1 - slot],
                        to_hbm=True, wait=True)

    @pl.when(i == pl.num_programs(0) - 1)
    def _():
        _segment_copies(base, *tabs, stage_ref.at[slot], sem.at[slot], to_hbm=True, wait=True)


def _scatter(hn, mi, seg, n16, loc, tails, n_valid, n_rows):
    t, d = hn.shape
    tm = SEG_TILE
    return pl.pallas_call(
        _scatter_kernel,
        out_shape=jax.ShapeDtypeStruct((n_rows, d), BF16),
        grid_spec=pltpu.PrefetchScalarGridSpec(
            num_scalar_prefetch=5,
            grid=(t // tm,),
            in_specs=[pl.BlockSpec((tm, d), lambda i, *_: (i, 0)),
                      pl.BlockSpec((tm, LANES), lambda i, *_: (i, 0))],
            out_specs=pl.BlockSpec(memory_space=pl.ANY),
            scratch_shapes=[pltpu.VMEM((2, SEG_STAGE, d), BF16), pltpu.VMEM((MOE_TILE, d), BF16),
                            pltpu.SemaphoreType.DMA((3,))]),
        compiler_params=pltpu.CompilerParams(dimension_semantics=("arbitrary",),
                                             vmem_limit_bytes=VMEM_LIMIT, has_side_effects=True),
        name="moe_scatter",
    )(seg, n16, loc, tails, n_valid, hn, mi)


def _seg_expert_kernel(te_ref, tb_ref, nr_ref, nv_ref, xs_ref, wg_ref, wu_ref, wd_ref, ys_ref, acc_ref):
    i, f = pl.program_id(0), pl.program_id(1)
    step = xs_ref.shape[0] // MOE_HEIGHTS

    @pl.when(i < nv_ref[0])
    def _():
        @pl.when(f == 0)
        def _():
            acc_ref[...] = jnp.zeros_like(acc_ref)

        nrow = nr_ref[i]
        for q in range(1, MOE_HEIGHTS + 1):
            lo_ok = nrow > (q - 1) * step
            in_q = lo_ok if q == MOE_HEIGHTS else lo_ok & (nrow <= q * step)

            @pl.when(in_q)
            def _(q=q):
                _swiglu_accumulate(xs_ref, wg_ref, wu_ref, wd_ref, acc_ref, rows=q * step)

        @pl.when(f == pl.num_programs(1) - 1)
        def _():
            ys_ref[...] = acc_ref[...].astype(ys_ref.dtype)

    @pl.when((i >= nv_ref[0]) & (f == pl.num_programs(1) - 1))
    def _():
        ys_ref[...] = jnp.zeros_like(ys_ref)


def _seg_experts(xs, w_gate_up, w_down, slot, tile_expert, tile_block, tile_rows, n_valid, n_tiles):
    n_rows, d = xs.shape
    ff = w_down.shape[-2]
    tg, tf = MOE_TILE, FF_TILE
    nf = ff // tf

    def fidx(i, f, nv):
        return jnp.where(i < nv[0], f, nf - 1)

    return pl.pallas_call(
        _seg_expert_kernel,
        out_shape=jax.ShapeDtypeStruct((n_tiles * tg, d), BF16),
        grid_spec=pltpu.PrefetchScalarGridSpec(
            num_scalar_prefetch=4,
            grid=(n_tiles, nf),
            in_specs=[pl.BlockSpec((tg, d), lambda i, f, te, tb, nr, nv: (tb[i], 0)),
                      pl.BlockSpec((None, None, d, tf),
                                   lambda i, f, te, tb, nr, nv: (slot, te[i], 0, fidx(i, f, nv))),
                      pl.BlockSpec((None, None, d, tf),
                                   lambda i, f, te, tb, nr, nv: (slot, te[i], 0, nf + fidx(i, f, nv))),
                      pl.BlockSpec((None, None, tf, d),
                                   lambda i, f, te, tb, nr, nv: (slot, te[i], fidx(i, f, nv), 0))],
            out_specs=pl.BlockSpec((tg, d), lambda i, f, te, tb, nr, nv: (i, 0)),
            scratch_shapes=[pltpu.VMEM((tg, d), F32)]),
        compiler_params=_cparams(2),
        name="moe_experts",
    )(tile_expert, tile_block, tile_rows, n_valid, xs, w_gate_up, w_gate_up, w_down)


def _gather_kernel(seg_ref, n16_ref, loc_ref, x_ref, mi_ref, mf_ref, gf_ref, ys_ref, o_ref,
                   *rest, norm):
    if norm == "next":
        nxt_ref, ybuf_ref, sem = rest
    else:
        ybuf_ref, sem = rest
    i = pl.program_id(0)
    tm = x_ref.shape[0]
    rows = ybuf_ref.shape[1]
    slot = lax.rem(i, 2)
    base = i * N_EXPERTS
    tabs = (seg_ref, n16_ref, loc_ref, ys_ref)

    @pl.when(i == 0)
    def _():
        ybuf_ref[...] = jnp.zeros_like(ybuf_ref)
        _segment_copies(base, *tabs, ybuf_ref.at[0], sem.at[0], to_hbm=False, wait=False)

    @pl.when(i + 1 < pl.num_programs(0))
    def _():
        _segment_copies(base + N_EXPERTS, *tabs, ybuf_ref.at[1 - slot], sem.at[1 - slot],
                        to_hbm=False, wait=False)

    mi = mi_ref[...]
    d1, d2 = _slot_rows([mi[:, k:k + 1] for k in range(4)], loc_ref, base)
    col = lax.broadcasted_iota(I32, (tm, rows), 1)
    gates = mf_ref[...]
    _segment_copies(base, *tabs, ybuf_ref.at[slot], sem.at[slot], to_hbm=False, wait=True)
    ybuf = ybuf_ref[slot]
    out = x_ref[...]
    for k, dk in enumerate((d1, d2)):
        yk = jnp.dot((col == dk).astype(BF16), ybuf, preferred_element_type=F32)
        out = out + gates[:, k:k + 1] * yk
    if norm == "final":
        out = _rms_rows(out, gf_ref[...])
    o_ref[...] = out
    if norm == "next":
        nxt_ref[...] = _rms_rows(out, gf_ref[...]).astype(BF16)


def _gather(x, mi, mf, ys, seg, n16, loc, g_norm, norm):
    t, d = x.shape
    tm = SEG_TILE
    row_spec = pl.BlockSpec((tm, d), lambda i, *_: (i, 0))
    out_shape = jax.ShapeDtypeStruct((t, d), F32)
    emit_next = norm == "next"
    return pl.pallas_call(
        functools.partial(_gather_kernel, norm=norm),
        out_shape=(out_shape, jax.ShapeDtypeStruct((t, d), BF16)) if emit_next else out_shape,
        grid_spec=pltpu.PrefetchScalarGridSpec(
            num_scalar_prefetch=3,
            grid=(t // tm,),
            in_specs=[row_spec,
                      pl.BlockSpec((tm, LANES), lambda i, *_: (i, 0)),
                      pl.BlockSpec((tm, LANES), lambda i, *_: (i, 0)),
                      pl.BlockSpec((1, d), lambda i, *_: (0, 0)),
                      pl.BlockSpec(memory_space=pl.ANY)],
            out_specs=(row_spec, row_spec) if emit_next else row_spec,
            scratch_shapes=[pltpu.VMEM((2, SEG_STAGE, d), BF16), pltpu.SemaphoreType.DMA((2,))]),
        compiler_params=_cparams(1),
        name="moe_gather",
    )(seg, n16, loc, x, mi, mf, g_norm.reshape(1, d).astype(F32), ys)


def _moe_seg(a, w_out, out_slot, x, g, router, w_gate_up, w_down, slot, g_norm, norm):
    t, d = x.shape
    tg = MOE_TILE
    x, hn, mi, mf, cnt = _route(a, w_out, out_slot, x, g, router)

    counts = cnt[:, 0, :N_EXPERTS]
    n16 = (counts + SEG_ALIGN - 1) // SEG_ALIGN * SEG_ALIGN
    rows_e = jnp.sum(n16, axis=0)
    tiles_e = (rows_e + tg - 1) // tg
    tile_start = jnp.cumsum(tiles_e) - tiles_e
    row_start = tile_start * tg
    seg = row_start[None, :] + jnp.cumsum(n16, axis=0) - n16
    loc = jnp.cumsum(n16, axis=1) - n16
    n_valid = jnp.sum(tiles_e).astype(I32)
    n_tiles = (2 * t + counts.size * (SEG_ALIGN - 1)) // tg + N_EXPERTS
    n_rows = (n_tiles + 1) * tg
    tidx = jnp.minimum(jnp.arange(n_tiles, dtype=I32), n_valid - 1)
    tile_expert = (jnp.sum(tidx[:, None] >= tile_start[None, :], axis=1) - 1).astype(I32)
    tails = (row_start + rows_e).astype(I32)
    seg, n16, loc = (a.reshape(-1).astype(I32) for a in (seg, n16, loc))

    n_valid = n_valid.reshape(1)
    xs = _scatter(hn, mi, seg, n16, loc, tails, n_valid, n_rows)
    tile_rows = (tails[tile_expert] - tidx * tg).astype(I32)
    ys = _seg_experts(xs, w_gate_up, w_down, slot, tile_expert, tidx, tile_rows, n_valid.reshape(1), n_tiles)
    return _gather(x, mi, mf, ys, seg, n16, loc, g_norm, norm)


def kernel(x, rel_bias, norm_mix, norm_ffn, norm_final, a_w_in, a_w_group, a_scale, a_w_out,
           b_w_in, b_w_out, c_w_in, c_lambda, c_subln, c_w_out, d_w_in, d_sink, d_w_out,
           f_w_gate_up, f_w_down, m_router, m_w_gate_up, m_w_down):
    batch, seq, d = x.shape
    h = x.reshape(batch * seq, d)

    u = _norm_matmul(h, norm_mix[0], a_w_in, 0, F32)
    y = _pool_mix(u, a_w_group, a_scale, 0, batch, seq)
    h = _matmul_res(y, a_w_out, 0, h)
    h = _ffn(h, norm_ffn[0], f_w_gate_up, f_w_down, 0)

    qkv = _dil_proj(h.reshape(batch, seq, d), norm_mix[1], b_w_in, 0, tuple(p[1] for p in DIL_PAIRS))
    o = _dilated_attention(qkv, rel_bias, batch, seq)
    h, hn = _moe_seg(o, b_w_out, 0, h, norm_ffn[1], m_router[0], m_w_gate_up, m_w_down, 0,
                     norm_mix[2], "next")

    qkv = _matmul(hn, c_w_in, 0, BF16)
    o = _diff_attention(qkv, c_lambda, c_subln, 0, rel_bias, batch, seq)
    h = _matmul_res(o, c_w_out, 0, h)
    h, hn = _ffn(h, norm_ffn[2], f_w_gate_up, f_w_down, 1, g_next=norm_mix[3])

    qkv = _matmul(hn, d_w_in, 0, BF16)
    o = _gqa_attention(qkv, d_sink[0], rel_bias, batch, seq)
    h = _moe_seg(o, d_w_out, 0, h, norm_ffn[3], m_router[1], m_w_gate_up, m_w_down, 1,
                 norm_final, "final")
    return h.reshape(batch, seq, d)
```

```python
import collections
import functools
import math

import numpy as np
import jax
import jax.numpy as jnp
from jax import lax
from jax.experimental import pallas as pl
from jax.experimental.pallas import tpu as pltpu

F32 = jnp.float32
BF16 = jnp.bfloat16
I32 = jnp.int32

RMS_EPS = 1e-6
HEAD_DIM = 64
N_BUCKETS = 32
MAX_DISTANCE = 1024
POOL_WINDOWS = (2, 4, 8, 16)
DIL_PAIRS = ((128, 1), (512, 4), (2048, 16))
Q_SCALE = HEAD_DIM ** -0.5
GQA_KV_HEADS = 4
GQA_WINDOW = 128
N_EXPERTS = 8
DIFF_LAYER = 2

LANES = 128
SUBLANES = 8
VMEM_LIMIT = 56 * 1024 * 1024
ROW_TILE = 1024
FF_TILE = 512
MOE_TILE = 1024
MOE_HEIGHTS = 4
ROUTE_ROW_SPLIT = 2
SEG_TILE = 512
ATT_TQ = 128
DIL_UNITS = 8
GQA_UNITS = 1
DIFF_ROW_SPLIT = 4
DIL_HEAD_SPLIT = 2
DIFF_TQ = 1024
MASK_VALUE = -1e30


def _cparams(n_axes, vmem=VMEM_LIMIT):
    return pltpu.CompilerParams(dimension_semantics=("arbitrary",) * n_axes,
                                vmem_limit_bytes=vmem)


def _next_pow2(n):
    return 1 << (n - 1).bit_length()


def _rms_rows(x, g):
    ms = jnp.mean(x * x, axis=-1, keepdims=True)
    return x * lax.rsqrt(ms + RMS_EPS) * g


def _norm_matmul_kernel(x_ref, g_ref, w_ref, o_ref, hn_ref):
    @pl.when(pl.program_id(1) == 0)
    def _():
        hn_ref[...] = _rms_rows(x_ref[...], g_ref[...]).astype(BF16)

    o_ref[...] = jnp.dot(hn_ref[...], w_ref[...].astype(BF16),
                         preferred_element_type=F32).astype(o_ref.dtype)


def _norm_matmul(x, g, w, slot, out_dtype):
    t, d = x.shape
    n = w.shape[-1]
    tm = ROW_TILE
    tn = next(c for c in (1024, 768, 512, 256, 128) if n % c == 0)
    return pl.pallas_call(
        _norm_matmul_kernel,
        out_shape=jax.ShapeDtypeStruct((t, n), out_dtype),
        grid=(t // tm, n // tn),
        in_specs=[pl.BlockSpec((tm, d), lambda i, j: (i, 0)),
                  pl.BlockSpec((1, d), lambda i, j: (0, 0)),
                  pl.BlockSpec((None, d, tn), lambda i, j: (slot, 0, j))],
        out_specs=pl.BlockSpec((tm, tn), lambda i, j: (i, j)),
        scratch_shapes=[pltpu.VMEM((tm, d), BF16)],
        compiler_params=_cparams(2),
        name="norm_matmul",
    )(x, g.reshape(1, d), w)


def _matmul_kernel(a_ref, w_ref, o_ref):
    o_ref[...] = jnp.dot(a_ref[...], w_ref[...].astype(BF16),
                         preferred_element_type=F32).astype(o_ref.dtype)


def _matmul(a, w, slot, out_dtype):
    t, d = a.shape
    n = w.shape[-1]
    tm = 2 * ROW_TILE
    tn = next(c for c in (1024, 768, 512, 256, 128) if n % c == 0)
    return pl.pallas_call(
        _matmul_kernel,
        out_shape=jax.ShapeDtypeStruct((t, n), out_dtype),
        grid=(t // tm, n // tn),
        in_specs=[pl.BlockSpec((tm, d), lambda i, j: (i, 0)),
                  pl.BlockSpec((None, d, tn), lambda i, j: (slot, 0, j))],
        out_specs=pl.BlockSpec((tm, tn), lambda i, j: (i, j)),
        compiler_params=_cparams(2),
        name="matmul",
    )(a, w)


def _matmul_res_kernel(a_ref, w_ref, x_ref, o_ref, wb_ref):
    @pl.when(pl.program_id(0) == 0)
    def _():
        wb_ref[...] = w_ref[...].astype(BF16)

    o_ref[...] = x_ref[...] + jnp.dot(a_ref[...], wb_ref[...], preferred_element_type=F32)


def _matmul_res(a, w, slot, x):
    t, k = a.shape
    d = w.shape[-1]
    tm = ROW_TILE
    return pl.pallas_call(
        _matmul_res_kernel,
        out_shape=jax.ShapeDtypeStruct((t, d), F32),
        grid=(t // tm,),
        in_specs=[pl.BlockSpec((tm, k), lambda i: (i, 0)),
                  pl.BlockSpec((None, k, d), lambda i: (slot, 0, 0)),
                  pl.BlockSpec((tm, d), lambda i: (i, 0))],
        out_specs=pl.BlockSpec((tm, d), lambda i: (i, 0)),
        scratch_shapes=[pltpu.VMEM((k, d), BF16)],
        compiler_params=_cparams(1),
        name="matmul_res",
    )(a, w, x)


def _silu_mul(g, u):
    return g * (1.0 / (1.0 + jnp.exp(-g))) * u


def _swiglu_accumulate(h_ref, wg_ref, wu_ref, wd_ref, acc_ref, rows=None):
    rows = h_ref.shape[0] if rows is None else rows
    h = h_ref[pl.ds(0, rows), :]
    gate = jnp.dot(h, wg_ref[...].astype(BF16), preferred_element_type=F32)
    up = jnp.dot(h, wu_ref[...].astype(BF16), preferred_element_type=F32)
    a = _silu_mul(gate, up).astype(BF16)
    acc_ref[pl.ds(0, rows), :] += jnp.dot(a, wd_ref[...].astype(BF16), preferred_element_type=F32)


def _ffn_kernel(x_ref, g_ref, gn_ref, wg_ref, wu_ref, wd_ref, o_ref, *rest, emit_next):
    if emit_next:
        nxt_ref, hn_ref, acc_ref = rest
    else:
        hn_ref, acc_ref = rest
    f = pl.program_id(1)

    @pl.when(f == 0)
    def _():
        hn_ref[...] = _rms_rows(x_ref[...], g_ref[...]).astype(BF16)
        acc_ref[...] = jnp.zeros_like(acc_ref)

    _swiglu_accumulate(hn_ref, wg_ref, wu_ref, wd_ref, acc_ref)

    @pl.when(f == pl.num_programs(1) - 1)
    def _():
        out = x_ref[...] + acc_ref[...]
        o_ref[...] = out
        if emit_next:
            nxt_ref[...] = _rms_rows(out, gn_ref[...]).astype(BF16)


def _ffn(x, g, w_gate_up, w_down, slot, g_next=None):
    t, d = x.shape
    ff = w_down.shape[-2]
    tm, tf = ROW_TILE, FF_TILE
    nf = ff // tf
    emit_next = g_next is not None
    gn = (g_next if emit_next else g).reshape(1, d)
    row_spec = pl.BlockSpec((tm, d), lambda i, f: (i, 0))
    vec_spec = pl.BlockSpec((1, d), lambda i, f: (0, 0))
    out_shape = jax.ShapeDtypeStruct((t, d), F32)
    return pl.pallas_call(
        functools.partial(_ffn_kernel, emit_next=emit_next),
        out_shape=(out_shape, jax.ShapeDtypeStruct((t, d), BF16)) if emit_next else out_shape,
        grid=(t // tm, nf),
        in_specs=[row_spec, vec_spec, vec_spec,
                  pl.BlockSpec((None, d, tf), lambda i, f: (slot, 0, f)),
                  pl.BlockSpec((None, d, tf), lambda i, f: (slot, 0, nf + f)),
                  pl.BlockSpec((None, tf, d), lambda i, f: (slot, f, 0))],
        out_specs=(row_spec, row_spec) if emit_next else row_spec,
        scratch_shapes=[pltpu.VMEM((tm, d), BF16), pltpu.VMEM((tm, d), F32)],
        compiler_params=_cparams(2),
        name="ffn",
    )(x, g.reshape(1, d), gn, w_gate_up, w_gate_up, w_down)


POOL_PAD = 16


def _pool_kernel(u_ref, wg_ref, sc_ref, o_ref, pad_ref):
    s, c = u_ref.shape
    grp = pl.program_id(1)
    pad_ref[pl.ds(0, POOL_PAD), :] = jnp.zeros((POOL_PAD, c), F32)
    pad_ref[pl.ds(POOL_PAD + s, POOL_PAD), :] = jnp.zeros((POOL_PAD, c), F32)
    pad_ref[pl.ds(POOL_PAD, s), :] = u_ref[...]
    wb = wg_ref[...].astype(BF16)
    chunk = 256

    for gi, w in enumerate(POOL_WINDOWS):
        @pl.when(grp == gi)
        def _(w=w):
            for c0 in range(0, s, chunk):
                acc = None
                for dlt in range(-(w // 2), w - w // 2):
                    piece = pad_ref[pl.ds(POOL_PAD + c0 + dlt, chunk), :]
                    acc = piece if acc is None else acc + piece
                pos = c0 + lax.broadcasted_iota(I32, (chunk, 1), 0)
                lo = jnp.maximum(pos - w // 2, 0)
                hi = jnp.minimum(pos + (w - w // 2), s)
                cnt = (hi - lo).astype(F32)
                mixed = acc / cnt - u_ref[pl.ds(c0, chunk), :]
                y = jnp.dot(mixed.astype(BF16), wb, preferred_element_type=F32)
                o_ref[pl.ds(c0, chunk), :] = (y * sc_ref[...]).astype(o_ref.dtype)


def _pool_mix(u, w_group, scale, slot, batch, seq):
    d = u.shape[1]
    ng = len(POOL_WINDOWS)
    c = d // ng
    u3 = u.reshape(batch, seq, d)
    out = pl.pallas_call(
        _pool_kernel,
        out_shape=jax.ShapeDtypeStruct((batch, seq, d), BF16),
        grid=(batch, ng),
        in_specs=[pl.BlockSpec((None, seq, c), lambda b, g: (b, 0, g)),
                  pl.BlockSpec((None, None, c, c), lambda b, g: (slot, g, 0, 0)),
                  pl.BlockSpec((1, c), lambda b, g: (slot, g))],
        out_specs=pl.BlockSpec((None, seq, c), lambda b, g: (b, 0, g)),
        scratch_shapes=[pltpu.VMEM((seq + 2 * POOL_PAD, c), F32)],
        compiler_params=_cparams(2),
        name="pool_mix",
    )(u3, w_group, scale)
    return out.reshape(batch * seq, d)


def _rel_bucket_np(rel):
    half = N_BUCKETS // 2
    max_exact = half // 2
    n = np.abs(rel)
    ratio = np.log(np.maximum(n, 1).astype(np.float32) / np.float32(max_exact))
    big = max_exact + (ratio / np.float32(math.log(MAX_DISTANCE / max_exact))
                       * np.float32(half - max_exact)).astype(np.int32)
    big = np.minimum(big, half - 1)
    return np.where(rel > 0, half, 0) + np.where(n < max_exact, n, big)


def _seg_table(rel_bias, rel, valid):
    onehot = jnp.asarray(np.eye(N_BUCKETS, dtype=np.float32)[_rel_bucket_np(rel)])
    vals = jnp.einsum("vwb,bh->vwh", onehot, rel_bias.astype(F32),
                      precision=lax.Precision.HIGHEST)
    vals = jnp.where(jnp.asarray(valid)[..., None], vals, MASK_VALUE)
    return jnp.transpose(vals, (2, 0, 1))


def _toeplitz(seg_row, rows, cols):
    w = seg_row.shape[1]
    full = jnp.broadcast_to(seg_row, (rows, w))
    rolled = pltpu.roll(full, w - (rows - 1), 1, stride=1, stride_axis=0)
    return rolled[:, :cols]


def _band_variants(hw, n_tiles):
    if n_tiles == 1:
        return (0,)
    return (0, -hw, -2 * hw)


def _band_segs(rel_bias, tq, kw, half, dil, variants, width):
    c = np.arange(width)
    rel = np.stack([r0 + c - (tq - 1) for r0 in variants])
    valid = (np.abs(rel) <= half) & (c[None, :] < tq + kw - 1)
    return _seg_table(rel_bias, rel * dil, valid)


def _band_window(qi, nq, tq, hw, kw, ln):
    if nq == 1:
        return 0, 0
    var = jnp.where(qi == 0, 0, jnp.where(qi == nq - 1, 2, 1))
    return var, jnp.clip(qi * tq - hw, 0, ln - kw)


DilCfg = collections.namedtuple("DilCfg", "dil ln nq hw kw n_var")


def _dil_proj_kernel(x_ref, g_ref, w_ref, o_ref, slab_ref, hn_ref, *, dils):
    grp, c = pl.program_id(1), pl.program_id(2)
    seq, d = x_ref.shape
    n_slabs = d // LANES
    chunk = 256

    @pl.when((grp == 0) & (c == 0))
    def _():
        for c0 in range(0, seq, chunk):
            hn = _rms_rows(x_ref[pl.ds(c0, chunk), :], g_ref[...])
            for s in range(n_slabs):
                slab_ref[s, pl.ds(c0, chunk), :] = hn[:, s * LANES:(s + 1) * LANES]

    for gi, dil in enumerate(dils):
        @pl.when((grp == gi) & (c == 0))
        def _(dil=dil):
            ln = seq // dil
            for r in range(dil):
                for c0 in range(0, ln, chunk):
                    n = min(chunk, ln)
                    rows = [slab_ref[s, pl.ds(r + c0 * dil, n, stride=dil), :] for s in range(n_slabs)]
                    hn_ref[pl.ds(r * ln + c0, n), :] = jnp.concatenate(rows, axis=1).astype(BF16)

    res = jnp.dot(hn_ref[...], w_ref[...].astype(BF16), preferred_element_type=F32)
    res = res * jnp.where(c < pl.num_programs(2) // 3, Q_SCALE, 1.0)
    for s in range(o_ref.shape[0]):
        o_ref[s] = res[:, s * LANES:(s + 1) * LANES].astype(o_ref.dtype)


def _dil_proj(x3, g, w_in, slot, dils):
    batch, seq, d = x3.shape
    n = w_in.shape[-1]
    tn = FF_TILE
    per_group = n // len(dils) // tn
    spt = tn // LANES
    return pl.pallas_call(
        functools.partial(_dil_proj_kernel, dils=dils),
        out_shape=jax.ShapeDtypeStruct((batch, n // LANES, seq, LANES), BF16),
        grid=(batch, len(dils), per_group),
        in_specs=[pl.BlockSpec((None, seq, d), lambda b, g, c: (b, 0, 0)),
                  pl.BlockSpec((1, d), lambda b, g, c: (0, 0)),
                  pl.BlockSpec((None, d, tn), lambda b, g, c: (slot, 0, g * per_group + c))],
        out_specs=pl.BlockSpec((None, spt, seq, LANES), lambda b, g, c: (b, g * per_group + c, 0, 0)),
        scratch_shapes=[pltpu.VMEM((d // LANES, seq, LANES), F32), pltpu.VMEM((seq, d), BF16)],
        compiler_params=_cparams(3),
        name="dil_proj",
    )(x3, g.reshape(1, d), w_in)


def _dil_unit(cfg, unit, q4, k_ref, v_ref, bias_ref, acc_ref, m_ref, l_ref, *, tq, first, last):
    dil, ln, nq, kw = cfg.dil, cfg.ln, cfg.nq, cfg.kw
    n_pairs = q4.shape[0]
    if nq == 1:
        r, qi = unit, 0
    elif dil == 1:
        r, qi = 0, unit
    else:
        r, qi = unit // nq, unit % nq
    var, ks = _band_window(qi, nq, tq, cfg.hw, kw, ln)
    krow = pl.multiple_of(r * ln + ks, cfg.hw)
    if dil == 1:
        rows = pl.ds(pl.multiple_of(qi * tq, tq), tq)
    else:
        rows = pl.ds(qi * tq * dil + r, tq, stride=dil)

    even = lax.broadcasted_iota(I32, (n_pairs, tq, LANES), 2) < HEAD_DIM
    zero = jnp.zeros_like(q4)
    q8 = jnp.concatenate([jnp.where(even, q4, zero), jnp.where(even, zero, q4)], axis=0)
    k4 = k_ref[:, pl.ds(krow, kw), :]
    v4 = v_ref[:, pl.ds(krow, kw), :]
    k8 = jnp.concatenate([k4, k4], axis=0)
    va = jnp.concatenate([v4, jnp.ones_like(v4)], axis=2)
    v8 = jnp.concatenate([va, va], axis=0)
    s = lax.dot_general(q8, k8, (((2,), (2,)), ((0,), (0,))), preferred_element_type=F32)
    s = s + bias_ref[var]
    m8 = jnp.max(s, axis=-1, keepdims=True)
    p = jnp.exp(s - m8).astype(BF16)
    ov = lax.dot_general(p, v8, (((2,), (1,)), ((0,), (0,))), preferred_element_type=F32)
    o_c = jnp.where(even, ov[:n_pairs, :, :LANES], ov[n_pairs:, :, :LANES])
    l_c = jnp.where(even, ov[:n_pairs, :, LANES:], ov[n_pairs:, :, LANES:])
    m_c = jnp.where(even, m8[:n_pairs], m8[n_pairs:])
    if first:
        m_n, l_n, acc_n = m_c, l_c, o_c
    else:
        m_o = jnp.stack([m_ref[pp, rows, :] for pp in range(n_pairs)])
        l_o = jnp.stack([l_ref[pp, rows, :] for pp in range(n_pairs)])
        acc_o = jnp.stack([acc_ref[pp, rows, :] for pp in range(n_pairs)])
        m_n = jnp.maximum(m_o, m_c)
        a_o = jnp.exp(m_o - m_n)
        a_c = jnp.exp(m_c - m_n)
        l_n = a_o * l_o + a_c * l_c
        acc_n = a_o * acc_o + a_c * o_c
    if last:
        acc_n = acc_n / l_n
    for pp in range(n_pairs):
        acc_ref[pp, rows, :] = acc_n[pp]
        if not last:
            m_ref[pp, rows, :] = m_n[pp]
            l_ref[pp, rows, :] = l_n[pp]


def _dil_attn_kernel(*refs, cfgs, tq):
    ng = len(cfgs)
    seg_refs = refs[:ng]
    q_ref, k_ref, v_ref, o_ref = refs[ng:ng + 4]
    bias_refs = refs[ng + 4:2 * ng + 4]
    acc_ref, m_ref, l_ref = refs[2 * ng + 4:]
    hf, b, grp, step = (pl.program_id(i) for i in range(4))
    n_pairs = q_ref.shape[0]

    @pl.when((b == 0) & (grp == 0) & (step == 0))
    def _():
        for gi, cfg in enumerate(cfgs):
            for v in range(cfg.n_var):
                for odd in range(2):
                    for pp in range(n_pairs):
                        head = hf * 2 * n_pairs + 2 * pp + odd
                        row = seg_refs[gi][head, pl.ds(v, 1), :]
                        bias_refs[gi][v, odd * n_pairs + pp] = _toeplitz(row, tq, cfg.kw)

    for gi, cfg in enumerate(cfgs):
        @pl.when(grp == gi)
        def _(gi=gi, cfg=cfg):
            for uu in range(DIL_UNITS):
                _dil_unit(cfg, step * DIL_UNITS + uu, q_ref[:, pl.ds(uu * tq, tq), :], k_ref, v_ref,
                          bias_refs[gi], acc_ref, m_ref, l_ref, tq=tq, first=gi == 0, last=gi == ng - 1)

    @pl.when((grp == ng - 1) & (step == pl.num_programs(3) - 1))
    def _():
        slabs = [acc_ref[s] for s in range(n_pairs)]
        o_ref[...] = jnp.concatenate(slabs, axis=1).astype(o_ref.dtype)


def _dilated_attention(qkv, rel_bias, batch, seq):
    ng = len(DIL_PAIRS)
    d = qkv.shape[1] * LANES // (3 * ng)
    hd = d // DIL_HEAD_SPLIT
    n_pairs = hd // LANES
    tq = ATT_TQ
    cfgs, segs = [], []
    for win, dil in DIL_PAIRS:
        half = win // (2 * dil)
        ln = seq // dil
        nq = ln // tq
        kw = min(tq + 2 * half, ln)
        variants = _band_variants(half, nq)
        cfgs.append(DilCfg(dil, ln, nq, half, kw, len(variants)))
        segs.append(_band_segs(rel_bias, tq, kw, half, dil, variants, _next_pow2(tq + kw - 1)))
    steps = seq // tq // DIL_UNITS
    cb = DIL_HEAD_SPLIT

    in_specs = [pl.BlockSpec(sg.shape, lambda hf, b, g, s: (0, 0, 0)) for sg in segs]
    in_specs += [pl.BlockSpec((None, n_pairs, DIL_UNITS * tq, LANES),
                              lambda hf, b, g, s: (b, (g * 3) * cb + hf, s, 0)),
                 pl.BlockSpec((None, n_pairs, seq, LANES),
                              lambda hf, b, g, s: (b, (g * 3 + 1) * cb + hf, 0, 0)),
                 pl.BlockSpec((None, n_pairs, seq, LANES),
                              lambda hf, b, g, s: (b, (g * 3 + 2) * cb + hf, 0, 0))]
    scratch = [pltpu.VMEM((c.n_var, 2 * n_pairs, tq, c.kw), F32) for c in cfgs]
    scratch += [pltpu.VMEM((n_pairs, seq, LANES), F32) for _ in range(3)]
    out = pl.pallas_call(
        functools.partial(_dil_attn_kernel, cfgs=tuple(cfgs), tq=tq),
        out_shape=jax.ShapeDtypeStruct((batch, seq, d), BF16),
        grid=(DIL_HEAD_SPLIT, batch, ng, steps),
        in_specs=in_specs,
        out_specs=pl.BlockSpec((None, seq, hd), lambda hf, b, g, s: (b, 0, hf)),
        scratch_shapes=scratch,
        compiler_params=_cparams(4),
        name="dil_attn",
    )(*segs, qkv, qkv, qkv)
    return out.reshape(batch * seq, d)


def _gqa_kernel(sink_ref, seg_ref, q_ref, k_ref, v_ref, o_ref, bias_ref, *, tq, kw, hw, grp):
    b, step = pl.program_id(0), pl.program_id(1)
    seq_len = k_ref.shape[0]
    nq = seq_len // tq
    n_kv = k_ref.shape[1] // HEAD_DIM

    @pl.when((b == 0) & (step == 0))
    def _():
        for v in range(3):
            for kh in range(n_kv):
                for gq in range(grp):
                    bias_ref[v, kh, pl.ds(gq * tq, tq), :] = _toeplitz(
                        seg_ref[kh * grp + gq, pl.ds(v, 1), :], tq, kw)

    row = lax.broadcasted_iota(I32, (grp * tq, 1), 0)
    for u in range(GQA_UNITS):
        qi = step * GQA_UNITS + u
        qrows = pl.ds(u * tq, tq)
        var, ks = _band_window(qi, nq, tq, hw, kw, seq_len)
        ks = pl.multiple_of(ks, hw)
        outs = [None] * (n_kv * grp)
        for kh in range(n_kv):
            cs = slice(kh * HEAD_DIM, (kh + 1) * HEAD_DIM)
            qs = jnp.concatenate(
                [q_ref[qrows, (kh * grp + gq) * HEAD_DIM:(kh * grp + gq + 1) * HEAD_DIM]
                 for gq in range(grp)], axis=0) * Q_SCALE
            kk = k_ref[pl.ds(ks, kw), cs]
            vv = v_ref[pl.ds(ks, kw), cs]
            s = lax.dot_general(qs, kk, (((1,), (1,)), ((), ())), preferred_element_type=F32)
            s = s + bias_ref[var, kh]
            sk = jnp.zeros((grp * tq, 1), F32)
            for gq in range(grp):
                sk = jnp.where((row >= gq * tq) & (row < (gq + 1) * tq), sink_ref[kh * grp + gq], sk)
            m = jnp.maximum(jnp.max(s, axis=-1, keepdims=True), sk)
            e = jnp.exp(s - m).astype(BF16)
            va = jnp.concatenate([vv, jnp.ones_like(vv)], axis=1)
            ov = jnp.dot(e, va, preferred_element_type=F32)
            o = ov[:, :HEAD_DIM] / (ov[:, HEAD_DIM:] + jnp.exp(sk - m))
            for gq in range(grp):
                outs[kh * grp + gq] = o[gq * tq:(gq + 1) * tq, :]
        o_ref[qrows, :] = jnp.concatenate(outs, axis=1).astype(o_ref.dtype)


def _gqa_attention(qkv, sink, rel_bias, batch, seq):
    t, ncol = qkv.shape
    n_q = sink.shape[0]
    d = n_q * HEAD_DIM
    kvw = GQA_KV_HEADS * HEAD_DIM
    grp = n_q // GQA_KV_HEADS
    tq = hw = GQA_WINDOW
    kw = 3 * GQA_WINDOW
    nq = seq // tq
    variants = _band_variants(hw, nq)
    width = _next_pow2(tq + kw - 1)
    seg = _band_segs(rel_bias, tq, kw, GQA_WINDOW, 1, variants, width)
    qkv_v = qkv.reshape(batch, seq, ncol)
    out = pl.pallas_call(
        functools.partial(_gqa_kernel, tq=tq, kw=kw, hw=hw, grp=grp),
        out_shape=jax.ShapeDtypeStruct((batch, seq, d), BF16),
        grid=(batch, nq // GQA_UNITS),
        in_specs=[pl.BlockSpec(memory_space=pltpu.SMEM),
                  pl.BlockSpec((n_q, 3, width), lambda b, qi: (0, 0, 0)),
                  pl.BlockSpec((None, GQA_UNITS * tq, d), lambda b, qi: (b, qi, 0)),
                  pl.BlockSpec((None, seq, kvw), lambda b, qi: (b, 0, d // kvw)),
                  pl.BlockSpec((None, seq, kvw), lambda b, qi: (b, 0, d // kvw + 1))],
        out_specs=pl.BlockSpec((None, GQA_UNITS * tq, d), lambda b, qi: (b, qi, 0)),
        scratch_shapes=[pltpu.VMEM((3, GQA_KV_HEADS, grp * tq, kw), F32)],
        compiler_params=_cparams(2),
        name="gqa_attn",
    )(sink.astype(F32), seg, qkv_v, qkv_v, qkv_v)
    return out.reshape(t, d)


def _diff_kernel(seg_ref, lam_ref, sub_ref, q_ref, k_ref, v_ref, o_ref, bias_ref, *, tq, lam_init):
    h, qi, b = pl.program_id(0), pl.program_id(1), pl.program_id(2)
    seq_len = k_ref.shape[0]
    nk = seq_len // tq

    @pl.when(b == 0)
    def _():
        for j in range(2):
            for ki in range(nk):
                row = seg_ref[h * 2 + j, pl.ds(ki - qi + nk - 1, 1), :]
                bias_ref[j, :, pl.ds(ki * tq, tq)] = _toeplitz(row, tq, tq)

    lv = lam_ref[...]
    s01 = jnp.sum(lv[0:1, :] * lv[1:2, :], axis=-1, keepdims=True)
    s23 = jnp.sum(lv[2:3, :] * lv[3:4, :], axis=-1, keepdims=True)
    lam = jnp.exp(s01) - jnp.exp(s23) + lam_init

    q = q_ref[...] * Q_SCALE
    k = k_ref[...]
    v = v_ref[...]
    va = jnp.concatenate([v, jnp.ones_like(v)], axis=1)
    map0 = lax.broadcasted_iota(I32, q.shape, 1) < HEAD_DIM
    zero = jnp.zeros_like(q)
    qm = (jnp.where(map0, q, zero), jnp.where(map0, zero, q))
    rc = tq // DIFF_ROW_SPLIT
    for c0 in range(0, tq, rc):
        outs = []
        for j in range(2):
            s = lax.dot_general(qm[j][c0:c0 + rc], k, (((1,), (1,)), ((), ())),
                                preferred_element_type=F32)
            s = s + bias_ref[j, pl.ds(c0, rc), :]
            m = jnp.max(s, axis=-1, keepdims=True)
            p = jnp.exp(s - m).astype(BF16)
            ov = jnp.dot(p, va, preferred_element_type=F32)
            outs.append(ov[:, :2 * HEAD_DIM] / ov[:, 2 * HEAD_DIM:])
        o = outs[0] - lam * outs[1]
        o = _rms_rows(o, sub_ref[...]) * (1.0 - lam_init)
        o_ref[pl.ds(c0, rc), :] = o.astype(o_ref.dtype)


def _diff_attention(qkv, lam_vecs, subln, slot, rel_bias, batch, seq):
    t, ncol = qkv.shape
    d = ncol // 3
    hd2 = 2 * HEAD_DIM
    n_heads = d // hd2
    tq = DIFF_TQ
    nk = seq // tq
    width = _next_pow2(2 * tq - 1)
    c = np.arange(width)
    rel = np.stack([(dl - (nk - 1)) * tq + c - (tq - 1) for dl in range(2 * nk - 1)])
    seg = _seg_table(rel_bias, rel, np.ones_like(rel, dtype=bool))
    lam_init = 0.8 - 0.6 * math.exp(-0.3 * DIFF_LAYER)
    qkv_v = qkv.reshape(batch, seq, ncol)
    out = pl.pallas_call(
        functools.partial(_diff_kernel, tq=tq, lam_init=lam_init),
        out_shape=jax.ShapeDtypeStruct((batch, seq, d), BF16),
        grid=(n_heads, seq // tq, batch),
        in_specs=[pl.BlockSpec(seg.shape, lambda h, qi, b: (0, 0, 0)),
                  pl.BlockSpec((None,) + lam_vecs.shape[1:], lambda h, qi, b: (slot, 0, 0)),
                  pl.BlockSpec((1, hd2), lambda h, qi, b: (slot, 0)),
                  pl.BlockSpec((None, tq, hd2), lambda h, qi, b: (b, qi, h)),
                  pl.BlockSpec((None, seq, hd2), lambda h, qi, b: (b, 0, n_heads + h)),
                  pl.BlockSpec((None, seq, hd2), lambda h, qi, b: (b, 0, 2 * n_heads + h))],
        out_specs=pl.BlockSpec((None, tq, hd2), lambda h, qi, b: (b, qi, h)),
        scratch_shapes=[pltpu.VMEM((2, tq, seq), F32)],
        compiler_params=_cparams(3),
        name="diff_attn",
    )(seg, lam_vecs, subln, qkv_v, qkv_v, qkv_v)
    return out.reshape(t, d)


SEG_ALIGN = 16
SEG_SIZES = (512, 256, 128, 64, 32, 16)
SEG_STAGE = 2 * SEG_TILE + N_EXPERTS * SEG_ALIGN


def _route_kernel(a_ref, wo_ref, x_ref, g_ref, r_ref, xn_ref, hn_ref, mi_ref, mf_ref, cnt_ref,
                  tri_ref, wb_ref):
    tm, d = x_ref.shape

    @pl.when(pl.program_id(0) == 0)
    def _():
        rr = lax.broadcasted_iota(I32, (tm, tm), 0)
        cc = lax.broadcasted_iota(I32, (tm, tm), 1)
        tri_ref[...] = (cc < rr).astype(BF16)
        wb_ref[...] = wo_ref[...].astype(BF16)

    r = r_ref[...]
    r_hi = r.astype(BF16)
    r_lo = (r - r_hi.astype(F32)).astype(BF16)
    rows = tm // ROUTE_ROW_SPLIT
    lane = lax.broadcasted_iota(I32, (rows, LANES), 1)

    picks = []
    for r0 in range(0, tm, rows):
        rs = pl.ds(r0, rows)
        xn = x_ref[rs, :] + jnp.dot(a_ref[rs, :], wb_ref[...], preferred_element_type=F32)
        xn_ref[rs, :] = xn
        hn = _rms_rows(xn, g_ref[...])
        h_hi = hn.astype(BF16)
        hn_ref[rs, :] = h_hi
        h_lo = (hn - h_hi.astype(F32)).astype(BF16)
        logits = (jnp.dot(h_hi, r_hi, preferred_element_type=F32)
                  + jnp.dot(h_hi, r_lo, preferred_element_type=F32)
                  + jnp.dot(h_lo, r_hi, preferred_element_type=F32))
        logits = jnp.where(lane < N_EXPERTS, logits, -jnp.inf)
        v1 = jnp.max(logits, axis=-1, keepdims=True)
        i1 = jnp.min(jnp.where(logits == v1, lane, LANES), axis=-1, keepdims=True)
        oh1 = lane == i1
        rest = jnp.where(oh1, -jnp.inf, logits)
        v2 = jnp.max(rest, axis=-1, keepdims=True)
        i2 = jnp.min(jnp.where(rest == v2, lane, LANES), axis=-1, keepdims=True)
        oh2 = lane == i2
        e2 = jnp.exp(v2 - v1)
        mf_ref[rs, :] = jnp.where(lane == 0, 1.0 / (1.0 + e2), jnp.where(lane == 1, e2 / (1.0 + e2), 0.0))
        picks.append((rs, i1, i2, oh1, oh2))

    sel = jnp.concatenate([(oh1 | oh2).astype(BF16) for _, _, _, oh1, oh2 in picks], axis=0)
    for rs, i1, i2, oh1, oh2 in picks:
        before = jnp.dot(tri_ref[rs, :], sel, preferred_element_type=F32)
        rank1 = jnp.sum(jnp.where(oh1, before, 0.0), axis=-1, keepdims=True).astype(I32)
        rank2 = jnp.sum(jnp.where(oh2, before, 0.0), axis=-1, keepdims=True).astype(I32)
        mi_ref[rs, :] = jnp.where(lane == 0, i1, jnp.where(lane == 1, i2,
                                  jnp.where(lane == 2, rank1, jnp.where(lane == 3, rank2, 0))))
    counts = jnp.sum(sel.astype(F32), axis=0, keepdims=True).astype(I32)
    cnt_ref[...] = jnp.broadcast_to(counts, cnt_ref.shape)


def _route(a, w_out, out_slot, x, g, router):
    t, d = x.shape
    k = a.shape[1]
    tm = SEG_TILE
    r_pad = jnp.zeros((d, LANES), F32).at[:, :N_EXPERTS].set(router.astype(F32))
    return pl.pallas_call(
        _route_kernel,
        out_shape=(jax.ShapeDtypeStruct((t, d), F32),
                   jax.ShapeDtypeStruct((t, d), BF16),
                   jax.ShapeDtypeStruct((t, LANES), I32),
                   jax.ShapeDtypeStruct((t, LANES), F32),
                   jax.ShapeDtypeStruct((t // tm, SUBLANES, LANES), I32)),
        grid=(t // tm,),
        in_specs=[pl.BlockSpec((tm, k), lambda i: (i, 0)),
                  pl.BlockSpec((None, k, d), lambda i: (out_slot, 0, 0)),
                  pl.BlockSpec((tm, d), lambda i: (i, 0)),
                  pl.BlockSpec((1, d), lambda i: (0, 0)),
                  pl.BlockSpec((d, LANES), lambda i: (0, 0))],
        out_specs=(pl.BlockSpec((tm, d), lambda i: (i, 0)),
                   pl.BlockSpec((tm, d), lambda i: (i, 0)),
                   pl.BlockSpec((tm, LANES), lambda i: (i, 0)),
                   pl.BlockSpec((tm, LANES), lambda i: (i, 0)),
                   pl.BlockSpec((None, SUBLANES, LANES), lambda i: (i, 0, 0))),
        scratch_shapes=[pltpu.VMEM((tm, tm), BF16), pltpu.VMEM((k, d), BF16)],
        compiler_params=_cparams(1),
        name="moe_route",
    )(a, w_out, x, g.reshape(1, d), r_pad)


def _slot_rows(mi, loc_ref, base):
    e1, e2, d1, d2 = mi[0], mi[1], mi[2], mi[3]
    for e in range(N_EXPERTS):
        off = loc_ref[base + e]
        d1 = d1 + jnp.where(e1 == e, off, 0)
        d2 = d2 + jnp.where(e2 == e, off, 0)
    return d1, d2


def _segment_copies(base, seg_ref, n16_ref, loc_ref, hbm_ref, vmem_ref, sem, *, to_hbm, wait):
    for e in range(N_EXPERTS):
        n16 = n16_ref[base + e]
        hbm0 = seg_ref[base + e]
        vmem0 = loc_ref[base + e]
        off = 0
        for size in SEG_SIZES:
            @pl.when((n16 & size) != 0)
            def _(off=off, size=size):
                h = hbm_ref.at[pl.ds(pl.multiple_of(hbm0 + off, SEG_ALIGN), size)]
                v = vmem_ref.at[pl.ds(pl.multiple_of(vmem0 + off, SEG_ALIGN), size)]
                cp = pltpu.make_async_copy(v, h, sem) if to_hbm else pltpu.make_async_copy(h, v, sem)
                if wait:
                    cp.wait()
                else:
                    cp.start()
            off = off + (n16 & size)


def _scatter_kernel(seg_ref, n16_ref, loc_ref, tail_ref, nv_ref, hn_ref, mi_ref, xs_ref, stage_ref,
                    zero_ref, sem):
    i = pl.program_id(0)
    tm = hn_ref.shape[0]
    rows = stage_ref.shape[1]
    tg = zero_ref.shape[0]
    slot = lax.rem(i, 2)

    @pl.when(i == 0)
    def _():
        zero_ref[...] = jnp.zeros_like(zero_ref)
        for e in range(N_EXPERTS):
            tail = pl.multiple_of(tail_ref[e], SEG_ALIGN)
            pltpu.make_async_copy(zero_ref, xs_ref.at[pl.ds(tail, tg)], sem.at[2]).start()
        for e in range(N_EXPERTS):
            tail = pl.multiple_of(tail_ref[e], SEG_ALIGN)
            pltpu.make_async_copy(zero_ref, xs_ref.at[pl.ds(tail, tg)], sem.at[2]).wait()

        def zero_tile(tile, carry):
            cp = pltpu.make_async_copy(zero_ref, xs_ref.at[pl.ds(pl.multiple_of(tile * tg, tg), tg)],
                                       sem.at[2])
            cp.start()
            cp.wait()
            return carry

        lax.fori_loop(nv_ref[0], xs_ref.shape[0] // tg, zero_tile, 0)

    base = i * N_EXPERTS
    mi_t = mi_ref[...].T
    d1, d2 = _slot_rows([mi_t[k:k + 1, :] for k in range(4)], loc_ref, base)
    row = lax.broadcasted_iota(I32, (rows, tm), 0)
    onehot = ((row == d1) | (row == d2)).astype(BF16)
    stage_ref[slot] = jnp.dot(onehot, hn_ref[...], preferred_element_type=F32).astype(BF16)
    tabs = (seg_ref, n16_ref, loc_ref, xs_ref)
    _segment_copies(base, *tabs, stage_ref.at[slot], sem.at[slot], to_hbm=True, wait=False)

    @pl.when(i > 0)
    def _():
        _segment_copies(base - N_EXPERTS, *tabs, stage_ref.at[1 - slot], sem.at[1 - slot],
                        to_hbm=True, wait=True)

    @pl.when(i == pl.num_programs(0) - 1)
    def _():
        _segment_copies(base, *tabs, stage_ref.at[slot], sem.at[slot], to_hbm=True, wait=True)


def _scatter(hn, mi, seg, n16, loc, tails, n_valid, n_rows):
    t, d = hn.shape
    tm = SEG_TILE
    return pl.pallas_call(
        _scatter_kernel,
        out_shape=jax.ShapeDtypeStruct((n_rows, d), BF16),
        grid_spec=pltpu.PrefetchScalarGridSpec(
            num_scalar_prefetch=5,
            grid=(t // tm,),
            in_specs=[pl.BlockSpec((tm, d), lambda i, *_: (i, 0)),
                      pl.BlockSpec((tm, LANES), lambda i, *_: (i, 0))],
            out_specs=pl.BlockSpec(memory_space=pl.ANY),
            scratch_shapes=[pltpu.VMEM((2, SEG_STAGE, d), BF16), pltpu.VMEM((MOE_TILE, d), BF16),
                            pltpu.SemaphoreType.DMA((3,))]),
        compiler_params=pltpu.CompilerParams(dimension_semantics=("arbitrary",),
                                             vmem_limit_bytes=VMEM_LIMIT, has_side_effects=True),
        name="moe_scatter",
    )(seg, n16, loc, tails, n_valid, hn, mi)


def _seg_expert_kernel(te_ref, tb_ref, nr_ref, nv_ref, xs_ref, wg_ref, wu_ref, wd_ref, ys_ref, acc_ref):
    i, f = pl.program_id(0), pl.program_id(1)
    step = xs_ref.shape[0] // MOE_HEIGHTS

    @pl.when(i < nv_ref[0])
    def _():
        @pl.when(f == 0)
        def _():
            acc_ref[...] = jnp.zeros_like(acc_ref)

        nrow = nr_ref[i]
        for q in range(1, MOE_HEIGHTS + 1):
            lo_ok = nrow > (q - 1) * step
            in_q = lo_ok if q == MOE_HEIGHTS else lo_ok & (nrow <= q * step)

            @pl.when(in_q)
            def _(q=q):
                _swiglu_accumulate(xs_ref, wg_ref, wu_ref, wd_ref, acc_ref, rows=q * step)

        @pl.when(f == pl.num_programs(1) - 1)
        def _():
            ys_ref[...] = acc_ref[...].astype(ys_ref.dtype)

    @pl.when((i >= nv_ref[0]) & (f == pl.num_programs(1) - 1))
    def _():
        ys_ref[...] = jnp.zeros_like(ys_ref)


def _seg_experts(xs, w_gate_up, w_down, slot, tile_expert, tile_block, tile_rows, n_valid, n_tiles):
    n_rows, d = xs.shape
    ff = w_down.shape[-2]
    tg, tf = MOE_TILE, FF_TILE
    nf = ff // tf

    def fidx(i, f, nv):
        return jnp.where(i < nv[0], f, nf - 1)

    return pl.pallas_call(
        _seg_expert_kernel,
        out_shape=jax.ShapeDtypeStruct((n_tiles * tg, d), BF16),
        grid_spec=pltpu.PrefetchScalarGridSpec(
            num_scalar_prefetch=4,
            grid=(n_tiles, nf),
            in_specs=[pl.BlockSpec((tg, d), lambda i, f, te, tb, nr, nv: (tb[i], 0)),
                      pl.BlockSpec((None, None, d, tf),
                                   lambda i, f, te, tb, nr, nv: (slot, te[i], 0, fidx(i, f, nv))),
                      pl.BlockSpec((None, None, d, tf),
                                   lambda i, f, te, tb, nr, nv: (slot, te[i], 0, nf + fidx(i, f, nv))),
                      pl.BlockSpec((None, None, tf, d),
                                   lambda i, f, te, tb, nr, nv: (slot, te[i], fidx(i, f, nv), 0))],
            out_specs=pl.BlockSpec((tg, d), lambda i, f, te, tb, nr, nv: (i, 0)),
            scratch_shapes=[pltpu.VMEM((tg, d), F32)]),
        compiler_params=_cparams(2),
        name="moe_experts",
    )(tile_expert, tile_block, tile_rows, n_valid, xs, w_gate_up, w_gate_up, w_down)


def _gather_kernel(seg_ref, n16_ref, loc_ref, x_ref, mi_ref, mf_ref, gf_ref, ys_ref, o_ref,
                   *rest, norm):
    if norm == "next":
        nxt_ref, ybuf_ref, sem = rest
    else:
        ybuf_ref, sem = rest
    i = pl.program_id(0)
    tm = x_ref.shape[0]
    rows = ybuf_ref.shape[1]
    slot = lax.rem(i, 2)
    base = i * N_EXPERTS
    tabs = (seg_ref, n16_ref, loc_ref, ys_ref)

    @pl.when(i == 0)
    def _():
        ybuf_ref[...] = jnp.zeros_like(ybuf_ref)
        _segment_copies(base, *tabs, ybuf_ref.at[0], sem.at[0], to_hbm=False, wait=False)

    @pl.when(i + 1 < pl.num_programs(0))
    def _():
        _segment_copies(base + N_EXPERTS, *tabs, ybuf_ref.at[1 - slot], sem.at[1 - slot],
                        to_hbm=False, wait=False)

    mi = mi_ref[...]
    d1, d2 = _slot_rows([mi[:, k:k + 1] for k in range(4)], loc_ref, base)
    col = lax.broadcasted_iota(I32, (tm, rows), 1)
    gates = mf_ref[...]
    _segment_copies(base, *tabs, ybuf_ref.at[slot], sem.at[slot], to_hbm=False, wait=True)
    ybuf = ybuf_ref[slot]
    out = x_ref[...]
    for k, dk in enumerate((d1, d2)):
        yk = jnp.dot((col == dk).astype(BF16), ybuf, preferred_element_type=F32)
        out = out + gates[:, k:k + 1] * yk
    if norm == "final":
        out = _rms_rows(out, gf_ref[...])
    o_ref[...] = out
    if norm == "next":
        nxt_ref[...] = _rms_rows(out, gf_ref[...]).astype(BF16)


def _gather(x, mi, mf, ys, seg, n16, loc, g_norm, norm):
    t, d = x.shape
    tm = SEG_TILE
    row_spec = pl.BlockSpec((tm, d), lambda i, *_: (i, 0))
    out_shape = jax.ShapeDtypeStruct((t, d), F32)
    emit_next = norm == "next"
    return pl.pallas_call(
        functools.partial(_gather_kernel, norm=norm),
        out_shape=(out_shape, jax.ShapeDtypeStruct((t, d), BF16)) if emit_next else out_shape,
        grid_spec=pltpu.PrefetchScalarGridSpec(
            num_scalar_prefetch=3,
            grid=(t // tm,),
            in_specs=[row_spec,
                      pl.BlockSpec((tm, LANES), lambda i, *_: (i, 0)),
                      pl.BlockSpec((tm, LANES), lambda i, *_: (i, 0)),
                      pl.BlockSpec((1, d), lambda i, *_: (0, 0)),
                      pl.BlockSpec(memory_space=pl.ANY)],
            out_specs=(row_spec, row_spec) if emit_next else row_spec,
            scratch_shapes=[pltpu.VMEM((2, SEG_STAGE, d), BF16), pltpu.SemaphoreType.DMA((2,))]),
        compiler_params=_cparams(1),
        name="moe_gather",
    )(seg, n16, loc, x, mi, mf, g_norm.reshape(1, d).astype(F32), ys)


def _moe_seg(a, w_out, out_slot, x, g, router, w_gate_up, w_down, slot, g_norm, norm):
    t, d = x.shape
    tg = MOE_TILE
    x, hn, mi, mf, cnt = _route(a, w_out, out_slot, x, g, router)

    counts = cnt[:, 0, :N_EXPERTS]
    n16 = (counts + SEG_ALIGN - 1) // SEG_ALIGN * SEG_ALIGN
    rows_e = jnp.sum(n16, axis=0)
    tiles_e = (rows_e + tg - 1) // tg
    tile_start = jnp.cumsum(tiles_e) - tiles_e
    row_start = tile_start * tg
    seg = row_start[None, :] + jnp.cumsum(n16, axis=0) - n16
    loc = jnp.cumsum(n16, axis=1) - n16
    n_valid = jnp.sum(tiles_e).astype(I32)
    n_tiles = (2 * t + counts.size * (SEG_ALIGN - 1)) // tg + N_EXPERTS
    n_rows = (n_tiles + 1) * tg
    tidx = jnp.minimum(jnp.arange(n_tiles, dtype=I32), n_valid - 1)
    tile_expert = (jnp.sum(tidx[:, None] >= tile_start[None, :], axis=1) - 1).astype(I32)
    tails = (row_start + rows_e).astype(I32)
    seg, n16, loc = (a.reshape(-1).astype(I32) for a in (seg, n16, loc))

    n_valid = n_valid.reshape(1)
    xs = _scatter(hn, mi, seg, n16, loc, tails, n_valid, n_rows)
    tile_rows = (tails[tile_expert] - tidx * tg).astype(I32)
    ys = _seg_experts(xs, w_gate_up, w_down, slot, tile_expert, tidx, tile_rows, n_valid.reshape(1), n_tiles)
    return _gather(x, mi, mf, ys, seg, n16, loc, g_norm, norm)


def kernel(x, rel_bias, norm_mix, norm_ffn, norm_final, a_w_in, a_w_group, a_scale, a_w_out,
           b_w_in, b_w_out, c_w_in, c_lambda, c_subln, c_w_out, d_w_in, d_sink, d_w_out,
           f_w_gate_up, f_w_down, m_router, m_w_gate_up, m_w_down):
    batch, seq, d = x.shape
    h = x.reshape(batch * seq, d)

    u = _norm_matmul(h, norm_mix[0], a_w_in, 0, F32)
    y = _pool_mix(u, a_w_group, a_scale, 0, batch, seq)
    h = _matmul_res(y, a_w_out, 0, h)
    h = _ffn(h, norm_ffn[0], f_w_gate_up, f_w_down, 0)

    qkv = _dil_proj(h.reshape(batch, seq, d), norm_mix[1], b_w_in, 0, tuple(p[1] for p in DIL_PAIRS))
    o = _dilated_attention(qkv, rel_bias, batch, seq)
    h, hn = _moe_seg(o, b_w_out, 0, h, norm_ffn[1], m_router[0], m_w_gate_up, m_w_down, 0,
                     norm_mix[2], "next")

    qkv = _matmul(hn, c_w_in, 0, BF16)
    o = _diff_attention(qkv, c_lambda, c_subln, 0, rel_bias, batch, seq)
    h = _matmul_res(o, c_w_out, 0, h)
    h, hn = _ffn(h, norm_ffn[2], f_w_gate_up, f_w_down, 1, g_next=norm_mix[3])

    qkv = _matmul(hn, d_w_in, 0, BF16)
    o = _gqa_attention(qkv, d_sink[0], rel_bias, batch, seq)
    h = _moe_seg(o, d_w_out, 0, h, norm_ffn[3], m_router[1], m_w_gate_up, m_w_down, 1,
                 norm_final, "final")
    return h.reshape(batch, seq, d)
```
